```python
import math
import jax, jax.numpy as jnp
from jax import lax
import numpy as np

D_MODEL = 2048
BATCH = 32
SEQ = 256
DEPTH = 2
DEC_BATCH = 2
DEC_SEQ = 2048
PAST_LEN = 256

GRID_W = 64
N_EVEN = (DEPTH + 1) // 2
N_ODD = DEPTH // 2
EPS = 1e-6
ROPE_BASE = 10000.0
Q_BLOCK = 128
H_A = 4
DK_A = 128
DV_A = 2 * DK_A
H_B = 4
DK_B = 128
DV_B = 256
RET_CHUNK = 128
E_C = 3 * D_MODEL
H_C = 8
CHUNK_C = 128
N_EXPERTS = 16
N_EXPERT_GROUPS = 4
EXPERTS_PER_GROUP = N_EXPERTS // N_EXPERT_GROUPS
TOPK_GROUPS = 1
TOP_K = 2
F_E = 512
A_QK = H_A * 2 * DK_A
A_V = H_A * DV_A
B_QK = H_B * DK_B
B_V = H_B * DV_B
B_G = H_B * DV_B
IN_EVEN = 2 * A_QK + A_V + 2 * B_QK + B_V + B_G
OUT_EVEN = A_V + B_V
EVEN_SPLITS = [A_QK, 2 * A_QK, 2 * A_QK + A_V, 2 * A_QK + A_V + B_QK, 2 * A_QK + A_V + 2 * B_QK, 2 * A_QK + A_V + 2 * B_QK + B_V]

kernel_name = 'hybrid_diffusion_ctxprefix_step'

F32 = jnp.float32


def _rms(x):
    xf = x.astype(F32)
    return xf * lax.rsqrt(jnp.mean(xf * xf, axis=-1, keepdims=True) + EPS)


def rmsnorm(x, g):
    return (_rms(x) * g.astype(F32)).astype(x.dtype)


def ada(cond, w, b):
    m = jax.nn.silu(cond) @ w + b
    return jnp.split(m[:, None, :], 6, axis=-1)


def modulate(h, shift, scale):
    return h * (1.0 + scale) + shift


def axial_rope_tables(n_tok):
    rows = n_tok // GRID_W
    r, col = jnp.meshgrid(jnp.arange(rows), jnp.arange(GRID_W), indexing='ij')
    pos = jnp.stack([r.reshape(-1), col.reshape(-1)], axis=-1).astype(F32)
    n_freq = DK_A // 4
    inv = jnp.power(ROPE_BASE, -jnp.arange(n_freq, dtype=F32) / n_freq)
    ang = pos[:, :, None] * inv
    return jnp.cos(ang), jnp.sin(ang)


def apply_axial_rope(x, cos, sin):
    b_, l_, h_, dim = x.shape
    xr = x.reshape(b_, l_, h_, 2, 2, dim // 4).astype(F32)
    a, b = xr[..., 0, :], xr[..., 1, :]
    c = cos[None, :, None]
    s = sin[None, :, None]
    out = jnp.stack([a * c - b * s, a * s + b * c], axis=-2)
    return out.reshape(x.shape).astype(x.dtype)


def diff_attention(q, k, v, lam):
    b_, lq = q.shape[:2]
    nb = lq // Q_BLOCK
    qb = q.reshape(b_, nb, Q_BLOCK, H_A, 2, DK_A).swapaxes(0, 1)
    scale = DK_A ** -0.5

    def one_block(qi):
        s = jnp.einsum('bqhcd,bkhcd->bchqk', qi, k).astype(F32) * scale
        p = jax.nn.softmax(s, axis=-1)
        a = p[:, 0] - lam * p[:, 1]
        return jnp.einsum('bhqk,bkhe->bqhe', a.astype(v.dtype), v)

    o = lax.map(one_block, qb)
    return o.swapaxes(0, 1).reshape(b_, lq, H_A, DV_A)


def retention_scan(q, k, v, logg, s0):
    b_, l_, h_, dk = q.shape
    cs = RET_CHUNK
    nc = l_ // cs

    def to_chunks(t):
        return t.astype(F32).reshape(b_, nc, cs, h_, t.shape[-1]).swapaxes(0, 1)

    idx = jnp.arange(cs)
    diff = (idx[:, None] - idx[None, :]).astype(F32)
    decay_mask = jnp.where(diff[None] >= 0, jnp.exp(jnp.maximum(diff, 0.0)[None] * logg[:, None, None]), 0.0)
    q_decay = jnp.exp((idx + 1).astype(F32)[None, :] * logg[:, None]).T[None, :, :, None]
    k_decay = jnp.exp((cs - 1 - idx).astype(F32)[None, :] * logg[:, None]).T[None, :, :, None]
    chunk_decay = jnp.exp(cs * logg)[None, :, None, None]

    def step(state, inp):
        qc, kc, vc = inp
        qk = jnp.einsum('bihd,bjhd->bhij', qc, kc) * decay_mask
        inner = jnp.einsum('bhij,bjhe->bihe', qk, vc)
        cross = jnp.einsum('bihd,bhde->bihe', qc * q_decay, state)
        new_state = chunk_decay * state + jnp.einsum('bjhd,bjhe->bhde', kc * k_decay, vc)
        return new_state, inner + cross

    s_final, o = lax.scan(step, s0, (to_chunks(q), to_chunks(k), to_chunks(v)))
    return o.swapaxes(0, 1).reshape(b_, l_, h_, v.shape[-1]), s_final


def even_mixer(h, w_in, w_out, lam_q, lam_k, decay_logit, lam_init, rope, ctx):
    b_, l_, _ = h.shape
    p = h @ w_in
    aq, ak, av, bq, bk, bv, bg = jnp.split(p, EVEN_SPLITS, axis=-1)
    aq = aq.reshape(b_, l_, H_A, 2, DK_A)
    ak = ak.reshape(b_, l_, H_A, 2, DK_A)
    av = av.reshape(b_, l_, H_A, DV_A)
    if rope is not None:
        cos, sin = rope
        aq = apply_axial_rope(aq.reshape(b_, l_, H_A * 2, DK_A), cos, sin).reshape(b_, l_, H_A, 2, DK_A)
        ak = apply_axial_rope(ak.reshape(b_, l_, H_A * 2, DK_A), cos, sin).reshape(b_, l_, H_A, 2, DK_A)
    own_k, own_v = ak, av
    if ctx is None:
        keys, vals = ak, av
        s0 = jnp.zeros((b_, 2, H_B, DK_B, DV_B), F32)
    else:
        ctx_k, ctx_v, s0 = ctx
        keys = jnp.concatenate([ctx_k.astype(ak.dtype), ak], axis=1)
        vals = jnp.concatenate([ctx_v.astype(av.dtype), av], axis=1)
    lq = lam_q.astype(F32)
    lk = lam_k.astype(F32)
    lam = jnp.exp(jnp.sum(lq[0] * lk[0])) - jnp.exp(jnp.sum(lq[1] * lk[1])) + lam_init
    oa = _rms(diff_attention(aq, keys, vals, lam)) * (1.0 - lam_init)
    bq = bq.reshape(b_, l_, H_B, DK_B)
    bk = bk.reshape(b_, l_, H_B, DK_B) * (DK_B ** -0.5)
    bv = bv.reshape(b_, l_, H_B, DV_B)
    logg = jax.nn.log_sigmoid(decay_logit.astype(F32))
    s0 = s0.astype(F32)
    o_f, s_f = retention_scan(bq, bk, bv, logg[0], s0[:, 0])
    o_b, s_b = retention_scan(jnp.flip(bq, 1), jnp.flip(bk, 1), jnp.flip(bv, 1), logg[1], s0[:, 1])
    o_r = _rms(o_f + jnp.flip(o_b, 1)).reshape(b_, l_, B_V) * jax.nn.silu(bg.astype(F32))
    merged = jnp.concatenate([oa.reshape(b_, l_, A_V), o_r], axis=-1).astype(h.dtype)
    return merged @ w_out, (own_k, own_v, jnp.stack([s_f, s_b], axis=1))


def odd_mixer(h, w_in, norm_g, w_s, b_s, w_out):
    b_, l_, _ = h.shape
    z = jax.nn.gelu(h @ w_in, approximate=False)
    u, v = jnp.split(z, 2, axis=-1)
    v = rmsnorm(v, norm_g)
    nc = l_ // CHUNK_C
    vr = v.reshape(b_, nc, CHUNK_C, H_C, E_C // H_C)
    mixed = jnp.einsum('hij,bnjhe->bnihe', w_s, vr) + b_s.T[None, None, :, :, None]
    return (u * mixed.reshape(b_, l_, E_C)) @ w_out


def moe(h, router_w, router_b, w_gate, w_up, w_down):
    b_, l_, d_ = h.shape
    t = h.reshape(b_ * l_, d_)
    scores = jax.nn.sigmoid((t @ router_w).astype(F32))
    biased = scores + router_b.astype(F32)
    grp_score = lax.top_k(biased.reshape(-1, N_EXPERT_GROUPS, EXPERTS_PER_GROUP), 2)[0].sum(-1)
    _, gsel = lax.top_k(grp_score, TOPK_GROUPS)
    gmask = jax.nn.one_hot(gsel, N_EXPERT_GROUPS, dtype=F32).sum(-2)
    emask = jnp.repeat(gmask, EXPERTS_PER_GROUP, axis=-1) > 0
    _, esel = lax.top_k(jnp.where(emask, biased, -jnp.inf), TOP_K)
    wsel = jnp.take_along_axis(scores, esel, axis=-1)
    wsel = wsel / jnp.sum(wsel, axis=-1, keepdims=True)
    combine = jnp.einsum('tk,tke->te', wsel, jax.nn.one_hot(esel, N_EXPERTS, dtype=F32))
    a = jnp.einsum('td,edf->tef', t, w_gate)
    b = jnp.einsum('td,edf->tef', t, w_up)
    act = jax.nn.silu(a) * b * combine[:, :, None].astype(t.dtype)
    return jnp.einsum('tef,efd->td', act, w_down).reshape(b_, l_, d_)


def setup_inputs(seed: int = 0) -> dict:
    key = jax.random.key(seed)
    ks = jax.random.split(key, 32)

    def nrm(k, shape, scale):
        return jax.random.normal(k, shape, F32) * scale

    g = 1.0 - jnp.power(2.0, -5.0 - jnp.arange(H_B, dtype=F32))
    base_logit = jnp.log(g / (1.0 - g))
    return {
        'x_prompt': nrm(ks[0], (BATCH, SEQ, D_MODEL), 1.0),
        'x_sample': nrm(ks[1], (DEC_BATCH, DEC_SEQ, D_MODEL), 1.0),
        'cache_k': nrm(ks[2], (DEC_BATCH, N_EVEN, PAST_LEN, H_A, 2, DK_A), 1.0),
        'cache_v': nrm(ks[3], (DEC_BATCH, N_EVEN, PAST_LEN, H_A, DV_A), 1.0),
        'state_ret': nrm(ks[4], (DEC_BATCH, N_EVEN, 2, H_B, DK_B, DV_B), 0.5),
        'c': nrm(ks[5], (DEC_BATCH, D_MODEL), 1.0),
        'c_ctx': nrm(ks[6], (D_MODEL,), 1.0),
        'norm_g': 1.0 + nrm(ks[7], (DEPTH, 2, D_MODEL), 0.02),
        'w_mod': nrm(ks[8], (DEPTH, D_MODEL, 6 * D_MODEL), 0.5 * D_MODEL ** -0.5),
        'b_mod': nrm(ks[9], (DEPTH, 6 * D_MODEL), 0.01),
        'w_in_even': nrm(ks[10], (N_EVEN, D_MODEL, IN_EVEN), D_MODEL ** -0.5),
        'w_out_even': nrm(ks[11], (N_EVEN, OUT_EVEN, D_MODEL), OUT_EVEN ** -0.5),
        'lam_q': nrm(ks[12], (N_EVEN, 2, DK_A), 0.1),
        'lam_k': nrm(ks[13], (N_EVEN, 2, DK_A), 0.1),
        'ret_decay_logit': base_logit + nrm(ks[14], (N_EVEN, 2, H_B), 0.05),
        'w_in_odd': nrm(ks[15], (N_ODD, D_MODEL, 2 * E_C), D_MODEL ** -0.5),
        'gmlp_norm_g': 1.0 + nrm(ks[16], (N_ODD, E_C), 0.02),
        'w_spatial': nrm(ks[17], (N_ODD, H_C, CHUNK_C, CHUNK_C), CHUNK_C ** -0.5),
        'b_spatial': nrm(ks[18], (N_ODD, H_C, CHUNK_C), 0.02),
        'w_out_odd': nrm(ks[19], (N_ODD, E_C, D_MODEL), E_C ** -0.5),
        'router_w': nrm(ks[20], (D_MODEL, N_EXPERTS), D_MODEL ** -0.5),
        'router_b': nrm(ks[21], (N_EXPERTS,), 0.01),
        'w_gate': nrm(ks[22], (DEPTH, N_EXPERTS, D_MODEL, F_E), D_MODEL ** -0.5),
        'w_up': nrm(ks[23], (DEPTH, N_EXPERTS, D_MODEL, F_E), D_MODEL ** -0.5),
        'w_down': nrm(ks[24], (DEPTH, N_EXPERTS, F_E, D_MODEL), F_E ** -0.5),
        'final_norm_g': 1.0 + nrm(ks[25], (D_MODEL,), 0.02),
    }


def reference(x_prompt, x_sample, cache_k, cache_v, state_ret, c, c_ctx, norm_g, w_mod, b_mod,
              w_in_even, w_out_even, lam_q, lam_k, ret_decay_logit, w_in_odd, gmlp_norm_g,
              w_spatial, b_spatial, w_out_odd, router_w, router_b, w_gate, w_up, w_down, final_norm_g):
    x_ctx, x_lat = x_prompt, x_sample
    rope = axial_rope_tables(x_sample.shape[1])
    new_k, new_v, new_s = [], [], []
    for l in range(DEPTH):
        m_ctx = ada(c_ctx[None, :], w_mod[l], b_mod[l])
        m_lat = ada(c, w_mod[l], b_mod[l])
        h_ctx = modulate(rmsnorm(x_ctx, norm_g[l, 0]), m_ctx[0], m_ctx[1])
        h_lat = modulate(rmsnorm(x_lat, norm_g[l, 0]), m_lat[0], m_lat[1])
        i = l // 2
        if l % 2 == 0:
            lam_init = 0.8 - 0.6 * math.exp(-0.3 * l)
            o_ctx, (k_i, v_i, s_i) = even_mixer(h_ctx, w_in_even[i], w_out_even[i], lam_q[i], lam_k[i],
                                                ret_decay_logit[i], lam_init, None, None)
            o_lat, _ = even_mixer(h_lat, w_in_even[i], w_out_even[i], lam_q[i], lam_k[i],
                                  ret_decay_logit[i], lam_init, rope,
                                  (cache_k[:, i], cache_v[:, i], state_ret[:, i]))
            new_k.append(k_i)
            new_v.append(v_i)
            new_s.append(s_i)
        else:
            o_ctx = odd_mixer(h_ctx, w_in_odd[i], gmlp_norm_g[i], w_spatial[i], b_spatial[i], w_out_odd[i])
            o_lat = odd_mixer(h_lat, w_in_odd[i], gmlp_norm_g[i], w_spatial[i], b_spatial[i], w_out_odd[i])
        x_ctx = x_ctx + m_ctx[2] * o_ctx
        x_lat = x_lat + m_lat[2] * o_lat
        h_ctx = modulate(rmsnorm(x_ctx, norm_g[l, 1]), m_ctx[3], m_ctx[4])
        h_lat = modulate(rmsnorm(x_lat, norm_g[l, 1]), m_lat[3], m_lat[4])
        x_ctx = x_ctx + m_ctx[5] * moe(h_ctx, router_w, router_b, w_gate[l], w_up[l], w_down[l])
        x_lat = x_lat + m_lat[5] * moe(h_lat, router_w, router_b, w_gate[l], w_up[l], w_down[l])
    y_prompt = rmsnorm(x_ctx, final_norm_g)
    y_sample = rmsnorm(x_lat, final_norm_g)
    new_cache_k = jnp.stack(new_k, axis=1)
    new_cache_v = jnp.stack(new_v, axis=1)
    new_state_ret = jnp.stack(new_s, axis=1)
    return (y_prompt, y_sample, new_cache_k, new_cache_v, new_state_ret)
```

```python
import functools
import math

import jax
import jax.numpy as jnp
from jax import lax
from jax.experimental import pallas as pl
from jax.experimental.pallas import tpu as pltpu

F32 = jnp.float32
BF16 = jnp.bfloat16

EPS = 1e-6
ROPE_BASE = 10000.0
GRID_W = 64
H_A, DK_A, DV_A = 4, 128, 256
H_B, DK_B, DV_B = 4, 128, 256
RET_CHUNK = 128
H_C, CHUNK_C = 8, 128
N_EXPERTS, N_GROUPS, GROUP_SIZE = 16, 4, 4

VMEM_LIMIT_BYTES = 56 * 1024 * 1024


def _params(*sem):
    return pltpu.CompilerParams(dimension_semantics=sem, vmem_limit_bytes=VMEM_LIMIT_BYTES)


def _rms_f32(x):
    return x * lax.rsqrt(jnp.mean(x * x, axis=-1, keepdims=True) + EPS)


def _silu(x):
    return x * jax.nn.sigmoid(x)


def _dot(a, b):
    return jnp.dot(a, b, preferred_element_type=F32)


def _dot_nt(a, b):
    return lax.dot_general(a, b, (((1,), (1,)), ((), ())), preferred_element_type=F32)


def _mod_kernel(c_ref, w_ref, b_ref, o_ref):
    s = _silu(c_ref[...]).astype(BF16)
    o_ref[...] = _dot(s, w_ref[...].astype(BF16)) + b_ref[...]


def _modulation(cond8, w_mod, b_mod, tn=1024):
    depth, d, n = w_mod.shape
    return pl.pallas_call(
        _mod_kernel,
        grid=(depth, n // tn),
        in_specs=[
            pl.BlockSpec((8, d), lambda l, j: (0, 0)),
            pl.BlockSpec((None, d, tn), lambda l, j: (l, 0, j)),
            pl.BlockSpec((None, 1, tn), lambda l, j: (l, 0, j)),
        ],
        out_specs=pl.BlockSpec((None, 8, tn), lambda l, j: (l, 0, j)),
        out_shape=jax.ShapeDtypeStruct((depth, 8, n), F32),
        compiler_params=_params("arbitrary", "arbitrary"),
    )(cond8, w_mod, b_mod.reshape(depth, 1, n))


def _in_kernel(x_ref, g_ref, sh_ref, sc_ref, w_ref, *rest, mode, kv_blocks):
    if mode == "even_kv":
        p_ref, k_ref, v_ref, h_ref = rest
    else:
        p_ref, h_ref = rest
    j = pl.program_id(1)

    @pl.when(j == 0)
    def _():
        h = (_rms_f32(x_ref[...]) * g_ref[...]) * (1.0 + sc_ref[...]) + sh_ref[...]
        h_ref[...] = h.astype(BF16)

    acc = _dot(h_ref[...], w_ref[...].astype(BF16))
    if mode == "odd":
        p_ref[...] = (0.5 * acc * (1.0 + lax.erf(acc * (2.0 ** -0.5)))).astype(BF16)
    else:
        p_ref[...] = acc.astype(BF16)
    if mode == "even_kv":
        k_lo, v_lo, nb = kv_blocks

        @pl.when((j >= k_lo) & (j < k_lo + nb))
        def _():
            k_ref[...] = acc

        @pl.when((j >= v_lo) & (j < v_lo + nb))
        def _():
            v_ref[...] = acc


def _in_proj(x, g, mod3, layer, which, row_fn, w, mode, tm, tn):
    t, d = x.shape
    n = w.shape[1]
    base = layer * 48

    def mod_spec(kind):
        return pl.BlockSpec((None, 1, d), lambda i, j: (base + row_fn(i) * 6 + kind, 0, 0))

    in_specs = [
        pl.BlockSpec((tm, d), lambda i, j: (i, 0)),
        pl.BlockSpec((1, d), lambda i, j: (0, 0)),
        mod_spec(which),
        mod_spec(which + 1),
        pl.BlockSpec((d, tn), lambda i, j: (0, j)),
    ]
    out_specs = [pl.BlockSpec((tm, tn), lambda i, j: (i, j))]
    out_shape = [jax.ShapeDtypeStruct((t, n), BF16)]
    kv_blocks = None
    if mode == "even_kv":
        a_qk = H_A * 2 * DK_A
        a_v = H_A * DV_A
        nb = a_qk // tn
        k_lo, v_lo = a_qk // tn, 2 * a_qk // tn
        kv_blocks = (k_lo, v_lo, nb)
        out_specs += [
            pl.BlockSpec((tm, tn), lambda i, j: (i, jnp.clip(j - k_lo, 0, nb - 1))),
            pl.BlockSpec((tm, tn), lambda i, j: (i, jnp.clip(j - v_lo, 0, nb - 1))),
        ]
        out_shape += [jax.ShapeDtypeStruct((t, a_qk), F32), jax.ShapeDtypeStruct((t, a_v), F32)]
    return pl.pallas_call(
        functools.partial(_in_kernel, mode=mode, kv_blocks=kv_blocks),
        grid=(t // tm, n // tn),
        in_specs=in_specs,
        out_specs=out_specs,
        out_shape=out_shape,
        scratch_shapes=[pltpu.VMEM((tm, d), BF16)],
        compiler_params=_params("arbitrary", "arbitrary"),
    )(x, g.reshape(1, d), mod3, mod3, w)


def _rope(x, c, s):
    lane = lax.broadcasted_iota(jnp.int32, x.shape, 1)
    partner = jnp.where((lane & 32) == 0, pltpu.roll(x, 96, 1), pltpu.roll(x, 32, 1))
    return x * c + partner * s


def _mixer_kernel(*refs, seq, has_ctx, lam_init, past):
    if has_ctx:
        (aq_ref, ak_ref, av_ref, bq_ref, bk_ref, bv_ref, bg_ref, lq_ref, lk_ref, dl_ref,
         ck_ref, cv_ref, s0_ref, cos_ref, sin_ref,
         oa_ref, or_ref, kall_ref, vall_ref, of_ref, st_ref) = refs
    else:
        (aq_ref, ak_ref, av_ref, bq_ref, bk_ref, bv_ref, bg_ref, lq_ref, lk_ref, dl_ref,
         oa_ref, or_ref, sout_ref, kall_ref, vall_ref, of_ref, st_ref) = refs
    lk_total = seq + past
    tq = 256
    rt = 256

    if has_ctx:
        for c in range(2):
            kall_ref[c, 0:past, :] = ck_ref[:, c * DK_A:(c + 1) * DK_A].astype(BF16)
        vall_ref[0:past, :] = cv_ref[...].astype(BF16)

        def copy_body(r, carry):
            r0 = pl.multiple_of(r * rt, rt)
            cs_, sn_ = cos_ref[pl.ds(r0, rt), :], sin_ref[pl.ds(r0, rt), :]
            kk = ak_ref[pl.ds(r0, rt), :].astype(F32)
            for c in range(2):
                kall_ref[c, pl.ds(past + r0, rt), :] = _rope(
                    kk[:, c * DK_A:(c + 1) * DK_A], cs_, sn_).astype(BF16)
            vall_ref[pl.ds(past + r0, rt), :] = av_ref[pl.ds(r0, rt), :]
            return carry

        lax.fori_loop(0, seq // rt, copy_body, 0)
    else:
        for c in range(2):
            kall_ref[c] = ak_ref[:, c * DK_A:(c + 1) * DK_A]
        vall_ref[...] = av_ref[...]

    prod = lq_ref[...] * lk_ref[...]
    ee = jnp.exp(jnp.sum(prod, axis=1, keepdims=True))
    lam = ee[0:1, :] - ee[1:2, :] + lam_init
    scale = DK_A ** -0.5

    def attn_body(qt, carry):
        r0 = pl.multiple_of(qt * tq, tq)
        q = aq_ref[pl.ds(r0, tq), :]
        if has_ctx:
            qf = q.astype(F32)
            cs_, sn_ = cos_ref[pl.ds(r0, tq), :], sin_ref[pl.ds(r0, tq), :]
        probs = []
        for c in range(2):
            if has_ctx:
                qc = _rope(qf[:, c * DK_A:(c + 1) * DK_A], cs_, sn_).astype(BF16)
            else:
                qc = q[:, c * DK_A:(c + 1) * DK_A]
            s = _dot_nt(qc, kall_ref[c]) * scale
            e = jnp.exp(s - jnp.max(s, axis=-1, keepdims=True))
            probs.append(e * (1.0 / jnp.sum(e, axis=-1, keepdims=True)))
        a = (probs[0] - lam * probs[1]).astype(BF16)
        o = _dot(a, vall_ref[...])
        oa_ref[pl.ds(r0, tq), :] = (_rms_f32(o) * (1.0 - lam_init)).astype(BF16)
        return carry

    lax.fori_loop(0, seq // tq, attn_body, 0)

    cs = RET_CHUNK
    nc = seq // cs
    ii = lax.broadcasted_iota(jnp.int32, (cs, cs), 0)
    jj = lax.broadcasted_iota(jnp.int32, (cs, cs), 1)
    ic = lax.broadcasted_iota(jnp.int32, (cs, 1), 0).astype(F32)
    for d in range(2):
        lg = jax.nn.log_sigmoid(dl_ref[d])
        diff = (ii - jj if d == 0 else jj - ii).astype(F32)
        mask = jnp.where(diff >= 0, jnp.exp(jnp.maximum(diff, 0.0) * lg), 0.0)
        if d == 0:
            qdec = jnp.exp((ic + 1.0) * lg)
            kdec = jnp.exp((cs - 1.0 - ic) * lg)
        else:
            qdec = jnp.exp((cs - ic) * lg)
            kdec = jnp.exp(ic * lg)
        cdec = jnp.exp(cs * lg)
        if has_ctx:
            st_ref[...] = s0_ref[d]
        else:
            st_ref[...] = jnp.zeros_like(st_ref)

        def chunk_body(n, carry, d=d, mask=mask, qdec=qdec, kdec=kdec, cdec=cdec):
            m = n if d == 0 else nc - 1 - n
            r0 = pl.multiple_of(m * cs, cs)
            qc = bq_ref[pl.ds(r0, cs), :].astype(F32)
            kc = bk_ref[pl.ds(r0, cs), :].astype(F32) * (DK_B ** -0.5)
            vc = bv_ref[pl.ds(r0, cs), :]
            state = st_ref[...]
            qk = _dot_nt(qc.astype(BF16), kc.astype(BF16)) * mask
            o = _dot(qk.astype(BF16), vc) + _dot((qc * qdec).astype(BF16), state.astype(BF16))
            kd = (kc * kdec).T.astype(BF16)
            st_ref[...] = cdec * state + _dot(kd, vc)
            if d == 0:
                of_ref[pl.ds(r0, cs), :] = o
            else:
                tot = of_ref[pl.ds(r0, cs), :] + o
                gate = _silu(bg_ref[pl.ds(r0, cs), :].astype(F32))
                or_ref[pl.ds(r0, cs), :] = (_rms_f32(tot) * gate).astype(BF16)
            return carry

        lax.fori_loop(0, nc, chunk_body, 0)
        if not has_ctx:
            sout_ref[d] = st_ref[...]


def _mixer(p, lam_q, lam_k, decay_logit, lam_init, batch, seq, ctx=None):
    has_ctx = ctx is not None
    past = ctx[0].shape[1] if has_ctx else 0

    def col(width, off):
        return pl.BlockSpec((seq, width), lambda b, h: (b, off // width + h))

    a_qk = H_A * 2 * DK_A
    a_v = H_A * DV_A
    b_qk = H_B * DK_B
    b_v = H_B * DV_B
    o_ak, o_av = a_qk, 2 * a_qk
    o_bq = 2 * a_qk + a_v
    o_bk, o_bv = o_bq + b_qk, o_bq + 2 * b_qk
    o_bg = o_bv + b_v
    in_specs = [
        col(2 * DK_A, 0), col(2 * DK_A, o_ak), col(DV_A, o_av),
        col(DK_B, o_bq), col(DK_B, o_bk), col(DV_B, o_bv), col(DV_B, o_bg),
        pl.BlockSpec((2, DK_A), lambda b, h: (0, 0)),
        pl.BlockSpec((2, DK_A), lambda b, h: (0, 0)),
        pl.BlockSpec((2, None, 1, 1), lambda b, h: (0, h, 0, 0)),
    ]
    args = [p] * 7 + [lam_q, lam_k, decay_logit.reshape(2, H_B, 1, 1)]
    out_specs = [
        pl.BlockSpec((seq, DV_A), lambda b, h: (b, h)),
        pl.BlockSpec((seq, DV_B), lambda b, h: (b, h)),
    ]
    out_shape = [
        jax.ShapeDtypeStruct((batch * seq, a_v), BF16),
        jax.ShapeDtypeStruct((batch * seq, b_v), BF16),
    ]
    if has_ctx:
        ck, cv, s0, cos_t, sin_t = ctx
        in_specs += [
            pl.BlockSpec((None, past, 2 * DK_A), lambda b, h: (b, 0, h)),
            pl.BlockSpec((None, past, DV_A), lambda b, h: (b, 0, h)),
            pl.BlockSpec((None, 2, None, DK_B, DV_B), lambda b, h: (b, 0, h, 0, 0)),
            pl.BlockSpec((seq, DK_A), lambda b, h: (0, 0)),
            pl.BlockSpec((seq, DK_A), lambda b, h: (0, 0)),
        ]
        args += [ck, cv, s0, cos_t, sin_t]
    else:
        out_specs.append(pl.BlockSpec((None, 2, None, DK_B, DV_B), lambda b, h: (b, 0, h, 0, 0)))
        out_shape.append(jax.ShapeDtypeStruct((batch, 2, H_B, DK_B, DV_B), F32))
    return pl.pallas_call(
        functools.partial(_mixer_kernel, seq=seq, has_ctx=has_ctx, lam_init=lam_init, past=past),
        grid=(batch, H_A),
        in_specs=in_specs,
        out_specs=out_specs,
        out_shape=out_shape,
        scratch_shapes=[
            pltpu.VMEM((2, seq + past, DK_A), BF16),
            pltpu.VMEM((seq + past, DV_A), BF16),
            pltpu.VMEM((seq, DV_B), F32),
            pltpu.VMEM((DK_B, DV_B), F32),
        ],
        compiler_params=_params("arbitrary", "arbitrary"),
    )(*args)


def _out_kernel(*refs, n_lhs, nk):
    lhs_refs = refs[:n_lhs]
    (w_ref, x_ref, gate_ref, g_ref, sh_ref, sc_ref, rw_ref,
     x1_ref, h2_ref, lg_ref, acc_ref) = refs[n_lhs:]
    k = pl.program_id(1)

    @pl.when(k == 0)
    def _():
        acc_ref[...] = jnp.zeros_like(acc_ref)

    wb = w_ref[...].astype(BF16)
    if n_lhs == 1:
        acc_ref[...] += _dot(lhs_refs[0][...], wb)
    else:
        kb = nk // n_lhs
        for idx in range(n_lhs):
            @pl.when((k >= idx * kb) & (k < (idx + 1) * kb))
            def _(idx=idx):
                acc_ref[...] += _dot(lhs_refs[idx][...], wb)

    @pl.when(k == nk - 1)
    def _():
        x1 = x_ref[...] + gate_ref[...] * acc_ref[...]
        x1_ref[...] = x1
        h2 = (_rms_f32(x1) * g_ref[...]) * (1.0 + sc_ref[...]) + sh_ref[...]
        h2_ref[...] = h2.astype(BF16)
        lg_ref[...] = lax.dot_general(rw_ref[...], h2, (((1,), (1,)), ((), ())),
                                      precision=lax.Precision.HIGHEST,
                                      preferred_element_type=F32)


def _out_proj(lhs_list, w, x, g, mod3, layer, row_fn, router_wt, t_total, tile_off, prev, tm, tk):
    n_lhs = len(lhs_list)
    t_seg, d = x.shape
    kdim = w.shape[0]
    nk = kdim // tk
    base = layer * 48
    ne = router_wt.shape[0]

    def mod_spec(kind):
        return pl.BlockSpec((None, 1, d), lambda i, k: (base + row_fn(i) * 6 + kind, 0, 0))

    kb = nk // n_lhs
    lhs_specs = [
        pl.BlockSpec((tm, tk), lambda i, k, idx=idx: (i, jnp.clip(k - idx * kb, 0, kb - 1)))
        for idx in range(n_lhs)
    ]
    in_specs = lhs_specs + [
        pl.BlockSpec((tk, d), lambda i, k: (k, 0)),
        pl.BlockSpec((tm, d), lambda i, k: (i, 0)),
        mod_spec(2),
        pl.BlockSpec((1, d), lambda i, k: (0, 0)),
        mod_spec(3),
        mod_spec(4),
        pl.BlockSpec((ne, d), lambda i, k: (0, 0)),
    ]
    args = list(lhs_list) + [w, x, mod3, g.reshape(1, d), mod3, mod3, router_wt]
    out_specs = [
        pl.BlockSpec((tm, d), lambda i, k: (i + tile_off, 0)),
        pl.BlockSpec((tm, d), lambda i, k: (i + tile_off, 0)),
        pl.BlockSpec((ne, tm), lambda i, k: (0, i + tile_off)),
    ]
    out_shape = [
        jax.ShapeDtypeStruct((t_total, d), F32),
        jax.ShapeDtypeStruct((t_total, d), BF16),
        jax.ShapeDtypeStruct((ne, t_total), F32),
    ]
    aliases = {}
    if prev is not None:
        n_in = len(args)
        in_specs += [pl.BlockSpec(memory_space=pl.ANY)] * 3
        args += list(prev)
        aliases = {n_in: 0, n_in + 1: 1, n_in + 2: 2}
    kern = functools.partial(_out_kernel, n_lhs=n_lhs, nk=nk)
    if prev is not None:
        inner = kern

        def kern(*refs):
            n_in_refs = n_lhs + 7
            return inner(*refs[:n_in_refs], *refs[n_in_refs + 3:])

    return pl.pallas_call(
        kern,
        grid=(t_seg // tm, nk),
        in_specs=in_specs,
        out_specs=out_specs,
        out_shape=out_shape,
        scratch_shapes=[pltpu.VMEM((tm, d), F32)],
        input_output_aliases=aliases,
        compiler_params=_params("arbitrary", "arbitrary"),
    )(*args)


def _first_index(vals, target):
    idx = jnp.full(target.shape, len(vals) - 1, jnp.int32)
    for k in range(len(vals) - 2, -1, -1):
        idx = jnp.where(vals[k] == target, k, idx)
    return idx


def _pick(vals, idx):
    out = vals[-1]
    for k in range(len(vals) - 2, -1, -1):
        out = jnp.where(idx == k, vals[k], out)
    return out


def _route_kernel(lg_ref, rb_ref, comb_ref):
    logits = lg_ref[...]
    scores = jax.nn.sigmoid(logits)
    biased = scores + rb_ref[...]
    s_rows = [scores[e:e + 1, :] for e in range(N_EXPERTS)]
    b_rows = [biased[e:e + 1, :] for e in range(N_EXPERTS)]
    grp = []
    for g in range(N_GROUPS):
        r = b_rows[g * GROUP_SIZE:(g + 1) * GROUP_SIZE]
        best = None
        for a in range(GROUP_SIZE):
            for b in range(a + 1, GROUP_SIZE):
                pair = r[a] + r[b]
                best = pair if best is None else jnp.maximum(best, pair)
        grp.append(best)
    gmax = functools.reduce(jnp.maximum, grp)
    gsel = _first_index(grp, gmax)
    cand_b = [_pick([b_rows[g * GROUP_SIZE + k] for g in range(N_GROUPS)], gsel) for k in range(GROUP_SIZE)]
    cand_s = [_pick([s_rows[g * GROUP_SIZE + k] for g in range(N_GROUPS)], gsel) for k in range(GROUP_SIZE)]
    m1 = functools.reduce(jnp.maximum, cand_b)
    k1 = _first_index(cand_b, m1)
    rest = [jnp.where(k1 == k, -jnp.inf, cand_b[k]) for k in range(GROUP_SIZE)]
    m2 = functools.reduce(jnp.maximum, rest)
    k2 = _first_index(rest, m2)
    w1 = _pick(cand_s, k1)
    w2 = _pick(cand_s, k2)
    wsum = w1 + w2
    w1 = w1 / wsum
    w2 = w2 / wsum
    e1 = gsel * GROUP_SIZE + k1
    e2 = gsel * GROUP_SIZE + k2
    eid = lax.broadcasted_iota(jnp.int32, logits.shape, 0)
    comb_ref[...] = jnp.where(eid == e1, w1, 0.0) + jnp.where(eid == e2, w2, 0.0)


def _route(logits_t, router_b):
    ne, t = logits_t.shape
    return pl.pallas_call(
        _route_kernel,
        out_shape=jax.ShapeDtypeStruct((ne, t), F32),
        compiler_params=pltpu.CompilerParams(vmem_limit_bytes=VMEM_LIMIT_BYTES),
    )(logits_t, router_b.reshape(ne, 1))


def _ffn_kernel(h_ref, comb_ref, wg_ref, wu_ref, wd_ref, x_ref, gate_ref, o_ref, acc_ref):
    e = pl.program_id(1)

    @pl.when(e == 0)
    def _():
        acc_ref[...] = jnp.zeros_like(acc_ref)

    h = h_ref[...]
    a = _dot(h, wg_ref[...].astype(BF16))
    b = _dot(h, wu_ref[...].astype(BF16))
    comb = comb_ref[...]
    lane = lax.broadcasted_iota(jnp.int32, comb.shape, 1)
    cw = jnp.sum(jnp.where(lane == e, comb, 0.0), axis=1, keepdims=True)
    act = (_silu(a) * b * cw).astype(BF16)
    acc_ref[...] += _dot(act, wd_ref[...].astype(BF16))

    @pl.when(e == pl.num_programs(1) - 1)
    def _():
        o_ref[...] = x_ref[...] + gate_ref[...] * acc_ref[...]


def _ffn(h2, comb, w_gate, w_up, w_down, x1, mod3, layer, row_fn, tm):
    t, d = x1.shape
    _, ne, _, f = w_gate.shape
    base = layer * 48
    in_specs = [
        pl.BlockSpec((tm, d), lambda i, e: (i, 0)),
        pl.BlockSpec((tm, ne), lambda i, e: (i, 0)),
        pl.BlockSpec((None, None, d, f), lambda i, e: (layer, e, 0, 0)),
        pl.BlockSpec((None, None, d, f), lambda i, e: (layer, e, 0, 0)),
        pl.BlockSpec((None, None, f, d), lambda i, e: (layer, e, 0, 0)),
        pl.BlockSpec((tm, d), lambda i, e: (i, 0), pipeline_mode=pl.Buffered(1)),
        pl.BlockSpec((None, 1, d), lambda i, e: (base + row_fn(i) * 6 + 5, 0, 0)),
    ]
    return pl.pallas_call(
        _ffn_kernel,
        grid=(t // tm, ne),
        in_specs=in_specs,
        out_specs=pl.BlockSpec((tm, d), lambda i, e: (i, 0)),
        out_shape=jax.ShapeDtypeStruct((t, d), F32),
        scratch_shapes=[pltpu.VMEM((tm, d), F32)],
        compiler_params=_params("arbitrary", "arbitrary"),
    )(h2, comb, w_gate, w_up, w_down, x1, mod3)


def _final_kernel(x_ref, g_ref, o_ref):
    o_ref[...] = _rms_f32(x_ref[...]) * g_ref[...]


def _final_norm(x, g, row0, rows, tm=512):
    d = x.shape[1]
    off = row0 // tm
    return pl.pallas_call(
        _final_kernel,
        grid=(rows // tm,),
        in_specs=[
            pl.BlockSpec((tm, d), lambda i: (i + off, 0)),
            pl.BlockSpec((1, d), lambda i: (0, 0)),
        ],
        out_specs=pl.BlockSpec((tm, d), lambda i: (i, 0)),
        out_shape=jax.ShapeDtypeStruct((rows, d), F32),
        compiler_params=_params("arbitrary"),
    )(x, g.reshape(1, d))


def _sgu_kernel(u_ref, v_ref, ng_ref, ws_ref, bs_ref, o_ref, *, n_chunks):
    e_c = v_ref.shape[1]
    gw = e_c // H_C
    vn = (_rms_f32(v_ref[...].astype(F32)) * ng_ref[...]).astype(BF16)
    for n in range(n_chunks):
        r0 = n * CHUNK_C
        for h in range(H_C):
            c0 = h * gw
            mixed = _dot(ws_ref[h].astype(BF16), vn[r0:r0 + CHUNK_C, c0:c0 + gw]) + bs_ref[h]
            u = u_ref[r0:r0 + CHUNK_C, c0:c0 + gw].astype(F32)
            o_ref[r0:r0 + CHUNK_C, c0:c0 + gw] = (u * mixed).astype(BF16)


def _sgu(z, norm_g, w_s, b_s, tm=256):
    t, e2 = z.shape
    e_c = e2 // 2
    return pl.pallas_call(
        functools.partial(_sgu_kernel, n_chunks=tm // CHUNK_C),
        grid=(t // tm,),
        in_specs=[
            pl.BlockSpec((tm, e_c), lambda i: (i, 0)),
            pl.BlockSpec((tm, e_c), lambda i: (i, 1)),
            pl.BlockSpec((1, e_c), lambda i: (0, 0)),
            pl.BlockSpec((H_C, CHUNK_C, CHUNK_C), lambda i: (0, 0, 0)),
            pl.BlockSpec((H_C, CHUNK_C, 1), lambda i: (0, 0, 0)),
        ],
        out_specs=pl.BlockSpec((tm, e_c), lambda i: (i, 0)),
        out_shape=jax.ShapeDtypeStruct((t, e_c), BF16),
        compiler_params=_params("arbitrary"),
    )(z, z, norm_g.reshape(1, e_c), w_s, b_s.reshape(H_C, CHUNK_C, 1))


def _rope_tables(n_tok):
    rows = n_tok // GRID_W
    pos_r = jnp.repeat(jnp.arange(rows), GRID_W).astype(F32)
    pos_c = jnp.tile(jnp.arange(GRID_W), rows).astype(F32)
    n_freq = DK_A // 4
    inv = jnp.power(ROPE_BASE, -jnp.arange(n_freq, dtype=F32) / n_freq)
    ang_r = pos_r[:, None] * inv
    ang_c = pos_c[:, None] * inv
    cos_t = jnp.concatenate([jnp.cos(ang_r), jnp.cos(ang_r), jnp.cos(ang_c), jnp.cos(ang_c)], axis=1)
    sin_t = jnp.concatenate([-jnp.sin(ang_r), jnp.sin(ang_r), -jnp.sin(ang_c), jnp.sin(ang_c)], axis=1)
    return cos_t, sin_t


def kernel(x_prompt, x_sample, cache_k, cache_v, state_ret, c, c_ctx, norm_g, w_mod, b_mod, w_in_even, w_out_even, lam_q, lam_k, ret_decay_logit, w_in_odd, gmlp_norm_g, w_spatial, b_spatial, w_out_odd, router_w, router_b, w_gate, w_up, w_down, final_norm_g):
    batch, seq, d = x_prompt.shape
    dbatch, dseq, _ = x_sample.shape
    past = cache_k.shape[2]
    depth = norm_g.shape[0]
    t_ctx, t_lat = batch * seq, dbatch * dseq
    t_all = t_ctx + t_lat
    xc = x_prompt.reshape(t_ctx, d)
    xl = x_sample.reshape(t_lat, d)

    cond8 = jnp.concatenate([c_ctx[None, :], c, jnp.zeros((8 - 1 - dbatch, d), F32)], axis=0)
    mod3 = _modulation(cond8, w_mod, b_mod).reshape(depth * 8 * 6, 1, d)
    router_wt = router_w.T
    cos_t, sin_t = _rope_tables(dseq)

    tm_in, tn_in = 1024, 512
    tm_out, tk_out = 512, 512
    tm_ffn = 512
    w_in0 = w_in_even.reshape(d, -1)
    w_out0 = w_out_even.reshape(-1, d)
    w_in1 = w_in_odd.reshape(d, -1)
    w_out1 = w_out_odd.reshape(-1, d)
    lq, lk, dlog = lam_q.reshape(2, DK_A), lam_k.reshape(2, DK_A), ret_decay_logit.reshape(2, H_B)

    def rows_ctx(i):
        return 0

    def rows_lat(tiles_per_batch):
        return lambda i: 1 + i // tiles_per_batch

    def rows_all(tm):
        n_ctx, per = t_ctx // tm, dseq // tm
        return lambda i: jnp.maximum(i - n_ctx, 0) // per + jnp.where(i >= n_ctx, 1, 0)

    lam_init = 0.8 - 0.6 * math.exp(-0.3 * 0)
    p_ctx, k_new, v_new = _in_proj(xc, norm_g[0, 0], mod3, 0, 0, rows_ctx, w_in0, "even_kv", tm_in, tn_in)
    (p_lat,) = _in_proj(xl, norm_g[0, 0], mod3, 0, 0, rows_lat(dseq // tm_in), w_in0, "even", tm_in, tn_in)
    oa_c, or_c, s_new = _mixer(p_ctx, lq, lk, dlog, lam_init, batch, seq)
    ctx = (cache_k.reshape(dbatch, past, H_A * 2 * DK_A), cache_v.reshape(dbatch, past, H_A * DV_A),
           state_ret.reshape(dbatch, 2, H_B, DK_B, DV_B), cos_t, sin_t)
    oa_l, or_l = _mixer(p_lat, lq, lk, dlog, lam_init, dbatch, dseq, ctx)
    seg = _out_proj([oa_c, or_c], w_out0, xc, norm_g[0, 1], mod3, 0, rows_ctx, router_wt,
                    t_all, 0, None, tm_out, tk_out)
    x1, h2, lg = _out_proj([oa_l, or_l], w_out0, xl, norm_g[0, 1], mod3, 0, rows_lat(dseq // tm_out),
                           router_wt, t_all, t_ctx // tm_out, seg, tm_out, tk_out)
    comb = _route(lg, router_b).T
    x2 = _ffn(h2, comb, w_gate, w_up, w_down, x1, mod3, 0, rows_all(tm_ffn), tm_ffn)

    (z,) = _in_proj(x2, norm_g[1, 0], mod3, 1, 0, rows_all(tm_in), w_in1, "odd", tm_in, tn_in)
    gated = _sgu(z, gmlp_norm_g.reshape(-1), w_spatial.reshape(H_C, CHUNK_C, CHUNK_C), b_spatial.reshape(H_C, CHUNK_C))
    x3, h4, lg2 = _out_proj([gated], w_out1, x2, norm_g[1, 1], mod3, 1, rows_all(tm_out), router_wt,
                            t_all, 0, None, tm_out, tk_out)
    comb2 = _route(lg2, router_b).T
    x4 = _ffn(h4, comb2, w_gate, w_up, w_down, x3, mod3, 1, rows_all(tm_ffn), tm_ffn)
    y_p = _final_norm(x4, final_norm_g, 0, t_ctx)
    y_s = _final_norm(x4, final_norm_g, t_ctx, t_lat)

    y_prompt = y_p.reshape(batch, seq, d)
    y_sample = y_s.reshape(dbatch, dseq, d)
    new_cache_k = k_new.reshape(batch, 1, seq, H_A, 2, DK_A)
    new_cache_v = v_new.reshape(batch, 1, seq, H_A, DV_A)
    new_state_ret = s_new.reshape(batch, 1, 2, H_B, DK_B, DV_B)
    return (y_prompt, y_sample, new_cache_k, new_cache_v, new_state_ret)
```

```python
import functools
import math

import jax
import jax.numpy as jnp
from jax import lax
from jax.experimental import pallas as pl
from jax.experimental.pallas import tpu as pltpu

F32 = jnp.float32
BF16 = jnp.bfloat16

EPS = 1e-6
ROPE_BASE = 10000.0
GRID_W = 64
H_A, DK_A, DV_A = 4, 128, 256
H_B, DK_B, DV_B = 4, 128, 256
RET_CHUNK = 128
H_C, CHUNK_C = 8, 128
N_EXPERTS, N_GROUPS, GROUP_SIZE = 16, 4, 4

VMEM_LIMIT_BYTES = 56 * 1024 * 1024


def _params(*sem):
    return pltpu.CompilerParams(dimension_semantics=sem, vmem_limit_bytes=VMEM_LIMIT_BYTES)


def _rms_f32(x):
    return x * lax.rsqrt(jnp.mean(x * x, axis=-1, keepdims=True) + EPS)


def _silu(x):
    return x * jax.nn.sigmoid(x)


def _dot(a, b):
    return jnp.dot(a, b, preferred_element_type=F32)


def _dot_nt(a, b):
    return lax.dot_general(a, b, (((1,), (1,)), ((), ())), preferred_element_type=F32)


LANES = 128


def _store_token_major(ref, val):
    tm, d = val.shape
    rpt = d // LANES
    for c in range(rpt):
        ref[pl.ds(c, tm, stride=rpt), :] = val[:, c * LANES:(c + 1) * LANES]


def _load_token_major(ref, tm, c):
    rpt = ref.shape[0] // tm
    return ref[pl.ds(c, tm, stride=rpt), :]


def _mod_kernel(c_ref, w_ref, b_ref, o_ref):
    s = _silu(c_ref[...]).astype(BF16)
    o_ref[...] = _dot(s, w_ref[...].astype(BF16)) + b_ref[...]


def _modulation(cond8, w_mod, b_mod, tn=1024):
    depth, d, n = w_mod.shape
    return pl.pallas_call(
        _mod_kernel,
        grid=(depth, n // tn),
        in_specs=[
            pl.BlockSpec((8, d), lambda l, j: (0, 0)),
            pl.BlockSpec((None, d, tn), lambda l, j: (l, 0, j)),
            pl.BlockSpec((None, 1, tn), lambda l, j: (l, 0, j)),
        ],
        out_specs=pl.BlockSpec((None, 8, tn), lambda l, j: (l, 0, j)),
        out_shape=jax.ShapeDtypeStruct((depth, 8, n), F32),
        compiler_params=_params("arbitrary", "arbitrary"),
    )(cond8, w_mod, b_mod.reshape(depth, 1, n))


def _in_kernel(x_ref, g_ref, sh_ref, sc_ref, w_ref, *rest, mode, kv_blocks):
    if mode == "even_kv":
        p_ref, k_ref, v_ref, h_ref = rest
    else:
        p_ref, h_ref = rest
    j = pl.program_id(1)

    @pl.when(j == 0)
    def _():
        h = (_rms_f32(x_ref[...]) * g_ref[...]) * (1.0 + sc_ref[...]) + sh_ref[...]
        h_ref[...] = h.astype(BF16)

    acc = _dot(h_ref[...], w_ref[...].astype(BF16))
    if mode == "odd":
        p_ref[...] = (0.5 * acc * (1.0 + lax.erf(acc * (2.0 ** -0.5)))).astype(BF16)
    else:
        p_ref[...] = acc.astype(BF16)
    if mode == "even_kv":
        k_lo, v_lo, nb = kv_blocks

        @pl.when((j >= k_lo) & (j < k_lo + nb))
        def _():
            k_ref[...] = acc

        @pl.when((j >= v_lo) & (j < v_lo + nb))
        def _():
            v_ref[...] = acc


def _in_proj(x, g, mod3, layer, which, row_fn, w, mode, tm, tn):
    t, d = x.shape
    n = w.shape[1]
    base = layer * 48

    def mod_spec(kind):
        return pl.BlockSpec((None, 1, d), lambda i, j: (base + row_fn(i) * 6 + kind, 0, 0))

    in_specs = [
        pl.BlockSpec((tm, d), lambda i, j: (i, 0)),
        pl.BlockSpec((1, d), lambda i, j: (0, 0)),
        mod_spec(which),
        mod_spec(which + 1),
        pl.BlockSpec((d, tn), lambda i, j: (0, j)),
    ]
    out_specs = [pl.BlockSpec((tm, tn), lambda i, j: (i, j))]
    out_shape = [jax.ShapeDtypeStruct((t, n), BF16)]
    kv_blocks = None
    if mode == "even_kv":
        a_qk = H_A * 2 * DK_A
        a_v = H_A * DV_A
        nb = a_qk // tn
        k_lo, v_lo = a_qk // tn, 2 * a_qk // tn
        kv_blocks = (k_lo, v_lo, nb)
        out_specs += [
            pl.BlockSpec((tm, tn), lambda i, j: (i, jnp.clip(j - k_lo, 0, nb - 1))),
            pl.BlockSpec((tm, tn), lambda i, j: (i, jnp.clip(j - v_lo, 0, nb - 1))),
        ]
        out_shape += [jax.ShapeDtypeStruct((t, a_qk), F32), jax.ShapeDtypeStruct((t, a_v), F32)]
    return pl.pallas_call(
        functools.partial(_in_kernel, mode=mode, kv_blocks=kv_blocks),
        grid=(t // tm, n // tn),
        in_specs=in_specs,
        out_specs=out_specs,
        out_shape=out_shape,
        scratch_shapes=[pltpu.VMEM((tm, d), BF16)],
        compiler_params=_params("arbitrary", "arbitrary"),
    )(x, g.reshape(1, d), mod3, mod3, w)


def _rope(x, c, s):
    lane = lax.broadcasted_iota(jnp.int32, x.shape, 1)
    partner = jnp.where((lane & 32) == 0, pltpu.roll(x, 96, 1), pltpu.roll(x, 32, 1))
    return x * c + partner * s


def _mixer_kernel(*refs, seq, has_ctx, lam_init, past):
    if has_ctx:
        (aq_ref, ak_ref, av_ref, bq_ref, bk_ref, bv_ref, bg_ref, lq_ref, lk_ref, dl_ref,
         ck_ref, cv_ref, s0_ref, cos_ref, sin_ref,
         oa_ref, or_ref, kall_ref, vall_ref, of_ref, st_ref) = refs
    else:
        (aq_ref, ak_ref, av_ref, bq_ref, bk_ref, bv_ref, bg_ref, lq_ref, lk_ref, dl_ref,
         oa_ref, or_ref, sout_ref, kall_ref, vall_ref, of_ref, st_ref) = refs
    lk_total = seq + past
    tq = 256
    rt = 256

    if has_ctx:
        for c in range(2):
            kall_ref[c, 0:past, :] = ck_ref[:, c * DK_A:(c + 1) * DK_A].astype(BF16)
        vall_ref[0:past, :] = cv_ref[...].astype(BF16)

        def copy_body(r, carry):
            r0 = pl.multiple_of(r * rt, rt)
            cs_, sn_ = cos_ref[pl.ds(r0, rt), :], sin_ref[pl.ds(r0, rt), :]
            kk = ak_ref[pl.ds(r0, rt), :].astype(F32)
            for c in range(2):
                kall_ref[c, pl.ds(past + r0, rt), :] = _rope(
                    kk[:, c * DK_A:(c + 1) * DK_A], cs_, sn_).astype(BF16)
            vall_ref[pl.ds(past + r0, rt), :] = av_ref[pl.ds(r0, rt), :]
            return carry

        lax.fori_loop(0, seq // rt, copy_body, 0)
    else:
        for c in range(2):
            kall_ref[c] = ak_ref[:, c * DK_A:(c + 1) * DK_A]
        vall_ref[...] = av_ref[...]

    prod = lq_ref[...] * lk_ref[...]
    ee = jnp.exp(jnp.sum(prod, axis=1, keepdims=True))
    lam = ee[0:1, :] - ee[1:2, :] + lam_init
    scale = DK_A ** -0.5

    def attn_body(qt, carry):
        r0 = pl.multiple_of(qt * tq, tq)
        q = aq_ref[pl.ds(r0, tq), :]
        if has_ctx:
            qf = q.astype(F32)
            cs_, sn_ = cos_ref[pl.ds(r0, tq), :], sin_ref[pl.ds(r0, tq), :]
        probs = []
        for c in range(2):
            if has_ctx:
                qc = _rope(qf[:, c * DK_A:(c + 1) * DK_A], cs_, sn_).astype(BF16)
            else:
                qc = q[:, c * DK_A:(c + 1) * DK_A]
            s = _dot_nt(qc, kall_ref[c]) * scale
            e = jnp.exp(s - jnp.max(s, axis=-1, keepdims=True))
            probs.append(e * (1.0 / jnp.sum(e, axis=-1, keepdims=True)))
        a = (probs[0] - lam * probs[1]).astype(BF16)
        o = _dot(a, vall_ref[...])
        oa_ref[pl.ds(r0, tq), :] = (_rms_f32(o) * (1.0 - lam_init)).astype(BF16)
        return carry

    lax.fori_loop(0, seq // tq, attn_body, 0)

    cs = RET_CHUNK
    nc = seq // cs
    ii = lax.broadcasted_iota(jnp.int32, (cs, cs), 0)
    jj = lax.broadcasted_iota(jnp.int32, (cs, cs), 1)
    ic = lax.broadcasted_iota(jnp.int32, (cs, 1), 0).astype(F32)
    for d in range(2):
        lg = jax.nn.log_sigmoid(dl_ref[d])
        diff = (ii - jj if d == 0 else jj - ii).astype(F32)
        mask = jnp.where(diff >= 0, jnp.exp(jnp.maximum(diff, 0.0) * lg), 0.0)
        if d == 0:
            qdec = jnp.exp((ic + 1.0) * lg)
            kdec = jnp.exp((cs - 1.0 - ic) * lg)
        else:
            qdec = jnp.exp((cs - ic) * lg)
            kdec = jnp.exp(ic * lg)
        cdec = jnp.exp(cs * lg)
        if has_ctx:
            st_ref[...] = s0_ref[d]
        else:
            st_ref[...] = jnp.zeros_like(st_ref)

        def chunk_body(n, carry, d=d, mask=mask, qdec=qdec, kdec=kdec, cdec=cdec):
            m = n if d == 0 else nc - 1 - n
            r0 = pl.multiple_of(m * cs, cs)
            qc = bq_ref[pl.ds(r0, cs), :].astype(F32)
            kc = bk_ref[pl.ds(r0, cs), :].astype(F32) * (DK_B ** -0.5)
            vc = bv_ref[pl.ds(r0, cs), :]
            state = st_ref[...]
            qk = _dot_nt(qc.astype(BF16), kc.astype(BF16)) * mask
            o = _dot(qk.astype(BF16), vc) + _dot((qc * qdec).astype(BF16), state.astype(BF16))
            kd = (kc * kdec).T.astype(BF16)
            st_ref[...] = cdec * state + _dot(kd, vc)
            if d == 0:
                of_ref[pl.ds(r0, cs), :] = o
            else:
                tot = of_ref[pl.ds(r0, cs), :] + o
                gate = _silu(bg_ref[pl.ds(r0, cs), :].astype(F32))
                or_ref[pl.ds(r0, cs), :] = (_rms_f32(tot) * gate).astype(BF16)
            return carry

        lax.fori_loop(0, nc, chunk_body, 0)
        if not has_ctx:
            sout_ref[d] = st_ref[...]


def _mixer(p, lam_q, lam_k, decay_logit, lam_init, batch, seq, ctx=None):
    has_ctx = ctx is not None
    past = ctx[0].shape[1] if has_ctx else 0

    def col(width, off):
        return pl.BlockSpec((seq, width), lambda b, h: (b, off // width + h))

    a_qk = H_A * 2 * DK_A
    a_v = H_A * DV_A
    b_qk = H_B * DK_B
    b_v = H_B * DV_B
    o_ak, o_av = a_qk, 2 * a_qk
    o_bq = 2 * a_qk + a_v
    o_bk, o_bv = o_bq + b_qk, o_bq + 2 * b_qk
    o_bg = o_bv + b_v
    in_specs = [
        col(2 * DK_A, 0), col(2 * DK_A, o_ak), col(DV_A, o_av),
        col(DK_B, o_bq), col(DK_B, o_bk), col(DV_B, o_bv), col(DV_B, o_bg),
        pl.BlockSpec((2, DK_A), lambda b, h: (0, 0)),
        pl.BlockSpec((2, DK_A), lambda b, h: (0, 0)),
        pl.BlockSpec((2, None, 1, 1), lambda b, h: (0, h, 0, 0)),
    ]
    args = [p] * 7 + [lam_q, lam_k, decay_logit.reshape(2, H_B, 1, 1)]
    out_specs = [
        pl.BlockSpec((seq, DV_A), lambda b, h: (b, h)),
        pl.BlockSpec((seq, DV_B), lambda b, h: (b, h)),
    ]
    out_shape = [
        jax.ShapeDtypeStruct((batch * seq, a_v), BF16),
        jax.ShapeDtypeStruct((batch * seq, b_v), BF16),
    ]
    if has_ctx:
        ck, cv, s0, cos_t, sin_t = ctx
        in_specs += [
            pl.BlockSpec((None, past, 2 * DK_A), lambda b, h: (b, 0, h)),
            pl.BlockSpec((None, past, DV_A), lambda b, h: (b, 0, h)),
            pl.BlockSpec((None, 2, None, DK_B, DV_B), lambda b, h: (b, 0, h, 0, 0)),
            pl.BlockSpec((seq, DK_A), lambda b, h: (0, 0)),
            pl.BlockSpec((seq, DK_A), lambda b, h: (0, 0)),
        ]
        args += [ck, cv, s0, cos_t, sin_t]
    else:
        out_specs.append(pl.BlockSpec((None, 2, None, DK_B, DV_B), lambda b, h: (b, 0, h, 0, 0)))
        out_shape.append(jax.ShapeDtypeStruct((batch, 2, H_B, DK_B, DV_B), F32))
    return pl.pallas_call(
        functools.partial(_mixer_kernel, seq=seq, has_ctx=has_ctx, lam_init=lam_init, past=past),
        grid=(batch, H_A),
        in_specs=in_specs,
        out_specs=out_specs,
        out_shape=out_shape,
        scratch_shapes=[
            pltpu.VMEM((2, seq + past, DK_A), BF16),
            pltpu.VMEM((seq + past, DV_A), BF16),
            pltpu.VMEM((seq, DV_B), F32),
            pltpu.VMEM((DK_B, DV_B), F32),
        ],
        compiler_params=_params("arbitrary", "arbitrary"),
    )(*args)


def _out_kernel(*refs, n_lhs, nk):
    lhs_refs = refs[:n_lhs]
    (w_ref, x_ref, gate_ref, g_ref, sh_ref, sc_ref, rw_ref,
     x1_ref, h2_ref, lg_ref, acc_ref) = refs[n_lhs:]
    k = pl.program_id(1)

    @pl.when(k == 0)
    def _():
        acc_ref[...] = jnp.zeros_like(acc_ref)

    wb = w_ref[...].astype(BF16)
    if n_lhs == 1:
        acc_ref[...] += _dot(lhs_refs[0][...], wb)
    else:
        kb = nk // n_lhs
        for idx in range(n_lhs):
            @pl.when((k >= idx * kb) & (k < (idx + 1) * kb))
            def _(idx=idx):
                acc_ref[...] += _dot(lhs_refs[idx][...], wb)

    @pl.when(k == nk - 1)
    def _():
        x1 = x_ref[...] + gate_ref[...] * acc_ref[...]
        x1_ref[...] = x1
        h2 = (_rms_f32(x1) * g_ref[...]) * (1.0 + sc_ref[...]) + sh_ref[...]
        _store_token_major(h2_ref, h2)
        lg_ref[...] = lax.dot_general(rw_ref[...], h2, (((1,), (1,)), ((), ())),
                                      precision=lax.Precision.HIGHEST,
                                      preferred_element_type=F32)


def _out_proj(lhs_list, w, x, g, mod3, layer, row_fn, router_wt, t_total, tile_off, prev, tm, tk):
    n_lhs = len(lhs_list)
    t_seg, d = x.shape
    kdim = w.shape[0]
    nk = kdim // tk
    base = layer * 48
    ne = router_wt.shape[0]

    def mod_spec(kind):
        return pl.BlockSpec((None, 1, d), lambda i, k: (base + row_fn(i) * 6 + kind, 0, 0))

    kb = nk // n_lhs
    lhs_specs = [
        pl.BlockSpec((tm, tk), lambda i, k, idx=idx: (i, jnp.clip(k - idx * kb, 0, kb - 1)))
        for idx in range(n_lhs)
    ]
    in_specs = lhs_specs + [
        pl.BlockSpec((tk, d), lambda i, k: (k, 0)),
        pl.BlockSpec((tm, d), lambda i, k: (i, 0)),
        mod_spec(2),
        pl.BlockSpec((1, d), lambda i, k: (0, 0)),
        mod_spec(3),
        mod_spec(4),
        pl.BlockSpec((ne, d), lambda i, k: (0, 0)),
    ]
    args = list(lhs_list) + [w, x, mod3, g.reshape(1, d), mod3, mod3, router_wt]
    rpt = d // LANES
    out_specs = [
        pl.BlockSpec((tm, d), lambda i, k: (i + tile_off, 0)),
        pl.BlockSpec((tm * rpt, LANES), lambda i, k: (i + tile_off, 0)),
        pl.BlockSpec((ne, tm), lambda i, k: (0, i + tile_off)),
    ]
    out_shape = [
        jax.ShapeDtypeStruct((t_total, d), F32),
        jax.ShapeDtypeStruct((t_total * rpt, LANES), F32),
        jax.ShapeDtypeStruct((ne, t_total), F32),
    ]
    aliases = {}
    if prev is not None:
        n_in = len(args)
        in_specs += [pl.BlockSpec(memory_space=pl.ANY)] * 3
        args += list(prev)
        aliases = {n_in: 0, n_in + 1: 1, n_in + 2: 2}
    kern = functools.partial(_out_kernel, n_lhs=n_lhs, nk=nk)
    if prev is not None:
        inner = kern

        def kern(*refs):
            n_in_refs = n_lhs + 7
            return inner(*refs[:n_in_refs], *refs[n_in_refs + 3:])

    return pl.pallas_call(
        kern,
        grid=(t_seg // tm, nk),
        in_specs=in_specs,
        out_specs=out_specs,
        out_shape=out_shape,
        scratch_shapes=[pltpu.VMEM((tm, d), F32)],
        input_output_aliases=aliases,
        compiler_params=_params("arbitrary", "arbitrary"),
    )(*args)


def _first_index(vals, target):
    idx = jnp.full(target.shape, len(vals) - 1, jnp.int32)
    for k in range(len(vals) - 2, -1, -1):
        idx = jnp.where(vals[k] == target, k, idx)
    return idx


def _pick(vals, idx):
    out = vals[-1]
    for k in range(len(vals) - 2, -1, -1):
        out = jnp.where(idx == k, vals[k], out)
    return out


def _route_kernel(lg_ref, rb_ref, pos_ref, wt_ref, meta_ref, rank_ref, *, tile_rows, scan):
    logits = lg_ref[...]
    t = logits.shape[1]
    scores = jax.nn.sigmoid(logits)
    biased = scores + rb_ref[...]
    s_rows = [scores[e:e + 1, :] for e in range(N_EXPERTS)]
    b_rows = [biased[e:e + 1, :] for e in range(N_EXPERTS)]
    grp = []
    for g in range(N_GROUPS):
        r = b_rows[g * GROUP_SIZE:(g + 1) * GROUP_SIZE]
        best = None
        for a in range(GROUP_SIZE):
            for b in range(a + 1, GROUP_SIZE):
                pair = r[a] + r[b]
                best = pair if best is None else jnp.maximum(best, pair)
        grp.append(best)
    gmax = functools.reduce(jnp.maximum, grp)
    gsel = _first_index(grp, gmax)
    cand_b = [_pick([b_rows[g * GROUP_SIZE + k] for g in range(N_GROUPS)], gsel) for k in range(GROUP_SIZE)]
    cand_s = [_pick([s_rows[g * GROUP_SIZE + k] for g in range(N_GROUPS)], gsel) for k in range(GROUP_SIZE)]
    m1 = functools.reduce(jnp.maximum, cand_b)
    k1 = _first_index(cand_b, m1)
    rest = [jnp.where(k1 == k, -jnp.inf, cand_b[k]) for k in range(GROUP_SIZE)]
    m2 = functools.reduce(jnp.maximum, rest)
    k2 = _first_index(rest, m2)
    w1 = _pick(cand_s, k1)
    w2 = _pick(cand_s, k2)
    wsum = w1 + w2
    w1 = w1 / wsum
    w2 = w2 / wsum
    e1 = gsel * GROUP_SIZE + k1
    e2 = gsel * GROUP_SIZE + k2
    wt_ref[...] = jnp.zeros_like(wt_ref)
    wt_ref[0:1, :] = w1
    wt_ref[1:2, :] = w2

    eid = lax.broadcasted_iota(jnp.int32, logits.shape, 0)
    sel1 = eid == e1
    sel2 = eid == e2
    ind = jnp.where(sel1 | sel2, 1.0, 0.0)
    ua = lax.broadcasted_iota(jnp.int32, (scan, scan), 0)
    ub = lax.broadcasted_iota(jnp.int32, (scan, scan), 1)
    upper = jnp.where(ua <= ub, 1.0, 0.0).astype(BF16)
    carry = jnp.zeros((N_EXPERTS, 1), F32)
    for blk in range(t // scan):
        seg = ind[:, blk * scan:(blk + 1) * scan]
        incl = _dot(seg.astype(BF16), upper)
        rank_ref[:, blk * scan:(blk + 1) * scan] = incl - seg + carry
        carry = carry + incl[:, scan - 1:scan]
    n_tiles = jnp.floor((carry + (tile_rows - 1.0)) * (1.0 / tile_rows))
    la = lax.broadcasted_iota(jnp.int32, (N_EXPERTS, N_EXPERTS), 0)
    lb = lax.broadcasted_iota(jnp.int32, (N_EXPERTS, N_EXPERTS), 1)
    lower = jnp.where(lb < la, 1.0, 0.0).astype(BF16)
    start = _dot(lower, jnp.broadcast_to(n_tiles, (N_EXPERTS, LANES)).astype(BF16))
    end = start + n_tiles
    slot = start[:, 0:1] * tile_rows + rank_ref[...]
    pos_ref[...] = jnp.zeros_like(pos_ref)
    pos_ref[0:1, :] = jnp.sum(jnp.where(sel1, slot, 0.0), axis=0, keepdims=True).astype(jnp.int32)
    pos_ref[1:2, :] = jnp.sum(jnp.where(sel2, slot, 0.0), axis=0, keepdims=True).astype(jnp.int32)

    n_used = end[N_EXPERTS - 1:N_EXPERTS, :]
    tile = lax.broadcasted_iota(jnp.int32, (N_EXPERTS, LANES), 1).astype(F32)
    tile = jnp.minimum(tile, n_used - 1.0)
    tile_expert = jnp.sum(jnp.where(tile >= end, 1.0, 0.0), axis=0, keepdims=True)
    lane = lax.broadcasted_iota(jnp.int32, (1, LANES), 1)
    meta_ref[...] = jnp.zeros_like(meta_ref)
    meta_ref[0:1, :] = jnp.where(lane == LANES - 1, n_used, tile_expert).astype(jnp.int32)


def _route(logits_t, router_b, tile_rows, scan=512):
    ne, t = logits_t.shape
    return pl.pallas_call(
        functools.partial(_route_kernel, tile_rows=tile_rows, scan=scan),
        out_shape=[
            jax.ShapeDtypeStruct((8, t), jnp.int32),
            jax.ShapeDtypeStruct((8, t), F32),
            jax.ShapeDtypeStruct((8, LANES), jnp.int32),
        ],
        scratch_shapes=[pltpu.VMEM((ne, t), F32)],
        compiler_params=pltpu.CompilerParams(vmem_limit_bytes=VMEM_LIMIT_BYTES),
    )(logits_t, router_b.reshape(ne, 1))


def _dispatch_kernel(pos_ref, h_ref, xs_in_ref, xs_ref, sem, *, t, rpt, chunk):
    del xs_in_ref
    c = pl.program_id(0)
    nchunks = pl.num_programs(0)

    def row_copy(tok, slot_row, s):
        return pltpu.make_async_copy(
            h_ref.at[pl.ds(pl.multiple_of(tok * rpt, rpt), rpt)],
            xs_ref.at[pl.ds(pl.multiple_of(slot_row * rpt, rpt), rpt)],
            sem.at[s])

    def wait_chunk(s):
        pltpu.make_async_copy(h_ref.at[pl.ds(0, 2 * chunk * rpt)], xs_ref.at[pl.ds(0, 2 * chunk * rpt)],
                              sem.at[s]).wait()

    def body(r, carry):
        tok = c * chunk + r
        for k in range(2):
            row_copy(tok, pos_ref[k * t + tok], c % 2).start()
        return carry

    lax.fori_loop(0, chunk, body, 0, unroll=8)

    @pl.when(c > 0)
    def _():
        wait_chunk((c - 1) % 2)

    @pl.when(c == nchunks - 1)
    def _():
        wait_chunk(c % 2)


def _dispatch(pos_flat, h_lin, n_slots, d, chunk=512):
    t = pos_flat.shape[0] // 2
    rpt = d // LANES
    xs0 = jnp.zeros((n_slots * rpt, LANES), F32)
    return pl.pallas_call(
        functools.partial(_dispatch_kernel, t=t, rpt=rpt, chunk=chunk),
        grid_spec=pltpu.PrefetchScalarGridSpec(
            num_scalar_prefetch=1,
            grid=(t // chunk,),
            in_specs=[pl.BlockSpec(memory_space=pl.ANY), pl.BlockSpec(memory_space=pl.ANY)],
            out_specs=pl.BlockSpec(memory_space=pl.ANY),
            scratch_shapes=[pltpu.SemaphoreType.DMA((2,))],
        ),
        out_shape=jax.ShapeDtypeStruct((n_slots * rpt, LANES), F32),
        input_output_aliases={2: 0},
        compiler_params=_params("arbitrary"),
    )(pos_flat, h_lin, xs0)


def _ffn_kernel(meta_ref, xs_ref, wg_ref, wu_ref, wd_ref, ys_ref, xb_ref, wgb_ref, wub_ref, wdb_ref, *, tm):
    n = pl.program_id(0)
    n_used = meta_ref[LANES - 1]

    @pl.when(n < n_used)
    def _():
        @pl.when((n == 0) | (meta_ref[n] != meta_ref[jnp.maximum(n - 1, 0)]))
        def _():
            wgb_ref[...] = wg_ref[...].astype(BF16)
            wub_ref[...] = wu_ref[...].astype(BF16)
            wdb_ref[...] = wd_ref[...].astype(BF16)

        d = xb_ref.shape[1]
        for c in range(d // LANES):
            xb_ref[:, c * LANES:(c + 1) * LANES] = _load_token_major(xs_ref, tm, c).astype(BF16)
        x = xb_ref[...]
        a = _dot(x, wgb_ref[...])
        b = _dot(x, wub_ref[...])
        act = (_silu(a) * b).astype(BF16)
        _store_token_major(ys_ref, _dot(act, wdb_ref[...]))


def _ffn(meta, xs, w_gate, w_up, w_down, layer, tm, n_tiles):
    _, ne, d, f = w_gate.shape
    rpt = d // LANES

    def row_tile(n, m):
        return (jnp.minimum(n, m[LANES - 1] - 1), 0)

    return pl.pallas_call(
        functools.partial(_ffn_kernel, tm=tm),
        grid_spec=pltpu.PrefetchScalarGridSpec(
            num_scalar_prefetch=1,
            grid=(n_tiles,),
            in_specs=[
                pl.BlockSpec((tm * rpt, LANES), row_tile),
                pl.BlockSpec((None, None, d, f), lambda n, m: (layer, m[n], 0, 0)),
                pl.BlockSpec((None, None, d, f), lambda n, m: (layer, m[n], 0, 0)),
                pl.BlockSpec((None, None, f, d), lambda n, m: (layer, m[n], 0, 0)),
            ],
            out_specs=pl.BlockSpec((tm * rpt, LANES), row_tile),
            scratch_shapes=[
                pltpu.VMEM((tm, d), BF16),
                pltpu.VMEM((d, f), BF16),
                pltpu.VMEM((d, f), BF16),
                pltpu.VMEM((f, d), BF16),
            ],
        ),
        out_shape=jax.ShapeDtypeStruct(xs.shape, F32),
        input_output_aliases={1: 0},
        compiler_params=_params("arbitrary"),
    )(meta, xs, w_gate, w_up, w_down)


def _combine_kernel(pos_ref, ys_ref, x_ref, wt_ref, gate_ref, o_ref, a0, b0, a1, b1, sem, *, t, rpt, tm):
    i = pl.program_id(0)
    n = pl.num_programs(0)
    bufs = ((a0, b0), (a1, b1))

    def issue(tile, s):
        def body(r, carry):
            tok = tile * tm + r
            for k in range(2):
                slot_row = pos_ref[k * t + tok]
                pltpu.make_async_copy(
                    ys_ref.at[pl.ds(pl.multiple_of(slot_row * rpt, rpt), rpt)],
                    bufs[s][k].at[pl.ds(pl.multiple_of(r * rpt, rpt), rpt)],
                    sem.at[s]).start()
            return carry

        lax.fori_loop(0, tm, body, 0, unroll=8)

    def wait(s):
        for k in range(2):
            pltpu.make_async_copy(ys_ref.at[pl.ds(0, tm * rpt)], bufs[s][k], sem.at[s]).wait()

    def compute(s):
        w = wt_ref[...]
        w0, w1 = w[:, 0:1], w[:, 1:2]
        for c in range(rpt):
            cols = slice(c * LANES, (c + 1) * LANES)
            y = w0 * _load_token_major(bufs[s][0], tm, c) + w1 * _load_token_major(bufs[s][1], tm, c)
            o_ref[:, cols] = x_ref[:, cols] + gate_ref[:, cols] * y

    @pl.when(i == 0)
    def _():
        issue(0, 0)

    for s in range(2):
        @pl.when(i % 2 == s)
        def _(s=s):
            @pl.when(i + 1 < n)
            def _():
                issue(i + 1, 1 - s)

            wait(s)
            compute(s)


def _combine(pos_flat, ys, x1, wt, mod3, layer, row_fn, tm=256):
    t, d = x1.shape
    rpt = d // LANES
    base = layer * 48
    return pl.pallas_call(
        functools.partial(_combine_kernel, t=t, rpt=rpt, tm=tm),
        grid_spec=pltpu.PrefetchScalarGridSpec(
            num_scalar_prefetch=1,
            grid=(t // tm,),
            in_specs=[
                pl.BlockSpec(memory_space=pl.ANY),
                pl.BlockSpec((tm, d), lambda i, p: (i, 0)),
                pl.BlockSpec((tm, 2), lambda i, p: (i, 0)),
                pl.BlockSpec((None, 1, d), lambda i, p: (base + row_fn(i) * 6 + 5, 0, 0)),
            ],
            out_specs=pl.BlockSpec((tm, d), lambda i, p: (i, 0)),
            scratch_shapes=[pltpu.VMEM((tm * rpt, LANES), F32)] * 4 + [pltpu.SemaphoreType.DMA((2,))],
        ),
        out_shape=jax.ShapeDtypeStruct((t, d), F32),
        compiler_params=_params("arbitrary"),
    )(pos_flat, ys, x1, wt, mod3)


def _moe(lg, router_b, h_lin, x1, w_gate, w_up, w_down, mod3, layer, row_fn, tile_rows=256):
    t, d = x1.shape
    n_tiles = (2 * t) // tile_rows + N_EXPERTS
    pos, wts, meta = _route(lg, router_b, tile_rows)
    pos_flat = pos[:2].reshape(-1)
    xs = _dispatch(pos_flat, h_lin, n_tiles * tile_rows, d)
    ys = _ffn(meta[0], xs, w_gate, w_up, w_down, layer, tile_rows, n_tiles)
    return _combine(pos_flat, ys, x1, wts[:2].T, mod3, layer, row_fn)


def _final_kernel(x_ref, g_ref, o_ref):
    o_ref[...] = _rms_f32(x_ref[...]) * g_ref[...]


def _final_norm(x, g, row0, rows, tm=512):
    d = x.shape[1]
    off = row0 // tm
    return pl.pallas_call(
        _final_kernel,
        grid=(rows // tm,),
        in_specs=[
            pl.BlockSpec((tm, d), lambda i: (i + off, 0)),
            pl.BlockSpec((1, d), lambda i: (0, 0)),
        ],
        out_specs=pl.BlockSpec((tm, d), lambda i: (i, 0)),
        out_shape=jax.ShapeDtypeStruct((rows, d), F32),
        compiler_params=_params("arbitrary"),
    )(x, g.reshape(1, d))


def _sgu_kernel(u_ref, v_ref, ng_ref, ws_ref, bs_ref, o_ref, *, n_chunks):
    e_c = v_ref.shape[1]
    gw = e_c // H_C
    vn = (_rms_f32(v_ref[...].astype(F32)) * ng_ref[...]).astype(BF16)
    for n in range(n_chunks):
        r0 = n * CHUNK_C
        for h in range(H_C):
            c0 = h * gw
            mixed = _dot(ws_ref[h].astype(BF16), vn[r0:r0 + CHUNK_C, c0:c0 + gw]) + bs_ref[h]
            u = u_ref[r0:r0 + CHUNK_C, c0:c0 + gw].astype(F32)
            o_ref[r0:r0 + CHUNK_C, c0:c0 + gw] = (u * mixed).astype(BF16)


def _sgu(z, norm_g, w_s, b_s, tm=256):
    t, e2 = z.shape
    e_c = e2 // 2
    return pl.pallas_call(
        functools.partial(_sgu_kernel, n_chunks=tm // CHUNK_C),
        grid=(t // tm,),
        in_specs=[
            pl.BlockSpec((tm, e_c), lambda i: (i, 0)),
            pl.BlockSpec((tm, e_c), lambda i: (i, 1)),
            pl.BlockSpec((1, e_c), lambda i: (0, 0)),
            pl.BlockSpec((H_C, CHUNK_C, CHUNK_C), lambda i: (0, 0, 0)),
            pl.BlockSpec((H_C, CHUNK_C, 1), lambda i: (0, 0, 0)),
        ],
        out_specs=pl.BlockSpec((tm, e_c), lambda i: (i, 0)),
        out_shape=jax.ShapeDtypeStruct((t, e_c), BF16),
        compiler_params=_params("arbitrary"),
    )(z, z, norm_g.reshape(1, e_c), w_s, b_s.reshape(H_C, CHUNK_C, 1))


def _rope_tables(n_tok):
    rows = n_tok // GRID_W
    pos_r = jnp.repeat(jnp.arange(rows), GRID_W).astype(F32)
    pos_c = jnp.tile(jnp.arange(GRID_W), rows).astype(F32)
    n_freq = DK_A // 4
    inv = jnp.power(ROPE_BASE, -jnp.arange(n_freq, dtype=F32) / n_freq)
    ang_r = pos_r[:, None] * inv
    ang_c = pos_c[:, None] * inv
    cos_t = jnp.concatenate([jnp.cos(ang_r), jnp.cos(ang_r), jnp.cos(ang_c), jnp.cos(ang_c)], axis=1)
    sin_t = jnp.concatenate([-jnp.sin(ang_r), jnp.sin(ang_r), -jnp.sin(ang_c), jnp.sin(ang_c)], axis=1)
    return cos_t, sin_t


def kernel(x_prompt, x_sample, cache_k, cache_v, state_ret, c, c_ctx, norm_g, w_mod, b_mod, w_in_even, w_out_even, lam_q, lam_k, ret_decay_logit, w_in_odd, gmlp_norm_g, w_spatial, b_spatial, w_out_odd, router_w, router_b, w_gate, w_up, w_down, final_norm_g):
    batch, seq, d = x_prompt.shape
    dbatch, dseq, _ = x_sample.shape
    past = cache_k.shape[2]
    depth = norm_g.shape[0]
    t_ctx, t_lat = batch * seq, dbatch * dseq
    t_all = t_ctx + t_lat
    xc = x_prompt.reshape(t_ctx, d)
    xl = x_sample.reshape(t_lat, d)

    cond8 = jnp.concatenate([c_ctx[None, :], c, jnp.zeros((8 - 1 - dbatch, d), F32)], axis=0)
    mod3 = _modulation(cond8, w_mod, b_mod).reshape(depth * 8 * 6, 1, d)
    router_wt = router_w.T
    cos_t, sin_t = _rope_tables(dseq)

    tm_in, tn_in = 1024, 512
    tm_out, tk_out = 512, 512
    tm_cmb = 256
    w_in0 = w_in_even.reshape(d, -1)
    w_out0 = w_out_even.reshape(-1, d)
    w_in1 = w_in_odd.reshape(d, -1)
    w_out1 = w_out_odd.reshape(-1, d)
    lq, lk, dlog = lam_q.reshape(2, DK_A), lam_k.reshape(2, DK_A), ret_decay_logit.reshape(2, H_B)

    def rows_ctx(i):
        return 0

    def rows_lat(tiles_per_batch):
        return lambda i: 1 + i // tiles_per_batch

    def rows_all(tm):
        n_ctx, per = t_ctx // tm, dseq // tm
        return lambda i: jnp.maximum(i - n_ctx, 0) // per + jnp.where(i >= n_ctx, 1, 0)

    lam_init = 0.8 - 0.6 * math.exp(-0.3 * 0)
    p_ctx, k_new, v_new = _in_proj(xc, norm_g[0, 0], mod3, 0, 0, rows_ctx, w_in0, "even_kv", tm_in, tn_in)
    (p_lat,) = _in_proj(xl, norm_g[0, 0], mod3, 0, 0, rows_lat(dseq // tm_in), w_in0, "even", tm_in, tn_in)
    oa_c, or_c, s_new = _mixer(p_ctx, lq, lk, dlog, lam_init, batch, seq)
    ctx = (cache_k.reshape(dbatch, past, H_A * 2 * DK_A), cache_v.reshape(dbatch, past, H_A * DV_A),
           state_ret.reshape(dbatch, 2, H_B, DK_B, DV_B), cos_t, sin_t)
    oa_l, or_l = _mixer(p_lat, lq, lk, dlog, lam_init, dbatch, dseq, ctx)
    seg = _out_proj([oa_c, or_c], w_out0, xc, norm_g[0, 1], mod3, 0, rows_ctx, router_wt,
                    t_all, 0, None, tm_out, tk_out)
    x1, h2, lg = _out_proj([oa_l, or_l], w_out0, xl, norm_g[0, 1], mod3, 0, rows_lat(dseq // tm_out),
                           router_wt, t_all, t_ctx // tm_out, seg, tm_out, tk_out)
    x2 = _moe(lg, router_b, h2, x1, w_gate, w_up, w_down, mod3, 0, rows_all(tm_cmb))

    (z,) = _in_proj(x2, norm_g[1, 0], mod3, 1, 0, rows_all(tm_in), w_in1, "odd", tm_in, tn_in)
    gated = _sgu(z, gmlp_norm_g.reshape(-1), w_spatial.reshape(H_C, CHUNK_C, CHUNK_C), b_spatial.reshape(H_C, CHUNK_C))
    x3, h4, lg2 = _out_proj([gated], w_out1, x2, norm_g[1, 1], mod3, 1, rows_all(tm_out), router_wt,
                            t_all, 0, None, tm_out, tk_out)
    x4 = _moe(lg2, router_b, h4, x3, w_gate, w_up, w_down, mod3, 1, rows_all(tm_cmb))
    y_p = _final_norm(x4, final_norm_g, 0, t_ctx)
    y_s = _final_norm(x4, final_norm_g, t_ctx, t_lat)

    y_prompt = y_p.reshape(batch, seq, d)
    y_sample = y_s.reshape(dbatch, dseq, d)
    new_cache_k = k_new.reshape(batch, 1, seq, H_A, 2, DK_A)
    new_cache_v = v_new.reshape(batch, 1, seq, H_A, DV_A)
    new_state_ret = s_new.reshape(batch, 1, 2, H_B, DK_B, DV_B)
    return (y_prompt, y_sample, new_cache_k, new_cache_v, new_state_ret)
```

```python
import functools
import math

import jax
import jax.numpy as jnp
from jax import lax
from jax.experimental import pallas as pl
from jax.experimental.pallas import tpu as pltpu

F32 = jnp.float32
BF16 = jnp.bfloat16

EPS = 1e-6
ROPE_BASE = 10000.0
GRID_W = 64
H_A, DK_A, DV_A = 4, 128, 256
H_B, DK_B, DV_B = 4, 128, 256
RET_CHUNK = 128
H_C, CHUNK_C = 8, 128
N_EXPERTS, N_GROUPS, GROUP_SIZE = 16, 4, 4

VMEM_LIMIT_BYTES = 56 * 1024 * 1024


def _params(*sem):
    return pltpu.CompilerParams(dimension_semantics=sem, vmem_limit_bytes=VMEM_LIMIT_BYTES)


def _rms_f32(x):
    return x * lax.rsqrt(jnp.mean(x * x, axis=-1, keepdims=True) + EPS)


def _silu(x):
    return x * jax.nn.sigmoid(x)


def _dot(a, b):
    return jnp.dot(a, b, preferred_element_type=F32)


def _dot_nt(a, b):
    return lax.dot_general(a, b, (((1,), (1,)), ((), ())), preferred_element_type=F32)


LANES = 128


def _store_token_major(ref, val):
    tm, d = val.shape
    rpt = d // LANES
    for c in range(rpt):
        ref[pl.ds(c, tm, stride=rpt), :] = val[:, c * LANES:(c + 1) * LANES]


def _load_token_major(ref, tm, c):
    rpt = ref.shape[0] // tm
    return ref[pl.ds(c, tm, stride=rpt), :]


def _pack_bf16_pairs(h):
    half = h.shape[1] // 2
    lo = lax.bitcast_convert_type(h[:, :half].astype(BF16).astype(F32), jnp.int32)
    hi = lax.bitcast_convert_type(h[:, half:].astype(BF16).astype(F32), jnp.int32)
    return hi | lax.shift_right_logical(lo, 16)


def _unpack_bf16_pairs(w):
    lo = lax.bitcast_convert_type(lax.shift_left(w, 16), F32)
    hi = lax.bitcast_convert_type(w & jnp.int32(-65536), F32)
    return lo.astype(BF16), hi.astype(BF16)


def _mod_kernel(c_ref, w_ref, b_ref, o_ref):
    s = _silu(c_ref[...]).astype(BF16)
    o_ref[...] = _dot(s, w_ref[...].astype(BF16)) + b_ref[...]


def _modulation(cond8, w_mod, b_mod, tn=1024):
    depth, d, n = w_mod.shape
    return pl.pallas_call(
        _mod_kernel,
        grid=(depth, n // tn),
        in_specs=[
            pl.BlockSpec((8, d), lambda l, j: (0, 0)),
            pl.BlockSpec((None, d, tn), lambda l, j: (l, 0, j)),
            pl.BlockSpec((None, 1, tn), lambda l, j: (l, 0, j)),
        ],
        out_specs=pl.BlockSpec((None, 8, tn), lambda l, j: (l, 0, j)),
        out_shape=jax.ShapeDtypeStruct((depth, 8, n), F32),
        compiler_params=_params("arbitrary", "arbitrary"),
    )(cond8, w_mod, b_mod.reshape(depth, 1, n))


def _in_kernel(x_ref, g_ref, sh_ref, sc_ref, w_ref, *rest, mode, kv_blocks):
    if mode == "even_kv":
        p_ref, k_ref, v_ref, h_ref = rest
    else:
        p_ref, h_ref = rest
    j = pl.program_id(1)

    @pl.when(j == 0)
    def _():
        h = (_rms_f32(x_ref[...]) * g_ref[...]) * (1.0 + sc_ref[...]) + sh_ref[...]
        h_ref[...] = h.astype(BF16)

    acc = _dot(h_ref[...], w_ref[...].astype(BF16))
    if mode == "odd":
        p_ref[...] = (0.5 * acc * (1.0 + lax.erf(acc * (2.0 ** -0.5)))).astype(BF16)
    else:
        p_ref[...] = acc.astype(BF16)
    if mode == "even_kv":
        k_lo, v_lo, nb = kv_blocks

        @pl.when((j >= k_lo) & (j < k_lo + nb))
        def _():
            k_ref[...] = acc

        @pl.when((j >= v_lo) & (j < v_lo + nb))
        def _():
            v_ref[...] = acc


def _in_proj(x, g, mod3, layer, which, row_fn, w, mode, tm, tn):
    t, d = x.shape
    n = w.shape[1]
    base = layer * 48

    def mod_spec(kind):
        return pl.BlockSpec((None, 1, d), lambda i, j: (base + row_fn(i) * 6 + kind, 0, 0))

    in_specs = [
        pl.BlockSpec((tm, d), lambda i, j: (i, 0)),
        pl.BlockSpec((1, d), lambda i, j: (0, 0)),
        mod_spec(which),
        mod_spec(which + 1),
        pl.BlockSpec((d, tn), lambda i, j: (0, j)),
    ]
    out_specs = [pl.BlockSpec((tm, tn), lambda i, j: (i, j))]
    out_shape = [jax.ShapeDtypeStruct((t, n), BF16)]
    kv_blocks = None
    if mode == "even_kv":
        a_qk = H_A * 2 * DK_A
        a_v = H_A * DV_A
        nb = a_qk // tn
        k_lo, v_lo = a_qk // tn, 2 * a_qk // tn
        kv_blocks = (k_lo, v_lo, nb)
        out_specs += [
            pl.BlockSpec((tm, tn), lambda i, j: (i, jnp.clip(j - k_lo, 0, nb - 1))),
            pl.BlockSpec((tm, tn), lambda i, j: (i, jnp.clip(j - v_lo, 0, nb - 1))),
        ]
        out_shape += [jax.ShapeDtypeStruct((t, a_qk), F32), jax.ShapeDtypeStruct((t, a_v), F32)]
    return pl.pallas_call(
        functools.partial(_in_kernel, mode=mode, kv_blocks=kv_blocks),
        grid=(t // tm, n // tn),
        in_specs=in_specs,
        out_specs=out_specs,
        out_shape=out_shape,
        scratch_shapes=[pltpu.VMEM((tm, d), BF16)],
        compiler_params=_params("arbitrary", "arbitrary"),
    )(x, g.reshape(1, d), mod3, mod3, w)


def _rope(x, c, s):
    lane = lax.broadcasted_iota(jnp.int32, x.shape, 1)
    partner = jnp.where((lane & 32) == 0, pltpu.roll(x, 96, 1), pltpu.roll(x, 32, 1))
    return x * c + partner * s


def _mixer_kernel(*refs, seq, has_ctx, lam_init, past):
    if has_ctx:
        (aq_ref, ak_ref, av_ref, bq_ref, bk_ref, bv_ref, bg_ref, lq_ref, lk_ref, dl_ref,
         ck_ref, cv_ref, s0_ref, cos_ref, sin_ref,
         oa_ref, or_ref, kall_ref, vall_ref, of_ref, st_ref) = refs
    else:
        (aq_ref, ak_ref, av_ref, bq_ref, bk_ref, bv_ref, bg_ref, lq_ref, lk_ref, dl_ref,
         oa_ref, or_ref, sout_ref, kall_ref, vall_ref, of_ref, st_ref) = refs
    lk_total = seq + past
    tq = 256
    rt = 256

    if has_ctx:
        for c in range(2):
            kall_ref[c, 0:past, :] = ck_ref[:, c * DK_A:(c + 1) * DK_A].astype(BF16)
        vall_ref[0:past, :] = cv_ref[...].astype(BF16)

        def copy_body(r, carry):
            r0 = pl.multiple_of(r * rt, rt)
            cs_, sn_ = cos_ref[pl.ds(r0, rt), :], sin_ref[pl.ds(r0, rt), :]
            kk = ak_ref[pl.ds(r0, rt), :].astype(F32)
            for c in range(2):
                kall_ref[c, pl.ds(past + r0, rt), :] = _rope(
                    kk[:, c * DK_A:(c + 1) * DK_A], cs_, sn_).astype(BF16)
            vall_ref[pl.ds(past + r0, rt), :] = av_ref[pl.ds(r0, rt), :]
            return carry

        lax.fori_loop(0, seq // rt, copy_body, 0)
    else:
        for c in range(2):
            kall_ref[c] = ak_ref[:, c * DK_A:(c + 1) * DK_A]
        vall_ref[...] = av_ref[...]

    prod = lq_ref[...] * lk_ref[...]
    ee = jnp.exp(jnp.sum(prod, axis=1, keepdims=True))
    lam = ee[0:1, :] - ee[1:2, :] + lam_init
    scale = DK_A ** -0.5

    def attn_body(qt, carry):
        r0 = pl.multiple_of(qt * tq, tq)
        q = aq_ref[pl.ds(r0, tq), :]
        if has_ctx:
            qf = q.astype(F32)
            cs_, sn_ = cos_ref[pl.ds(r0, tq), :], sin_ref[pl.ds(r0, tq), :]
        probs = []
        for c in range(2):
            if has_ctx:
                qc = _rope(qf[:, c * DK_A:(c + 1) * DK_A], cs_, sn_).astype(BF16)
            else:
                qc = q[:, c * DK_A:(c + 1) * DK_A]
            s = _dot_nt(qc, kall_ref[c]) * scale
            e = jnp.exp(s - jnp.max(s, axis=-1, keepdims=True))
            probs.append(e * (1.0 / jnp.sum(e, axis=-1, keepdims=True)))
        a = (probs[0] - lam * probs[1]).astype(BF16)
        o = _dot(a, vall_ref[...])
        oa_ref[pl.ds(r0, tq), :] = (_rms_f32(o) * (1.0 - lam_init)).astype(BF16)
        return carry

    lax.fori_loop(0, seq // tq, attn_body, 0)

    cs = RET_CHUNK
    nc = seq // cs
    ii = lax.broadcasted_iota(jnp.int32, (cs, cs), 0)
    jj = lax.broadcasted_iota(jnp.int32, (cs, cs), 1)
    ic = lax.broadcasted_iota(jnp.int32, (cs, 1), 0).astype(F32)
    for d in range(2):
        lg = jax.nn.log_sigmoid(dl_ref[d])
        diff = (ii - jj if d == 0 else jj - ii).astype(F32)
        mask = jnp.where(diff >= 0, jnp.exp(jnp.maximum(diff, 0.0) * lg), 0.0)
        if d == 0:
            qdec = jnp.exp((ic + 1.0) * lg)
            kdec = jnp.exp((cs - 1.0 - ic) * lg)
        else:
            qdec = jnp.exp((cs - ic) * lg)
            kdec = jnp.exp(ic * lg)
        cdec = jnp.exp(cs * lg)
        if has_ctx:
            st_ref[...] = s0_ref[d]
        else:
            st_ref[...] = jnp.zeros_like(st_ref)

        def chunk_body(n, carry, d=d, mask=mask, qdec=qdec, kdec=kdec, cdec=cdec):
            m = n if d == 0 else nc - 1 - n
            r0 = pl.multiple_of(m * cs, cs)
            qc = bq_ref[pl.ds(r0, cs), :].astype(F32)
            kc = bk_ref[pl.ds(r0, cs), :].astype(F32) * (DK_B ** -0.5)
            vc = bv_ref[pl.ds(r0, cs), :]
            state = st_ref[...]
            qk = _dot_nt(qc.astype(BF16), kc.astype(BF16)) * mask
            o = _dot(qk.astype(BF16), vc) + _dot((qc * qdec).astype(BF16), state.astype(BF16))
            kd = (kc * kdec).T.astype(BF16)
            st_ref[...] = cdec * state + _dot(kd, vc)
            if d == 0:
                of_ref[pl.ds(r0, cs), :] = o
            else:
                tot = of_ref[pl.ds(r0, cs), :] + o
                gate = _silu(bg_ref[pl.ds(r0, cs), :].astype(F32))
                or_ref[pl.ds(r0, cs), :] = (_rms_f32(tot) * gate).astype(BF16)
            return carry

        lax.fori_loop(0, nc, chunk_body, 0)
        if not has_ctx:
            sout_ref[d] = st_ref[...]


def _mixer(p, lam_q, lam_k, decay_logit, lam_init, batch, seq, ctx=None):
    has_ctx = ctx is not None
    past = ctx[0].shape[1] if has_ctx else 0

    def col(width, off):
        return pl.BlockSpec((seq, width), lambda b, h: (b, off // width + h))

    a_qk = H_A * 2 * DK_A
    a_v = H_A * DV_A
    b_qk = H_B * DK_B
    b_v = H_B * DV_B
    o_ak, o_av = a_qk, 2 * a_qk
    o_bq = 2 * a_qk + a_v
    o_bk, o_bv = o_bq + b_qk, o_bq + 2 * b_qk
    o_bg = o_bv + b_v
    in_specs = [
        col(2 * DK_A, 0), col(2 * DK_A, o_ak), col(DV_A, o_av),
        col(DK_B, o_bq), col(DK_B, o_bk), col(DV_B, o_bv), col(DV_B, o_bg),
        pl.BlockSpec((2, DK_A), lambda b, h: (0, 0)),
        pl.BlockSpec((2, DK_A), lambda b, h: (0, 0)),
        pl.BlockSpec((2, None, 1, 1), lambda b, h: (0, h, 0, 0)),
    ]
    args = [p] * 7 + [lam_q, lam_k, decay_logit.reshape(2, H_B, 1, 1)]
    out_specs = [
        pl.BlockSpec((seq, DV_A), lambda b, h: (b, h)),
        pl.BlockSpec((seq, DV_B), lambda b, h: (b, h)),
    ]
    out_shape = [
        jax.ShapeDtypeStruct((batch * seq, a_v), BF16),
        jax.ShapeDtypeStruct((batch * seq, b_v), BF16),
    ]
    if has_ctx:
        ck, cv, s0, cos_t, sin_t = ctx
        in_specs += [
            pl.BlockSpec((None, past, 2 * DK_A), lambda b, h: (b, 0, h)),
            pl.BlockSpec((None, past, DV_A), lambda b, h: (b, 0, h)),
            pl.BlockSpec((None, 2, None, DK_B, DV_B), lambda b, h: (b, 0, h, 0, 0)),
            pl.BlockSpec((seq, DK_A), lambda b, h: (0, 0)),
            pl.BlockSpec((seq, DK_A), lambda b, h: (0, 0)),
        ]
        args += [ck, cv, s0, cos_t, sin_t]
    else:
        out_specs.append(pl.BlockSpec((None, 2, None, DK_B, DV_B), lambda b, h: (b, 0, h, 0, 0)))
        out_shape.append(jax.ShapeDtypeStruct((batch, 2, H_B, DK_B, DV_B), F32))
    return pl.pallas_call(
        functools.partial(_mixer_kernel, seq=seq, has_ctx=has_ctx, lam_init=lam_init, past=past),
        grid=(batch, H_A),
        in_specs=in_specs,
        out_specs=out_specs,
        out_shape=out_shape,
        scratch_shapes=[
            pltpu.VMEM((2, seq + past, DK_A), BF16),
            pltpu.VMEM((seq + past, DV_A), BF16),
            pltpu.VMEM((seq, DV_B), F32),
            pltpu.VMEM((DK_B, DV_B), F32),
        ],
        compiler_params=_params("arbitrary", "arbitrary"),
    )(*args)


def _out_kernel(*refs, n_lhs, nk):
    lhs_refs = refs[:n_lhs]
    (w_ref, x_ref, gate_ref, g_ref, sh_ref, sc_ref, rw_ref,
     x1_ref, h2_ref, lg_ref, acc_ref) = refs[n_lhs:]
    k = pl.program_id(1)

    @pl.when(k == 0)
    def _():
        acc_ref[...] = jnp.zeros_like(acc_ref)

    wb = w_ref[...].astype(BF16)
    if n_lhs == 1:
        acc_ref[...] += _dot(lhs_refs[0][...], wb)
    else:
        kb = nk // n_lhs
        for idx in range(n_lhs):
            @pl.when((k >= idx * kb) & (k < (idx + 1) * kb))
            def _(idx=idx):
                acc_ref[...] += _dot(lhs_refs[idx][...], wb)

    @pl.when(k == nk - 1)
    def _():
        x1 = x_ref[...] + gate_ref[...] * acc_ref[...]
        x1_ref[...] = x1
        h2 = (_rms_f32(x1) * g_ref[...]) * (1.0 + sc_ref[...]) + sh_ref[...]
        _store_token_major(h2_ref, _pack_bf16_pairs(h2))
        lg_ref[...] = lax.dot_general(rw_ref[...], h2, (((1,), (1,)), ((), ())),
                                      precision=lax.Precision.HIGHEST,
                                      preferred_element_type=F32)


def _out_proj(lhs_list, w, x, g, mod3, layer, row_fn, router_wt, t_total, tile_off, prev, tm, tk):
    n_lhs = len(lhs_list)
    t_seg, d = x.shape
    kdim = w.shape[0]
    nk = kdim // tk
    base = layer * 48
    ne = router_wt.shape[0]

    def mod_spec(kind):
        return pl.BlockSpec((None, 1, d), lambda i, k: (base + row_fn(i) * 6 + kind, 0, 0))

    kb = nk // n_lhs
    lhs_specs = [
        pl.BlockSpec((tm, tk), lambda i, k, idx=idx: (i, jnp.clip(k - idx * kb, 0, kb - 1)))
        for idx in range(n_lhs)
    ]
    in_specs = lhs_specs + [
        pl.BlockSpec((tk, d), lambda i, k: (k, 0)),
        pl.BlockSpec((tm, d), lambda i, k: (i, 0)),
        mod_spec(2),
        pl.BlockSpec((1, d), lambda i, k: (0, 0)),
        mod_spec(3),
        mod_spec(4),
        pl.BlockSpec((ne, d), lambda i, k: (0, 0)),
    ]
    args = list(lhs_list) + [w, x, mod3, g.reshape(1, d), mod3, mod3, router_wt]
    rpt = d // (2 * LANES)
    out_specs = [
        pl.BlockSpec((tm, d), lambda i, k: (i + tile_off, 0)),
        pl.BlockSpec((tm * rpt, LANES), lambda i, k: (i + tile_off, 0)),
        pl.BlockSpec((ne, tm), lambda i, k: (0, i + tile_off)),
    ]
    out_shape = [
        jax.ShapeDtypeStruct((t_total, d), F32),
        jax.ShapeDtypeStruct((t_total * rpt, LANES), jnp.int32),
        jax.ShapeDtypeStruct((ne, t_total), F32),
    ]
    aliases = {}
    if prev is not None:
        n_in = len(args)
        in_specs += [pl.BlockSpec(memory_space=pl.ANY)] * 3
        args += list(prev)
        aliases = {n_in: 0, n_in + 1: 1, n_in + 2: 2}
    kern = functools.partial(_out_kernel, n_lhs=n_lhs, nk=nk)
    if prev is not None:
        inner = kern

        def kern(*refs):
            n_in_refs = n_lhs + 7
            return inner(*refs[:n_in_refs], *refs[n_in_refs + 3:])

    return pl.pallas_call(
        kern,
        grid=(t_seg // tm, nk),
        in_specs=in_specs,
        out_specs=out_specs,
        out_shape=out_shape,
        scratch_shapes=[pltpu.VMEM((tm, d), F32)],
        input_output_aliases=aliases,
        compiler_params=_params("arbitrary", "arbitrary"),
    )(*args)


def _first_index(vals, target):
    idx = jnp.full(target.shape, len(vals) - 1, jnp.int32)
    for k in range(len(vals) - 2, -1, -1):
        idx = jnp.where(vals[k] == target, k, idx)
    return idx


def _pick(vals, idx):
    out = vals[-1]
    for k in range(len(vals) - 2, -1, -1):
        out = jnp.where(idx == k, vals[k], out)
    return out


def _route_kernel(lg_ref, rb_ref, pos_ref, wt_ref, meta_ref, rank_ref, *, tile_rows, scan):
    logits = lg_ref[...]
    t = logits.shape[1]
    scores = jax.nn.sigmoid(logits)
    biased = scores + rb_ref[...]
    s_rows = [scores[e:e + 1, :] for e in range(N_EXPERTS)]
    b_rows = [biased[e:e + 1, :] for e in range(N_EXPERTS)]
    grp = []
    for g in range(N_GROUPS):
        r = b_rows[g * GROUP_SIZE:(g + 1) * GROUP_SIZE]
        best = None
        for a in range(GROUP_SIZE):
            for b in range(a + 1, GROUP_SIZE):
                pair = r[a] + r[b]
                best = pair if best is None else jnp.maximum(best, pair)
        grp.append(best)
    gmax = functools.reduce(jnp.maximum, grp)
    gsel = _first_index(grp, gmax)
    cand_b = [_pick([b_rows[g * GROUP_SIZE + k] for g in range(N_GROUPS)], gsel) for k in range(GROUP_SIZE)]
    cand_s = [_pick([s_rows[g * GROUP_SIZE + k] for g in range(N_GROUPS)], gsel) for k in range(GROUP_SIZE)]
    m1 = functools.reduce(jnp.maximum, cand_b)
    k1 = _first_index(cand_b, m1)
    rest = [jnp.where(k1 == k, -jnp.inf, cand_b[k]) for k in range(GROUP_SIZE)]
    m2 = functools.reduce(jnp.maximum, rest)
    k2 = _first_index(rest, m2)
    w1 = _pick(cand_s, k1)
    w2 = _pick(cand_s, k2)
    wsum = w1 + w2
    w1 = w1 / wsum
    w2 = w2 / wsum
    e1 = gsel * GROUP_SIZE + k1
    e2 = gsel * GROUP_SIZE + k2
    wt_ref[...] = jnp.zeros_like(wt_ref)
    wt_ref[0:1, :] = w1
    wt_ref[1:2, :] = w2

    eid = lax.broadcasted_iota(jnp.int32, logits.shape, 0)
    sel1 = eid == e1
    sel2 = eid == e2
    ind = jnp.where(sel1 | sel2, 1.0, 0.0)
    ua = lax.broadcasted_iota(jnp.int32, (scan, scan), 0)
    ub = lax.broadcasted_iota(jnp.int32, (scan, scan), 1)
    upper = jnp.where(ua <= ub, 1.0, 0.0).astype(BF16)
    carry = jnp.zeros((N_EXPERTS, 1), F32)
    for blk in range(t // scan):
        seg = ind[:, blk * scan:(blk + 1) * scan]
        incl = _dot(seg.astype(BF16), upper)
        rank_ref[:, blk * scan:(blk + 1) * scan] = incl - seg + carry
        carry = carry + incl[:, scan - 1:scan]
    n_tiles = jnp.floor((carry + (tile_rows - 1.0)) * (1.0 / tile_rows))
    la = lax.broadcasted_iota(jnp.int32, (N_EXPERTS, N_EXPERTS), 0)
    lb = lax.broadcasted_iota(jnp.int32, (N_EXPERTS, N_EXPERTS), 1)
    lower = jnp.where(lb < la, 1.0, 0.0).astype(BF16)
    start = _dot(lower, jnp.broadcast_to(n_tiles, (N_EXPERTS, LANES)).astype(BF16))
    end = start + n_tiles
    slot = start[:, 0:1] * tile_rows + rank_ref[...]
    pos_ref[...] = jnp.zeros_like(pos_ref)
    pos_ref[0:1, :] = jnp.sum(jnp.where(sel1, slot, 0.0), axis=0, keepdims=True).astype(jnp.int32)
    pos_ref[1:2, :] = jnp.sum(jnp.where(sel2, slot, 0.0), axis=0, keepdims=True).astype(jnp.int32)

    n_used = end[N_EXPERTS - 1:N_EXPERTS, :]
    tile = lax.broadcasted_iota(jnp.int32, (N_EXPERTS, LANES), 1).astype(F32)
    tile = jnp.minimum(tile, n_used - 1.0)
    tile_expert = jnp.sum(jnp.where(tile >= end, 1.0, 0.0), axis=0, keepdims=True)
    lane = lax.broadcasted_iota(jnp.int32, (1, LANES), 1)
    meta_ref[...] = jnp.zeros_like(meta_ref)
    meta_ref[0:1, :] = jnp.where(lane == LANES - 1, n_used, tile_expert).astype(jnp.int32)
    diag = lax.broadcasted_iota(jnp.int32, (N_EXPERTS, LANES), 0) == lax.broadcasted_iota(
        jnp.int32, (N_EXPERTS, LANES), 1)
    meta_ref[1:2, :] = jnp.sum(jnp.where(diag, carry, 0.0), axis=0, keepdims=True).astype(jnp.int32)
    meta_ref[2:3, :] = jnp.sum(jnp.where(diag, start * tile_rows, 0.0), axis=0, keepdims=True).astype(jnp.int32)


def _route(logits_t, router_b, tile_rows, scan=512):
    ne, t = logits_t.shape
    return pl.pallas_call(
        functools.partial(_route_kernel, tile_rows=tile_rows, scan=scan),
        out_shape=[
            jax.ShapeDtypeStruct((8, t), jnp.int32),
            jax.ShapeDtypeStruct((8, t), F32),
            jax.ShapeDtypeStruct((8, LANES), jnp.int32),
        ],
        scratch_shapes=[pltpu.VMEM((ne, t), F32)],
        compiler_params=pltpu.CompilerParams(vmem_limit_bytes=VMEM_LIMIT_BYTES),
    )(logits_t, router_b.reshape(ne, 1))


def _dispatch_kernel(pos_ref, meta_ref, h_ref, xs_ref, zero_ref, sem, *, t, rpt, chunk, tile_rows, n_tiles):
    c = pl.program_id(0)

    def slot_rows(slot):
        return xs_ref.at[pl.ds(pl.multiple_of(slot * rpt, rpt), rpt)]

    def body(r, carry):
        tok = c * chunk + r
        src = h_ref.at[pl.ds(pl.multiple_of(r * rpt, rpt), rpt)]
        for k in range(2):
            pltpu.make_async_copy(src, slot_rows(pos_ref[k * t + tok]), sem.at[0]).start()
        return carry

    lax.fori_loop(0, chunk, body, 0, unroll=8)

    @pl.when(c == 0)
    def _():
        zero_ref[...] = jnp.zeros_like(zero_ref)
        zero_row = zero_ref.at[pl.ds(0, rpt)]
        for e in range(N_EXPERTS):
            count = meta_ref[LANES + e]
            first = meta_ref[2 * LANES + e] + count
            n_pad = (tile_rows - count % tile_rows) % tile_rows

            def pad_start(r, carry, first=first):
                pltpu.make_async_copy(zero_row, slot_rows(first + r), sem.at[1]).start()
                return carry

            def pad_wait(r, carry, first=first):
                pltpu.make_async_copy(zero_row, slot_rows(first + r), sem.at[1]).wait()
                return carry

            lax.fori_loop(0, n_pad, pad_start, 0)
            lax.fori_loop(0, n_pad, pad_wait, 0)

        def tile_copy(n):
            rows = tile_rows * rpt
            return pltpu.make_async_copy(zero_ref, xs_ref.at[pl.ds(pl.multiple_of(n * rows, rows), rows)],
                                         sem.at[1])

        def unused_start(n, carry):
            tile_copy(n).start()
            return carry

        def unused_wait(n, carry):
            tile_copy(n).wait()
            return carry

        lax.fori_loop(meta_ref[LANES - 1], n_tiles, unused_start, 0)
        lax.fori_loop(meta_ref[LANES - 1], n_tiles, unused_wait, 0)

    for _ in range(2):
        pltpu.make_async_copy(h_ref, xs_ref.at[pl.ds(0, chunk * rpt)], sem.at[0]).wait()


def _dispatch(pos_flat, meta_flat, h_lin, n_slots, rpt, tile_rows, chunk=512):
    t = pos_flat.shape[0] // 2
    return pl.pallas_call(
        functools.partial(_dispatch_kernel, t=t, rpt=rpt, chunk=chunk, tile_rows=tile_rows,
                          n_tiles=n_slots // tile_rows),
        grid_spec=pltpu.PrefetchScalarGridSpec(
            num_scalar_prefetch=2,
            grid=(t // chunk,),
            in_specs=[pl.BlockSpec((chunk * rpt, LANES), lambda c, p, m: (c, 0))],
            out_specs=pl.BlockSpec(memory_space=pl.ANY),
            scratch_shapes=[pltpu.VMEM((tile_rows * rpt, LANES), h_lin.dtype), pltpu.SemaphoreType.DMA((2,))],
        ),
        out_shape=jax.ShapeDtypeStruct((n_slots * rpt, LANES), h_lin.dtype),
        compiler_params=_params("arbitrary"),
    )(pos_flat, meta_flat, h_lin)


def _ffn_kernel(meta_ref, xs_ref, wg_ref, wu_ref, wd_ref, ys_ref, xb_ref, wgb_ref, wub_ref, wdb_ref, *, tm):
    n = pl.program_id(0)
    n_used = meta_ref[LANES - 1]

    @pl.when(n < n_used)
    def _():
        @pl.when((n == 0) | (meta_ref[n] != meta_ref[jnp.maximum(n - 1, 0)]))
        def _():
            wgb_ref[...] = wg_ref[...].astype(BF16)
            wub_ref[...] = wu_ref[...].astype(BF16)
            wdb_ref[...] = wd_ref[...].astype(BF16)

        half = xb_ref.shape[1] // 2
        for c in range(half // LANES):
            lo, hi = _unpack_bf16_pairs(_load_token_major(xs_ref, tm, c))
            xb_ref[:, c * LANES:(c + 1) * LANES] = lo
            xb_ref[:, half + c * LANES:half + (c + 1) * LANES] = hi
        x = xb_ref[...]
        a = _dot(x, wgb_ref[...])
        b = _dot(x, wub_ref[...])
        act = (_silu(a) * b).astype(BF16)
        _store_token_major(ys_ref, _dot(act, wdb_ref[...]))

    @pl.when(n >= n_used)
    def _():
        ys_ref[...] = jnp.zeros_like(ys_ref)


def _ffn(meta, xs, w_gate, w_up, w_down, layer, tm, n_tiles):
    _, ne, d, f = w_gate.shape
    rpt = d // LANES
    rpt_in = xs.shape[0] // (n_tiles * tm)

    def row_tile(n, m):
        return (jnp.minimum(n, m[LANES - 1] - 1), 0)

    return pl.pallas_call(
        functools.partial(_ffn_kernel, tm=tm),
        grid_spec=pltpu.PrefetchScalarGridSpec(
            num_scalar_prefetch=1,
            grid=(n_tiles,),
            in_specs=[
                pl.BlockSpec((tm * rpt_in, LANES), row_tile),
                pl.BlockSpec((None, None, d, f), lambda n, m: (layer, m[n], 0, 0)),
                pl.BlockSpec((None, None, d, f), lambda n, m: (layer, m[n], 0, 0)),
                pl.BlockSpec((None, None, f, d), lambda n, m: (layer, m[n], 0, 0)),
            ],
            out_specs=pl.BlockSpec((tm * rpt, LANES), lambda n, m: (n, 0)),
            scratch_shapes=[
                pltpu.VMEM((tm, d), BF16),
                pltpu.VMEM((d, f), BF16),
                pltpu.VMEM((d, f), BF16),
                pltpu.VMEM((f, d), BF16),
            ],
        ),
        out_shape=jax.ShapeDtypeStruct((n_tiles * tm * rpt, LANES), F32),
        compiler_params=_params("arbitrary"),
    )(meta, xs, w_gate, w_up, w_down)


def _combine_kernel(pos_ref, ys_ref, x_ref, wt_ref, gate_ref, o_ref, a0, b0, a1, b1, sem, *, t, rpt, tm):
    i = pl.program_id(0)
    n = pl.num_programs(0)
    bufs = ((a0, b0), (a1, b1))

    def issue(tile, s):
        def body(r, carry):
            tok = tile * tm + r
            for k in range(2):
                slot_row = pos_ref[k * t + tok]
                pltpu.make_async_copy(
                    ys_ref.at[pl.ds(pl.multiple_of(slot_row * rpt, rpt), rpt)],
                    bufs[s][k].at[pl.ds(pl.multiple_of(r * rpt, rpt), rpt)],
                    sem.at[s]).start()
            return carry

        lax.fori_loop(0, tm, body, 0, unroll=8)

    def wait(s):
        for k in range(2):
            pltpu.make_async_copy(ys_ref.at[pl.ds(0, tm * rpt)], bufs[s][k], sem.at[s]).wait()

    def compute(s):
        w = wt_ref[...]
        w0, w1 = w[:, 0:1], w[:, 1:2]
        for c in range(rpt):
            cols = slice(c * LANES, (c + 1) * LANES)
            y = w0 * _load_token_major(bufs[s][0], tm, c) + w1 * _load_token_major(bufs[s][1], tm, c)
            o_ref[:, cols] = x_ref[:, cols] + gate_ref[:, cols] * y

    @pl.when(i == 0)
    def _():
        issue(0, 0)

    for s in range(2):
        @pl.when(i % 2 == s)
        def _(s=s):
            @pl.when(i + 1 < n)
            def _():
                issue(i + 1, 1 - s)

            wait(s)
            compute(s)


def _combine(pos_flat, ys, x1, wt, mod3, layer, row_fn, tm=256):
    t, d = x1.shape
    rpt = d // LANES
    base = layer * 48
    return pl.pallas_call(
        functools.partial(_combine_kernel, t=t, rpt=rpt, tm=tm),
        grid_spec=pltpu.PrefetchScalarGridSpec(
            num_scalar_prefetch=1,
            grid=(t // tm,),
            in_specs=[
                pl.BlockSpec(memory_space=pl.ANY),
                pl.BlockSpec((tm, d), lambda i, p: (i, 0)),
                pl.BlockSpec((tm, 2), lambda i, p: (i, 0)),
                pl.BlockSpec((None, 1, d), lambda i, p: (base + row_fn(i) * 6 + 5, 0, 0)),
            ],
            out_specs=pl.BlockSpec((tm, d), lambda i, p: (i, 0)),
            scratch_shapes=[pltpu.VMEM((tm * rpt, LANES), F32)] * 4 + [pltpu.SemaphoreType.DMA((2,))],
        ),
        out_shape=jax.ShapeDtypeStruct((t, d), F32),
        compiler_params=_params("arbitrary"),
    )(pos_flat, ys, x1, wt, mod3)


def _moe(lg, router_b, h_lin, x1, w_gate, w_up, w_down, mod3, layer, row_fn, tile_rows=256):
    t, d = x1.shape
    n_tiles = (2 * t) // tile_rows + N_EXPERTS
    pos, wts, meta = _route(lg, router_b, tile_rows)
    pos_flat = pos[:2].reshape(-1)
    xs = _dispatch(pos_flat, meta[:3].reshape(-1), h_lin, n_tiles * tile_rows, h_lin.shape[0] // t, tile_rows)
    ys = _ffn(meta[0], xs, w_gate, w_up, w_down, layer, tile_rows, n_tiles)
    return _combine(pos_flat, ys, x1, wts[:2].T, mod3, layer, row_fn)


def _final_kernel(x_ref, g_ref, o_ref):
    o_ref[...] = _rms_f32(x_ref[...]) * g_ref[...]


def _final_norm(x, g, row0, rows, tm=512):
    d = x.shape[1]
    off = row0 // tm
    return pl.pallas_call(
        _final_kernel,
        grid=(rows // tm,),
        in_specs=[
            pl.BlockSpec((tm, d), lambda i: (i + off, 0)),
            pl.BlockSpec((1, d), lambda i: (0, 0)),
        ],
        out_specs=pl.BlockSpec((tm, d), lambda i: (i, 0)),
        out_shape=jax.ShapeDtypeStruct((rows, d), F32),
        compiler_params=_params("arbitrary"),
    )(x, g.reshape(1, d))


def _sgu_kernel(u_ref, v_ref, ng_ref, ws_ref, bs_ref, o_ref, *, n_chunks):
    e_c = v_ref.shape[1]
    gw = e_c // H_C
    vn = (_rms_f32(v_ref[...].astype(F32)) * ng_ref[...]).astype(BF16)
    for n in range(n_chunks):
        r0 = n * CHUNK_C
        for h in range(H_C):
            c0 = h * gw
            mixed = _dot(ws_ref[h].astype(BF16), vn[r0:r0 + CHUNK_C, c0:c0 + gw]) + bs_ref[h]
            u = u_ref[r0:r0 + CHUNK_C, c0:c0 + gw].astype(F32)
            o_ref[r0:r0 + CHUNK_C, c0:c0 + gw] = (u * mixed).astype(BF16)


def _sgu(z, norm_g, w_s, b_s, tm=256):
    t, e2 = z.shape
    e_c = e2 // 2
    return pl.pallas_call(
        functools.partial(_sgu_kernel, n_chunks=tm // CHUNK_C),
        grid=(t // tm,),
        in_specs=[
            pl.BlockSpec((tm, e_c), lambda i: (i, 0)),
            pl.BlockSpec((tm, e_c), lambda i: (i, 1)),
            pl.BlockSpec((1, e_c), lambda i: (0, 0)),
            pl.BlockSpec((H_C, CHUNK_C, CHUNK_C), lambda i: (0, 0, 0)),
            pl.BlockSpec((H_C, CHUNK_C, 1), lambda i: (0, 0, 0)),
        ],
        out_specs=pl.BlockSpec((tm, e_c), lambda i: (i, 0)),
        out_shape=jax.ShapeDtypeStruct((t, e_c), BF16),
        compiler_params=_params("arbitrary"),
    )(z, z, norm_g.reshape(1, e_c), w_s, b_s.reshape(H_C, CHUNK_C, 1))


def _rope_tables(n_tok):
    rows = n_tok // GRID_W
    pos_r = jnp.repeat(jnp.arange(rows), GRID_W).astype(F32)
    pos_c = jnp.tile(jnp.arange(GRID_W), rows).astype(F32)
    n_freq = DK_A // 4
    inv = jnp.power(ROPE_BASE, -jnp.arange(n_freq, dtype=F32) / n_freq)
    ang_r = pos_r[:, None] * inv
    ang_c = pos_c[:, None] * inv
    cos_t = jnp.concatenate([jnp.cos(ang_r), jnp.cos(ang_r), jnp.cos(ang_c), jnp.cos(ang_c)], axis=1)
    sin_t = jnp.concatenate([-jnp.sin(ang_r), jnp.sin(ang_r), -jnp.sin(ang_c), jnp.sin(ang_c)], axis=1)
    return cos_t, sin_t


def kernel(x_prompt, x_sample, cache_k, cache_v, state_ret, c, c_ctx, norm_g, w_mod, b_mod, w_in_even, w_out_even, lam_q, lam_k, ret_decay_logit, w_in_odd, gmlp_norm_g, w_spatial, b_spatial, w_out_odd, router_w, router_b, w_gate, w_up, w_down, final_norm_g):
    batch, seq, d = x_prompt.shape
    dbatch, dseq, _ = x_sample.shape
    past = cache_k.shape[2]
    depth = norm_g.shape[0]
    t_ctx, t_lat = batch * seq, dbatch * dseq
    t_all = t_ctx + t_lat
    xc = x_prompt.reshape(t_ctx, d)
    xl = x_sample.reshape(t_lat, d)

    cond8 = jnp.concatenate([c_ctx[None, :], c, jnp.zeros((8 - 1 - dbatch, d), F32)], axis=0)
    mod3 = _modulation(cond8, w_mod, b_mod).reshape(depth * 8 * 6, 1, d)
    router_wt = router_w.T
    cos_t, sin_t = _rope_tables(dseq)

    tm_in, tn_in = 1024, 512
    tm_out, tk_out = 512, 512
    tm_cmb = 256
    w_in0 = w_in_even.reshape(d, -1)
    w_out0 = w_out_even.reshape(-1, d)
    w_in1 = w_in_odd.reshape(d, -1)
    w_out1 = w_out_odd.reshape(-1, d)
    lq, lk, dlog = lam_q.reshape(2, DK_A), lam_k.reshape(2, DK_A), ret_decay_logit.reshape(2, H_B)

    def rows_ctx(i):
        return 0

    def rows_lat(tiles_per_batch):
        return lambda i: 1 + i // tiles_per_batch

    def rows_all(tm):
        n_ctx, per = t_ctx // tm, dseq // tm
        return lambda i: jnp.maximum(i - n_ctx, 0) // per + jnp.where(i >= n_ctx, 1, 0)

    lam_init = 0.8 - 0.6 * math.exp(-0.3 * 0)
    p_ctx, k_new, v_new = _in_proj(xc, norm_g[0, 0], mod3, 0, 0, rows_ctx, w_in0, "even_kv", tm_in, tn_in)
    (p_lat,) = _in_proj(xl, norm_g[0, 0], mod3, 0, 0, rows_lat(dseq // tm_in), w_in0, "even", tm_in, tn_in)
    oa_c, or_c, s_new = _mixer(p_ctx, lq, lk, dlog, lam_init, batch, seq)
    ctx = (cache_k.reshape(dbatch, past, H_A * 2 * DK_A), cache_v.reshape(dbatch, past, H_A * DV_A),
           state_ret.reshape(dbatch, 2, H_B, DK_B, DV_B), cos_t, sin_t)
    oa_l, or_l = _mixer(p_lat, lq, lk, dlog, lam_init, dbatch, dseq, ctx)
    seg = _out_proj([oa_c, or_c], w_out0, xc, norm_g[0, 1], mod3, 0, rows_ctx, router_wt,
                    t_all, 0, None, tm_out, tk_out)
    x1, h2, lg = _out_proj([oa_l, or_l], w_out0, xl, norm_g[0, 1], mod3, 0, rows_lat(dseq // tm_out),
                           router_wt, t_all, t_ctx // tm_out, seg, tm_out, tk_out)
    x2 = _moe(lg, router_b, h2, x1, w_gate, w_up, w_down, mod3, 0, rows_all(tm_cmb))

    (z,) = _in_proj(x2, norm_g[1, 0], mod3, 1, 0, rows_all(tm_in), w_in1, "odd", tm_in, tn_in)
    gated = _sgu(z, gmlp_norm_g.reshape(-1), w_spatial.reshape(H_C, CHUNK_C, CHUNK_C), b_spatial.reshape(H_C, CHUNK_C))
    x3, h4, lg2 = _out_proj([gated], w_out1, x2, norm_g[1, 1], mod3, 1, rows_all(tm_out), router_wt,
                            t_all, 0, None, tm_out, tk_out)
    x4 = _moe(lg2, router_b, h4, x3, w_gate, w_up, w_down, mod3, 1, rows_all(tm_cmb))
    y_p = _final_norm(x4, final_norm_g, 0, t_ctx)
    y_s = _final_norm(x4, final_norm_g, t_ctx, t_lat)

    y_prompt = y_p.reshape(batch, seq, d)
    y_sample = y_s.reshape(dbatch, dseq, d)
    new_cache_k = k_new.reshape(batch, 1, seq, H_A, 2, DK_A)
    new_cache_v = v_new.reshape(batch, 1, seq, H_A, DV_A)
    new_state_ret = s_new.reshape(batch, 1, 2, H_B, DK_B, DV_B)
    return (y_prompt, y_sample, new_cache_k, new_cache_v, new_state_ret)
```

```python
import functools
import math

import jax
import jax.numpy as jnp
from jax import lax
from jax.experimental import pallas as pl
from jax.experimental.pallas import tpu as pltpu

F32 = jnp.float32
BF16 = jnp.bfloat16

EPS = 1e-6
ROPE_BASE = 10000.0
GRID_W = 64
H_A, DK_A, DV_A = 4, 128, 256
H_B, DK_B, DV_B = 4, 128, 256
RET_CHUNK = 128
H_C, CHUNK_C = 8, 128
N_EXPERTS, N_GROUPS, GROUP_SIZE = 16, 4, 4

VMEM_LIMIT_BYTES = 56 * 1024 * 1024


def _params(*sem):
    return pltpu.CompilerParams(dimension_semantics=sem, vmem_limit_bytes=VMEM_LIMIT_BYTES)


def _rms_f32(x):
    return x * lax.rsqrt(jnp.mean(x * x, axis=-1, keepdims=True) + EPS)


def _silu(x):
    return x * jax.nn.sigmoid(x)


def _dot(a, b):
    return jnp.dot(a, b, preferred_element_type=F32)


def _dot_nt(a, b):
    return lax.dot_general(a, b, (((1,), (1,)), ((), ())), preferred_element_type=F32)


LANES = 128


def _store_token_major(ref, val):
    tm, d = val.shape
    rpt = d // LANES
    for c in range(rpt):
        ref[pl.ds(c, tm, stride=rpt), :] = val[:, c * LANES:(c + 1) * LANES]


def _load_token_major(ref, tm, c):
    rpt = ref.shape[0] // tm
    return ref[pl.ds(c, tm, stride=rpt), :]


def _mod_kernel(c_ref, w_ref, b_ref, o_ref):
    s = _silu(c_ref[...]).astype(BF16)
    o_ref[...] = _dot(s, w_ref[...].astype(BF16)) + b_ref[...]


def _modulation(cond8, w_mod, b_mod, tn=1024):
    depth, d, n = w_mod.shape
    return pl.pallas_call(
        _mod_kernel,
        grid=(depth, n // tn),
        in_specs=[
            pl.BlockSpec((8, d), lambda l, j: (0, 0)),
            pl.BlockSpec((None, d, tn), lambda l, j: (l, 0, j)),
            pl.BlockSpec((None, 1, tn), lambda l, j: (l, 0, j)),
        ],
        out_specs=pl.BlockSpec((None, 8, tn), lambda l, j: (l, 0, j)),
        out_shape=jax.ShapeDtypeStruct((depth, 8, n), F32),
        compiler_params=_params("arbitrary", "arbitrary"),
    )(cond8, w_mod, b_mod.reshape(depth, 1, n))


def _load_weight_slab(w_hbm, slab_ref, stage_ref, sem, col0):
    kdim, n = slab_ref.shape
    rows = stage_ref.shape[1]
    nb = kdim // rows

    def copy(j):
        return pltpu.make_async_copy(w_hbm.at[pl.ds(j * rows, rows), pl.ds(col0, n)],
                                     stage_ref.at[j % 2], sem.at[j % 2])

    copy(0).start()
    for j in range(nb):
        if j + 1 < nb:
            copy(j + 1).start()
        copy(j).wait()
        slab_ref[j * rows:(j + 1) * rows, :] = stage_ref[j % 2].astype(BF16)


def _in_kernel(x_ref, g_ref, sh_ref, sc_ref, w_hbm, *rest, col0, mode, kv_cols, chunk):
    if mode == "even_kv":
        p_ref, k_ref, v_ref, slab_ref, stage_ref, h_ref, sem = rest
    else:
        p_ref, slab_ref, stage_ref, h_ref, sem = rest

    @pl.when(pl.program_id(0) == 0)
    def _():
        _load_weight_slab(w_hbm, slab_ref, stage_ref, sem, col0)

    h = (_rms_f32(x_ref[...]) * g_ref[...]) * (1.0 + sc_ref[...]) + sh_ref[...]
    h_ref[...] = h.astype(BF16)
    n = slab_ref.shape[1]
    for c0 in range(0, n, chunk):
        acc = _dot(h_ref[...], slab_ref[:, c0:c0 + chunk])
        if mode == "odd":
            p_ref[:, c0:c0 + chunk] = (0.5 * acc * (1.0 + lax.erf(acc * (2.0 ** -0.5)))).astype(BF16)
        else:
            p_ref[:, c0:c0 + chunk] = acc.astype(BF16)
        if mode == "even_kv":
            (k0, k1), (v0, v1) = kv_cols
            a0 = col0 + c0
            if k0 <= a0 < k1:
                k_ref[:, a0 - k0:a0 - k0 + chunk] = acc
            if v0 <= a0 < v1:
                v_ref[:, a0 - v0:a0 - v0 + chunk] = acc


def _in_proj(x, g, mod3, layer, row_fn, w, col0, n_cols, mode, tm=512, chunk=512, stage_rows=256):
    t, d = x.shape
    base = layer * 48

    def mod_spec(kind):
        return pl.BlockSpec((None, 1, d), lambda i: (base + row_fn(i) * 6 + kind, 0, 0))

    in_specs = [
        pl.BlockSpec((tm, d), lambda i: (i, 0)),
        pl.BlockSpec((1, d), lambda i: (0, 0)),
        mod_spec(0),
        mod_spec(1),
        pl.BlockSpec(memory_space=pl.ANY),
    ]
    out_specs = [pl.BlockSpec((tm, n_cols), lambda i: (i, 0))]
    out_shape = [jax.ShapeDtypeStruct((t, n_cols), BF16)]
    kv_cols = None
    if mode == "even_kv":
        a_qk = H_A * 2 * DK_A
        a_v = H_A * DV_A
        kv_cols = ((a_qk, 2 * a_qk), (2 * a_qk, 2 * a_qk + a_v))
        out_specs += [pl.BlockSpec((tm, a_qk), lambda i: (i, 0)), pl.BlockSpec((tm, a_v), lambda i: (i, 0))]
        out_shape += [jax.ShapeDtypeStruct((t, a_qk), F32), jax.ShapeDtypeStruct((t, a_v), F32)]
    return pl.pallas_call(
        functools.partial(_in_kernel, col0=col0, mode=mode, kv_cols=kv_cols, chunk=chunk),
        grid=(t // tm,),
        in_specs=in_specs,
        out_specs=out_specs,
        out_shape=out_shape,
        scratch_shapes=[
            pltpu.VMEM((d, n_cols), BF16),
            pltpu.VMEM((2, stage_rows, n_cols), F32),
            pltpu.VMEM((tm, d), BF16),
            pltpu.SemaphoreType.DMA((2,)),
        ],
        compiler_params=_params("arbitrary"),
    )(x, g.reshape(1, d), mod3, mod3, w)


def _rope(x, c, s):
    lane = lax.broadcasted_iota(jnp.int32, x.shape, 1)
    partner = jnp.where((lane & 32) == 0, pltpu.roll(x, 96, 1), pltpu.roll(x, 32, 1))
    return x * c + partner * s


def _mixer_kernel(*refs, seq, has_ctx, lam_init, past, hps):
    if has_ctx:
        (aq_ref, ak_ref, av_ref, bq_ref, bk_ref, bv_ref, bg_ref, lq_ref, lk_ref, dl_ref,
         ck_ref, cv_ref, s0_ref, cos_ref, sin_ref,
         oa_ref, or_ref, kall_ref, vall_ref, of_ref, ob_ref, stf_ref, stb_ref) = refs
    else:
        (aq_ref, ak_ref, av_ref, bq_ref, bk_ref, bv_ref, bg_ref, lq_ref, lk_ref, dl_ref,
         oa_ref, or_ref, sout_ref, of_ref, ob_ref, stf_ref, stb_ref) = refs
    tq = 256
    rt = 256
    straight = seq <= tq
    cs = RET_CHUNK
    nc = seq // cs

    prod = lq_ref[...] * lk_ref[...]
    ee = jnp.exp(jnp.sum(prod, axis=1, keepdims=True))
    lam = ee[0:1, :] - ee[1:2, :] + lam_init
    scale = DK_A ** -0.5
    ii = lax.broadcasted_iota(jnp.int32, (cs, cs), 0)
    jj = lax.broadcasted_iota(jnp.int32, (cs, cs), 1)
    ic = lax.broadcasted_iota(jnp.int32, (cs, 1), 0).astype(F32)

    def loop(n, body):
        if straight:
            for it in range(n):
                body(it)
        else:
            lax.fori_loop(0, n, lambda it, c: (body(it), c)[1], 0)

    def rows(it, size):
        return pl.ds(it * size, size) if straight else pl.ds(pl.multiple_of(it * size, size), size)

    for hh in range(hps):
        qk_cols = slice(hh * 2 * DK_A, (hh + 1) * 2 * DK_A)
        va_cols = slice(hh * DV_A, (hh + 1) * DV_A)
        kb_cols = slice(hh * DK_B, (hh + 1) * DK_B)
        vb_cols = slice(hh * DV_B, (hh + 1) * DV_B)

        if has_ctx:
            for c in range(2):
                kall_ref[c, 0:past, :] = ck_ref[:, c * DK_A:(c + 1) * DK_A].astype(BF16)
            vall_ref[0:past, :] = cv_ref[...].astype(BF16)

            def copy_body(r):
                rr = rows(r, rt)
                cs_, sn_ = cos_ref[rr, :], sin_ref[rr, :]
                kk = ak_ref[rr, qk_cols].astype(F32)
                for c in range(2):
                    kall_ref[c, pl.ds(past + pl.multiple_of(r * rt, rt), rt), :] = _rope(
                        kk[:, c * DK_A:(c + 1) * DK_A], cs_, sn_).astype(BF16)
                vall_ref[pl.ds(past + pl.multiple_of(r * rt, rt), rt), :] = av_ref[rr, va_cols]

            loop(seq // rt, copy_body)

        def attn_body(qt, qk_cols=qk_cols, va_cols=va_cols):
            rr = rows(qt, tq)
            q = aq_ref[rr, qk_cols]
            if has_ctx:
                qf = q.astype(F32)
                cs_, sn_ = cos_ref[rr, :], sin_ref[rr, :]
            probs = []
            for c in range(2):
                if has_ctx:
                    qc = _rope(qf[:, c * DK_A:(c + 1) * DK_A], cs_, sn_).astype(BF16)
                    keys = kall_ref[c]
                else:
                    qc = q[:, c * DK_A:(c + 1) * DK_A]
                    keys = ak_ref[:, hh * 2 * DK_A + c * DK_A:hh * 2 * DK_A + (c + 1) * DK_A]
                s = _dot_nt(qc, keys) * scale
                e = jnp.exp(s - jnp.max(s, axis=-1, keepdims=True))
                probs.append(e * (1.0 / jnp.sum(e, axis=-1, keepdims=True)))
            a = (probs[0] - lam * probs[1]).astype(BF16)
            o = _dot(a, vall_ref[...] if has_ctx else av_ref[:, va_cols])
            oa_ref[rr, va_cols] = (_rms_f32(o) * (1.0 - lam_init)).astype(BF16)

        loop(seq // tq, attn_body)

        tables = []
        for d in range(2):
            lg = jax.nn.log_sigmoid(dl_ref[d, hh])
            diff = (ii - jj if d == 0 else jj - ii).astype(F32)
            mask = jnp.where(diff >= 0, jnp.exp(jnp.maximum(diff, 0.0) * lg), 0.0)
            if d == 0:
                qdec = jnp.exp((ic + 1.0) * lg)
                kdec = jnp.exp((cs - 1.0 - ic) * lg)
            else:
                qdec = jnp.exp((cs - ic) * lg)
                kdec = jnp.exp(ic * lg)
            tables.append((mask, qdec, kdec, jnp.exp(cs * lg)))
        st_refs = (stf_ref, stb_ref)
        o_refs = (of_ref, ob_ref)
        for d in range(2):
            if has_ctx:
                st_refs[d][hh] = s0_ref[d]
            else:
                st_refs[d][hh] = jnp.zeros((DK_B, DV_B), F32)

        def chunk_body(n, hh=hh, kb_cols=kb_cols, vb_cols=vb_cols, tables=tables):
            for d in range(2):
                mask, qdec, kdec, cdec = tables[d]
                rr = rows(n if d == 0 else nc - 1 - n, cs)
                qc = bq_ref[rr, kb_cols].astype(F32)
                kc = bk_ref[rr, kb_cols].astype(F32) * (DK_B ** -0.5)
                vc = bv_ref[rr, vb_cols]
                state = st_refs[d][hh]
                qk = _dot_nt(qc.astype(BF16), kc.astype(BF16)) * mask
                o_refs[d][rr, vb_cols] = (_dot(qk.astype(BF16), vc)
                                          + _dot((qc * qdec).astype(BF16), state.astype(BF16)))
                kd = (kc * kdec).T.astype(BF16)
                st_refs[d][hh] = cdec * state + _dot(kd, vc)

        loop(nc, chunk_body)

        def final_body(r, vb_cols=vb_cols):
            rr = rows(r, rt)
            tot = of_ref[rr, vb_cols] + ob_ref[rr, vb_cols]
            gate = _silu(bg_ref[rr, vb_cols].astype(F32))
            or_ref[rr, vb_cols] = (_rms_f32(tot) * gate).astype(BF16)

        loop(seq // rt, final_body)
        if not has_ctx:
            for d in range(2):
                sout_ref[d, hh] = st_refs[d][hh]


def _mixer(pa, pb, lam_q, lam_k, decay_logit, lam_init, batch, seq, hps, ctx=None):
    has_ctx = ctx is not None
    past = ctx[0].shape[1] if has_ctx else 0

    def col(head_width, off):
        width = hps * head_width
        return pl.BlockSpec((seq, width), lambda b, hg: (b, off // width + hg))

    a_qk = H_A * 2 * DK_A
    b_qk = H_B * DK_B
    b_v = H_B * DV_B
    in_specs = [
        col(2 * DK_A, 0), col(2 * DK_A, a_qk), col(DV_A, 2 * a_qk),
        col(DK_B, 0), col(DK_B, b_qk), col(DV_B, 2 * b_qk), col(DV_B, 2 * b_qk + b_v),
        pl.BlockSpec((2, DK_A), lambda b, hg: (0, 0)),
        pl.BlockSpec((2, DK_A), lambda b, hg: (0, 0)),
        pl.BlockSpec((2, hps, 1, 1), lambda b, hg: (0, hg, 0, 0)),
    ]
    args = [pa] * 3 + [pb] * 4 + [lam_q, lam_k, decay_logit.reshape(2, H_B, 1, 1)]
    out_specs = [
        pl.BlockSpec((seq, hps * DV_A), lambda b, hg: (b, hg)),
        pl.BlockSpec((seq, hps * DV_B), lambda b, hg: (b, hg)),
    ]
    out_shape = [
        jax.ShapeDtypeStruct((batch * seq, H_A * DV_A), BF16),
        jax.ShapeDtypeStruct((batch * seq, b_v), BF16),
    ]
    scratch = []
    if has_ctx:
        assert hps == 1
        ck, cv, s0, cos_t, sin_t = ctx
        in_specs += [
            pl.BlockSpec((None, past, 2 * DK_A), lambda b, hg: (b, 0, hg)),
            pl.BlockSpec((None, past, DV_A), lambda b, hg: (b, 0, hg)),
            pl.BlockSpec((None, 2, None, DK_B, DV_B), lambda b, hg: (b, 0, hg, 0, 0)),
            pl.BlockSpec((seq, DK_A), lambda b, hg: (0, 0)),
            pl.BlockSpec((seq, DK_A), lambda b, hg: (0, 0)),
        ]
        args += [ck, cv, s0, cos_t, sin_t]
        scratch += [pltpu.VMEM((2, seq + past, DK_A), BF16), pltpu.VMEM((seq + past, DV_A), BF16)]
    else:
        out_specs.append(pl.BlockSpec((None, 2, hps, DK_B, DV_B), lambda b, hg: (b, 0, hg, 0, 0)))
        out_shape.append(jax.ShapeDtypeStruct((batch, 2, H_B, DK_B, DV_B), F32))
    scratch += [
        pltpu.VMEM((seq, hps * DV_B), F32),
        pltpu.VMEM((seq, hps * DV_B), F32),
        pltpu.VMEM((hps, DK_B, DV_B), F32),
        pltpu.VMEM((hps, DK_B, DV_B), F32),
    ]
    return pl.pallas_call(
        functools.partial(_mixer_kernel, seq=seq, has_ctx=has_ctx, lam_init=lam_init, past=past, hps=hps),
        grid=(batch, H_A // hps),
        in_specs=in_specs,
        out_specs=out_specs,
        out_shape=out_shape,
        scratch_shapes=scratch,
        compiler_params=_params("arbitrary", "arbitrary"),
    )(*args)


def _out_kernel(*refs, seg_tiles, n_lhs):
    n_seg = len(seg_tiles)
    per_seg = n_lhs + 1
    seg_refs = [refs[s * per_seg:(s + 1) * per_seg] for s in range(n_seg)]
    (gate_ref, g_ref, sh_ref, sc_ref, rw_ref, w_hbm,
     x1_ref, h2_ref, lg_ref, slab_ref, stage_ref, sem) = refs[n_seg * per_seg:]
    i = pl.program_id(0)

    @pl.when(i == 0)
    def _():
        _load_weight_slab(w_hbm, slab_ref, stage_ref, sem, 0)

    def run(lhs_refs, x_ref):
        acc = None
        r0 = 0
        for lhs_ref in lhs_refs:
            kk = lhs_ref.shape[1]
            part = _dot(lhs_ref[...], slab_ref[r0:r0 + kk, :])
            acc = part if acc is None else acc + part
            r0 += kk
        x1 = x_ref[...] + gate_ref[...] * acc
        x1_ref[...] = x1
        h2 = (_rms_f32(x1) * g_ref[...]) * (1.0 + sc_ref[...]) + sh_ref[...]
        _store_token_major(h2_ref, h2)
        lg_ref[...] = lax.dot_general(rw_ref[...], h2, (((1,), (1,)), ((), ())),
                                      precision=lax.Precision.HIGHEST,
                                      preferred_element_type=F32)

    if n_seg == 1:
        run(seg_refs[0][:n_lhs], seg_refs[0][n_lhs])
    else:
        lo = 0
        for s in range(n_seg):
            @pl.when((i >= lo) & (i < lo + seg_tiles[s]))
            def _(s=s):
                run(seg_refs[s][:n_lhs], seg_refs[s][n_lhs])

            lo += seg_tiles[s]


def _out_proj(segments, w, g, mod3, layer, row_fn, router_wt, tm=256, stage_rows=256):
    n_lhs = len(segments[0][0])
    d = segments[0][1].shape[1]
    kdim = w.shape[0]
    base = layer * 48
    ne = router_wt.shape[0]
    seg_tiles = tuple(x.shape[0] // tm for _, x in segments)
    t_total = sum(x.shape[0] for _, x in segments)

    def mod_spec(kind):
        return pl.BlockSpec((None, 1, d), lambda i: (base + row_fn(i) * 6 + kind, 0, 0))

    in_specs, args = [], []
    lo = 0
    for (lhs_list, x), nt in zip(segments, seg_tiles):
        def seg_map(i, lo=lo, nt=nt):
            return (jnp.clip(i - lo, 0, nt - 1), 0)

        for lhs in lhs_list:
            in_specs.append(pl.BlockSpec((tm, lhs.shape[1]), seg_map))
            args.append(lhs)
        in_specs.append(pl.BlockSpec((tm, d), seg_map))
        args.append(x)
        lo += nt
    in_specs += [
        mod_spec(2),
        pl.BlockSpec((1, d), lambda i: (0, 0)),
        mod_spec(3),
        mod_spec(4),
        pl.BlockSpec((ne, d), lambda i: (0, 0)),
        pl.BlockSpec(memory_space=pl.ANY),
    ]
    args += [mod3, g.reshape(1, d), mod3, mod3, router_wt, w]
    rpt = d // LANES
    out_specs = [
        pl.BlockSpec((tm, d), lambda i: (i, 0)),
        pl.BlockSpec((tm * rpt, LANES), lambda i: (i, 0)),
        pl.BlockSpec((ne, tm), lambda i: (0, i)),
    ]
    out_shape = [
        jax.ShapeDtypeStruct((t_total, d), F32),
        jax.ShapeDtypeStruct((t_total * rpt, LANES), F32),
        jax.ShapeDtypeStruct((ne, t_total), F32),
    ]
    return pl.pallas_call(
        functools.partial(_out_kernel, seg_tiles=seg_tiles, n_lhs=n_lhs),
        grid=(t_total // tm,),
        in_specs=in_specs,
        out_specs=out_specs,
        out_shape=out_shape,
        scratch_shapes=[
            pltpu.VMEM((kdim, d), BF16),
            pltpu.VMEM((2, stage_rows, d), F32),
            pltpu.SemaphoreType.DMA((2,)),
        ],
        compiler_params=_params("arbitrary"),
    )(*args)


def _first_index(vals, target):
    idx = jnp.full(target.shape, len(vals) - 1, jnp.int32)
    for k in range(len(vals) - 2, -1, -1):
        idx = jnp.where(vals[k] == target, k, idx)
    return idx


def _pick(vals, idx):
    out = vals[-1]
    for k in range(len(vals) - 2, -1, -1):
        out = jnp.where(idx == k, vals[k], out)
    return out


def _route_kernel(lg_ref, rb_ref, pos_ref, wt_ref, meta_ref, rank_ref, *, tile_rows, scan):
    logits = lg_ref[...]
    t = logits.shape[1]
    scores = jax.nn.sigmoid(logits)
    biased = scores + rb_ref[...]
    s_rows = [scores[e:e + 1, :] for e in range(N_EXPERTS)]
    b_rows = [biased[e:e + 1, :] for e in range(N_EXPERTS)]
    grp = []
    for g in range(N_GROUPS):
        r = b_rows[g * GROUP_SIZE:(g + 1) * GROUP_SIZE]
        best = None
        for a in range(GROUP_SIZE):
            for b in range(a + 1, GROUP_SIZE):
                pair = r[a] + r[b]
                best = pair if best is None else jnp.maximum(best, pair)
        grp.append(best)
    gmax = functools.reduce(jnp.maximum, grp)
    gsel = _first_index(grp, gmax)
    cand_b = [_pick([b_rows[g * GROUP_SIZE + k] for g in range(N_GROUPS)], gsel) for k in range(GROUP_SIZE)]
    cand_s = [_pick([s_rows[g * GROUP_SIZE + k] for g in range(N_GROUPS)], gsel) for k in range(GROUP_SIZE)]
    m1 = functools.reduce(jnp.maximum, cand_b)
    k1 = _first_index(cand_b, m1)
    rest = [jnp.where(k1 == k, -jnp.inf, cand_b[k]) for k in range(GROUP_SIZE)]
    m2 = functools.reduce(jnp.maximum, rest)
    k2 = _first_index(rest, m2)
    w1 = _pick(cand_s, k1)
    w2 = _pick(cand_s, k2)
    wsum = w1 + w2
    w1 = w1 / wsum
    w2 = w2 / wsum
    e1 = gsel * GROUP_SIZE + k1
    e2 = gsel * GROUP_SIZE + k2
    wt_ref[...] = jnp.zeros_like(wt_ref)
    wt_ref[0:1, :] = w1
    wt_ref[1:2, :] = w2

    eid = lax.broadcasted_iota(jnp.int32, logits.shape, 0)
    sel1 = eid == e1
    sel2 = eid == e2
    ind = jnp.where(sel1 | sel2, 1.0, 0.0)
    ua = lax.broadcasted_iota(jnp.int32, (scan, scan), 0)
    ub = lax.broadcasted_iota(jnp.int32, (scan, scan), 1)
    upper = jnp.where(ua <= ub, 1.0, 0.0).astype(BF16)
    carry = jnp.zeros((N_EXPERTS, 1), F32)
    for blk in range(t // scan):
        seg = ind[:, blk * scan:(blk + 1) * scan]
        incl = _dot(seg.astype(BF16), upper)
        rank_ref[:, blk * scan:(blk + 1) * scan] = incl - seg + carry
        carry = carry + incl[:, scan - 1:scan]
    n_tiles = jnp.floor((carry + (tile_rows - 1.0)) * (1.0 / tile_rows))
    la = lax.broadcasted_iota(jnp.int32, (N_EXPERTS, N_EXPERTS), 0)
    lb = lax.broadcasted_iota(jnp.int32, (N_EXPERTS, N_EXPERTS), 1)
    lower = jnp.where(lb < la, 1.0, 0.0).astype(BF16)
    start = _dot(lower, jnp.broadcast_to(n_tiles, (N_EXPERTS, LANES)).astype(BF16))
    end = start + n_tiles
    slot = start[:, 0:1] * tile_rows + rank_ref[...]
    pos_ref[...] = jnp.zeros_like(pos_ref)
    pos_ref[0:1, :] = jnp.sum(jnp.where(sel1, slot, 0.0), axis=0, keepdims=True).astype(jnp.int32)
    pos_ref[1:2, :] = jnp.sum(jnp.where(sel2, slot, 0.0), axis=0, keepdims=True).astype(jnp.int32)

    n_used = end[N_EXPERTS - 1:N_EXPERTS, :]
    tile = lax.broadcasted_iota(jnp.int32, (N_EXPERTS, LANES), 1).astype(F32)
    tile = jnp.minimum(tile, n_used - 1.0)
    tile_expert = jnp.sum(jnp.where(tile >= end, 1.0, 0.0), axis=0, keepdims=True)
    lane = lax.broadcasted_iota(jnp.int32, (1, LANES), 1)
    meta_ref[...] = jnp.zeros_like(meta_ref)
    meta_ref[0:1, :] = jnp.where(lane == LANES - 1, n_used, tile_expert).astype(jnp.int32)
    diag = lax.broadcasted_iota(jnp.int32, (N_EXPERTS, LANES), 0) == lax.broadcasted_iota(
        jnp.int32, (N_EXPERTS, LANES), 1)
    meta_ref[1:2, :] = jnp.sum(jnp.where(diag, carry, 0.0), axis=0, keepdims=True).astype(jnp.int32)
    meta_ref[2:3, :] = jnp.sum(jnp.where(diag, start * tile_rows, 0.0), axis=0, keepdims=True).astype(jnp.int32)


def _route(logits_t, router_b, tile_rows, scan=512):
    ne, t = logits_t.shape
    return pl.pallas_call(
        functools.partial(_route_kernel, tile_rows=tile_rows, scan=scan),
        out_shape=[
            jax.ShapeDtypeStruct((8, t), jnp.int32),
            jax.ShapeDtypeStruct((8, t), F32),
            jax.ShapeDtypeStruct((8, LANES), jnp.int32),
        ],
        scratch_shapes=[pltpu.VMEM((ne, t), F32)],
        compiler_params=pltpu.CompilerParams(vmem_limit_bytes=VMEM_LIMIT_BYTES),
    )(logits_t, router_b.reshape(ne, 1))


def _dispatch_kernel(pos_ref, meta_ref, h_ref, xs_ref, zero_ref, sem, *, t, rpt, chunk, tile_rows, n_tiles):
    c = pl.program_id(0)

    def slot_rows(slot):
        return xs_ref.at[pl.ds(pl.multiple_of(slot * rpt, rpt), rpt)]

    def body(r, carry):
        tok = c * chunk + r
        src = h_ref.at[pl.ds(pl.multiple_of(r * rpt, rpt), rpt)]
        for k in range(2):
            pltpu.make_async_copy(src, slot_rows(pos_ref[k * t + tok]), sem.at[0]).start()
        return carry

    lax.fori_loop(0, chunk, body, 0, unroll=8)

    @pl.when(c == 0)
    def _():
        zero_ref[...] = jnp.zeros_like(zero_ref)
        zero_row = zero_ref.at[pl.ds(0, rpt)]
        for e in range(N_EXPERTS):
            count = meta_ref[LANES + e]
            first = meta_ref[2 * LANES + e] + count
            n_pad = (tile_rows - count % tile_rows) % tile_rows

            def pad_start(r, carry, first=first):
                pltpu.make_async_copy(zero_row, slot_rows(first + r), sem.at[1]).start()
                return carry

            def pad_wait(r, carry, first=first):
                pltpu.make_async_copy(zero_row, slot_rows(first + r), sem.at[1]).wait()
                return carry

            lax.fori_loop(0, n_pad, pad_start, 0)
            lax.fori_loop(0, n_pad, pad_wait, 0)

        def tile_copy(n):
            rows = tile_rows * rpt
            return pltpu.make_async_copy(zero_ref, xs_ref.at[pl.ds(pl.multiple_of(n * rows, rows), rows)],
                                         sem.at[1])

        def unused_start(n, carry):
            tile_copy(n).start()
            return carry

        def unused_wait(n, carry):
            tile_copy(n).wait()
            return carry

        lax.fori_loop(meta_ref[LANES - 1], n_tiles, unused_start, 0)
        lax.fori_loop(meta_ref[LANES - 1], n_tiles, unused_wait, 0)

    for _ in range(2):
        pltpu.make_async_copy(h_ref, xs_ref.at[pl.ds(0, chunk * rpt)], sem.at[0]).wait()


def _dispatch(pos_flat, meta_flat, h_lin, n_slots, rpt, tile_rows, chunk=512):
    t = pos_flat.shape[0] // 2
    return pl.pallas_call(
        functools.partial(_dispatch_kernel, t=t, rpt=rpt, chunk=chunk, tile_rows=tile_rows,
                          n_tiles=n_slots // tile_rows),
        grid_spec=pltpu.PrefetchScalarGridSpec(
            num_scalar_prefetch=2,
            grid=(t // chunk,),
            in_specs=[pl.BlockSpec((chunk * rpt, LANES), lambda c, p, m: (c, 0))],
            out_specs=pl.BlockSpec(memory_space=pl.ANY),
            scratch_shapes=[pltpu.VMEM((tile_rows * rpt, LANES), h_lin.dtype), pltpu.SemaphoreType.DMA((2,))],
        ),
        out_shape=jax.ShapeDtypeStruct((n_slots * rpt, LANES), h_lin.dtype),
        compiler_params=_params("arbitrary"),
    )(pos_flat, meta_flat, h_lin)


def _ffn_kernel(meta_ref, xs_ref, wg_ref, wu_ref, wd_ref, ys_ref, xb_ref, wgb_ref, wub_ref, wdb_ref, *, tm):
    n = pl.program_id(0)
    n_used = meta_ref[LANES - 1]

    @pl.when(n < n_used)
    def _():
        @pl.when((n == 0) | (meta_ref[n] != meta_ref[jnp.maximum(n - 1, 0)]))
        def _():
            wgb_ref[...] = wg_ref[...].astype(BF16)
            wub_ref[...] = wu_ref[...].astype(BF16)
            wdb_ref[...] = wd_ref[...].astype(BF16)

        for c in range(xb_ref.shape[1] // LANES):
            xb_ref[:, c * LANES:(c + 1) * LANES] = _load_token_major(xs_ref, tm, c).astype(BF16)
        x = xb_ref[...]
        a = _dot(x, wgb_ref[...])
        b = _dot(x, wub_ref[...])
        act = (_silu(a) * b).astype(BF16)
        _store_token_major(ys_ref, _dot(act, wdb_ref[...]))

    @pl.when(n >= n_used)
    def _():
        ys_ref[...] = jnp.zeros_like(ys_ref)


def _ffn(meta, xs, w_gate, w_up, w_down, layer, tm, n_tiles):
    _, ne, d, f = w_gate.shape
    rpt = d // LANES
    rpt_in = xs.shape[0] // (n_tiles * tm)

    def row_tile(n, m):
        return (jnp.minimum(n, m[LANES - 1] - 1), 0)

    return pl.pallas_call(
        functools.partial(_ffn_kernel, tm=tm),
        grid_spec=pltpu.PrefetchScalarGridSpec(
            num_scalar_prefetch=1,
            grid=(n_tiles,),
            in_specs=[
                pl.BlockSpec((tm * rpt_in, LANES), row_tile),
                pl.BlockSpec((None, None, d, f), lambda n, m: (layer, m[n], 0, 0)),
                pl.BlockSpec((None, None, d, f), lambda n, m: (layer, m[n], 0, 0)),
                pl.BlockSpec((None, None, f, d), lambda n, m: (layer, m[n], 0, 0)),
            ],
            out_specs=pl.BlockSpec((tm * rpt, LANES), lambda n, m: (n, 0)),
            scratch_shapes=[
                pltpu.VMEM((tm, d), BF16),
                pltpu.VMEM((d, f), BF16),
                pltpu.VMEM((d, f), BF16),
                pltpu.VMEM((f, d), BF16),
            ],
        ),
        out_shape=jax.ShapeDtypeStruct((n_tiles * tm * rpt, LANES), F32),
        compiler_params=_params("arbitrary"),
    )(meta, xs, w_gate, w_up, w_down)


def _combine_kernel(pos_ref, ys_ref, x_ref, wt_ref, gate_ref, o_ref, a0, b0, a1, b1, sem, *, t, rpt, tm):
    i = pl.program_id(0)
    n = pl.num_programs(0)
    bufs = ((a0, b0), (a1, b1))

    def issue(tile, s):
        def body(r, carry):
            tok = tile * tm + r
            for k in range(2):
                slot_row = pos_ref[k * t + tok]
                pltpu.make_async_copy(
                    ys_ref.at[pl.ds(pl.multiple_of(slot_row * rpt, rpt), rpt)],
                    bufs[s][k].at[pl.ds(pl.multiple_of(r * rpt, rpt), rpt)],
                    sem.at[s]).start()
            return carry

        lax.fori_loop(0, tm, body, 0, unroll=8)

    def wait(s):
        for k in range(2):
            pltpu.make_async_copy(ys_ref.at[pl.ds(0, tm * rpt)], bufs[s][k], sem.at[s]).wait()

    def compute(s):
        w = wt_ref[...]
        w0, w1 = w[:, 0:1], w[:, 1:2]
        for c in range(rpt):
            cols = slice(c * LANES, (c + 1) * LANES)
            y = w0 * _load_token_major(bufs[s][0], tm, c) + w1 * _load_token_major(bufs[s][1], tm, c)
            o_ref[:, cols] = x_ref[:, cols] + gate_ref[:, cols] * y

    @pl.when(i == 0)
    def _():
        issue(0, 0)

    for s in range(2):
        @pl.when(i % 2 == s)
        def _(s=s):
            @pl.when(i + 1 < n)
            def _():
                issue(i + 1, 1 - s)

            wait(s)
            compute(s)


def _combine(pos_flat, ys, x1, wt, mod3, layer, row_fn, tm=256):
    t, d = x1.shape
    rpt = d // LANES
    base = layer * 48
    return pl.pallas_call(
        functools.partial(_combine_kernel, t=t, rpt=rpt, tm=tm),
        grid_spec=pltpu.PrefetchScalarGridSpec(
            num_scalar_prefetch=1,
            grid=(t // tm,),
            in_specs=[
                pl.BlockSpec(memory_space=pl.ANY),
                pl.BlockSpec((tm, d), lambda i, p: (i, 0)),
                pl.BlockSpec((tm, 2), lambda i, p: (i, 0)),
                pl.BlockSpec((None, 1, d), lambda i, p: (base + row_fn(i) * 6 + 5, 0, 0)),
            ],
            out_specs=pl.BlockSpec((tm, d), lambda i, p: (i, 0)),
            scratch_shapes=[pltpu.VMEM((tm * rpt, LANES), F32)] * 4 + [pltpu.SemaphoreType.DMA((2,))],
        ),
        out_shape=jax.ShapeDtypeStruct((t, d), F32),
        compiler_params=_params("arbitrary"),
    )(pos_flat, ys, x1, wt, mod3)


def _moe(lg, router_b, h_lin, x1, w_gate, w_up, w_down, mod3, layer, row_fn, tile_rows=256):
    t, d = x1.shape
    n_tiles = (2 * t) // tile_rows + N_EXPERTS
    pos, wts, meta = _route(lg, router_b, tile_rows)
    pos_flat = pos[:2].reshape(-1)
    xs = _dispatch(pos_flat, meta[:3].reshape(-1), h_lin, n_tiles * tile_rows, h_lin.shape[0] // t, tile_rows)
    ys = _ffn(meta[0], xs, w_gate, w_up, w_down, layer, tile_rows, n_tiles)
    return _combine(pos_flat, ys, x1, wts[:2].T, mod3, layer, row_fn)


def _final_kernel(x_ref, g_ref, o_ref):
    o_ref[...] = _rms_f32(x_ref[...]) * g_ref[...]


def _final_norm(x, g, row0, rows, tm=512):
    d = x.shape[1]
    off = row0 // tm
    return pl.pallas_call(
        _final_kernel,
        grid=(rows // tm,),
        in_specs=[
            pl.BlockSpec((tm, d), lambda i: (i + off, 0)),
            pl.BlockSpec((1, d), lambda i: (0, 0)),
        ],
        out_specs=pl.BlockSpec((tm, d), lambda i: (i, 0)),
        out_shape=jax.ShapeDtypeStruct((rows, d), F32),
        compiler_params=_params("arbitrary"),
    )(x, g.reshape(1, d))


def _sgu_kernel(u0_ref, u1_ref, v0_ref, v1_ref, ng_ref, ws_ref, bs_ref, o0_ref, o1_ref, *, n_chunks):
    half = v0_ref.shape[1]
    gw = 2 * half // H_C
    v0 = v0_ref[...].astype(F32)
    v1 = v1_ref[...].astype(F32)
    ms = (jnp.sum(v0 * v0, axis=-1, keepdims=True) + jnp.sum(v1 * v1, axis=-1, keepdims=True)) / (2 * half)
    inv = lax.rsqrt(ms + EPS)
    halves = ((u0_ref, (v0 * inv * ng_ref[:, :half]).astype(BF16), o0_ref),
              (u1_ref, (v1 * inv * ng_ref[:, half:]).astype(BF16), o1_ref))
    for n in range(n_chunks):
        r0 = n * CHUNK_C
        for h in range(H_C):
            u_ref, vn, o_ref = halves[h // (H_C // 2)]
            c0 = (h % (H_C // 2)) * gw
            mixed = _dot(ws_ref[h].astype(BF16), vn[r0:r0 + CHUNK_C, c0:c0 + gw]) + bs_ref[h]
            u = u_ref[r0:r0 + CHUNK_C, c0:c0 + gw].astype(F32)
            o_ref[r0:r0 + CHUNK_C, c0:c0 + gw] = (u * mixed).astype(BF16)


def _sgu(u0, u1, v0, v1, norm_g, w_s, b_s, tm=256):
    t, half = u0.shape
    blk = pl.BlockSpec((tm, half), lambda i: (i, 0))
    return pl.pallas_call(
        functools.partial(_sgu_kernel, n_chunks=tm // CHUNK_C),
        grid=(t // tm,),
        in_specs=[
            blk, blk, blk, blk,
            pl.BlockSpec((1, 2 * half), lambda i: (0, 0)),
            pl.BlockSpec((H_C, CHUNK_C, CHUNK_C), lambda i: (0, 0, 0)),
            pl.BlockSpec((H_C, CHUNK_C, 1), lambda i: (0, 0, 0)),
        ],
        out_specs=[blk, blk],
        out_shape=[jax.ShapeDtypeStruct((t, half), BF16)] * 2,
        compiler_params=_params("arbitrary"),
    )(u0, u1, v0, v1, norm_g.reshape(1, 2 * half), w_s, b_s.reshape(H_C, CHUNK_C, 1))


def _rope_tables(n_tok):
    rows = n_tok // GRID_W
    pos_r = jnp.repeat(jnp.arange(rows), GRID_W).astype(F32)
    pos_c = jnp.tile(jnp.arange(GRID_W), rows).astype(F32)
    n_freq = DK_A // 4
    inv = jnp.power(ROPE_BASE, -jnp.arange(n_freq, dtype=F32) / n_freq)
    ang_r = pos_r[:, None] * inv
    ang_c = pos_c[:, None] * inv
    cos_t = jnp.concatenate([jnp.cos(ang_r), jnp.cos(ang_r), jnp.cos(ang_c), jnp.cos(ang_c)], axis=1)
    sin_t = jnp.concatenate([-jnp.sin(ang_r), jnp.sin(ang_r), -jnp.sin(ang_c), jnp.sin(ang_c)], axis=1)
    return cos_t, sin_t


def kernel(x_prompt, x_sample, cache_k, cache_v, state_ret, c, c_ctx, norm_g, w_mod, b_mod, w_in_even, w_out_even, lam_q, lam_k, ret_decay_logit, w_in_odd, gmlp_norm_g, w_spatial, b_spatial, w_out_odd, router_w, router_b, w_gate, w_up, w_down, final_norm_g):
    batch, seq, d = x_prompt.shape
    dbatch, dseq, _ = x_sample.shape
    past = cache_k.shape[2]
    depth = norm_g.shape[0]
    t_ctx, t_lat = batch * seq, dbatch * dseq
    t_all = t_ctx + t_lat
    xc = x_prompt.reshape(t_ctx, d)
    xl = x_sample.reshape(t_lat, d)

    cond8 = jnp.concatenate([c_ctx[None, :], c, jnp.zeros((8 - 1 - dbatch, d), F32)], axis=0)
    mod3 = _modulation(cond8, w_mod, b_mod).reshape(depth * 8 * 6, 1, d)
    router_wt = router_w.T
    cos_t, sin_t = _rope_tables(dseq)

    tm_in = 512
    tm_out = 256
    tm_cmb = 256
    w_in0 = w_in_even.reshape(d, -1)
    w_out0 = w_out_even.reshape(-1, d)
    w_in1 = w_in_odd.reshape(d, -1)
    w_out1 = w_out_odd.reshape(-1, d)
    lq, lk, dlog = lam_q.reshape(2, DK_A), lam_k.reshape(2, DK_A), ret_decay_logit.reshape(2, H_B)

    def rows_ctx(i):
        return 0

    def rows_lat(tiles_per_batch):
        return lambda i: 1 + i // tiles_per_batch

    def rows_all(tm):
        n_ctx, per = t_ctx // tm, dseq // tm
        return lambda i: jnp.maximum(i - n_ctx, 0) // per + jnp.where(i >= n_ctx, 1, 0)

    lam_init = 0.8 - 0.6 * math.exp(-0.3 * 0)
    half0 = w_in0.shape[1] // 2
    g00 = norm_g[0, 0]
    rl = rows_lat(dseq // tm_in)
    pa_c, k_new, v_new = _in_proj(xc, g00, mod3, 0, rows_ctx, w_in0, 0, half0, "even_kv", tm_in)
    (pb_c,) = _in_proj(xc, g00, mod3, 0, rows_ctx, w_in0, half0, half0, "even", tm_in)
    (pa_l,) = _in_proj(xl, g00, mod3, 0, rl, w_in0, 0, half0, "even", tm_in)
    (pb_l,) = _in_proj(xl, g00, mod3, 0, rl, w_in0, half0, half0, "even", tm_in)
    oa_c, or_c, s_new = _mixer(pa_c, pb_c, lq, lk, dlog, lam_init, batch, seq, H_A)
    ctx = (cache_k.reshape(dbatch, past, H_A * 2 * DK_A), cache_v.reshape(dbatch, past, H_A * DV_A),
           state_ret.reshape(dbatch, 2, H_B, DK_B, DV_B), cos_t, sin_t)
    oa_l, or_l = _mixer(pa_l, pb_l, lq, lk, dlog, lam_init, dbatch, dseq, 1, ctx)
    x1, h2, lg = _out_proj([([oa_c, or_c], xc), ([oa_l, or_l], xl)], w_out0, norm_g[0, 1], mod3, 0,
                           rows_all(tm_out), router_wt, tm_out)
    x2 = _moe(lg, router_b, h2, x1, w_gate, w_up, w_down, mod3, 0, rows_all(tm_cmb))

    q1 = w_in1.shape[1] // 4
    zs = [_in_proj(x2, norm_g[1, 0], mod3, 1, rows_all(tm_in), w_in1, j * q1, q1, "odd", tm_in)[0]
          for j in range(4)]
    g0, g1 = _sgu(zs[0], zs[1], zs[2], zs[3], gmlp_norm_g.reshape(-1),
                  w_spatial.reshape(H_C, CHUNK_C, CHUNK_C), b_spatial.reshape(H_C, CHUNK_C))
    x3, h4, lg2 = _out_proj([([g0, g1], x2)], w_out1, norm_g[1, 1], mod3, 1, rows_all(tm_out), router_wt, tm_out)
    x4 = _moe(lg2, router_b, h4, x3, w_gate, w_up, w_down, mod3, 1, rows_all(tm_cmb))
    y_p = _final_norm(x4, final_norm_g, 0, t_ctx)
    y_s = _final_norm(x4, final_norm_g, t_ctx, t_lat)

    y_prompt = y_p.reshape(batch, seq, d)
    y_sample = y_s.reshape(dbatch, dseq, d)
    new_cache_k = k_new.reshape(batch, 1, seq, H_A, 2, DK_A)
    new_cache_v = v_new.reshape(batch, 1, seq, H_A, DV_A)
    new_state_ret = s_new.reshape(batch, 1, 2, H_B, DK_B, DV_B)
    return (y_prompt, y_sample, new_cache_k, new_cache_v, new_state_ret)
```

```python
import functools
import math

import jax
import jax.numpy as jnp
from jax import lax
from jax.experimental import pallas as pl
from jax.experimental.pallas import tpu as pltpu

F32 = jnp.float32
BF16 = jnp.bfloat16

EPS = 1e-6
ROPE_BASE = 10000.0
GRID_W = 64
H_A, DK_A, DV_A = 4, 128, 256
H_B, DK_B, DV_B = 4, 128, 256
RET_CHUNK = 128
H_C, CHUNK_C = 8, 128
N_EXPERTS, N_GROUPS, GROUP_SIZE = 16, 4, 4

VMEM_LIMIT_BYTES = 56 * 1024 * 1024


def _params(*sem):
    return pltpu.CompilerParams(dimension_semantics=sem, vmem_limit_bytes=VMEM_LIMIT_BYTES)


def _rms_f32(x):
    return x * lax.rsqrt(jnp.mean(x * x, axis=-1, keepdims=True) + EPS)


def _silu(x):
    return x * jax.nn.sigmoid(x)


def _dot(a, b):
    return jnp.dot(a, b, preferred_element_type=F32)


def _dot_nt(a, b):
    return lax.dot_general(a, b, (((1,), (1,)), ((), ())), preferred_element_type=F32)


LANES = 128
MXU_DIM = 256


def _store_token_major(ref, val, col0=0):
    tm, width = val.shape
    rpt = ref.shape[0] // tm
    for c in range(width // LANES):
        ref[pl.ds(col0 // LANES + c, tm, stride=rpt), :] = val[:, c * LANES:(c + 1) * LANES]


def _load_token_major(ref, tm, c):
    rpt = ref.shape[0] // tm
    return ref[pl.ds(c, tm, stride=rpt), :]


def _mod_kernel(c_ref, w_ref, b_ref, o_ref):
    s = _silu(c_ref[...]).astype(BF16)
    o_ref[...] = _dot(s, w_ref[...].astype(BF16)) + b_ref[...]


def _modulation(cond8, w_mod, b_mod, tn=1024):
    depth, d, n = w_mod.shape
    return pl.pallas_call(
        _mod_kernel,
        grid=(depth, n // tn),
        in_specs=[
            pl.BlockSpec((8, d), lambda l, j: (0, 0)),
            pl.BlockSpec((None, d, tn), lambda l, j: (l, 0, j)),
            pl.BlockSpec((None, 1, tn), lambda l, j: (l, 0, j)),
        ],
        out_specs=pl.BlockSpec((None, 8, tn), lambda l, j: (l, 0, j)),
        out_shape=jax.ShapeDtypeStruct((depth, 8, n), F32),
        compiler_params=_params("arbitrary", "arbitrary"),
    )(cond8, w_mod, b_mod.reshape(depth, 1, n))


def _load_weight_slab(w_hbm, slab_ref, stage_ref, sem, col0):
    kdim, n = slab_ref.shape
    rows = stage_ref.shape[1]
    nb = kdim // rows

    def copy(j):
        return pltpu.make_async_copy(w_hbm.at[pl.ds(j * rows, rows), pl.ds(col0, n)],
                                     stage_ref.at[j % 2], sem.at[j % 2])

    copy(0).start()
    for j in range(nb):
        if j + 1 < nb:
            copy(j + 1).start()
        copy(j).wait()
        slab_ref[j * rows:(j + 1) * rows, :] = stage_ref[j % 2].astype(BF16)


def _in_kernel(x_ref, g_ref, sh_ref, sc_ref, w_hbm, *rest, col0, mode, kv_cols, chunk):
    if mode == "even_kv":
        p_ref, k_ref, v_ref, slab_ref, stage_ref, h_ref, sem = rest
    else:
        p_ref, slab_ref, stage_ref, h_ref, sem = rest

    @pl.when(pl.program_id(0) == 0)
    def _():
        _load_weight_slab(w_hbm, slab_ref, stage_ref, sem, col0)

    h = (_rms_f32(x_ref[...]) * g_ref[...]) * (1.0 + sc_ref[...]) + sh_ref[...]
    h_ref[...] = h.astype(BF16)
    n = slab_ref.shape[1]
    for c0 in range(0, n, chunk):
        acc = _dot(h_ref[...], slab_ref[:, c0:c0 + chunk])
        if mode == "odd":
            p_ref[:, c0:c0 + chunk] = (0.5 * acc * (1.0 + lax.erf(acc * (2.0 ** -0.5)))).astype(BF16)
        else:
            p_ref[:, c0:c0 + chunk] = acc.astype(BF16)
        if mode == "even_kv":
            (k0, k1), (v0, v1) = kv_cols
            a0 = col0 + c0
            if k0 <= a0 < k1:
                k_ref[:, a0 - k0:a0 - k0 + chunk] = acc
            if v0 <= a0 < v1:
                v_ref[:, a0 - v0:a0 - v0 + chunk] = acc


def _in_proj(x, g, mod3, layer, row_fn, w, col0, n_cols, mode, tm=512, chunk=512, stage_rows=256):
    t, d = x.shape
    base = layer * 48

    def mod_spec(kind):
        return pl.BlockSpec((None, 1, d), lambda i: (base + row_fn(i) * 6 + kind, 0, 0))

    in_specs = [
        pl.BlockSpec((tm, d), lambda i: (i, 0)),
        pl.BlockSpec((1, d), lambda i: (0, 0)),
        mod_spec(0),
        mod_spec(1),
        pl.BlockSpec(memory_space=pl.ANY),
    ]
    out_specs = [pl.BlockSpec((tm, n_cols), lambda i: (i, 0))]
    out_shape = [jax.ShapeDtypeStruct((t, n_cols), BF16)]
    kv_cols = None
    if mode == "even_kv":
        a_qk = H_A * 2 * DK_A
        a_v = H_A * DV_A
        kv_cols = ((a_qk, 2 * a_qk), (2 * a_qk, 2 * a_qk + a_v))
        out_specs += [pl.BlockSpec((tm, a_qk), lambda i: (i, 0)), pl.BlockSpec((tm, a_v), lambda i: (i, 0))]
        out_shape += [jax.ShapeDtypeStruct((t, a_qk), F32), jax.ShapeDtypeStruct((t, a_v), F32)]
    return pl.pallas_call(
        functools.partial(_in_kernel, col0=col0, mode=mode, kv_cols=kv_cols, chunk=chunk),
        grid=(t // tm,),
        in_specs=in_specs,
        out_specs=out_specs,
        out_shape=out_shape,
        scratch_shapes=[
            pltpu.VMEM((d, n_cols), BF16),
            pltpu.VMEM((2, stage_rows, n_cols), F32),
            pltpu.VMEM((tm, d), BF16),
            pltpu.SemaphoreType.DMA((2,)),
        ],
        compiler_params=_params("arbitrary"),
    )(x, g.reshape(1, d), mod3, mod3, w)


def _rope(x, c, s):
    lane = lax.broadcasted_iota(jnp.int32, x.shape, 1)
    partner = jnp.where((lane & 32) == 0, pltpu.roll(x, 96, 1), pltpu.roll(x, 32, 1))
    return x * c + partner * s


def _mixer_kernel(*refs, seq, has_ctx, lam_init, past, hps):
    if has_ctx:
        (aq_ref, ak_ref, av_ref, bq_ref, bk_ref, bv_ref, bg_ref, lq_ref, lk_ref, dl_ref,
         ck_ref, cv_ref, s0_ref, cos_ref, sin_ref,
         oa_ref, or_ref, kall_ref, vall_ref, of_ref, ob_ref, stf_ref, stb_ref) = refs
    else:
        (aq_ref, ak_ref, av_ref, bq_ref, bk_ref, bv_ref, bg_ref, lq_ref, lk_ref, dl_ref,
         oa_ref, or_ref, sout_ref, of_ref, ob_ref, stf_ref, stb_ref) = refs
    tq = 256
    rt = 256
    straight = seq <= tq
    cs = RET_CHUNK
    nc = seq // cs

    prod = lq_ref[...] * lk_ref[...]
    ee = jnp.exp(jnp.sum(prod, axis=1, keepdims=True))
    lam = ee[0:1, :] - ee[1:2, :] + lam_init
    scale = DK_A ** -0.5
    ii = lax.broadcasted_iota(jnp.int32, (cs, cs), 0)
    jj = lax.broadcasted_iota(jnp.int32, (cs, cs), 1)
    ic = lax.broadcasted_iota(jnp.int32, (cs, 1), 0).astype(F32)

    def loop(n, body):
        if straight:
            for it in range(n):
                body(it)
        else:
            lax.fori_loop(0, n, lambda it, c: (body(it), c)[1], 0)

    def rows(it, size):
        return pl.ds(it * size, size) if straight else pl.ds(pl.multiple_of(it * size, size), size)

    for hh in range(hps):
        qk_cols = slice(hh * 2 * DK_A, (hh + 1) * 2 * DK_A)
        va_cols = slice(hh * DV_A, (hh + 1) * DV_A)
        kb_cols = slice(hh * DK_B, (hh + 1) * DK_B)
        vb_cols = slice(hh * DV_B, (hh + 1) * DV_B)

        if has_ctx:
            for c in range(2):
                kall_ref[c, 0:past, :] = ck_ref[:, c * DK_A:(c + 1) * DK_A].astype(BF16)
            vall_ref[0:past, :] = cv_ref[...].astype(BF16)

            def copy_body(r):
                rr = rows(r, rt)
                cs_, sn_ = cos_ref[rr, :], sin_ref[rr, :]
                kk = ak_ref[rr, qk_cols].astype(F32)
                for c in range(2):
                    kall_ref[c, pl.ds(past + pl.multiple_of(r * rt, rt), rt), :] = _rope(
                        kk[:, c * DK_A:(c + 1) * DK_A], cs_, sn_).astype(BF16)
                vall_ref[pl.ds(past + pl.multiple_of(r * rt, rt), rt), :] = av_ref[rr, va_cols]

            loop(seq // rt, copy_body)

        def attn_body(qt, qk_cols=qk_cols, va_cols=va_cols):
            rr = rows(qt, tq)
            q = aq_ref[rr, qk_cols]
            if has_ctx:
                qf = q.astype(F32)
                cs_, sn_ = cos_ref[rr, :], sin_ref[rr, :]
            probs = []
            for c in range(2):
                if has_ctx:
                    qc = _rope(qf[:, c * DK_A:(c + 1) * DK_A], cs_, sn_).astype(BF16)
                    keys = kall_ref[c]
                else:
                    qc = q[:, c * DK_A:(c + 1) * DK_A]
                    keys = ak_ref[:, hh * 2 * DK_A + c * DK_A:hh * 2 * DK_A + (c + 1) * DK_A]
                s = _dot_nt(qc, keys) * scale
                e = jnp.exp(s - jnp.max(s, axis=-1, keepdims=True))
                probs.append(e * (1.0 / jnp.sum(e, axis=-1, keepdims=True)))
            a = (probs[0] - lam * probs[1]).astype(BF16)
            o = _dot(a, vall_ref[...] if has_ctx else av_ref[:, va_cols])
            oa_ref[rr, va_cols] = (_rms_f32(o) * (1.0 - lam_init)).astype(BF16)

        loop(seq // tq, attn_body)

        tables = []
        for d in range(2):
            lg = jax.nn.log_sigmoid(dl_ref[d, hh])
            diff = (ii - jj if d == 0 else jj - ii).astype(F32)
            mask = jnp.where(diff >= 0, jnp.exp(jnp.maximum(diff, 0.0) * lg), 0.0)
            if d == 0:
                qdec = jnp.exp((ic + 1.0) * lg)
                kdec = jnp.exp((cs - 1.0 - ic) * lg)
            else:
                qdec = jnp.exp((cs - ic) * lg)
                kdec = jnp.exp(ic * lg)
            tables.append((mask, qdec, kdec, jnp.exp(cs * lg)))
        st_refs = (stf_ref, stb_ref)
        o_refs = (of_ref, ob_ref)
        for d in range(2):
            if has_ctx:
                st_refs[d][hh] = s0_ref[d]
            else:
                st_refs[d][hh] = jnp.zeros((DK_B, DV_B), F32)

        def chunk_body(n, hh=hh, kb_cols=kb_cols, vb_cols=vb_cols, tables=tables):
            for d in range(2):
                mask, qdec, kdec, cdec = tables[d]
                rr = rows(n if d == 0 else nc - 1 - n, cs)
                qc = bq_ref[rr, kb_cols].astype(F32)
                kc = bk_ref[rr, kb_cols].astype(F32) * (DK_B ** -0.5)
                vc = bv_ref[rr, vb_cols]
                state = st_refs[d][hh]
                qk = _dot_nt(qc.astype(BF16), kc.astype(BF16)) * mask
                o_refs[d][rr, vb_cols] = (_dot(qk.astype(BF16), vc)
                                          + _dot((qc * qdec).astype(BF16), state.astype(BF16)))
                kd = (kc * kdec).T.astype(BF16)
                st_refs[d][hh] = cdec * state + _dot(kd, vc)

        loop(nc, chunk_body)

        def final_body(r, vb_cols=vb_cols):
            rr = rows(r, rt)
            tot = of_ref[rr, vb_cols] + ob_ref[rr, vb_cols]
            gate = _silu(bg_ref[rr, vb_cols].astype(F32))
            or_ref[rr, vb_cols] = (_rms_f32(tot) * gate).astype(BF16)

        loop(seq // rt, final_body)
        if not has_ctx:
            for d in range(2):
                sout_ref[d, hh] = st_refs[d][hh]


def _mixer(pa, pb, lam_q, lam_k, decay_logit, lam_init, batch, seq, hps, ctx=None):
    has_ctx = ctx is not None
    past = ctx[0].shape[1] if has_ctx else 0

    def col(head_width, off):
        width = hps * head_width
        return pl.BlockSpec((seq, width), lambda b, hg: (b, off // width + hg))

    a_qk = H_A * 2 * DK_A
    b_qk = H_B * DK_B
    b_v = H_B * DV_B
    in_specs = [
        col(2 * DK_A, 0), col(2 * DK_A, a_qk), col(DV_A, 2 * a_qk),
        col(DK_B, 0), col(DK_B, b_qk), col(DV_B, 2 * b_qk), col(DV_B, 2 * b_qk + b_v),
        pl.BlockSpec((2, DK_A), lambda b, hg: (0, 0)),
        pl.BlockSpec((2, DK_A), lambda b, hg: (0, 0)),
        pl.BlockSpec((2, hps, 1, 1), lambda b, hg: (0, hg, 0, 0)),
    ]
    args = [pa] * 3 + [pb] * 4 + [lam_q, lam_k, decay_logit.reshape(2, H_B, 1, 1)]
    out_specs = [
        pl.BlockSpec((seq, hps * DV_A), lambda b, hg: (b, hg)),
        pl.BlockSpec((seq, hps * DV_B), lambda b, hg: (b, hg)),
    ]
    out_shape = [
        jax.ShapeDtypeStruct((batch * seq, H_A * DV_A), BF16),
        jax.ShapeDtypeStruct((batch * seq, b_v), BF16),
    ]
    scratch = []
    if has_ctx:
        assert hps == 1
        ck, cv, s0, cos_t, sin_t = ctx
        in_specs += [
            pl.BlockSpec((None, past, 2 * DK_A), lambda b, hg: (b, 0, hg)),
            pl.BlockSpec((None, past, DV_A), lambda b, hg: (b, 0, hg)),
            pl.BlockSpec((None, 2, None, DK_B, DV_B), lambda b, hg: (b, 0, hg, 0, 0)),
            pl.BlockSpec((seq, DK_A), lambda b, hg: (0, 0)),
            pl.BlockSpec((seq, DK_A), lambda b, hg: (0, 0)),
        ]
        args += [ck, cv, s0, cos_t, sin_t]
        scratch += [pltpu.VMEM((2, seq + past, DK_A), BF16), pltpu.VMEM((seq + past, DV_A), BF16)]
    else:
        out_specs.append(pl.BlockSpec((None, 2, hps, DK_B, DV_B), lambda b, hg: (b, 0, hg, 0, 0)))
        out_shape.append(jax.ShapeDtypeStruct((batch, 2, H_B, DK_B, DV_B), F32))
    scratch += [
        pltpu.VMEM((seq, hps * DV_B), F32),
        pltpu.VMEM((seq, hps * DV_B), F32),
        pltpu.VMEM((hps, DK_B, DV_B), F32),
        pltpu.VMEM((hps, DK_B, DV_B), F32),
    ]
    return pl.pallas_call(
        functools.partial(_mixer_kernel, seq=seq, has_ctx=has_ctx, lam_init=lam_init, past=past, hps=hps),
        grid=(batch, H_A // hps),
        in_specs=in_specs,
        out_specs=out_specs,
        out_shape=out_shape,
        scratch_shapes=scratch,
        compiler_params=_params("arbitrary", "arbitrary"),
    )(*args)


def _out_kernel(*refs, seg_tiles, n_lhs, part, n_parts):
    n_seg = len(seg_tiles)
    per_seg = n_lhs + 1
    seg_refs = [refs[s * per_seg:(s + 1) * per_seg] for s in range(n_seg)]
    rest = refs[n_seg * per_seg:]
    last = part == n_parts - 1
    if last:
        gate_ref, g_ref, sh_ref, sc_ref, rw_ref = rest[:5]
        prev_refs = rest[5:5 + part]
        w_hbm, x1_ref, h2_ref, lg_ref, slab_ref, stage_ref, sem = rest[5 + part:]
    else:
        gate_ref, w_hbm, x1_ref, slab_ref, stage_ref, sem = rest
    i = pl.program_id(0)
    dn = slab_ref.shape[1]

    @pl.when(i == 0)
    def _():
        _load_weight_slab(w_hbm, slab_ref, stage_ref, sem, part * dn)

    def run(lhs_refs, x_ref):
        acc = None
        r0 = 0
        for lhs_ref in lhs_refs:
            kk = lhs_ref.shape[1]
            prod = _dot(lhs_ref[...], slab_ref[r0:r0 + kk, :])
            acc = prod if acc is None else acc + prod
            r0 += kk
        x1 = x_ref[...] + gate_ref[...] * acc
        x1_ref[...] = x1
        if last:
            cols = [p_ref[...] for p_ref in prev_refs] + [x1]
            d = dn * n_parts
            ms = sum(jnp.sum(v * v, axis=-1, keepdims=True) for v in cols) / d
            inv = lax.rsqrt(ms + EPS)
            lg = None
            for j, v in enumerate(cols):
                cs_ = slice(j * dn, (j + 1) * dn)
                h2 = ((v * inv) * g_ref[:, cs_]) * (1.0 + sc_ref[:, cs_]) + sh_ref[:, cs_]
                _store_token_major(h2_ref, h2, j * dn)
                pl_ = lax.dot_general(rw_ref[:, cs_], h2, (((1,), (1,)), ((), ())),
                                      precision=lax.Precision.HIGHEST, preferred_element_type=F32)
                lg = pl_ if lg is None else lg + pl_
            lg_ref[...] = lg

    if n_seg == 1:
        run(seg_refs[0][:n_lhs], seg_refs[0][n_lhs])
    else:
        lo = 0
        for s in range(n_seg):
            @pl.when((i >= lo) & (i < lo + seg_tiles[s]))
            def _(s=s):
                run(seg_refs[s][:n_lhs], seg_refs[s][n_lhs])

            lo += seg_tiles[s]


def _out_proj(segments, w, g, mod3, layer, row_fn, router_wt, n_parts=2, tm=512, stage_rows=256):
    n_lhs = len(segments[0][0])
    d = segments[0][1].shape[1]
    dn = d // n_parts
    kdim = w.shape[0]
    base = layer * 48
    ne = router_wt.shape[0]
    seg_tiles = tuple(x.shape[0] // tm for _, x in segments)
    t_total = sum(x.shape[0] for _, x in segments)
    rpt = d // LANES
    x1_parts = []
    for part in range(n_parts):
        last = part == n_parts - 1
        in_specs, args = [], []
        lo = 0
        for (lhs_list, x), nt in zip(segments, seg_tiles):
            def seg_row(i, lo=lo, nt=nt):
                return jnp.clip(i - lo, 0, nt - 1)

            for lhs in lhs_list:
                in_specs.append(pl.BlockSpec((tm, lhs.shape[1]), lambda i, f=seg_row: (f(i), 0)))
                args.append(lhs)
            in_specs.append(pl.BlockSpec((tm, dn), lambda i, f=seg_row: (f(i), part)))
            args.append(x)
            lo += nt
        in_specs.append(pl.BlockSpec((None, 1, dn), lambda i: (base + row_fn(i) * 6 + 2, 0, part)))
        args.append(mod3)
        out_specs = [pl.BlockSpec((tm, dn), lambda i: (i, 0))]
        out_shape = [jax.ShapeDtypeStruct((t_total, dn), F32)]
        if last:
            in_specs += [
                pl.BlockSpec((1, d), lambda i: (0, 0)),
                pl.BlockSpec((None, 1, d), lambda i: (base + row_fn(i) * 6 + 3, 0, 0)),
                pl.BlockSpec((None, 1, d), lambda i: (base + row_fn(i) * 6 + 4, 0, 0)),
                pl.BlockSpec((ne, d), lambda i: (0, 0)),
            ] + [pl.BlockSpec((tm, dn), lambda i: (i, 0))] * part
            args += [g.reshape(1, d), mod3, mod3, router_wt] + x1_parts
            out_specs += [
                pl.BlockSpec((tm * rpt, LANES), lambda i: (i, 0)),
                pl.BlockSpec((ne, tm), lambda i: (0, i)),
            ]
            out_shape += [
                jax.ShapeDtypeStruct((t_total * rpt, LANES), F32),
                jax.ShapeDtypeStruct((ne, t_total), F32),
            ]
        in_specs.append(pl.BlockSpec(memory_space=pl.ANY))
        args.append(w)
        outs = pl.pallas_call(
            functools.partial(_out_kernel, seg_tiles=seg_tiles, n_lhs=n_lhs, part=part, n_parts=n_parts),
            grid=(t_total // tm,),
            in_specs=in_specs,
            out_specs=out_specs,
            out_shape=out_shape,
            scratch_shapes=[
                pltpu.VMEM((kdim, dn), BF16),
                pltpu.VMEM((2, stage_rows, dn), F32),
                pltpu.SemaphoreType.DMA((2,)),
            ],
            compiler_params=_params("arbitrary"),
        )(*args)
        x1_parts.append(outs[0])
    return x1_parts, outs[1], outs[2]


def _first_index(vals, target):
    idx = jnp.full(target.shape, len(vals) - 1, jnp.int32)
    for k in range(len(vals) - 2, -1, -1):
        idx = jnp.where(vals[k] == target, k, idx)
    return idx


def _pick(vals, idx):
    out = vals[-1]
    for k in range(len(vals) - 2, -1, -1):
        out = jnp.where(idx == k, vals[k], out)
    return out


def _route_kernel(lg_ref, rb_ref, pos_ref, wt_ref, meta_ref, rank_ref, *, tile_rows, scan):
    logits = lg_ref[...]
    t = logits.shape[1]
    scores = jax.nn.sigmoid(logits)
    biased = scores + rb_ref[...]
    s_rows = [scores[e:e + 1, :] for e in range(N_EXPERTS)]
    b_rows = [biased[e:e + 1, :] for e in range(N_EXPERTS)]
    grp = []
    for g in range(N_GROUPS):
        r = b_rows[g * GROUP_SIZE:(g + 1) * GROUP_SIZE]
        best = None
        for a in range(GROUP_SIZE):
            for b in range(a + 1, GROUP_SIZE):
                pair = r[a] + r[b]
                best = pair if best is None else jnp.maximum(best, pair)
        grp.append(best)
    gmax = functools.reduce(jnp.maximum, grp)
    gsel = _first_index(grp, gmax)
    cand_b = [_pick([b_rows[g * GROUP_SIZE + k] for g in range(N_GROUPS)], gsel) for k in range(GROUP_SIZE)]
    cand_s = [_pick([s_rows[g * GROUP_SIZE + k] for g in range(N_GROUPS)], gsel) for k in range(GROUP_SIZE)]
    m1 = functools.reduce(jnp.maximum, cand_b)
    k1 = _first_index(cand_b, m1)
    rest = [jnp.where(k1 == k, -jnp.inf, cand_b[k]) for k in range(GROUP_SIZE)]
    m2 = functools.reduce(jnp.maximum, rest)
    k2 = _first_index(rest, m2)
    w1 = _pick(cand_s, k1)
    w2 = _pick(cand_s, k2)
    wsum = w1 + w2
    w1 = w1 / wsum
    w2 = w2 / wsum
    e1 = gsel * GROUP_SIZE + k1
    e2 = gsel * GROUP_SIZE + k2
    wt_ref[...] = jnp.zeros_like(wt_ref)
    wt_ref[0:1, :] = w1
    wt_ref[1:2, :] = w2

    eid = lax.broadcasted_iota(jnp.int32, logits.shape, 0)
    sel1 = eid == e1
    sel2 = eid == e2
    ind = jnp.where(sel1 | sel2, 1.0, 0.0)
    ua = lax.broadcasted_iota(jnp.int32, (scan, scan), 0)
    ub = lax.broadcasted_iota(jnp.int32, (scan, scan), 1)
    upper = jnp.where(ua <= ub, 1.0, 0.0).astype(BF16)
    carry = jnp.zeros((N_EXPERTS, 1), F32)
    for blk in range(t // scan):
        seg = ind[:, blk * scan:(blk + 1) * scan]
        incl = _dot(seg.astype(BF16), upper)
        rank_ref[:, blk * scan:(blk + 1) * scan] = incl - seg + carry
        carry = carry + incl[:, scan - 1:scan]
    n_tiles = jnp.floor((carry + (tile_rows - 1.0)) * (1.0 / tile_rows))
    la = lax.broadcasted_iota(jnp.int32, (N_EXPERTS, N_EXPERTS), 0)
    lb = lax.broadcasted_iota(jnp.int32, (N_EXPERTS, N_EXPERTS), 1)
    lower = jnp.where(lb < la, 1.0, 0.0).astype(BF16)
    start = _dot(lower, jnp.broadcast_to(n_tiles, (N_EXPERTS, LANES)).astype(BF16))
    end = start + n_tiles
    slot = start[:, 0:1] * tile_rows + rank_ref[...]
    pos_ref[...] = jnp.zeros_like(pos_ref)
    pos_ref[0:1, :] = jnp.sum(jnp.where(sel1, slot, 0.0), axis=0, keepdims=True).astype(jnp.int32)
    pos_ref[1:2, :] = jnp.sum(jnp.where(sel2, slot, 0.0), axis=0, keepdims=True).astype(jnp.int32)

    n_used = end[N_EXPERTS - 1:N_EXPERTS, :]
    tile = lax.broadcasted_iota(jnp.int32, (N_EXPERTS, LANES), 1).astype(F32)
    tile = jnp.minimum(tile, n_used - 1.0)
    tile_expert = jnp.sum(jnp.where(tile >= end, 1.0, 0.0), axis=0, keepdims=True)
    lane = lax.broadcasted_iota(jnp.int32, (1, LANES), 1)
    meta_ref[...] = jnp.zeros_like(meta_ref)
    meta_ref[0:1, :] = jnp.where(lane == LANES - 1, n_used, tile_expert).astype(jnp.int32)
    diag = lax.broadcasted_iota(jnp.int32, (N_EXPERTS, LANES), 0) == lax.broadcasted_iota(
        jnp.int32, (N_EXPERTS, LANES), 1)
    meta_ref[1:2, :] = jnp.sum(jnp.where(diag, carry, 0.0), axis=0, keepdims=True).astype(jnp.int32)
    meta_ref[2:3, :] = jnp.sum(jnp.where(diag, start * tile_rows, 0.0), axis=0, keepdims=True).astype(jnp.int32)


def _route(logits_t, router_b, tile_rows, scan=512):
    ne, t = logits_t.shape
    return pl.pallas_call(
        functools.partial(_route_kernel, tile_rows=tile_rows, scan=scan),
        out_shape=[
            jax.ShapeDtypeStruct((8, t), jnp.int32),
            jax.ShapeDtypeStruct((8, t), F32),
            jax.ShapeDtypeStruct((8, LANES), jnp.int32),
        ],
        scratch_shapes=[pltpu.VMEM((ne, t), F32)],
        compiler_params=pltpu.CompilerParams(vmem_limit_bytes=VMEM_LIMIT_BYTES),
    )(logits_t, router_b.reshape(ne, 1))


def _dispatch_kernel(pos_ref, meta_ref, h_ref, xs_ref, zero_ref, sem, *, t, rpt, chunk, tile_rows, n_tiles):
    c = pl.program_id(0)

    def slot_rows(slot):
        return xs_ref.at[pl.ds(pl.multiple_of(slot * rpt, rpt), rpt)]

    def body(r, carry):
        tok = c * chunk + r
        src = h_ref.at[pl.ds(pl.multiple_of(r * rpt, rpt), rpt)]
        for k in range(2):
            pltpu.make_async_copy(src, slot_rows(pos_ref[k * t + tok]), sem.at[0]).start(priority=k)
        return carry

    lax.fori_loop(0, chunk, body, 0, unroll=8)

    @pl.when(c == 0)
    def _():
        zero_ref[...] = jnp.zeros_like(zero_ref)
        zero_row = zero_ref.at[pl.ds(0, rpt)]
        for e in range(N_EXPERTS):
            count = meta_ref[LANES + e]
            first = meta_ref[2 * LANES + e] + count
            n_pad = (tile_rows - count % tile_rows) % tile_rows

            def pad_start(r, carry, first=first):
                pltpu.make_async_copy(zero_row, slot_rows(first + r), sem.at[1]).start()
                return carry

            def pad_wait(r, carry, first=first):
                pltpu.make_async_copy(zero_row, slot_rows(first + r), sem.at[1]).wait()
                return carry

            lax.fori_loop(0, n_pad, pad_start, 0)
            lax.fori_loop(0, n_pad, pad_wait, 0)

        def tile_copy(n):
            rows = tile_rows * rpt
            return pltpu.make_async_copy(zero_ref, xs_ref.at[pl.ds(pl.multiple_of(n * rows, rows), rows)],
                                         sem.at[1])

        def unused_start(n, carry):
            tile_copy(n).start()
            return carry

        def unused_wait(n, carry):
            tile_copy(n).wait()
            return carry

        lax.fori_loop(meta_ref[LANES - 1], n_tiles, unused_start, 0)
        lax.fori_loop(meta_ref[LANES - 1], n_tiles, unused_wait, 0)

    for _ in range(2):
        pltpu.make_async_copy(h_ref, xs_ref.at[pl.ds(0, chunk * rpt)], sem.at[0]).wait()


def _dispatch(pos_flat, meta_flat, h_lin, n_slots, rpt, tile_rows, chunk=512):
    t = pos_flat.shape[0] // 2
    return pl.pallas_call(
        functools.partial(_dispatch_kernel, t=t, rpt=rpt, chunk=chunk, tile_rows=tile_rows,
                          n_tiles=n_slots // tile_rows),
        grid_spec=pltpu.PrefetchScalarGridSpec(
            num_scalar_prefetch=2,
            grid=(t // chunk,),
            in_specs=[pl.BlockSpec((chunk * rpt, LANES), lambda c, p, m: (c, 0))],
            out_specs=pl.BlockSpec(memory_space=pl.ANY),
            scratch_shapes=[pltpu.VMEM((tile_rows * rpt, LANES), h_lin.dtype), pltpu.SemaphoreType.DMA((2,))],
        ),
        out_shape=jax.ShapeDtypeStruct((n_slots * rpt, LANES), h_lin.dtype),
        compiler_params=_params("arbitrary"),
    )(pos_flat, meta_flat, h_lin)


def _ffn_kernel(meta_ref, xs_ref, wg_ref, wu_ref, wd_ref, ys_ref, wgb_ref, wub_ref, wdb_ref, *, tm):
    n = pl.program_id(0)
    n_used = meta_ref[LANES - 1]

    @pl.when(n < n_used)
    def _():
        @pl.when((n == 0) | (meta_ref[n] != meta_ref[jnp.maximum(n - 1, 0)]))
        def _():
            wgb_ref[...] = wg_ref[...].astype(BF16)
            wub_ref[...] = wu_ref[...].astype(BF16)
            wdb_ref[...] = wd_ref[...].astype(BF16)

        d = wgb_ref.shape[0]
        a = b = None
        for k0 in range(0, d, MXU_DIM):
            x = jnp.concatenate(
                [_load_token_major(xs_ref, tm, k0 // LANES + j).astype(BF16) for j in range(MXU_DIM // LANES)],
                axis=1)
            pa = _dot(x, wgb_ref[k0:k0 + MXU_DIM, :])
            pb = _dot(x, wub_ref[k0:k0 + MXU_DIM, :])
            a = pa if a is None else a + pa
            b = pb if b is None else b + pb
        act = (_silu(a) * b).astype(BF16)
        for n0 in range(0, d, MXU_DIM):
            _store_token_major(ys_ref, _dot(act, wdb_ref[:, n0:n0 + MXU_DIM]), n0)

    @pl.when(n >= n_used)
    def _():
        ys_ref[...] = jnp.zeros_like(ys_ref)


def _ffn(meta, xs, w_gate, w_up, w_down, layer, tm, n_tiles):
    _, ne, d, f = w_gate.shape
    rpt = d // LANES
    rpt_in = xs.shape[0] // (n_tiles * tm)

    def row_tile(n, m):
        return (jnp.minimum(n, m[LANES - 1] - 1), 0)

    return pl.pallas_call(
        functools.partial(_ffn_kernel, tm=tm),
        grid_spec=pltpu.PrefetchScalarGridSpec(
            num_scalar_prefetch=1,
            grid=(n_tiles,),
            in_specs=[
                pl.BlockSpec((tm * rpt_in, LANES), row_tile),
                pl.BlockSpec((None, None, d, f), lambda n, m: (layer, m[n], 0, 0)),
                pl.BlockSpec((None, None, d, f), lambda n, m: (layer, m[n], 0, 0)),
                pl.BlockSpec((None, None, f, d), lambda n, m: (layer, m[n], 0, 0)),
            ],
            out_specs=pl.BlockSpec((tm * rpt, LANES), lambda n, m: (n, 0)),
            scratch_shapes=[
                pltpu.VMEM((d, f), BF16),
                pltpu.VMEM((d, f), BF16),
                pltpu.VMEM((f, d), BF16),
            ],
        ),
        out_shape=jax.ShapeDtypeStruct((n_tiles * tm * rpt, LANES), F32),
        compiler_params=_params("arbitrary"),
    )(meta, xs, w_gate, w_up, w_down)


def _combine_kernel(pos_ref, ys_ref, *refs, t, rpt, tm, n_parts):
    x_refs = refs[:n_parts]
    wt_ref, gate_ref, o_ref, a0, b0, a1, b1, sem = refs[n_parts:]
    i = pl.program_id(0)
    n = pl.num_programs(0)
    bufs = ((a0, b0), (a1, b1))

    def issue(tile, s):
        def body(r, carry):
            tok = tile * tm + r
            for k in range(2):
                slot_row = pos_ref[k * t + tok]
                pltpu.make_async_copy(
                    ys_ref.at[pl.ds(pl.multiple_of(slot_row * rpt, rpt), rpt)],
                    bufs[s][k].at[pl.ds(pl.multiple_of(r * rpt, rpt), rpt)],
                    sem.at[s]).start(priority=k)
            return carry

        lax.fori_loop(0, tm, body, 0, unroll=8)

    def wait(s):
        for k in range(2):
            pltpu.make_async_copy(ys_ref.at[pl.ds(0, tm * rpt)], bufs[s][k], sem.at[s]).wait()

    def compute(s):
        w = wt_ref[...]
        w0, w1 = w[:, 0:1], w[:, 1:2]
        dn = x_refs[0].shape[1]
        for c in range(rpt):
            cols = slice(c * LANES, (c + 1) * LANES)
            part, pc = divmod(c * LANES, dn)
            y = w0 * _load_token_major(bufs[s][0], tm, c) + w1 * _load_token_major(bufs[s][1], tm, c)
            o_ref[:, cols] = x_refs[part][:, pc:pc + LANES] + gate_ref[:, cols] * y

    @pl.when(i == 0)
    def _():
        issue(0, 0)

    for s in range(2):
        @pl.when(i % 2 == s)
        def _(s=s):
            @pl.when(i + 1 < n)
            def _():
                issue(i + 1, 1 - s)

            wait(s)
            compute(s)


def _combine(pos_flat, ys, x1_parts, wt, mod3, layer, row_fn, tm=256):
    t, dn = x1_parts[0].shape
    n_parts = len(x1_parts)
    d = dn * n_parts
    rpt = d // LANES
    base = layer * 48
    return pl.pallas_call(
        functools.partial(_combine_kernel, t=t, rpt=rpt, tm=tm, n_parts=n_parts),
        grid_spec=pltpu.PrefetchScalarGridSpec(
            num_scalar_prefetch=1,
            grid=(t // tm,),
            in_specs=[pl.BlockSpec(memory_space=pl.ANY)]
            + [pl.BlockSpec((tm, dn), lambda i, p: (i, 0))] * n_parts
            + [
                pl.BlockSpec((tm, 2), lambda i, p: (i, 0)),
                pl.BlockSpec((None, 1, d), lambda i, p: (base + row_fn(i) * 6 + 5, 0, 0)),
            ],
            out_specs=pl.BlockSpec((tm, d), lambda i, p: (i, 0)),
            scratch_shapes=[pltpu.VMEM((tm * rpt, LANES), F32)] * 4 + [pltpu.SemaphoreType.DMA((2,))],
        ),
        out_shape=jax.ShapeDtypeStruct((t, d), F32),
        compiler_params=_params("arbitrary"),
    )(pos_flat, ys, *x1_parts, wt, mod3)


def _moe(lg, router_b, h_lin, x1, w_gate, w_up, w_down, mod3, layer, row_fn, tile_rows=256):
    t = x1[0].shape[0]
    d = sum(p.shape[1] for p in x1)
    n_tiles = (2 * t) // tile_rows + N_EXPERTS
    pos, wts, meta = _route(lg, router_b, tile_rows)
    pos_flat = pos[:2].reshape(-1)
    xs = _dispatch(pos_flat, meta[:3].reshape(-1), h_lin, n_tiles * tile_rows, h_lin.shape[0] // t, tile_rows)
    ys = _ffn(meta[0], xs, w_gate, w_up, w_down, layer, tile_rows, n_tiles)
    return _combine(pos_flat, ys, x1, wts[:2].T, mod3, layer, row_fn)


def _final_kernel(x_ref, g_ref, o_ref):
    o_ref[...] = _rms_f32(x_ref[...]) * g_ref[...]


def _final_norm(x, g, row0, rows, tm=512):
    d = x.shape[1]
    off = row0 // tm
    return pl.pallas_call(
        _final_kernel,
        grid=(rows // tm,),
        in_specs=[
            pl.BlockSpec((tm, d), lambda i: (i + off, 0)),
            pl.BlockSpec((1, d), lambda i: (0, 0)),
        ],
        out_specs=pl.BlockSpec((tm, d), lambda i: (i, 0)),
        out_shape=jax.ShapeDtypeStruct((rows, d), F32),
        compiler_params=_params("arbitrary"),
    )(x, g.reshape(1, d))


def _sgu_kernel(u0_ref, u1_ref, v0_ref, v1_ref, ng_ref, ws_ref, bs_ref, o0_ref, o1_ref, *, n_chunks):
    half = v0_ref.shape[1]
    gw = 2 * half // H_C
    v0 = v0_ref[...].astype(F32)
    v1 = v1_ref[...].astype(F32)
    ms = (jnp.sum(v0 * v0, axis=-1, keepdims=True) + jnp.sum(v1 * v1, axis=-1, keepdims=True)) / (2 * half)
    inv = lax.rsqrt(ms + EPS)
    halves = ((u0_ref, (v0 * inv * ng_ref[:, :half]).astype(BF16), o0_ref),
              (u1_ref, (v1 * inv * ng_ref[:, half:]).astype(BF16), o1_ref))
    for n in range(n_chunks):
        r0 = n * CHUNK_C
        for h in range(H_C):
            u_ref, vn, o_ref = halves[h // (H_C // 2)]
            c0 = (h % (H_C // 2)) * gw
            mixed = _dot(ws_ref[h].astype(BF16), vn[r0:r0 + CHUNK_C, c0:c0 + gw]) + bs_ref[h]
            u = u_ref[r0:r0 + CHUNK_C, c0:c0 + gw].astype(F32)
            o_ref[r0:r0 + CHUNK_C, c0:c0 + gw] = (u * mixed).astype(BF16)


def _sgu(u0, u1, v0, v1, norm_g, w_s, b_s, tm=256):
    t, half = u0.shape
    blk = pl.BlockSpec((tm, half), lambda i: (i, 0))
    return pl.pallas_call(
        functools.partial(_sgu_kernel, n_chunks=tm // CHUNK_C),
        grid=(t // tm,),
        in_specs=[
            blk, blk, blk, blk,
            pl.BlockSpec((1, 2 * half), lambda i: (0, 0)),
            pl.BlockSpec((H_C, CHUNK_C, CHUNK_C), lambda i: (0, 0, 0)),
            pl.BlockSpec((H_C, CHUNK_C, 1), lambda i: (0, 0, 0)),
        ],
        out_specs=[blk, blk],
        out_shape=[jax.ShapeDtypeStruct((t, half), BF16)] * 2,
        compiler_params=_params("arbitrary"),
    )(u0, u1, v0, v1, norm_g.reshape(1, 2 * half), w_s, b_s.reshape(H_C, CHUNK_C, 1))


def _rope_tables(n_tok):
    rows = n_tok // GRID_W
    pos_r = jnp.repeat(jnp.arange(rows), GRID_W).astype(F32)
    pos_c = jnp.tile(jnp.arange(GRID_W), rows).astype(F32)
    n_freq = DK_A // 4
    inv = jnp.power(ROPE_BASE, -jnp.arange(n_freq, dtype=F32) / n_freq)
    ang_r = pos_r[:, None] * inv
    ang_c = pos_c[:, None] * inv
    cos_t = jnp.concatenate([jnp.cos(ang_r), jnp.cos(ang_r), jnp.cos(ang_c), jnp.cos(ang_c)], axis=1)
    sin_t = jnp.concatenate([-jnp.sin(ang_r), jnp.sin(ang_r), -jnp.sin(ang_c), jnp.sin(ang_c)], axis=1)
    return cos_t, sin_t


def kernel(x_prompt, x_sample, cache_k, cache_v, state_ret, c, c_ctx, norm_g, w_mod, b_mod, w_in_even, w_out_even, lam_q, lam_k, ret_decay_logit, w_in_odd, gmlp_norm_g, w_spatial, b_spatial, w_out_odd, router_w, router_b, w_gate, w_up, w_down, final_norm_g):
    batch, seq, d = x_prompt.shape
    dbatch, dseq, _ = x_sample.shape
    past = cache_k.shape[2]
    depth = norm_g.shape[0]
    t_ctx, t_lat = batch * seq, dbatch * dseq
    t_all = t_ctx + t_lat
    xc = x_prompt.reshape(t_ctx, d)
    xl = x_sample.reshape(t_lat, d)

    cond8 = jnp.concatenate([c_ctx[None, :], c, jnp.zeros((8 - 1 - dbatch, d), F32)], axis=0)
    mod3 = _modulation(cond8, w_mod, b_mod).reshape(depth * 8 * 6, 1, d)
    router_wt = router_w.T
    cos_t, sin_t = _rope_tables(dseq)

    tm_in = 512
    tm_out = 512
    tm_cmb = 256
    w_in0 = w_in_even.reshape(d, -1)
    w_out0 = w_out_even.reshape(-1, d)
    w_in1 = w_in_odd.reshape(d, -1)
    w_out1 = w_out_odd.reshape(-1, d)
    lq, lk, dlog = lam_q.reshape(2, DK_A), lam_k.reshape(2, DK_A), ret_decay_logit.reshape(2, H_B)

    def rows_ctx(i):
        return 0

    def rows_lat(tiles_per_batch):
        return lambda i: 1 + i // tiles_per_batch

    def rows_all(tm):
        n_ctx, per = t_ctx // tm, dseq // tm
        return lambda i: jnp.maximum(i - n_ctx, 0) // per + jnp.where(i >= n_ctx, 1, 0)

    lam_init = 0.8 - 0.6 * math.exp(-0.3 * 0)
    half0 = w_in0.shape[1] // 2
    g00 = norm_g[0, 0]
    rl = rows_lat(dseq // tm_in)
    pa_c, k_new, v_new = _in_proj(xc, g00, mod3, 0, rows_ctx, w_in0, 0, half0, "even_kv", tm_in)
    (pb_c,) = _in_proj(xc, g00, mod3, 0, rows_ctx, w_in0, half0, half0, "even", tm_in)
    (pa_l,) = _in_proj(xl, g00, mod3, 0, rl, w_in0, 0, half0, "even", tm_in)
    (pb_l,) = _in_proj(xl, g00, mod3, 0, rl, w_in0, half0, half0, "even", tm_in)
    oa_c, or_c, s_new = _mixer(pa_c, pb_c, lq, lk, dlog, lam_init, batch, seq, H_A)
    ctx = (cache_k.reshape(dbatch, past, H_A * 2 * DK_A), cache_v.reshape(dbatch, past, H_A * DV_A),
           state_ret.reshape(dbatch, 2, H_B, DK_B, DV_B), cos_t, sin_t)
    oa_l, or_l = _mixer(pa_l, pb_l, lq, lk, dlog, lam_init, dbatch, dseq, 1, ctx)
    x1, h2, lg = _out_proj([([oa_c, or_c], xc), ([oa_l, or_l], xl)], w_out0, norm_g[0, 1], mod3, 0,
                           rows_all(tm_out), router_wt, tm=tm_out)
    x2 = _moe(lg, router_b, h2, x1, w_gate, w_up, w_down, mod3, 0, rows_all(tm_cmb))

    q1 = w_in1.shape[1] // 4
    zs = [_in_proj(x2, norm_g[1, 0], mod3, 1, rows_all(tm_in), w_in1, j * q1, q1, "odd", tm_in)[0]
          for j in range(4)]
    g0, g1 = _sgu(zs[0], zs[1], zs[2], zs[3], gmlp_norm_g.reshape(-1),
                  w_spatial.reshape(H_C, CHUNK_C, CHUNK_C), b_spatial.reshape(H_C, CHUNK_C))
    x3, h4, lg2 = _out_proj([([g0, g1], x2)], w_out1, norm_g[1, 1], mod3, 1, rows_all(tm_out), router_wt,
                            tm=tm_out)
    x4 = _moe(lg2, router_b, h4, x3, w_gate, w_up, w_down, mod3, 1, rows_all(tm_cmb))
    y_p = _final_norm(x4, final_norm_g, 0, t_ctx)
    y_s = _final_norm(x4, final_norm_g, t_ctx, t_lat)

    y_prompt = y_p.reshape(batch, seq, d)
    y_sample = y_s.reshape(dbatch, dseq, d)
    new_cache_k = k_new.reshape(batch, 1, seq, H_A, 2, DK_A)
    new_cache_v = v_new.reshape(batch, 1, seq, H_A, DV_A)
    new_state_ret = s_new.reshape(batch, 1, 2, H_B, DK_B, DV_B)
    return (y_prompt, y_sample, new_cache_k, new_cache_v, new_state_ret)
```

```python
import functools
import math

import jax
import jax.numpy as jnp
from jax import lax
from jax.experimental import pallas as pl
from jax.experimental.pallas import tpu as pltpu

F32 = jnp.float32
BF16 = jnp.bfloat16

EPS = 1e-6
ROPE_BASE = 10000.0
GRID_W = 64
H_A, DK_A, DV_A = 4, 128, 256
H_B, DK_B, DV_B = 4, 128, 256
RET_CHUNK = 128
H_C, CHUNK_C = 8, 128
N_EXPERTS, N_GROUPS, GROUP_SIZE = 16, 4, 4

VMEM_LIMIT_BYTES = 56 * 1024 * 1024


def _params(*sem):
    return pltpu.CompilerParams(dimension_semantics=sem, vmem_limit_bytes=VMEM_LIMIT_BYTES)


def _rms_f32(x):
    return x * lax.rsqrt(jnp.mean(x * x, axis=-1, keepdims=True) + EPS)


def _silu(x):
    return x * jax.nn.sigmoid(x)


def _dot(a, b):
    return jnp.dot(a, b, preferred_element_type=F32)


def _dot_nt(a, b):
    return lax.dot_general(a, b, (((1,), (1,)), ((), ())), preferred_element_type=F32)


LANES = 128
MXU_DIM = 256


def _store_token_major(ref, val, col0=0):
    tm, width = val.shape
    rpt = ref.shape[0] // tm
    for c in range(width // LANES):
        ref[pl.ds(col0 // LANES + c, tm, stride=rpt), :] = val[:, c * LANES:(c + 1) * LANES]


def _load_token_major(ref, tm, c):
    rpt = ref.shape[0] // tm
    return ref[pl.ds(c, tm, stride=rpt), :]


def _mod_kernel(c_ref, w_ref, b_ref, o_ref):
    s = _silu(c_ref[...]).astype(BF16)
    o_ref[...] = _dot(s, w_ref[...].astype(BF16)) + b_ref[...]


def _modulation(cond8, w_mod, b_mod, tn=1024):
    depth, d, n = w_mod.shape
    return pl.pallas_call(
        _mod_kernel,
        grid=(depth, n // tn),
        in_specs=[
            pl.BlockSpec((8, d), lambda l, j: (0, 0)),
            pl.BlockSpec((None, d, tn), lambda l, j: (l, 0, j)),
            pl.BlockSpec((None, 1, tn), lambda l, j: (l, 0, j)),
        ],
        out_specs=pl.BlockSpec((None, 8, tn), lambda l, j: (l, 0, j)),
        out_shape=jax.ShapeDtypeStruct((depth, 8, n), F32),
        compiler_params=_params("arbitrary", "arbitrary"),
    )(cond8, w_mod, b_mod.reshape(depth, 1, n))


def _load_weight_slab(w_hbm, slab_ref, stage_ref, sem, col0):
    kdim, n = slab_ref.shape
    rows = stage_ref.shape[1]
    nb = kdim // rows

    def copy(j):
        return pltpu.make_async_copy(w_hbm.at[pl.ds(j * rows, rows), pl.ds(col0, n)],
                                     stage_ref.at[j % 2], sem.at[j % 2])

    copy(0).start()
    for j in range(nb):
        if j + 1 < nb:
            copy(j + 1).start()
        copy(j).wait()
        slab_ref[j * rows:(j + 1) * rows, :] = stage_ref[j % 2].astype(BF16)


def _in_kernel(x_ref, g_ref, sh_ref, sc_ref, w_hbm, *rest, col0, mode, kv_cols, chunk):
    if mode == "even_kv":
        p_ref, k_ref, v_ref, slab_ref, stage_ref, h_ref, sem = rest
    else:
        p_ref, slab_ref, stage_ref, h_ref, sem = rest

    @pl.when(pl.program_id(0) == 0)
    def _():
        _load_weight_slab(w_hbm, slab_ref, stage_ref, sem, col0)

    h = (_rms_f32(x_ref[...]) * g_ref[...]) * (1.0 + sc_ref[...]) + sh_ref[...]
    h_ref[...] = h.astype(BF16)
    n = slab_ref.shape[1]
    for c0 in range(0, n, chunk):
        acc = _dot(h_ref[...], slab_ref[:, c0:c0 + chunk])
        if mode == "odd":
            p_ref[:, c0:c0 + chunk] = (0.5 * acc * (1.0 + lax.erf(acc * (2.0 ** -0.5)))).astype(BF16)
        else:
            p_ref[:, c0:c0 + chunk] = acc.astype(BF16)
        if mode == "even_kv":
            (k0, k1), (v0, v1) = kv_cols
            a0 = col0 + c0
            if k0 <= a0 < k1:
                k_ref[:, a0 - k0:a0 - k0 + chunk] = acc
            if v0 <= a0 < v1:
                v_ref[:, a0 - v0:a0 - v0 + chunk] = acc


def _in_proj(x, g, mod3, layer, row_fn, w, col0, n_cols, mode, tm=512, chunk=512, stage_rows=256):
    t, d = x.shape
    base = layer * 48

    def mod_spec(kind):
        return pl.BlockSpec((None, 1, d), lambda i: (base + row_fn(i) * 6 + kind, 0, 0))

    in_specs = [
        pl.BlockSpec((tm, d), lambda i: (i, 0)),
        pl.BlockSpec((1, d), lambda i: (0, 0)),
        mod_spec(0),
        mod_spec(1),
        pl.BlockSpec(memory_space=pl.ANY),
    ]
    out_specs = [pl.BlockSpec((tm, n_cols), lambda i: (i, 0))]
    out_shape = [jax.ShapeDtypeStruct((t, n_cols), BF16)]
    kv_cols = None
    if mode == "even_kv":
        a_qk = H_A * 2 * DK_A
        a_v = H_A * DV_A
        kv_cols = ((a_qk, 2 * a_qk), (2 * a_qk, 2 * a_qk + a_v))
        out_specs += [pl.BlockSpec((tm, a_qk), lambda i: (i, 0)), pl.BlockSpec((tm, a_v), lambda i: (i, 0))]
        out_shape += [jax.ShapeDtypeStruct((t, a_qk), F32), jax.ShapeDtypeStruct((t, a_v), F32)]
    return pl.pallas_call(
        functools.partial(_in_kernel, col0=col0, mode=mode, kv_cols=kv_cols, chunk=chunk),
        grid=(t // tm,),
        in_specs=in_specs,
        out_specs=out_specs,
        out_shape=out_shape,
        scratch_shapes=[
            pltpu.VMEM((d, n_cols), BF16),
            pltpu.VMEM((2, stage_rows, n_cols), F32),
            pltpu.VMEM((tm, d), BF16),
            pltpu.SemaphoreType.DMA((2,)),
        ],
        compiler_params=_params("arbitrary"),
    )(x, g.reshape(1, d), mod3, mod3, w)


def _rope(x, c, s):
    lane = lax.broadcasted_iota(jnp.int32, x.shape, 1)
    partner = jnp.where((lane & 32) == 0, pltpu.roll(x, 96, 1), pltpu.roll(x, 32, 1))
    return x * c + partner * s


def _mixer_kernel(*refs, seq, has_ctx, lam_init, past, hps):
    if has_ctx:
        (aq_ref, ak_ref, av_ref, bq_ref, bk_ref, bv_ref, bg_ref, lq_ref, lk_ref, dl_ref,
         ck_ref, cv_ref, s0_ref, cos_ref, sin_ref,
         oa_ref, or_ref, kall_ref, vall_ref, of_ref, ob_ref, stf_ref, stb_ref) = refs
    else:
        (aq_ref, ak_ref, av_ref, bq_ref, bk_ref, bv_ref, bg_ref, lq_ref, lk_ref, dl_ref,
         oa_ref, or_ref, sout_ref, of_ref, ob_ref, stf_ref, stb_ref) = refs
    tq = 256
    rt = 256
    straight = seq <= tq
    cs = RET_CHUNK
    nc = seq // cs

    prod = lq_ref[...] * lk_ref[...]
    ee = jnp.exp(jnp.sum(prod, axis=1, keepdims=True))
    lam = ee[0:1, :] - ee[1:2, :] + lam_init
    scale = DK_A ** -0.5
    ii = lax.broadcasted_iota(jnp.int32, (cs, cs), 0)
    jj = lax.broadcasted_iota(jnp.int32, (cs, cs), 1)
    ic = lax.broadcasted_iota(jnp.int32, (cs, 1), 0).astype(F32)

    def loop(n, body):
        if straight:
            for it in range(n):
                body(it)
        else:
            lax.fori_loop(0, n, lambda it, c: (body(it), c)[1], 0)

    def rows(it, size):
        return pl.ds(it * size, size) if straight else pl.ds(pl.multiple_of(it * size, size), size)

    for hh in range(hps):
        qk_cols = slice(hh * 2 * DK_A, (hh + 1) * 2 * DK_A)
        va_cols = slice(hh * DV_A, (hh + 1) * DV_A)
        kb_cols = slice(hh * DK_B, (hh + 1) * DK_B)
        vb_cols = slice(hh * DV_B, (hh + 1) * DV_B)

        if has_ctx:
            for c in range(2):
                kall_ref[c, 0:past, :] = ck_ref[:, c * DK_A:(c + 1) * DK_A].astype(BF16)
            vall_ref[0:past, :] = cv_ref[...].astype(BF16)

            def copy_body(r):
                rr = rows(r, rt)
                cs_, sn_ = cos_ref[rr, :], sin_ref[rr, :]
                kk = ak_ref[rr, qk_cols].astype(F32)
                for c in range(2):
                    kall_ref[c, pl.ds(past + pl.multiple_of(r * rt, rt), rt), :] = _rope(
                        kk[:, c * DK_A:(c + 1) * DK_A], cs_, sn_).astype(BF16)
                vall_ref[pl.ds(past + pl.multiple_of(r * rt, rt), rt), :] = av_ref[rr, va_cols]

            loop(seq // rt, copy_body)

        def attn_body(qt, qk_cols=qk_cols, va_cols=va_cols):
            rr = rows(qt, tq)
            q = aq_ref[rr, qk_cols]
            if has_ctx:
                qf = q.astype(F32)
                cs_, sn_ = cos_ref[rr, :], sin_ref[rr, :]
            probs = []
            for c in range(2):
                if has_ctx:
                    qc = _rope(qf[:, c * DK_A:(c + 1) * DK_A], cs_, sn_).astype(BF16)
                    keys = kall_ref[c]
                else:
                    qc = q[:, c * DK_A:(c + 1) * DK_A]
                    keys = ak_ref[:, hh * 2 * DK_A + c * DK_A:hh * 2 * DK_A + (c + 1) * DK_A]
                s = _dot_nt(qc, keys) * scale
                e = jnp.exp(s - jnp.max(s, axis=-1, keepdims=True))
                probs.append(e * (1.0 / jnp.sum(e, axis=-1, keepdims=True)))
            a = (probs[0] - lam * probs[1]).astype(BF16)
            o = _dot(a, vall_ref[...] if has_ctx else av_ref[:, va_cols])
            oa_ref[rr, va_cols] = (_rms_f32(o) * (1.0 - lam_init)).astype(BF16)

        loop(seq // tq, attn_body)

        tables = []
        for d in range(2):
            lg = jax.nn.log_sigmoid(dl_ref[d, hh])
            diff = (ii - jj if d == 0 else jj - ii).astype(F32)
            mask = jnp.where(diff >= 0, jnp.exp(jnp.maximum(diff, 0.0) * lg), 0.0)
            if d == 0:
                qdec = jnp.exp((ic + 1.0) * lg)
                kdec = jnp.exp((cs - 1.0 - ic) * lg)
            else:
                qdec = jnp.exp((cs - ic) * lg)
                kdec = jnp.exp(ic * lg)
            tables.append((mask, qdec, kdec, jnp.exp(cs * lg)))
        st_refs = (stf_ref, stb_ref)
        o_refs = (of_ref, ob_ref)
        for d in range(2):
            if has_ctx:
                st_refs[d][hh] = s0_ref[d]
            else:
                st_refs[d][hh] = jnp.zeros((DK_B, DV_B), F32)

        def chunk_body(n, hh=hh, kb_cols=kb_cols, vb_cols=vb_cols, tables=tables):
            for d in range(2):
                mask, qdec, kdec, cdec = tables[d]
                rr = rows(n if d == 0 else nc - 1 - n, cs)
                qc = bq_ref[rr, kb_cols].astype(F32)
                kc = bk_ref[rr, kb_cols].astype(F32) * (DK_B ** -0.5)
                vc = bv_ref[rr, vb_cols]
                state = st_refs[d][hh]
                qk = _dot_nt(qc.astype(BF16), kc.astype(BF16)) * mask
                o_refs[d][rr, vb_cols] = (_dot(qk.astype(BF16), vc)
                                          + _dot((qc * qdec).astype(BF16), state.astype(BF16)))
                kd = (kc * kdec).T.astype(BF16)
                st_refs[d][hh] = cdec * state + _dot(kd, vc)

        loop(nc, chunk_body)

        def final_body(r, vb_cols=vb_cols):
            rr = rows(r, rt)
            tot = of_ref[rr, vb_cols] + ob_ref[rr, vb_cols]
            gate = _silu(bg_ref[rr, vb_cols].astype(F32))
            or_ref[rr, vb_cols] = (_rms_f32(tot) * gate).astype(BF16)

        loop(seq // rt, final_body)
        if not has_ctx:
            for d in range(2):
                sout_ref[d, hh] = st_refs[d][hh]


def _mixer(pa, pb, lam_q, lam_k, decay_logit, lam_init, batch, seq, hps, ctx=None):
    has_ctx = ctx is not None
    past = ctx[0].shape[1] if has_ctx else 0

    def col(head_width, off):
        width = hps * head_width
        return pl.BlockSpec((seq, width), lambda b, hg: (b, off // width + hg))

    a_qk = H_A * 2 * DK_A
    b_qk = H_B * DK_B
    b_v = H_B * DV_B
    in_specs = [
        col(2 * DK_A, 0), col(2 * DK_A, a_qk), col(DV_A, 2 * a_qk),
        col(DK_B, 0), col(DK_B, b_qk), col(DV_B, 2 * b_qk), col(DV_B, 2 * b_qk + b_v),
        pl.BlockSpec((2, DK_A), lambda b, hg: (0, 0)),
        pl.BlockSpec((2, DK_A), lambda b, hg: (0, 0)),
        pl.BlockSpec((2, hps, 1, 1), lambda b, hg: (0, hg, 0, 0)),
    ]
    args = [pa] * 3 + [pb] * 4 + [lam_q, lam_k, decay_logit.reshape(2, H_B, 1, 1)]
    out_specs = [
        pl.BlockSpec((seq, hps * DV_A), lambda b, hg: (b, hg)),
        pl.BlockSpec((seq, hps * DV_B), lambda b, hg: (b, hg)),
    ]
    out_shape = [
        jax.ShapeDtypeStruct((batch * seq, H_A * DV_A), BF16),
        jax.ShapeDtypeStruct((batch * seq, b_v), BF16),
    ]
    scratch = []
    if has_ctx:
        assert hps == 1
        ck, cv, s0, cos_t, sin_t = ctx
        in_specs += [
            pl.BlockSpec((None, past, 2 * DK_A), lambda b, hg: (b, 0, hg)),
            pl.BlockSpec((None, past, DV_A), lambda b, hg: (b, 0, hg)),
            pl.BlockSpec((None, 2, None, DK_B, DV_B), lambda b, hg: (b, 0, hg, 0, 0)),
            pl.BlockSpec((seq, DK_A), lambda b, hg: (0, 0)),
            pl.BlockSpec((seq, DK_A), lambda b, hg: (0, 0)),
        ]
        args += [ck, cv, s0, cos_t, sin_t]
        scratch += [pltpu.VMEM((2, seq + past, DK_A), BF16), pltpu.VMEM((seq + past, DV_A), BF16)]
    else:
        out_specs.append(pl.BlockSpec((None, 2, hps, DK_B, DV_B), lambda b, hg: (b, 0, hg, 0, 0)))
        out_shape.append(jax.ShapeDtypeStruct((batch, 2, H_B, DK_B, DV_B), F32))
    scratch += [
        pltpu.VMEM((seq, hps * DV_B), F32),
        pltpu.VMEM((seq, hps * DV_B), F32),
        pltpu.VMEM((hps, DK_B, DV_B), F32),
        pltpu.VMEM((hps, DK_B, DV_B), F32),
    ]
    return pl.pallas_call(
        functools.partial(_mixer_kernel, seq=seq, has_ctx=has_ctx, lam_init=lam_init, past=past, hps=hps),
        grid=(batch, H_A // hps),
        in_specs=in_specs,
        out_specs=out_specs,
        out_shape=out_shape,
        scratch_shapes=scratch,
        compiler_params=_params("arbitrary", "arbitrary"),
    )(*args)


def _out_kernel(*refs, seg_tiles, n_lhs, part, n_parts):
    n_seg = len(seg_tiles)
    per_seg = n_lhs + 1
    seg_refs = [refs[s * per_seg:(s + 1) * per_seg] for s in range(n_seg)]
    rest = refs[n_seg * per_seg:]
    last = part == n_parts - 1
    if last:
        gate_ref, g_ref, sh_ref, sc_ref, rw_ref = rest[:5]
        prev_refs = rest[5:5 + part]
        w_hbm, x1_ref, h2_ref, lg_ref, slab_ref, stage_ref, sem = rest[5 + part:]
    else:
        gate_ref, w_hbm, x1_ref, slab_ref, stage_ref, sem = rest
    i = pl.program_id(0)
    dn = slab_ref.shape[1]

    @pl.when(i == 0)
    def _():
        _load_weight_slab(w_hbm, slab_ref, stage_ref, sem, part * dn)

    def run(lhs_refs, x_ref):
        acc = None
        r0 = 0
        for lhs_ref in lhs_refs:
            kk = lhs_ref.shape[1]
            prod = _dot(lhs_ref[...], slab_ref[r0:r0 + kk, :])
            acc = prod if acc is None else acc + prod
            r0 += kk
        x1 = x_ref[...] + gate_ref[...] * acc
        x1_ref[...] = x1
        if last:
            cols = [p_ref[...] for p_ref in prev_refs] + [x1]
            d = dn * n_parts
            ms = sum(jnp.sum(v * v, axis=-1, keepdims=True) for v in cols) / d
            inv = lax.rsqrt(ms + EPS)
            lg = None
            for j, v in enumerate(cols):
                cs_ = slice(j * dn, (j + 1) * dn)
                h2 = ((v * inv) * g_ref[:, cs_]) * (1.0 + sc_ref[:, cs_]) + sh_ref[:, cs_]
                _store_token_major(h2_ref, h2, j * dn)
                pl_ = lax.dot_general(rw_ref[:, cs_], h2, (((1,), (1,)), ((), ())),
                                      precision=lax.Precision.HIGHEST, preferred_element_type=F32)
                lg = pl_ if lg is None else lg + pl_
            lg_ref[...] = lg

    if n_seg == 1:
        run(seg_refs[0][:n_lhs], seg_refs[0][n_lhs])
    else:
        lo = 0
        for s in range(n_seg):
            @pl.when((i >= lo) & (i < lo + seg_tiles[s]))
            def _(s=s):
                run(seg_refs[s][:n_lhs], seg_refs[s][n_lhs])

            lo += seg_tiles[s]


def _out_proj(segments, w, g, mod3, layer, row_fn, router_wt, n_parts=2, tm=512, stage_rows=256):
    n_lhs = len(segments[0][0])
    d = segments[0][1].shape[1]
    dn = d // n_parts
    kdim = w.shape[0]
    base = layer * 48
    ne = router_wt.shape[0]
    seg_tiles = tuple(x.shape[0] // tm for _, x in segments)
    t_total = sum(x.shape[0] for _, x in segments)
    rpt = d // LANES
    x1_parts = []
    for part in range(n_parts):
        last = part == n_parts - 1
        in_specs, args = [], []
        lo = 0
        for (lhs_list, x), nt in zip(segments, seg_tiles):
            def seg_row(i, lo=lo, nt=nt):
                return jnp.clip(i - lo, 0, nt - 1)

            for lhs in lhs_list:
                in_specs.append(pl.BlockSpec((tm, lhs.shape[1]), lambda i, f=seg_row: (f(i), 0)))
                args.append(lhs)
            in_specs.append(pl.BlockSpec((tm, dn), lambda i, f=seg_row: (f(i), part)))
            args.append(x)
            lo += nt
        in_specs.append(pl.BlockSpec((None, 1, dn), lambda i: (base + row_fn(i) * 6 + 2, 0, part)))
        args.append(mod3)
        out_specs = [pl.BlockSpec((tm, dn), lambda i: (i, 0))]
        out_shape = [jax.ShapeDtypeStruct((t_total, dn), F32)]
        if last:
            in_specs += [
                pl.BlockSpec((1, d), lambda i: (0, 0)),
                pl.BlockSpec((None, 1, d), lambda i: (base + row_fn(i) * 6 + 3, 0, 0)),
                pl.BlockSpec((None, 1, d), lambda i: (base + row_fn(i) * 6 + 4, 0, 0)),
                pl.BlockSpec((ne, d), lambda i: (0, 0)),
            ] + [pl.BlockSpec((tm, dn), lambda i: (i, 0))] * part
            args += [g.reshape(1, d), mod3, mod3, router_wt] + x1_parts
            out_specs += [
                pl.BlockSpec((tm * rpt, LANES), lambda i: (i, 0)),
                pl.BlockSpec((ne, tm), lambda i: (0, i)),
            ]
            out_shape += [
                jax.ShapeDtypeStruct((t_total * rpt, LANES), F32),
                jax.ShapeDtypeStruct((ne, t_total), F32),
            ]
        in_specs.append(pl.BlockSpec(memory_space=pl.ANY))
        args.append(w)
        outs = pl.pallas_call(
            functools.partial(_out_kernel, seg_tiles=seg_tiles, n_lhs=n_lhs, part=part, n_parts=n_parts),
            grid=(t_total // tm,),
            in_specs=in_specs,
            out_specs=out_specs,
            out_shape=out_shape,
            scratch_shapes=[
                pltpu.VMEM((kdim, dn), BF16),
                pltpu.VMEM((2, stage_rows, dn), F32),
                pltpu.SemaphoreType.DMA((2,)),
            ],
            compiler_params=_params("arbitrary"),
        )(*args)
        x1_parts.append(outs[0])
    return x1_parts, outs[1], outs[2]


def _first_index(vals, target):
    idx = jnp.full(target.shape, len(vals) - 1, jnp.int32)
    for k in range(len(vals) - 2, -1, -1):
        idx = jnp.where(vals[k] == target, k, idx)
    return idx


def _pick(vals, idx):
    out = vals[-1]
    for k in range(len(vals) - 2, -1, -1):
        out = jnp.where(idx == k, vals[k], out)
    return out


def _route_kernel(lg_ref, rb_ref, pos_ref, wt_ref, meta_ref, rank_ref, *, tile_rows, scan):
    logits = lg_ref[...]
    t = logits.shape[1]
    scores = jax.nn.sigmoid(logits)
    biased = scores + rb_ref[...]
    s_rows = [scores[e:e + 1, :] for e in range(N_EXPERTS)]
    b_rows = [biased[e:e + 1, :] for e in range(N_EXPERTS)]
    grp = []
    for g in range(N_GROUPS):
        r = b_rows[g * GROUP_SIZE:(g + 1) * GROUP_SIZE]
        best = None
        for a in range(GROUP_SIZE):
            for b in range(a + 1, GROUP_SIZE):
                pair = r[a] + r[b]
                best = pair if best is None else jnp.maximum(best, pair)
        grp.append(best)
    gmax = functools.reduce(jnp.maximum, grp)
    gsel = _first_index(grp, gmax)
    cand_b = [_pick([b_rows[g * GROUP_SIZE + k] for g in range(N_GROUPS)], gsel) for k in range(GROUP_SIZE)]
    cand_s = [_pick([s_rows[g * GROUP_SIZE + k] for g in range(N_GROUPS)], gsel) for k in range(GROUP_SIZE)]
    m1 = functools.reduce(jnp.maximum, cand_b)
    k1 = _first_index(cand_b, m1)
    rest = [jnp.where(k1 == k, -jnp.inf, cand_b[k]) for k in range(GROUP_SIZE)]
    m2 = functools.reduce(jnp.maximum, rest)
    k2 = _first_index(rest, m2)
    w1 = _pick(cand_s, k1)
    w2 = _pick(cand_s, k2)
    wsum = w1 + w2
    w1 = w1 / wsum
    w2 = w2 / wsum
    e1 = gsel * GROUP_SIZE + k1
    e2 = gsel * GROUP_SIZE + k2
    wt_ref[...] = jnp.zeros_like(wt_ref)
    wt_ref[0:1, :] = w1
    wt_ref[1:2, :] = w2

    eid = lax.broadcasted_iota(jnp.int32, logits.shape, 0)
    sel1 = eid == e1
    sel2 = eid == e2
    ind = jnp.where(sel1 | sel2, 1.0, 0.0)
    ua = lax.broadcasted_iota(jnp.int32, (scan, scan), 0)
    ub = lax.broadcasted_iota(jnp.int32, (scan, scan), 1)
    upper = jnp.where(ua <= ub, 1.0, 0.0).astype(BF16)
    carry = jnp.zeros((N_EXPERTS, 1), F32)
    for blk in range(t // scan):
        seg = ind[:, blk * scan:(blk + 1) * scan]
        incl = _dot(seg.astype(BF16), upper)
        rank_ref[:, blk * scan:(blk + 1) * scan] = incl - seg + carry
        carry = carry + incl[:, scan - 1:scan]
    n_tiles = jnp.floor((carry + (tile_rows - 1.0)) * (1.0 / tile_rows))
    la = lax.broadcasted_iota(jnp.int32, (N_EXPERTS, N_EXPERTS), 0)
    lb = lax.broadcasted_iota(jnp.int32, (N_EXPERTS, N_EXPERTS), 1)
    lower = jnp.where(lb < la, 1.0, 0.0).astype(BF16)
    start = _dot(lower, jnp.broadcast_to(n_tiles, (N_EXPERTS, LANES)).astype(BF16))
    end = start + n_tiles
    slot = start[:, 0:1] * tile_rows + rank_ref[...]
    pos_ref[...] = jnp.zeros_like(pos_ref)
    pos_ref[0:1, :] = jnp.sum(jnp.where(sel1, slot, 0.0), axis=0, keepdims=True).astype(jnp.int32)
    pos_ref[1:2, :] = jnp.sum(jnp.where(sel2, slot, 0.0), axis=0, keepdims=True).astype(jnp.int32)

    n_used = end[N_EXPERTS - 1:N_EXPERTS, :]
    tile = lax.broadcasted_iota(jnp.int32, (N_EXPERTS, LANES), 1).astype(F32)
    tile = jnp.minimum(tile, n_used - 1.0)
    tile_expert = jnp.sum(jnp.where(tile >= end, 1.0, 0.0), axis=0, keepdims=True)
    lane = lax.broadcasted_iota(jnp.int32, (1, LANES), 1)
    meta_ref[...] = jnp.zeros_like(meta_ref)
    meta_ref[0:1, :] = jnp.where(lane == LANES - 1, n_used, tile_expert).astype(jnp.int32)


def _route(logits_t, router_b, tile_rows, scan=512):
    ne, t = logits_t.shape
    return pl.pallas_call(
        functools.partial(_route_kernel, tile_rows=tile_rows, scan=scan),
        out_shape=[
            jax.ShapeDtypeStruct((8, t), jnp.int32),
            jax.ShapeDtypeStruct((8, t), F32),
            jax.ShapeDtypeStruct((8, LANES), jnp.int32),
        ],
        scratch_shapes=[pltpu.VMEM((ne, t), F32)],
        compiler_params=pltpu.CompilerParams(vmem_limit_bytes=VMEM_LIMIT_BYTES),
    )(logits_t, router_b.reshape(ne, 1))


def _invert_kernel(pos_ref, inv_ref, *, t):
    n_slots = inv_ref.shape[0]

    def clear(s, carry):
        inv_ref[s] = 0
        return carry

    lax.fori_loop(0, n_slots, clear, 0, unroll=8)

    def place(tok, carry):
        inv_ref[pos_ref[tok]] = tok
        inv_ref[pos_ref[t + tok]] = tok
        return carry

    lax.fori_loop(0, t, place, 0, unroll=8)


def _invert(pos_flat, n_slots):
    t = pos_flat.shape[0] // 2
    return pl.pallas_call(
        functools.partial(_invert_kernel, t=t),
        in_specs=[pl.BlockSpec(memory_space=pltpu.SMEM)],
        out_specs=pl.BlockSpec(memory_space=pltpu.SMEM),
        out_shape=jax.ShapeDtypeStruct((n_slots,), jnp.int32),
    )(pos_flat)


def _ffn_kernel(meta_ref, inv_ref, h_hbm, wg_ref, wu_ref, wd_ref, ys_ref,
                xg0_ref, xg1_ref, wgb_ref, wub_ref, wdb_ref, sem, *, tm, rpt):
    n = pl.program_id(0)
    n_used = meta_ref[LANES - 1]
    bufs = (xg0_ref, xg1_ref)

    def gather(tile, s):
        def body(r2, carry):
            for j in range(2):
                r = 2 * r2 + j
                tok = inv_ref[tile * tm + r]
                pltpu.make_async_copy(
                    h_hbm.at[pl.ds(pl.multiple_of(tok * rpt, rpt), rpt)],
                    bufs[s].at[pl.ds(pl.multiple_of(r * rpt, rpt), rpt)],
                    sem.at[s]).start(priority=j)
            return carry

        lax.fori_loop(0, tm // 2, body, 0, unroll=4)

    def compute(s):
        pltpu.make_async_copy(h_hbm.at[pl.ds(0, tm * rpt)], bufs[s], sem.at[s]).wait()
        d = wgb_ref.shape[0]
        a = b = None
        for k0 in range(0, d, MXU_DIM):
            x = jnp.concatenate(
                [_load_token_major(bufs[s], tm, k0 // LANES + j).astype(BF16) for j in range(MXU_DIM // LANES)],
                axis=1)
            pa = _dot(x, wgb_ref[k0:k0 + MXU_DIM, :])
            pb = _dot(x, wub_ref[k0:k0 + MXU_DIM, :])
            a = pa if a is None else a + pa
            b = pb if b is None else b + pb
        act = (_silu(a) * b).astype(BF16)
        for n0 in range(0, d, MXU_DIM):
            _store_token_major(ys_ref, _dot(act, wdb_ref[:, n0:n0 + MXU_DIM]), n0)

    @pl.when(n == 0)
    def _():
        gather(0, 0)

    @pl.when(n < n_used)
    def _():
        @pl.when((n == 0) | (meta_ref[n] != meta_ref[jnp.maximum(n - 1, 0)]))
        def _():
            wgb_ref[...] = wg_ref[...].astype(BF16)
            wub_ref[...] = wu_ref[...].astype(BF16)
            wdb_ref[...] = wd_ref[...].astype(BF16)

        for s in range(2):
            @pl.when(n % 2 == s)
            def _(s=s):
                @pl.when(n + 1 < n_used)
                def _():
                    gather(n + 1, 1 - s)

                compute(s)

    @pl.when(n >= n_used)
    def _():
        ys_ref[...] = jnp.zeros_like(ys_ref)


def _ffn(meta, inv, h_lin, w_gate, w_up, w_down, layer, tm, n_tiles):
    _, ne, d, f = w_gate.shape
    rpt = d // LANES
    return pl.pallas_call(
        functools.partial(_ffn_kernel, tm=tm, rpt=rpt),
        grid_spec=pltpu.PrefetchScalarGridSpec(
            num_scalar_prefetch=2,
            grid=(n_tiles,),
            in_specs=[
                pl.BlockSpec(memory_space=pl.ANY),
                pl.BlockSpec((None, None, d, f), lambda n, m, v: (layer, m[n], 0, 0)),
                pl.BlockSpec((None, None, d, f), lambda n, m, v: (layer, m[n], 0, 0)),
                pl.BlockSpec((None, None, f, d), lambda n, m, v: (layer, m[n], 0, 0)),
            ],
            out_specs=pl.BlockSpec((tm * rpt, LANES), lambda n, m, v: (n, 0)),
            scratch_shapes=[
                pltpu.VMEM((tm * rpt, LANES), F32),
                pltpu.VMEM((tm * rpt, LANES), F32),
                pltpu.VMEM((d, f), BF16),
                pltpu.VMEM((d, f), BF16),
                pltpu.VMEM((f, d), BF16),
                pltpu.SemaphoreType.DMA((2,)),
            ],
        ),
        out_shape=jax.ShapeDtypeStruct((n_tiles * tm * rpt, LANES), F32),
        compiler_params=_params("arbitrary"),
    )(meta, inv, h_lin, w_gate, w_up, w_down)


def _combine_kernel(pos_ref, ys_ref, *refs, t, rpt, tm, n_parts, final_tiles):
    x_refs = refs[:n_parts]
    if final_tiles is None:
        wt_ref, gate_ref, o_ref, a0, b0, a1, b1, sem = refs[n_parts:]
    else:
        wt_ref, gate_ref, fg_ref, op_ref, os_ref, a0, b0, a1, b1, sem = refs[n_parts:]
    i = pl.program_id(0)
    n = pl.num_programs(0)
    bufs = ((a0, b0), (a1, b1))

    def issue(tile, s):
        def body(r, carry):
            tok = tile * tm + r
            for k in range(2):
                slot_row = pos_ref[k * t + tok]
                pltpu.make_async_copy(
                    ys_ref.at[pl.ds(pl.multiple_of(slot_row * rpt, rpt), rpt)],
                    bufs[s][k].at[pl.ds(pl.multiple_of(r * rpt, rpt), rpt)],
                    sem.at[s]).start(priority=k)
            return carry

        lax.fori_loop(0, tm, body, 0, unroll=8)

    def wait(s):
        for k in range(2):
            pltpu.make_async_copy(ys_ref.at[pl.ds(0, tm * rpt)], bufs[s][k], sem.at[s]).wait()

    def compute(s, out_ref):
        w = wt_ref[...]
        w0, w1 = w[:, 0:1], w[:, 1:2]
        dn = x_refs[0].shape[1]
        for c in range(rpt):
            cols = slice(c * LANES, (c + 1) * LANES)
            part, pc = divmod(c * LANES, dn)
            y = w0 * _load_token_major(bufs[s][0], tm, c) + w1 * _load_token_major(bufs[s][1], tm, c)
            out_ref[:, cols] = x_refs[part][:, pc:pc + LANES] + gate_ref[:, cols] * y
        if final_tiles is not None:
            out_ref[...] = _rms_f32(out_ref[...]) * fg_ref[...]

    @pl.when(i == 0)
    def _():
        issue(0, 0)

    for s in range(2):
        @pl.when(i % 2 == s)
        def _(s=s):
            @pl.when(i + 1 < n)
            def _():
                issue(i + 1, 1 - s)

            wait(s)
            if final_tiles is None:
                compute(s, o_ref)
            else:
                @pl.when(i < final_tiles)
                def _():
                    compute(s, op_ref)

                @pl.when(i >= final_tiles)
                def _():
                    compute(s, os_ref)


def _combine(pos_flat, ys, x1_parts, wt, mod3, layer, row_fn, final=None, tm=256):
    t, dn = x1_parts[0].shape
    n_parts = len(x1_parts)
    d = dn * n_parts
    rpt = d // LANES
    base = layer * 48
    in_specs = ([pl.BlockSpec(memory_space=pl.ANY)]
                + [pl.BlockSpec((tm, dn), lambda i, p: (i, 0))] * n_parts
                + [pl.BlockSpec((tm, 2), lambda i, p: (i, 0)),
                   pl.BlockSpec((None, 1, d), lambda i, p: (base + row_fn(i) * 6 + 5, 0, 0))])
    args = [pos_flat, ys, *x1_parts, wt, mod3]
    if final is None:
        final_tiles = None
        out_specs = pl.BlockSpec((tm, d), lambda i, p: (i, 0))
        out_shape = jax.ShapeDtypeStruct((t, d), F32)
    else:
        fg, t_ctx = final
        final_tiles = t_ctx // tm
        in_specs.append(pl.BlockSpec((1, d), lambda i, p: (0, 0)))
        args.append(fg.reshape(1, d))
        out_specs = [
            pl.BlockSpec((tm, d), lambda i, p: (jnp.minimum(i, final_tiles - 1), 0)),
            pl.BlockSpec((tm, d), lambda i, p: (jnp.maximum(i - final_tiles, 0), 0)),
        ]
        out_shape = [jax.ShapeDtypeStruct((t_ctx, d), F32), jax.ShapeDtypeStruct((t - t_ctx, d), F32)]
    return pl.pallas_call(
        functools.partial(_combine_kernel, t=t, rpt=rpt, tm=tm, n_parts=n_parts, final_tiles=final_tiles),
        grid_spec=pltpu.PrefetchScalarGridSpec(
            num_scalar_prefetch=1,
            grid=(t // tm,),
            in_specs=in_specs,
            out_specs=out_specs,
            scratch_shapes=[pltpu.VMEM((tm * rpt, LANES), F32)] * 4 + [pltpu.SemaphoreType.DMA((2,))],
        ),
        out_shape=out_shape,
        compiler_params=_params("arbitrary"),
    )(*args)


def _moe(lg, router_b, h_lin, x1, w_gate, w_up, w_down, mod3, layer, row_fn, final=None, tile_rows=256):
    t = x1[0].shape[0]
    n_tiles = (2 * t) // tile_rows + N_EXPERTS
    pos, wts, meta = _route(lg, router_b, tile_rows)
    pos_flat = pos[:2].reshape(-1)
    inv = _invert(pos_flat, n_tiles * tile_rows)
    ys = _ffn(meta[0], inv, h_lin, w_gate, w_up, w_down, layer, tile_rows, n_tiles)
    return _combine(pos_flat, ys, x1, wts[:2].T, mod3, layer, row_fn, final)


def _sgu_kernel(u0_ref, u1_ref, v0_ref, v1_ref, ng_ref, ws_ref, bs_ref, o0_ref, o1_ref, *, n_chunks):
    half = v0_ref.shape[1]
    gw = 2 * half // H_C
    v0 = v0_ref[...].astype(F32)
    v1 = v1_ref[...].astype(F32)
    ms = (jnp.sum(v0 * v0, axis=-1, keepdims=True) + jnp.sum(v1 * v1, axis=-1, keepdims=True)) / (2 * half)
    inv = lax.rsqrt(ms + EPS)
    halves = ((u0_ref, (v0 * inv * ng_ref[:, :half]).astype(BF16), o0_ref),
              (u1_ref, (v1 * inv * ng_ref[:, half:]).astype(BF16), o1_ref))
    for n in range(n_chunks):
        r0 = n * CHUNK_C
        for h in range(H_C):
            u_ref, vn, o_ref = halves[h // (H_C // 2)]
            c0 = (h % (H_C // 2)) * gw
            mixed = _dot(ws_ref[h].astype(BF16), vn[r0:r0 + CHUNK_C, c0:c0 + gw]) + bs_ref[h]
            u = u_ref[r0:r0 + CHUNK_C, c0:c0 + gw].astype(F32)
            o_ref[r0:r0 + CHUNK_C, c0:c0 + gw] = (u * mixed).astype(BF16)


def _sgu(u0, u1, v0, v1, norm_g, w_s, b_s, tm=256):
    t, half = u0.shape
    blk = pl.BlockSpec((tm, half), lambda i: (i, 0))
    return pl.pallas_call(
        functools.partial(_sgu_kernel, n_chunks=tm // CHUNK_C),
        grid=(t // tm,),
        in_specs=[
            blk, blk, blk, blk,
            pl.BlockSpec((1, 2 * half), lambda i: (0, 0)),
            pl.BlockSpec((H_C, CHUNK_C, CHUNK_C), lambda i: (0, 0, 0)),
            pl.BlockSpec((H_C, CHUNK_C, 1), lambda i: (0, 0, 0)),
        ],
        out_specs=[blk, blk],
        out_shape=[jax.ShapeDtypeStruct((t, half), BF16)] * 2,
        compiler_params=_params("arbitrary"),
    )(u0, u1, v0, v1, norm_g.reshape(1, 2 * half), w_s, b_s.reshape(H_C, CHUNK_C, 1))


def _rope_tables(n_tok):
    rows = n_tok // GRID_W
    pos_r = jnp.repeat(jnp.arange(rows), GRID_W).astype(F32)
    pos_c = jnp.tile(jnp.arange(GRID_W), rows).astype(F32)
    n_freq = DK_A // 4
    inv = jnp.power(ROPE_BASE, -jnp.arange(n_freq, dtype=F32) / n_freq)
    ang_r = pos_r[:, None] * inv
    ang_c = pos_c[:, None] * inv
    cos_t = jnp.concatenate([jnp.cos(ang_r), jnp.cos(ang_r), jnp.cos(ang_c), jnp.cos(ang_c)], axis=1)
    sin_t = jnp.concatenate([-jnp.sin(ang_r), jnp.sin(ang_r), -jnp.sin(ang_c), jnp.sin(ang_c)], axis=1)
    return cos_t, sin_t


def kernel(x_prompt, x_sample, cache_k, cache_v, state_ret, c, c_ctx, norm_g, w_mod, b_mod, w_in_even, w_out_even, lam_q, lam_k, ret_decay_logit, w_in_odd, gmlp_norm_g, w_spatial, b_spatial, w_out_odd, router_w, router_b, w_gate, w_up, w_down, final_norm_g):
    batch, seq, d = x_prompt.shape
    dbatch, dseq, _ = x_sample.shape
    past = cache_k.shape[2]
    depth = norm_g.shape[0]
    t_ctx, t_lat = batch * seq, dbatch * dseq
    t_all = t_ctx + t_lat
    xc = x_prompt.reshape(t_ctx, d)
    xl = x_sample.reshape(t_lat, d)

    cond8 = jnp.concatenate([c_ctx[None, :], c, jnp.zeros((8 - 1 - dbatch, d), F32)], axis=0)
    mod3 = _modulation(cond8, w_mod, b_mod).reshape(depth * 8 * 6, 1, d)
    router_wt = router_w.T
    cos_t, sin_t = _rope_tables(dseq)

    tm_in = 512
    tm_out = 512
    tm_cmb = 256
    w_in0 = w_in_even.reshape(d, -1)
    w_out0 = w_out_even.reshape(-1, d)
    w_in1 = w_in_odd.reshape(d, -1)
    w_out1 = w_out_odd.reshape(-1, d)
    lq, lk, dlog = lam_q.reshape(2, DK_A), lam_k.reshape(2, DK_A), ret_decay_logit.reshape(2, H_B)

    def rows_ctx(i):
        return 0

    def rows_lat(tiles_per_batch):
        return lambda i: 1 + i // tiles_per_batch

    def rows_all(tm):
        n_ctx, per = t_ctx // tm, dseq // tm
        return lambda i: jnp.maximum(i - n_ctx, 0) // per + jnp.where(i >= n_ctx, 1, 0)

    lam_init = 0.8 - 0.6 * math.exp(-0.3 * 0)
    half0 = w_in0.shape[1] // 2
    g00 = norm_g[0, 0]
    rl = rows_lat(dseq // tm_in)
    pa_c, k_new, v_new = _in_proj(xc, g00, mod3, 0, rows_ctx, w_in0, 0, half0, "even_kv", tm_in)
    (pb_c,) = _in_proj(xc, g00, mod3, 0, rows_ctx, w_in0, half0, half0, "even", tm_in)
    (pa_l,) = _in_proj(xl, g00, mod3, 0, rl, w_in0, 0, half0, "even", tm_in)
    (pb_l,) = _in_proj(xl, g00, mod3, 0, rl, w_in0, half0, half0, "even", tm_in)
    oa_c, or_c, s_new = _mixer(pa_c, pb_c, lq, lk, dlog, lam_init, batch, seq, H_A)
    ctx = (cache_k.reshape(dbatch, past, H_A * 2 * DK_A), cache_v.reshape(dbatch, past, H_A * DV_A),
           state_ret.reshape(dbatch, 2, H_B, DK_B, DV_B), cos_t, sin_t)
    oa_l, or_l = _mixer(pa_l, pb_l, lq, lk, dlog, lam_init, dbatch, dseq, 1, ctx)
    x1, h2, lg = _out_proj([([oa_c, or_c], xc), ([oa_l, or_l], xl)], w_out0, norm_g[0, 1], mod3, 0,
                           rows_all(tm_out), router_wt, tm=tm_out)
    x2 = _moe(lg, router_b, h2, x1, w_gate, w_up, w_down, mod3, 0, rows_all(tm_cmb))

    q1 = w_in1.shape[1] // 4
    zs = [_in_proj(x2, norm_g[1, 0], mod3, 1, rows_all(tm_in), w_in1, j * q1, q1, "odd", tm_in)[0]
          for j in range(4)]
    g0, g1 = _sgu(zs[0], zs[1], zs[2], zs[3], gmlp_norm_g.reshape(-1),
                  w_spatial.reshape(H_C, CHUNK_C, CHUNK_C), b_spatial.reshape(H_C, CHUNK_C))
    x3, h4, lg2 = _out_proj([([g0, g1], x2)], w_out1, norm_g[1, 1], mod3, 1, rows_all(tm_out), router_wt,
                            tm=tm_out)
    y_p, y_s = _moe(lg2, router_b, h4, x3, w_gate, w_up, w_down, mod3, 1, rows_all(tm_cmb),
                    final=(final_norm_g, t_ctx))

    y_prompt = y_p.reshape(batch, seq, d)
    y_sample = y_s.reshape(dbatch, dseq, d)
    new_cache_k = k_new.reshape(batch, 1, seq, H_A, 2, DK_A)
    new_cache_v = v_new.reshape(batch, 1, seq, H_A, DV_A)
    new_state_ret = s_new.reshape(batch, 1, 2, H_B, DK_B, DV_B)
    return (y_prompt, y_sample, new_cache_k, new_cache_v, new_state_ret)
```

```python
import functools
import math

import jax
import jax.numpy as jnp
from jax import lax
from jax.experimental import pallas as pl
from jax.experimental.pallas import tpu as pltpu

F32 = jnp.float32
BF16 = jnp.bfloat16

EPS = 1e-6
ROPE_BASE = 10000.0
GRID_W = 64
H_A, DK_A, DV_A = 4, 128, 256
H_B, DK_B, DV_B = 4, 128, 256
RET_CHUNK = 128
H_C, CHUNK_C = 8, 128
N_EXPERTS, N_GROUPS, GROUP_SIZE = 16, 4, 4

VMEM_LIMIT_BYTES = 56 * 1024 * 1024


def _params(*sem):
    return pltpu.CompilerParams(dimension_semantics=sem, vmem_limit_bytes=VMEM_LIMIT_BYTES)


def _rms_f32(x):
    return x * lax.rsqrt(jnp.mean(x * x, axis=-1, keepdims=True) + EPS)


def _silu(x):
    return x * jax.nn.sigmoid(x)


def _dot(a, b):
    return jnp.dot(a, b, preferred_element_type=F32)


def _dot_nt(a, b):
    return lax.dot_general(a, b, (((1,), (1,)), ((), ())), preferred_element_type=F32)


LANES = 128
MXU_DIM = 256


def _store_token_major(ref, val, col0=0):
    tm, width = val.shape
    rpt = ref.shape[0] // tm
    for c in range(width // LANES):
        ref[pl.ds(col0 // LANES + c, tm, stride=rpt), :] = val[:, c * LANES:(c + 1) * LANES]


def _load_token_major(ref, tm, c):
    rpt = ref.shape[0] // tm
    return ref[pl.ds(c, tm, stride=rpt), :]


def _mod_kernel(c_ref, w_ref, b_ref, o_ref):
    s = _silu(c_ref[...]).astype(BF16)
    o_ref[...] = _dot(s, w_ref[...].astype(BF16)) + b_ref[...]


def _modulation(cond8, w_mod, b_mod, tn=1024):
    depth, d, n = w_mod.shape
    return pl.pallas_call(
        _mod_kernel,
        grid=(depth, n // tn),
        in_specs=[
            pl.BlockSpec((8, d), lambda l, j: (0, 0)),
            pl.BlockSpec((None, d, tn), lambda l, j: (l, 0, j)),
            pl.BlockSpec((None, 1, tn), lambda l, j: (l, 0, j)),
        ],
        out_specs=pl.BlockSpec((None, 8, tn), lambda l, j: (l, 0, j)),
        out_shape=jax.ShapeDtypeStruct((depth, 8, n), F32),
        compiler_params=_params("arbitrary", "arbitrary"),
    )(cond8, w_mod, b_mod.reshape(depth, 1, n))


def _load_weight_slab(w_hbm, slab_ref, stage_ref, sem, col0):
    kdim, n = slab_ref.shape
    rows = stage_ref.shape[1]
    nb = kdim // rows

    def copy(j):
        return pltpu.make_async_copy(w_hbm.at[pl.ds(j * rows, rows), pl.ds(col0, n)],
                                     stage_ref.at[j % 2], sem.at[j % 2])

    copy(0).start()
    for j in range(nb):
        if j + 1 < nb:
            copy(j + 1).start()
        copy(j).wait()
        slab_ref[j * rows:(j + 1) * rows, :] = stage_ref[j % 2].astype(BF16)


def _in_kernel(x_ref, g_ref, sh_ref, sc_ref, w_hbm, *rest, col0, mode, kv_cols, chunk):
    if mode == "even_kv":
        p_ref, k_ref, v_ref, slab_ref, stage_ref, h_ref, sem = rest
    else:
        p_ref, slab_ref, stage_ref, h_ref, sem = rest

    @pl.when(pl.program_id(0) == 0)
    def _():
        _load_weight_slab(w_hbm, slab_ref, stage_ref, sem, col0)

    h = (_rms_f32(x_ref[...]) * g_ref[...]) * (1.0 + sc_ref[...]) + sh_ref[...]
    h_ref[...] = h.astype(BF16)
    n = slab_ref.shape[1]
    for c0 in range(0, n, chunk):
        acc = _dot(h_ref[...], slab_ref[:, c0:c0 + chunk])
        if mode == "odd":
            p_ref[:, c0:c0 + chunk] = (0.5 * acc * (1.0 + lax.erf(acc * (2.0 ** -0.5)))).astype(BF16)
        else:
            p_ref[:, c0:c0 + chunk] = acc.astype(BF16)
        if mode == "even_kv":
            (k0, k1), (v0, v1) = kv_cols
            a0 = col0 + c0
            if k0 <= a0 < k1:
                _store_token_major(k_ref, acc, a0 - k0)
            if v0 <= a0 < v1:
                tm = acc.shape[0]
                rpt = v_ref.shape[0] // tm
                for cc in range(chunk // LANES):
                    head, half = divmod((a0 - v0) // LANES + cc, DV_A // LANES)
                    v_ref[pl.ds(half * H_A + head, tm, stride=rpt), :] = acc[:, cc * LANES:(cc + 1) * LANES]


def _in_proj(x, g, mod3, layer, row_fn, w, col0, n_cols, mode, tm=512, chunk=512, stage_rows=256):
    t, d = x.shape
    base = layer * 48

    def mod_spec(kind):
        return pl.BlockSpec((None, 1, d), lambda i: (base + row_fn(i) * 6 + kind, 0, 0))

    in_specs = [
        pl.BlockSpec((tm, d), lambda i: (i, 0)),
        pl.BlockSpec((1, d), lambda i: (0, 0)),
        mod_spec(0),
        mod_spec(1),
        pl.BlockSpec(memory_space=pl.ANY),
    ]
    out_specs = [pl.BlockSpec((tm, n_cols), lambda i: (i, 0))]
    out_shape = [jax.ShapeDtypeStruct((t, n_cols), BF16)]
    kv_cols = None
    if mode == "even_kv":
        a_qk = H_A * 2 * DK_A
        a_v = H_A * DV_A
        kv_cols = ((a_qk, 2 * a_qk), (2 * a_qk, 2 * a_qk + a_v))
        for width in (a_qk, a_v):
            out_specs.append(pl.BlockSpec((tm * width // LANES, LANES), lambda i: (i, 0)))
            out_shape.append(jax.ShapeDtypeStruct((t * width // LANES, LANES), F32))
    return pl.pallas_call(
        functools.partial(_in_kernel, col0=col0, mode=mode, kv_cols=kv_cols, chunk=chunk),
        grid=(t // tm,),
        in_specs=in_specs,
        out_specs=out_specs,
        out_shape=out_shape,
        scratch_shapes=[
            pltpu.VMEM((d, n_cols), BF16),
            pltpu.VMEM((2, stage_rows, n_cols), F32),
            pltpu.VMEM((tm, d), BF16),
            pltpu.SemaphoreType.DMA((2,)),
        ],
        compiler_params=_params("arbitrary"),
    )(x, g.reshape(1, d), mod3, mod3, w)


def _rope(x, c, s):
    lane = lax.broadcasted_iota(jnp.int32, x.shape, 1)
    partner = jnp.where((lane & 32) == 0, pltpu.roll(x, 96, 1), pltpu.roll(x, 32, 1))
    return x * c + partner * s


def _mixer_kernel(*refs, seq, has_ctx, lam_init, past, hps):
    if has_ctx:
        (aq_ref, ak_ref, av_ref, bq_ref, bk_ref, bv_ref, bg_ref, lq_ref, lk_ref, dl_ref,
         ck_ref, cv_ref, s0_ref, cos_ref, sin_ref,
         oa_ref, or_ref, kall_ref, vall_ref, of_ref, ob_ref, stf_ref, stb_ref) = refs
    else:
        (aq_ref, ak_ref, av_ref, bq_ref, bk_ref, bv_ref, bg_ref, lq_ref, lk_ref, dl_ref,
         oa_ref, or_ref, sout_ref, of_ref, ob_ref, stf_ref, stb_ref) = refs
    tq = 256
    rt = 256
    straight = seq <= tq
    cs = RET_CHUNK
    nc = seq // cs

    prod = lq_ref[...] * lk_ref[...]
    ee = jnp.exp(jnp.sum(prod, axis=1, keepdims=True))
    lam = ee[0:1, :] - ee[1:2, :] + lam_init
    scale = DK_A ** -0.5
    ii = lax.broadcasted_iota(jnp.int32, (cs, cs), 0)
    jj = lax.broadcasted_iota(jnp.int32, (cs, cs), 1)
    ic = lax.broadcasted_iota(jnp.int32, (cs, 1), 0).astype(F32)

    def loop(n, body):
        if straight:
            for it in range(n):
                body(it)
        else:
            lax.fori_loop(0, n, lambda it, c: (body(it), c)[1], 0)

    def rows(it, size):
        return pl.ds(it * size, size) if straight else pl.ds(pl.multiple_of(it * size, size), size)

    for hh in range(hps):
        qk_cols = slice(hh * 2 * DK_A, (hh + 1) * 2 * DK_A)
        va_cols = slice(hh * DV_A, (hh + 1) * DV_A)
        kb_cols = slice(hh * DK_B, (hh + 1) * DK_B)
        vb_cols = slice(hh * DV_B, (hh + 1) * DV_B)

        if has_ctx:
            for c in range(2):
                kall_ref[c, 0:past, :] = ck_ref[:, c * DK_A:(c + 1) * DK_A].astype(BF16)
            vall_ref[0:past, :] = cv_ref[...].astype(BF16)

            def copy_body(r):
                rr = rows(r, rt)
                cs_, sn_ = cos_ref[rr, :], sin_ref[rr, :]
                kk = ak_ref[rr, qk_cols].astype(F32)
                for c in range(2):
                    kall_ref[c, pl.ds(past + pl.multiple_of(r * rt, rt), rt), :] = _rope(
                        kk[:, c * DK_A:(c + 1) * DK_A], cs_, sn_).astype(BF16)
                vall_ref[pl.ds(past + pl.multiple_of(r * rt, rt), rt), :] = av_ref[rr, va_cols]

            loop(seq // rt, copy_body)

        def attn_body(qt, qk_cols=qk_cols, va_cols=va_cols):
            rr = rows(qt, tq)
            q = aq_ref[rr, qk_cols]
            if has_ctx:
                qf = q.astype(F32)
                cs_, sn_ = cos_ref[rr, :], sin_ref[rr, :]
            exps, recips = [], []
            for c in range(2):
                if has_ctx:
                    qc = _rope(qf[:, c * DK_A:(c + 1) * DK_A], cs_, sn_).astype(BF16)
                    keys = kall_ref[c]
                else:
                    qc = q[:, c * DK_A:(c + 1) * DK_A]
                    keys = ak_ref[:, hh * 2 * DK_A + c * DK_A:hh * 2 * DK_A + (c + 1) * DK_A]
                s = _dot_nt(qc, keys) * scale
                e = jnp.exp(s - jnp.max(s, axis=-1, keepdims=True))
                exps.append(e)
                recips.append(1.0 / jnp.sum(e, axis=-1, keepdims=True))
            a = (exps[0] * recips[0] - exps[1] * (lam * recips[1])).astype(BF16)
            o = _dot(a, vall_ref[...] if has_ctx else av_ref[:, va_cols])
            oa_ref[rr, va_cols] = (_rms_f32(o) * (1.0 - lam_init)).astype(BF16)

        loop(seq // tq, attn_body)

        tables = []
        for d in range(2):
            lg = jax.nn.log_sigmoid(dl_ref[d, hh])
            diff = (ii - jj if d == 0 else jj - ii).astype(F32)
            mask = jnp.where(diff >= 0, jnp.exp(jnp.maximum(diff, 0.0) * lg), 0.0)
            if d == 0:
                qdec = jnp.exp((ic + 1.0) * lg)
                kdec = jnp.exp((cs - 1.0 - ic) * lg)
            else:
                qdec = jnp.exp((cs - ic) * lg)
                kdec = jnp.exp(ic * lg)
            tables.append((mask, qdec, kdec, jnp.exp(cs * lg)))
        st_refs = (stf_ref, stb_ref)
        o_refs = (of_ref, ob_ref)
        for d in range(2):
            if has_ctx:
                st_refs[d][hh] = s0_ref[d]
            else:
                st_refs[d][hh] = jnp.zeros((DK_B, DV_B), F32)

        def chunk_body(n, hh=hh, kb_cols=kb_cols, vb_cols=vb_cols, tables=tables):
            for d in range(2):
                mask, qdec, kdec, cdec = tables[d]
                rr = rows(n if d == 0 else nc - 1 - n, cs)
                qc = bq_ref[rr, kb_cols].astype(F32)
                kc = bk_ref[rr, kb_cols].astype(F32) * (DK_B ** -0.5)
                vc = bv_ref[rr, vb_cols]
                state = st_refs[d][hh]
                qk = _dot_nt(qc.astype(BF16), kc.astype(BF16)) * mask
                o_refs[d][rr, vb_cols] = (_dot(qk.astype(BF16), vc)
                                          + _dot((qc * qdec).astype(BF16), state.astype(BF16)))
                kd = (kc * kdec).T.astype(BF16)
                st_refs[d][hh] = cdec * state + _dot(kd, vc)

        loop(nc, chunk_body)

        def final_body(r, vb_cols=vb_cols):
            rr = rows(r, rt)
            tot = of_ref[rr, vb_cols] + ob_ref[rr, vb_cols]
            gate = _silu(bg_ref[rr, vb_cols].astype(F32))
            or_ref[rr, vb_cols] = (_rms_f32(tot) * gate).astype(BF16)

        loop(seq // rt, final_body)
        if not has_ctx:
            for d in range(2):
                sout_ref[d, hh] = st_refs[d][hh]


def _mixer(pa, pb, lam_q, lam_k, decay_logit, lam_init, batch, seq, hps, ctx=None):
    has_ctx = ctx is not None
    past = ctx[0].shape[1] if has_ctx else 0

    def col(head_width, off):
        width = hps * head_width
        return pl.BlockSpec((seq, width), lambda b, hg: (b, off // width + hg))

    a_qk = H_A * 2 * DK_A
    b_qk = H_B * DK_B
    b_v = H_B * DV_B
    in_specs = [
        col(2 * DK_A, 0), col(2 * DK_A, a_qk), col(DV_A, 2 * a_qk),
        col(DK_B, 0), col(DK_B, b_qk), col(DV_B, 2 * b_qk), col(DV_B, 2 * b_qk + b_v),
        pl.BlockSpec((2, DK_A), lambda b, hg: (0, 0)),
        pl.BlockSpec((2, DK_A), lambda b, hg: (0, 0)),
        pl.BlockSpec((2, hps, 1, 1), lambda b, hg: (0, hg, 0, 0)),
    ]
    args = [pa] * 3 + [pb] * 4 + [lam_q, lam_k, decay_logit.reshape(2, H_B, 1, 1)]
    out_specs = [
        pl.BlockSpec((seq, hps * DV_A), lambda b, hg: (b, hg)),
        pl.BlockSpec((seq, hps * DV_B), lambda b, hg: (b, hg)),
    ]
    out_shape = [
        jax.ShapeDtypeStruct((batch * seq, H_A * DV_A), BF16),
        jax.ShapeDtypeStruct((batch * seq, b_v), BF16),
    ]
    scratch = []
    if has_ctx:
        assert hps == 1
        ck, cv, s0, cos_t, sin_t = ctx
        in_specs += [
            pl.BlockSpec((None, past, 2 * DK_A), lambda b, hg: (b, 0, hg)),
            pl.BlockSpec((None, past, DV_A), lambda b, hg: (b, 0, hg)),
            pl.BlockSpec((None, 2, None, DK_B, DV_B), lambda b, hg: (b, 0, hg, 0, 0)),
            pl.BlockSpec((seq, DK_A), lambda b, hg: (0, 0)),
            pl.BlockSpec((seq, DK_A), lambda b, hg: (0, 0)),
        ]
        args += [ck, cv, s0, cos_t, sin_t]
        scratch += [pltpu.VMEM((2, seq + past, DK_A), BF16), pltpu.VMEM((seq + past, DV_A), BF16)]
    else:
        out_specs.append(pl.BlockSpec((None, 2, hps, DK_B, DV_B), lambda b, hg: (b, 0, hg, 0, 0)))
        out_shape.append(jax.ShapeDtypeStruct((batch, 2, H_B, DK_B, DV_B), F32))
    scratch += [
        pltpu.VMEM((seq, hps * DV_B), F32),
        pltpu.VMEM((seq, hps * DV_B), F32),
        pltpu.VMEM((hps, DK_B, DV_B), F32),
        pltpu.VMEM((hps, DK_B, DV_B), F32),
    ]
    return pl.pallas_call(
        functools.partial(_mixer_kernel, seq=seq, has_ctx=has_ctx, lam_init=lam_init, past=past, hps=hps),
        grid=(batch, H_A // hps),
        in_specs=in_specs,
        out_specs=out_specs,
        out_shape=out_shape,
        scratch_shapes=scratch,
        compiler_params=_params("arbitrary", "arbitrary"),
    )(*args)


def _out_kernel(*refs, seg_tiles, n_lhs, part, n_parts):
    n_seg = len(seg_tiles)
    per_seg = n_lhs + 1
    seg_refs = [refs[s * per_seg:(s + 1) * per_seg] for s in range(n_seg)]
    rest = refs[n_seg * per_seg:]
    last = part == n_parts - 1
    if last:
        gate_ref, g_ref, sh_ref, sc_ref, rw_ref = rest[:5]
        prev_refs = rest[5:5 + part]
        w_hbm, x1_ref, h2_ref, lg_ref, slab_ref, stage_ref, sem = rest[5 + part:]
    else:
        gate_ref, w_hbm, x1_ref, slab_ref, stage_ref, sem = rest
    i = pl.program_id(0)
    dn = slab_ref.shape[1]

    @pl.when(i == 0)
    def _():
        _load_weight_slab(w_hbm, slab_ref, stage_ref, sem, part * dn)

    def run(lhs_refs, x_ref):
        acc = None
        r0 = 0
        for lhs_ref in lhs_refs:
            kk = lhs_ref.shape[1]
            prod = _dot(lhs_ref[...], slab_ref[r0:r0 + kk, :])
            acc = prod if acc is None else acc + prod
            r0 += kk
        x1 = x_ref[...] + gate_ref[...] * acc
        x1_ref[...] = x1
        if last:
            cols = [p_ref[...] for p_ref in prev_refs] + [x1]
            d = dn * n_parts
            ms = sum(jnp.sum(v * v, axis=-1, keepdims=True) for v in cols) / d
            inv = lax.rsqrt(ms + EPS)
            lg = None
            for j, v in enumerate(cols):
                cs_ = slice(j * dn, (j + 1) * dn)
                h2 = ((v * inv) * g_ref[:, cs_]) * (1.0 + sc_ref[:, cs_]) + sh_ref[:, cs_]
                _store_token_major(h2_ref, h2, j * dn)
                pl_ = lax.dot_general(rw_ref[:, cs_], h2, (((1,), (1,)), ((), ())),
                                      precision=lax.Precision.HIGHEST, preferred_element_type=F32)
                lg = pl_ if lg is None else lg + pl_
            lg_ref[...] = lg

    if n_seg == 1:
        run(seg_refs[0][:n_lhs], seg_refs[0][n_lhs])
    else:
        lo = 0
        for s in range(n_seg):
            @pl.when((i >= lo) & (i < lo + seg_tiles[s]))
            def _(s=s):
                run(seg_refs[s][:n_lhs], seg_refs[s][n_lhs])

            lo += seg_tiles[s]


def _out_proj(segments, w, g, mod3, layer, row_fn, router_wt, n_parts=2, tm=512, stage_rows=256):
    n_lhs = len(segments[0][0])
    d = segments[0][1].shape[1]
    dn = d // n_parts
    kdim = w.shape[0]
    base = layer * 48
    ne = router_wt.shape[0]
    seg_tiles = tuple(x.shape[0] // tm for _, x in segments)
    t_total = sum(x.shape[0] for _, x in segments)
    rpt = d // LANES
    x1_parts = []
    for part in range(n_parts):
        last = part == n_parts - 1
        in_specs, args = [], []
        lo = 0
        for (lhs_list, x), nt in zip(segments, seg_tiles):
            def seg_row(i, lo=lo, nt=nt):
                return jnp.clip(i - lo, 0, nt - 1)

            for lhs in lhs_list:
                in_specs.append(pl.BlockSpec((tm, lhs.shape[1]), lambda i, f=seg_row: (f(i), 0)))
                args.append(lhs)
            in_specs.append(pl.BlockSpec((tm, dn), lambda i, f=seg_row: (f(i), part)))
            args.append(x)
            lo += nt
        in_specs.append(pl.BlockSpec((None, 1, dn), lambda i: (base + row_fn(i) * 6 + 2, 0, part)))
        args.append(mod3)
        out_specs = [pl.BlockSpec((tm, dn), lambda i: (i, 0))]
        out_shape = [jax.ShapeDtypeStruct((t_total, dn), F32)]
        if last:
            in_specs += [
                pl.BlockSpec((1, d), lambda i: (0, 0)),
                pl.BlockSpec((None, 1, d), lambda i: (base + row_fn(i) * 6 + 3, 0, 0)),
                pl.BlockSpec((None, 1, d), lambda i: (base + row_fn(i) * 6 + 4, 0, 0)),
                pl.BlockSpec((ne, d), lambda i: (0, 0)),
            ] + [pl.BlockSpec((tm, dn), lambda i: (i, 0))] * part
            args += [g.reshape(1, d), mod3, mod3, router_wt] + x1_parts
            out_specs += [
                pl.BlockSpec((tm * rpt, LANES), lambda i: (i, 0)),
                pl.BlockSpec((ne, tm), lambda i: (0, i)),
            ]
            out_shape += [
                jax.ShapeDtypeStruct((t_total * rpt, LANES), F32),
                jax.ShapeDtypeStruct((ne, t_total), F32),
            ]
        in_specs.append(pl.BlockSpec(memory_space=pl.ANY))
        args.append(w)
        outs = pl.pallas_call(
            functools.partial(_out_kernel, seg_tiles=seg_tiles, n_lhs=n_lhs, part=part, n_parts=n_parts),
            grid=(t_total // tm,),
            in_specs=in_specs,
            out_specs=out_specs,
            out_shape=out_shape,
            scratch_shapes=[
                pltpu.VMEM((kdim, dn), BF16),
                pltpu.VMEM((2, stage_rows, dn), F32),
                pltpu.SemaphoreType.DMA((2,)),
            ],
            compiler_params=_params("arbitrary"),
        )(*args)
        x1_parts.append(outs[0])
    return x1_parts, outs[1], outs[2]


def _first_index(vals, target):
    idx = jnp.full(target.shape, len(vals) - 1, jnp.int32)
    for k in range(len(vals) - 2, -1, -1):
        idx = jnp.where(vals[k] == target, k, idx)
    return idx


def _pick(vals, idx):
    out = vals[-1]
    for k in range(len(vals) - 2, -1, -1):
        out = jnp.where(idx == k, vals[k], out)
    return out


def _route_kernel(lg_ref, rb_ref, pos_ref, wt_ref, meta_ref, rank_ref, *, tile_rows, scan):
    logits = lg_ref[...]
    t = logits.shape[1]
    scores = jax.nn.sigmoid(logits)
    biased = scores + rb_ref[...]
    s_rows = [scores[e:e + 1, :] for e in range(N_EXPERTS)]
    b_rows = [biased[e:e + 1, :] for e in range(N_EXPERTS)]
    grp = []
    for g in range(N_GROUPS):
        r = b_rows[g * GROUP_SIZE:(g + 1) * GROUP_SIZE]
        best = None
        for a in range(GROUP_SIZE):
            for b in range(a + 1, GROUP_SIZE):
                pair = r[a] + r[b]
                best = pair if best is None else jnp.maximum(best, pair)
        grp.append(best)
    gmax = functools.reduce(jnp.maximum, grp)
    gsel = _first_index(grp, gmax)
    cand_b = [_pick([b_rows[g * GROUP_SIZE + k] for g in range(N_GROUPS)], gsel) for k in range(GROUP_SIZE)]
    cand_s = [_pick([s_rows[g * GROUP_SIZE + k] for g in range(N_GROUPS)], gsel) for k in range(GROUP_SIZE)]
    m1 = functools.reduce(jnp.maximum, cand_b)
    k1 = _first_index(cand_b, m1)
    rest = [jnp.where(k1 == k, -jnp.inf, cand_b[k]) for k in range(GROUP_SIZE)]
    m2 = functools.reduce(jnp.maximum, rest)
    k2 = _first_index(rest, m2)
    w1 = _pick(cand_s, k1)
    w2 = _pick(cand_s, k2)
    wsum = w1 + w2
    w1 = w1 / wsum
    w2 = w2 / wsum
    e1 = gsel * GROUP_SIZE + k1
    e2 = gsel * GROUP_SIZE + k2
    wt_ref[...] = jnp.zeros_like(wt_ref)
    wt_ref[0:1, :] = w1
    wt_ref[1:2, :] = w2

    eid = lax.broadcasted_iota(jnp.int32, logits.shape, 0)
    sel1 = eid == e1
    sel2 = eid == e2
    ind = jnp.where(sel1 | sel2, 1.0, 0.0)
    ua = lax.broadcasted_iota(jnp.int32, (scan, scan), 0)
    ub = lax.broadcasted_iota(jnp.int32, (scan, scan), 1)
    upper = jnp.where(ua <= ub, 1.0, 0.0).astype(BF16)
    carry = jnp.zeros((N_EXPERTS, 1), F32)
    for blk in range(t // scan):
        seg = ind[:, blk * scan:(blk + 1) * scan]
        incl = _dot(seg.astype(BF16), upper)
        rank_ref[:, blk * scan:(blk + 1) * scan] = incl - seg + carry
        carry = carry + incl[:, scan - 1:scan]
    n_tiles = jnp.floor((carry + (tile_rows - 1.0)) * (1.0 / tile_rows))
    la = lax.broadcasted_iota(jnp.int32, (N_EXPERTS, N_EXPERTS), 0)
    lb = lax.broadcasted_iota(jnp.int32, (N_EXPERTS, N_EXPERTS), 1)
    lower = jnp.where(lb < la, 1.0, 0.0).astype(BF16)
    start = _dot(lower, jnp.broadcast_to(n_tiles, (N_EXPERTS, LANES)).astype(BF16))
    end = start + n_tiles
    slot = start[:, 0:1] * tile_rows + rank_ref[...]
    pos_ref[...] = jnp.zeros_like(pos_ref)
    pos_ref[0:1, :] = jnp.sum(jnp.where(sel1, slot, 0.0), axis=0, keepdims=True).astype(jnp.int32)
    pos_ref[1:2, :] = jnp.sum(jnp.where(sel2, slot, 0.0), axis=0, keepdims=True).astype(jnp.int32)

    n_used = end[N_EXPERTS - 1:N_EXPERTS, :]
    tile = lax.broadcasted_iota(jnp.int32, (N_EXPERTS, LANES), 1).astype(F32)
    tile = jnp.minimum(tile, n_used - 1.0)
    tile_expert = jnp.sum(jnp.where(tile >= end, 1.0, 0.0), axis=0, keepdims=True)
    lane = lax.broadcasted_iota(jnp.int32, (1, LANES), 1)
    meta_ref[...] = jnp.zeros_like(meta_ref)
    meta_ref[0:1, :] = jnp.where(lane == LANES - 1, n_used, tile_expert).astype(jnp.int32)
    diag = lax.broadcasted_iota(jnp.int32, (N_EXPERTS, LANES), 0) == lax.broadcasted_iota(
        jnp.int32, (N_EXPERTS, LANES), 1)
    meta_ref[1:2, :] = jnp.sum(jnp.where(diag, carry, 0.0), axis=0, keepdims=True).astype(jnp.int32)
    meta_ref[2:3, :] = jnp.sum(jnp.where(diag, start * tile_rows, 0.0), axis=0, keepdims=True).astype(jnp.int32)


def _route(logits_t, router_b, tile_rows, scan=512):
    ne, t = logits_t.shape
    return pl.pallas_call(
        functools.partial(_route_kernel, tile_rows=tile_rows, scan=scan),
        out_shape=[
            jax.ShapeDtypeStruct((8, t), jnp.int32),
            jax.ShapeDtypeStruct((8, t), F32),
            jax.ShapeDtypeStruct((8, LANES), jnp.int32),
        ],
        scratch_shapes=[pltpu.VMEM((ne, t), F32)],
        compiler_params=pltpu.CompilerParams(vmem_limit_bytes=VMEM_LIMIT_BYTES),
    )(logits_t, router_b.reshape(ne, 1))


def _dispatch_kernel(pos_ref, meta_ref, h_ref, xs_ref, zero_ref, sem, *, t, rpt, chunk, tile_rows, n_tiles):
    c = pl.program_id(0)

    def slot_rows(slot):
        return xs_ref.at[pl.ds(pl.multiple_of(slot * rpt, rpt), rpt)]

    def body(r, carry):
        tok = c * chunk + r
        src = h_ref.at[pl.ds(pl.multiple_of(r * rpt, rpt), rpt)]
        for k in range(2):
            pltpu.make_async_copy(src, slot_rows(pos_ref[k * t + tok]), sem.at[0]).start(priority=k)
        return carry

    lax.fori_loop(0, chunk, body, 0, unroll=8)

    @pl.when(c == 0)
    def _():
        zero_ref[...] = jnp.zeros_like(zero_ref)
        zero_row = zero_ref.at[pl.ds(0, rpt)]
        for e in range(N_EXPERTS):
            count = meta_ref[LANES + e]
            first = meta_ref[2 * LANES + e] + count
            n_pad = (tile_rows - count % tile_rows) % tile_rows

            def pad_start(r, carry, first=first):
                pltpu.make_async_copy(zero_row, slot_rows(first + r), sem.at[1]).start()
                return carry

            def pad_wait(r, carry, first=first):
                pltpu.make_async_copy(zero_row, slot_rows(first + r), sem.at[1]).wait()
                return carry

            lax.fori_loop(0, n_pad, pad_start, 0)
            lax.fori_loop(0, n_pad, pad_wait, 0)

        def tile_copy(n):
            rows = tile_rows * rpt
            return pltpu.make_async_copy(zero_ref, xs_ref.at[pl.ds(pl.multiple_of(n * rows, rows), rows)],
                                         sem.at[1])

        def unused_start(n, carry):
            tile_copy(n).start()
            return carry

        def unused_wait(n, carry):
            tile_copy(n).wait()
            return carry

        lax.fori_loop(meta_ref[LANES - 1], n_tiles, unused_start, 0)
        lax.fori_loop(meta_ref[LANES - 1], n_tiles, unused_wait, 0)

    for _ in range(2):
        pltpu.make_async_copy(h_ref, xs_ref.at[pl.ds(0, chunk * rpt)], sem.at[0]).wait()


def _dispatch(pos_flat, meta_flat, h_lin, n_slots, rpt, tile_rows, chunk=512):
    t = pos_flat.shape[0] // 2
    return pl.pallas_call(
        functools.partial(_dispatch_kernel, t=t, rpt=rpt, chunk=chunk, tile_rows=tile_rows,
                          n_tiles=n_slots // tile_rows),
        grid_spec=pltpu.PrefetchScalarGridSpec(
            num_scalar_prefetch=2,
            grid=(t // chunk,),
            in_specs=[pl.BlockSpec((chunk * rpt, LANES), lambda c, p, m: (c, 0))],
            out_specs=pl.BlockSpec(memory_space=pl.ANY),
            scratch_shapes=[pltpu.VMEM((tile_rows * rpt, LANES), h_lin.dtype), pltpu.SemaphoreType.DMA((2,))],
        ),
        out_shape=jax.ShapeDtypeStruct((n_slots * rpt, LANES), h_lin.dtype),
        compiler_params=_params("arbitrary"),
    )(pos_flat, meta_flat, h_lin)


def _ffn_kernel(meta_ref, xs_ref, wg_ref, wu_ref, wd_ref, ys_ref, wgb_ref, wub_ref, wdb_ref, *, tm):
    n = pl.program_id(0)
    n_used = meta_ref[LANES - 1]

    @pl.when(n < n_used)
    def _():
        @pl.when((n == 0) | (meta_ref[n] != meta_ref[jnp.maximum(n - 1, 0)]))
        def _():
            wgb_ref[...] = wg_ref[...].astype(BF16)
            wub_ref[...] = wu_ref[...].astype(BF16)
            wdb_ref[...] = wd_ref[...].astype(BF16)

        d = wgb_ref.shape[0]
        a = b = None
        for k0 in range(0, d, MXU_DIM):
            x = jnp.concatenate(
                [_load_token_major(xs_ref, tm, k0 // LANES + j).astype(BF16) for j in range(MXU_DIM // LANES)],
                axis=1)
            pa = _dot(x, wgb_ref[k0:k0 + MXU_DIM, :])
            pb = _dot(x, wub_ref[k0:k0 + MXU_DIM, :])
            a = pa if a is None else a + pa
            b = pb if b is None else b + pb
        act = (_silu(a) * b).astype(BF16)
        for n0 in range(0, d, MXU_DIM):
            _store_token_major(ys_ref, _dot(act, wdb_ref[:, n0:n0 + MXU_DIM]), n0)

    @pl.when(n >= n_used)
    def _():
        ys_ref[...] = jnp.zeros_like(ys_ref)


def _ffn(meta, xs, w_gate, w_up, w_down, layer, tm, n_tiles):
    _, ne, d, f = w_gate.shape
    rpt = d // LANES

    def row_tile(n, m):
        return (jnp.minimum(n, m[LANES - 1] - 1), 0)

    return pl.pallas_call(
        functools.partial(_ffn_kernel, tm=tm),
        grid_spec=pltpu.PrefetchScalarGridSpec(
            num_scalar_prefetch=1,
            grid=(n_tiles,),
            in_specs=[
                pl.BlockSpec((tm * rpt, LANES), row_tile),
                pl.BlockSpec((None, None, d, f), lambda n, m: (layer, m[n], 0, 0)),
                pl.BlockSpec((None, None, d, f), lambda n, m: (layer, m[n], 0, 0)),
                pl.BlockSpec((None, None, f, d), lambda n, m: (layer, m[n], 0, 0)),
            ],
            out_specs=pl.BlockSpec((tm * rpt, LANES), lambda n, m: (n, 0)),
            scratch_shapes=[
                pltpu.VMEM((d, f), BF16),
                pltpu.VMEM((d, f), BF16),
                pltpu.VMEM((f, d), BF16),
            ],
        ),
        out_shape=jax.ShapeDtypeStruct((n_tiles * tm * rpt, LANES), F32),
        compiler_params=_params("arbitrary"),
    )(meta, xs, w_gate, w_up, w_down)


def _combine_kernel(pos_ref, ys_ref, *refs, t, rpt, tm, n_parts, final_tiles):
    x_refs = refs[:n_parts]
    if final_tiles is None:
        wt_ref, gate_ref, o_ref, a0, b0, a1, b1, sem = refs[n_parts:]
    else:
        wt_ref, gate_ref, fg_ref, op_ref, os_ref, a0, b0, a1, b1, sem = refs[n_parts:]
    i = pl.program_id(0)
    n = pl.num_programs(0)
    bufs = ((a0, b0), (a1, b1))

    def issue(tile, s):
        def body(r, carry):
            tok = tile * tm + r
            for k in range(2):
                slot_row = pos_ref[k * t + tok]
                pltpu.make_async_copy(
                    ys_ref.at[pl.ds(pl.multiple_of(slot_row * rpt, rpt), rpt)],
                    bufs[s][k].at[pl.ds(pl.multiple_of(r * rpt, rpt), rpt)],
                    sem.at[s]).start(priority=k)
            return carry

        lax.fori_loop(0, tm, body, 0, unroll=8)

    def wait(s):
        for k in range(2):
            pltpu.make_async_copy(ys_ref.at[pl.ds(0, tm * rpt)], bufs[s][k], sem.at[s]).wait()

    def compute(s, out_ref):
        w = wt_ref[...]
        w0, w1 = w[:, 0:1], w[:, 1:2]
        dn = x_refs[0].shape[1]
        for c in range(rpt):
            cols = slice(c * LANES, (c + 1) * LANES)
            part, pc = divmod(c * LANES, dn)
            y = w0 * _load_token_major(bufs[s][0], tm, c) + w1 * _load_token_major(bufs[s][1], tm, c)
            out_ref[:, cols] = x_refs[part][:, pc:pc + LANES] + gate_ref[:, cols] * y
        if final_tiles is not None:
            out_ref[...] = _rms_f32(out_ref[...]) * fg_ref[...]

    @pl.when(i == 0)
    def _():
        issue(0, 0)

    for s in range(2):
        @pl.when(i % 2 == s)
        def _(s=s):
            @pl.when(i + 1 < n)
            def _():
                issue(i + 1, 1 - s)

            wait(s)
            if final_tiles is None:
                compute(s, o_ref)
            else:
                @pl.when(i < final_tiles)
                def _():
                    compute(s, op_ref)

                @pl.when(i >= final_tiles)
                def _():
                    compute(s, os_ref)


def _combine(pos_flat, ys, x1_parts, wt, mod3, layer, row_fn, final=None, tm=256):
    t, dn = x1_parts[0].shape
    n_parts = len(x1_parts)
    d = dn * n_parts
    rpt = d // LANES
    base = layer * 48
    in_specs = ([pl.BlockSpec(memory_space=pl.ANY)]
                + [pl.BlockSpec((tm, dn), lambda i, p: (i, 0))] * n_parts
                + [pl.BlockSpec((tm, 2), lambda i, p: (i, 0)),
                   pl.BlockSpec((None, 1, d), lambda i, p: (base + row_fn(i) * 6 + 5, 0, 0))])
    args = [pos_flat, ys, *x1_parts, wt, mod3]
    if final is None:
        final_tiles = None
        out_specs = pl.BlockSpec((tm, d), lambda i, p: (i, 0))
        out_shape = jax.ShapeDtypeStruct((t, d), F32)
    else:
        fg, t_ctx = final
        final_tiles = t_ctx // tm
        in_specs.append(pl.BlockSpec((1, d), lambda i, p: (0, 0)))
        args.append(fg.reshape(1, d))
        out_specs = [
            pl.BlockSpec((tm, d), lambda i, p: (jnp.minimum(i, final_tiles - 1), 0)),
            pl.BlockSpec((tm, d), lambda i, p: (jnp.maximum(i - final_tiles, 0), 0)),
        ]
        out_shape = [jax.ShapeDtypeStruct((t_ctx, d), F32), jax.ShapeDtypeStruct((t - t_ctx, d), F32)]
    return pl.pallas_call(
        functools.partial(_combine_kernel, t=t, rpt=rpt, tm=tm, n_parts=n_parts, final_tiles=final_tiles),
        grid_spec=pltpu.PrefetchScalarGridSpec(
            num_scalar_prefetch=1,
            grid=(t // tm,),
            in_specs=in_specs,
            out_specs=out_specs,
            scratch_shapes=[pltpu.VMEM((tm * rpt, LANES), F32)] * 4 + [pltpu.SemaphoreType.DMA((2,))],
        ),
        out_shape=out_shape,
        compiler_params=_params("arbitrary"),
    )(*args)


def _moe(lg, router_b, h_lin, x1, w_gate, w_up, w_down, mod3, layer, row_fn, final=None, tile_rows=256):
    t = x1[0].shape[0]
    n_tiles = (2 * t) // tile_rows + N_EXPERTS
    pos, wts, meta = _route(lg, router_b, tile_rows)
    pos_flat = pos[:2].reshape(-1)
    xs = _dispatch(pos_flat, meta[:3].reshape(-1), h_lin, n_tiles * tile_rows, h_lin.shape[0] // t, tile_rows)
    ys = _ffn(meta[0], xs, w_gate, w_up, w_down, layer, tile_rows, n_tiles)
    return _combine(pos_flat, ys, x1, wts[:2].T, mod3, layer, row_fn, final)


def _sgu_kernel(u0_ref, u1_ref, v0_ref, v1_ref, ng_ref, ws_ref, bs_ref, o0_ref, o1_ref, *, n_chunks):
    half = v0_ref.shape[1]
    gw = 2 * half // H_C
    v0 = v0_ref[...].astype(F32)
    v1 = v1_ref[...].astype(F32)
    ms = (jnp.sum(v0 * v0, axis=-1, keepdims=True) + jnp.sum(v1 * v1, axis=-1, keepdims=True)) / (2 * half)
    inv = lax.rsqrt(ms + EPS)
    halves = ((u0_ref, (v0 * inv * ng_ref[:, :half]).astype(BF16), o0_ref),
              (u1_ref, (v1 * inv * ng_ref[:, half:]).astype(BF16), o1_ref))
    for n in range(n_chunks):
        r0 = n * CHUNK_C
        for h in range(H_C):
            u_ref, vn, o_ref = halves[h // (H_C // 2)]
            c0 = (h % (H_C // 2)) * gw
            mixed = _dot(ws_ref[h].astype(BF16), vn[r0:r0 + CHUNK_C, c0:c0 + gw]) + bs_ref[h]
            u = u_ref[r0:r0 + CHUNK_C, c0:c0 + gw].astype(F32)
            o_ref[r0:r0 + CHUNK_C, c0:c0 + gw] = (u * mixed).astype(BF16)


def _sgu(u0, u1, v0, v1, norm_g, w_s, b_s, tm=256):
    t, half = u0.shape
    blk = pl.BlockSpec((tm, half), lambda i: (i, 0))
    return pl.pallas_call(
        functools.partial(_sgu_kernel, n_chunks=tm // CHUNK_C),
        grid=(t // tm,),
        in_specs=[
            blk, blk, blk, blk,
            pl.BlockSpec((1, 2 * half), lambda i: (0, 0)),
            pl.BlockSpec((H_C, CHUNK_C, CHUNK_C), lambda i: (0, 0, 0)),
            pl.BlockSpec((H_C, CHUNK_C, 1), lambda i: (0, 0, 0)),
        ],
        out_specs=[blk, blk],
        out_shape=[jax.ShapeDtypeStruct((t, half), BF16)] * 2,
        compiler_params=_params("arbitrary"),
    )(u0, u1, v0, v1, norm_g.reshape(1, 2 * half), w_s, b_s.reshape(H_C, CHUNK_C, 1))


def _rope_tables(n_tok):
    rows = n_tok // GRID_W
    pos_r = jnp.repeat(jnp.arange(rows), GRID_W).astype(F32)
    pos_c = jnp.tile(jnp.arange(GRID_W), rows).astype(F32)
    n_freq = DK_A // 4
    inv = jnp.power(ROPE_BASE, -jnp.arange(n_freq, dtype=F32) / n_freq)
    ang_r = pos_r[:, None] * inv
    ang_c = pos_c[:, None] * inv
    cos_t = jnp.concatenate([jnp.cos(ang_r), jnp.cos(ang_r), jnp.cos(ang_c), jnp.cos(ang_c)], axis=1)
    sin_t = jnp.concatenate([-jnp.sin(ang_r), jnp.sin(ang_r), -jnp.sin(ang_c), jnp.sin(ang_c)], axis=1)
    return cos_t, sin_t


def kernel(x_prompt, x_sample, cache_k, cache_v, state_ret, c, c_ctx, norm_g, w_mod, b_mod, w_in_even, w_out_even, lam_q, lam_k, ret_decay_logit, w_in_odd, gmlp_norm_g, w_spatial, b_spatial, w_out_odd, router_w, router_b, w_gate, w_up, w_down, final_norm_g):
    batch, seq, d = x_prompt.shape
    dbatch, dseq, _ = x_sample.shape
    past = cache_k.shape[2]
    depth = norm_g.shape[0]
    t_ctx, t_lat = batch * seq, dbatch * dseq
    t_all = t_ctx + t_lat
    xc = x_prompt.reshape(t_ctx, d)
    xl = x_sample.reshape(t_lat, d)

    cond8 = jnp.concatenate([c_ctx[None, :], c, jnp.zeros((8 - 1 - dbatch, d), F32)], axis=0)
    mod3 = _modulation(cond8, w_mod, b_mod).reshape(depth * 8 * 6, 1, d)
    router_wt = router_w.T
    cos_t, sin_t = _rope_tables(dseq)

    tm_in = 512
    tm_out = 512
    tm_cmb = 256
    w_in0 = w_in_even.reshape(d, -1)
    w_out0 = w_out_even.reshape(-1, d)
    w_in1 = w_in_odd.reshape(d, -1)
    w_out1 = w_out_odd.reshape(-1, d)
    lq, lk, dlog = lam_q.reshape(2, DK_A), lam_k.reshape(2, DK_A), ret_decay_logit.reshape(2, H_B)

    def rows_ctx(i):
        return 0

    def rows_lat(tiles_per_batch):
        return lambda i: 1 + i // tiles_per_batch

    def rows_all(tm):
        n_ctx, per = t_ctx // tm, dseq // tm
        return lambda i: jnp.maximum(i - n_ctx, 0) // per + jnp.where(i >= n_ctx, 1, 0)

    lam_init = 0.8 - 0.6 * math.exp(-0.3 * 0)
    half0 = w_in0.shape[1] // 2
    g00 = norm_g[0, 0]
    rl = rows_lat(dseq // tm_in)
    pa_c, k_new, v_new = _in_proj(xc, g00, mod3, 0, rows_ctx, w_in0, 0, half0, "even_kv", tm_in)
    (pb_c,) = _in_proj(xc, g00, mod3, 0, rows_ctx, w_in0, half0, half0, "even", tm_in)
    (pa_l,) = _in_proj(xl, g00, mod3, 0, rl, w_in0, 0, half0, "even", tm_in)
    (pb_l,) = _in_proj(xl, g00, mod3, 0, rl, w_in0, half0, half0, "even", tm_in)
    oa_c, or_c, s_new = _mixer(pa_c, pb_c, lq, lk, dlog, lam_init, batch, seq, H_A)
    ctx = (cache_k.reshape(dbatch, past, H_A * 2 * DK_A), cache_v.reshape(dbatch, past, H_A * DV_A),
           state_ret.reshape(dbatch, 2, H_B, DK_B, DV_B), cos_t, sin_t)
    oa_l, or_l = _mixer(pa_l, pb_l, lq, lk, dlog, lam_init, dbatch, dseq, 1, ctx)
    x1, h2, lg = _out_proj([([oa_c, or_c], xc), ([oa_l, or_l], xl)], w_out0, norm_g[0, 1], mod3, 0,
                           rows_all(tm_out), router_wt, tm=tm_out)
    x2 = _moe(lg, router_b, h2, x1, w_gate, w_up, w_down, mod3, 0, rows_all(tm_cmb))

    q1 = w_in1.shape[1] // 4
    zs = [_in_proj(x2, norm_g[1, 0], mod3, 1, rows_all(tm_in), w_in1, j * q1, q1, "odd", tm_in)[0]
          for j in range(4)]
    g0, g1 = _sgu(zs[0], zs[1], zs[2], zs[3], gmlp_norm_g.reshape(-1),
                  w_spatial.reshape(H_C, CHUNK_C, CHUNK_C), b_spatial.reshape(H_C, CHUNK_C))
    x3, h4, lg2 = _out_proj([([g0, g1], x2)], w_out1, norm_g[1, 1], mod3, 1, rows_all(tm_out), router_wt,
                            tm=tm_out)
    y_p, y_s = _moe(lg2, router_b, h4, x3, w_gate, w_up, w_down, mod3, 1, rows_all(tm_cmb),
                    final=(final_norm_g, t_ctx))

    y_prompt = y_p.reshape(batch, seq, d)
    y_sample = y_s.reshape(dbatch, dseq, d)
    new_cache_k = k_new.reshape(batch, 1, seq, H_A, 2, DK_A)
    new_cache_v = v_new.reshape(batch, 1, seq, DV_A // LANES, H_A, LANES).swapaxes(3, 4).reshape(
        batch, 1, seq, H_A, DV_A)
    new_state_ret = s_new.reshape(batch, 1, 2, H_B, DK_B, DV_B)
    return (y_prompt, y_sample, new_cache_k, new_cache_v, new_state_ret)
```

```python
import functools
import math

import jax
import jax.numpy as jnp
from jax import lax
from jax.experimental import pallas as pl
from jax.experimental.pallas import tpu as pltpu

F32 = jnp.float32
BF16 = jnp.bfloat16

EPS = 1e-6
ROPE_BASE = 10000.0
GRID_W = 64
H_A, DK_A, DV_A = 4, 128, 256
H_B, DK_B, DV_B = 4, 128, 256
RET_CHUNK = 128
H_C, CHUNK_C = 8, 128
N_EXPERTS, N_GROUPS, GROUP_SIZE = 16, 4, 4

VMEM_LIMIT_BYTES = 56 * 1024 * 1024


def _params(*sem):
    return pltpu.CompilerParams(dimension_semantics=sem, vmem_limit_bytes=VMEM_LIMIT_BYTES)


def _rms_f32(x):
    return x * lax.rsqrt(jnp.mean(x * x, axis=-1, keepdims=True) + EPS)


def _silu(x):
    return x * jax.nn.sigmoid(x)


def _dot(a, b):
    return jnp.dot(a, b, preferred_element_type=F32)


def _dot_nt(a, b):
    return lax.dot_general(a, b, (((1,), (1,)), ((), ())), preferred_element_type=F32)


LANES = 128
MXU_DIM = 256


def _store_token_major(ref, val, col0=0):
    tm, width = val.shape
    rpt = ref.shape[0] // tm
    for c in range(width // LANES):
        ref[pl.ds(col0 // LANES + c, tm, stride=rpt), :] = val[:, c * LANES:(c + 1) * LANES]


def _load_token_major(ref, tm, c):
    rpt = ref.shape[0] // tm
    return ref[pl.ds(c, tm, stride=rpt), :]


def _mod_kernel(c_ref, w_ref, b_ref, o_ref):
    s = _silu(c_ref[...]).astype(BF16)
    o_ref[...] = _dot(s, w_ref[...].astype(BF16)) + b_ref[...]


def _modulation(cond8, w_mod, b_mod, tn=1024):
    depth, d, n = w_mod.shape
    return pl.pallas_call(
        _mod_kernel,
        grid=(depth, n // tn),
        in_specs=[
            pl.BlockSpec((8, d), lambda l, j: (0, 0)),
            pl.BlockSpec((None, d, tn), lambda l, j: (l, 0, j)),
            pl.BlockSpec((None, 1, tn), lambda l, j: (l, 0, j)),
        ],
        out_specs=pl.BlockSpec((None, 8, tn), lambda l, j: (l, 0, j)),
        out_shape=jax.ShapeDtypeStruct((depth, 8, n), F32),
        compiler_params=_params("arbitrary", "arbitrary"),
    )(cond8, w_mod, b_mod.reshape(depth, 1, n))


def _load_weight_slab(w_hbm, slab_ref, stage_ref, sem, col0):
    kdim, n = slab_ref.shape
    rows = stage_ref.shape[1]
    nb = kdim // rows

    def copy(j):
        return pltpu.make_async_copy(w_hbm.at[pl.ds(j * rows, rows), pl.ds(col0, n)],
                                     stage_ref.at[j % 2], sem.at[j % 2])

    copy(0).start()
    for j in range(nb):
        if j + 1 < nb:
            copy(j + 1).start()
        copy(j).wait()
        slab_ref[j * rows:(j + 1) * rows, :] = stage_ref[j % 2].astype(BF16)


def _in_kernel(x_ref, g_ref, sh_ref, sc_ref, w_hbm, *rest, col0, mode, kv_cols, chunk):
    if mode == "even_kv":
        p_ref, k_ref, v_ref, slab_ref, stage_ref, h_ref, sem = rest
    else:
        p_ref, slab_ref, stage_ref, h_ref, sem = rest

    @pl.when(pl.program_id(0) == 0)
    def _():
        _load_weight_slab(w_hbm, slab_ref, stage_ref, sem, col0)

    h = (_rms_f32(x_ref[...]) * g_ref[...]) * (1.0 + sc_ref[...]) + sh_ref[...]
    h_ref[...] = h.astype(BF16)
    n = slab_ref.shape[1]
    for c0 in range(0, n, chunk):
        acc = _dot(h_ref[...], slab_ref[:, c0:c0 + chunk])
        if mode == "odd":
            p_ref[:, c0:c0 + chunk] = (0.5 * acc * (1.0 + lax.erf(acc * (2.0 ** -0.5)))).astype(BF16)
        else:
            p_ref[:, c0:c0 + chunk] = acc.astype(BF16)
        if mode == "even_kv":
            (k0, k1), (v0, v1) = kv_cols
            a0 = col0 + c0
            if k0 <= a0 < k1:
                _store_token_major(k_ref, acc, a0 - k0)
            if v0 <= a0 < v1:
                tm = acc.shape[0]
                rpt = v_ref.shape[0] // tm
                for cc in range(chunk // LANES):
                    head, half = divmod((a0 - v0) // LANES + cc, DV_A // LANES)
                    v_ref[pl.ds(half * H_A + head, tm, stride=rpt), :] = acc[:, cc * LANES:(cc + 1) * LANES]


def _in_proj(x, g, mod3, layer, row_fn, w, col0, n_cols, mode, tm=512, chunk=512, stage_rows=256):
    t, d = x.shape
    base = layer * 48

    def mod_spec(kind):
        return pl.BlockSpec((None, 1, d), lambda i: (base + row_fn(i) * 6 + kind, 0, 0))

    in_specs = [
        pl.BlockSpec((tm, d), lambda i: (i, 0)),
        pl.BlockSpec((1, d), lambda i: (0, 0)),
        mod_spec(0),
        mod_spec(1),
        pl.BlockSpec(memory_space=pl.ANY),
    ]
    out_specs = [pl.BlockSpec((tm, n_cols), lambda i: (i, 0))]
    out_shape = [jax.ShapeDtypeStruct((t, n_cols), BF16)]
    kv_cols = None
    if mode == "even_kv":
        a_qk = H_A * 2 * DK_A
        a_v = H_A * DV_A
        kv_cols = ((a_qk, 2 * a_qk), (2 * a_qk, 2 * a_qk + a_v))
        for width in (a_qk, a_v):
            out_specs.append(pl.BlockSpec((tm * width // LANES, LANES), lambda i: (i, 0)))
            out_shape.append(jax.ShapeDtypeStruct((t * width // LANES, LANES), F32))
    return pl.pallas_call(
        functools.partial(_in_kernel, col0=col0, mode=mode, kv_cols=kv_cols, chunk=chunk),
        grid=(t // tm,),
        in_specs=in_specs,
        out_specs=out_specs,
        out_shape=out_shape,
        scratch_shapes=[
            pltpu.VMEM((d, n_cols), BF16),
            pltpu.VMEM((2, stage_rows, n_cols), F32),
            pltpu.VMEM((tm, d), BF16),
            pltpu.SemaphoreType.DMA((2,)),
        ],
        compiler_params=_params("arbitrary"),
    )(x, g.reshape(1, d), mod3, mod3, w)


def _rope(x, c, s):
    lane = lax.broadcasted_iota(jnp.int32, x.shape, 1)
    partner = jnp.where((lane & 32) == 0, pltpu.roll(x, 96, 1), pltpu.roll(x, 32, 1))
    return x * c + partner * s


def _mixer_kernel(*refs, seq, has_ctx, lam_init, past, hps):
    if has_ctx:
        (aq_ref, ak_ref, av_ref, bq_ref, bk_ref, bv_ref, bg_ref, lq_ref, lk_ref, dl_ref,
         ck_ref, cv_ref, s0_ref, cos_ref, sin_ref,
         oa_ref, or_ref, kall_ref, vall_ref, of_ref, ob_ref, stf_ref, stb_ref) = refs
    else:
        (aq_ref, ak_ref, av_ref, bq_ref, bk_ref, bv_ref, bg_ref, lq_ref, lk_ref, dl_ref,
         oa_ref, or_ref, sout_ref, of_ref, ob_ref, stf_ref, stb_ref) = refs
    tq = 256
    rt = 256
    straight = seq <= tq
    cs = RET_CHUNK
    nc = seq // cs

    prod = lq_ref[...] * lk_ref[...]
    ee = jnp.exp(jnp.sum(prod, axis=1, keepdims=True))
    lam = ee[0:1, :] - ee[1:2, :] + lam_init
    scale = DK_A ** -0.5
    ii = lax.broadcasted_iota(jnp.int32, (cs, cs), 0)
    jj = lax.broadcasted_iota(jnp.int32, (cs, cs), 1)
    ic = lax.broadcasted_iota(jnp.int32, (cs, 1), 0).astype(F32)

    def loop(n, body):
        if straight:
            for it in range(n):
                body(it)
        else:
            lax.fori_loop(0, n, lambda it, c: (body(it), c)[1], 0)

    def rows(it, size):
        return pl.ds(it * size, size) if straight else pl.ds(pl.multiple_of(it * size, size), size)

    for hh in range(hps):
        qk_cols = slice(hh * 2 * DK_A, (hh + 1) * 2 * DK_A)
        va_cols = slice(hh * DV_A, (hh + 1) * DV_A)
        kb_cols = slice(hh * DK_B, (hh + 1) * DK_B)
        vb_cols = slice(hh * DV_B, (hh + 1) * DV_B)

        if has_ctx:
            for c in range(2):
                kall_ref[c, 0:past, :] = ck_ref[:, c * DK_A:(c + 1) * DK_A].astype(BF16)
            vall_ref[0:past, :] = cv_ref[...].astype(BF16)

            def copy_body(r):
                rr = rows(r, rt)
                cs_, sn_ = cos_ref[rr, :], sin_ref[rr, :]
                kk = ak_ref[rr, qk_cols].astype(F32)
                for c in range(2):
                    kall_ref[c, pl.ds(past + pl.multiple_of(r * rt, rt), rt), :] = _rope(
                        kk[:, c * DK_A:(c + 1) * DK_A], cs_, sn_).astype(BF16)
                vall_ref[pl.ds(past + pl.multiple_of(r * rt, rt), rt), :] = av_ref[rr, va_cols]

            loop(seq // rt, copy_body)

        def attn_body(qt, qk_cols=qk_cols, va_cols=va_cols):
            rr = rows(qt, tq)
            q = aq_ref[rr, qk_cols]
            if has_ctx:
                qf = q.astype(F32)
                cs_, sn_ = cos_ref[rr, :], sin_ref[rr, :]
            probs = []
            for c in range(2):
                if has_ctx:
                    qc = _rope(qf[:, c * DK_A:(c + 1) * DK_A], cs_, sn_).astype(BF16)
                    keys = kall_ref[c]
                else:
                    qc = q[:, c * DK_A:(c + 1) * DK_A]
                    keys = ak_ref[:, hh * 2 * DK_A + c * DK_A:hh * 2 * DK_A + (c + 1) * DK_A]
                s = _dot_nt(qc, keys) * scale
                e = jnp.exp(s - jnp.max(s, axis=-1, keepdims=True))
                probs.append(e * (1.0 / jnp.sum(e, axis=-1, keepdims=True)))
            a = (probs[0] - lam * probs[1]).astype(BF16)
            o = _dot(a, vall_ref[...] if has_ctx else av_ref[:, va_cols])
            oa_ref[rr, va_cols] = (_rms_f32(o) * (1.0 - lam_init)).astype(BF16)

        loop(seq // tq, attn_body)

        tables = []
        for d in range(2):
            lg = jax.nn.log_sigmoid(dl_ref[d, hh])
            diff = (ii - jj if d == 0 else jj - ii).astype(F32)
            mask = jnp.where(diff >= 0, jnp.exp(jnp.maximum(diff, 0.0) * lg), 0.0)
            if d == 0:
                qdec = jnp.exp((ic + 1.0) * lg)
                kdec = jnp.exp((cs - 1.0 - ic) * lg)
            else:
                qdec = jnp.exp((cs - ic) * lg)
                kdec = jnp.exp(ic * lg)
            tables.append((mask, qdec, kdec, jnp.exp(cs * lg)))
        st_refs = (stf_ref, stb_ref)
        o_refs = (of_ref, ob_ref)
        for d in range(2):
            if has_ctx:
                st_refs[d][hh] = s0_ref[d]
            else:
                st_refs[d][hh] = jnp.zeros((DK_B, DV_B), F32)

        def chunk_body(n, hh=hh, kb_cols=kb_cols, vb_cols=vb_cols, tables=tables):
            for d in range(2):
                mask, qdec, kdec, cdec = tables[d]
                rr = rows(n if d == 0 else nc - 1 - n, cs)
                qc = bq_ref[rr, kb_cols].astype(F32)
                kc = bk_ref[rr, kb_cols].astype(F32) * (DK_B ** -0.5)
                vc = bv_ref[rr, vb_cols]
                state = st_refs[d][hh]
                qk = _dot_nt(qc.astype(BF16), kc.astype(BF16)) * mask
                o_refs[d][rr, vb_cols] = (_dot(qk.astype(BF16), vc)
                                          + _dot((qc * qdec).astype(BF16), state.astype(BF16)))
                kd = (kc * kdec).T.astype(BF16)
                st_refs[d][hh] = cdec * state + _dot(kd, vc)

        loop(nc, chunk_body)

        def final_body(r, vb_cols=vb_cols):
            rr = rows(r, rt)
            tot = of_ref[rr, vb_cols] + ob_ref[rr, vb_cols]
            gate = _silu(bg_ref[rr, vb_cols].astype(F32))
            or_ref[rr, vb_cols] = (_rms_f32(tot) * gate).astype(BF16)

        loop(seq // rt, final_body)
        if not has_ctx:
            for d in range(2):
                sout_ref[d, hh] = st_refs[d][hh]


def _mixer(pa, pb, lam_q, lam_k, decay_logit, lam_init, batch, seq, hps, ctx=None):
    has_ctx = ctx is not None
    past = ctx[0].shape[1] if has_ctx else 0

    def col(head_width, off):
        width = hps * head_width
        return pl.BlockSpec((seq, width), lambda b, hg: (b, off // width + hg))

    a_qk = H_A * 2 * DK_A
    b_qk = H_B * DK_B
    b_v = H_B * DV_B
    in_specs = [
        col(2 * DK_A, 0), col(2 * DK_A, a_qk), col(DV_A, 2 * a_qk),
        col(DK_B, 0), col(DK_B, b_qk), col(DV_B, 2 * b_qk), col(DV_B, 2 * b_qk + b_v),
        pl.BlockSpec((2, DK_A), lambda b, hg: (0, 0)),
        pl.BlockSpec((2, DK_A), lambda b, hg: (0, 0)),
        pl.BlockSpec((2, hps, 1, 1), lambda b, hg: (0, hg, 0, 0)),
    ]
    args = [pa] * 3 + [pb] * 4 + [lam_q, lam_k, decay_logit.reshape(2, H_B, 1, 1)]
    out_specs = [
        pl.BlockSpec((seq, hps * DV_A), lambda b, hg: (b, hg)),
        pl.BlockSpec((seq, hps * DV_B), lambda b, hg: (b, hg)),
    ]
    out_shape = [
        jax.ShapeDtypeStruct((batch * seq, H_A * DV_A), BF16),
        jax.ShapeDtypeStruct((batch * seq, b_v), BF16),
    ]
    scratch = []
    if has_ctx:
        assert hps == 1
        ck, cv, s0, cos_t, sin_t = ctx
        in_specs += [
            pl.BlockSpec((None, past, 2 * DK_A), lambda b, hg: (b, 0, hg)),
            pl.BlockSpec((None, past, DV_A), lambda b, hg: (b, 0, hg)),
            pl.BlockSpec((None, 2, None, DK_B, DV_B), lambda b, hg: (b, 0, hg, 0, 0)),
            pl.BlockSpec((seq, DK_A), lambda b, hg: (0, 0)),
            pl.BlockSpec((seq, DK_A), lambda b, hg: (0, 0)),
        ]
        args += [ck, cv, s0, cos_t, sin_t]
        scratch += [pltpu.VMEM((2, seq + past, DK_A), BF16), pltpu.VMEM((seq + past, DV_A), BF16)]
    else:
        out_specs.append(pl.BlockSpec((None, 2, hps, DK_B, DV_B), lambda b, hg: (b, 0, hg, 0, 0)))
        out_shape.append(jax.ShapeDtypeStruct((batch, 2, H_B, DK_B, DV_B), F32))
    scratch += [
        pltpu.VMEM((seq, hps * DV_B), F32),
        pltpu.VMEM((seq, hps * DV_B), F32),
        pltpu.VMEM((hps, DK_B, DV_B), F32),
        pltpu.VMEM((hps, DK_B, DV_B), F32),
    ]
    return pl.pallas_call(
        functools.partial(_mixer_kernel, seq=seq, has_ctx=has_ctx, lam_init=lam_init, past=past, hps=hps),
        grid=(batch, H_A // hps),
        in_specs=in_specs,
        out_specs=out_specs,
        out_shape=out_shape,
        scratch_shapes=scratch,
        compiler_params=_params("arbitrary", "arbitrary"),
    )(*args)


def _out_kernel(*refs, seg_tiles, n_lhs, part, n_parts):
    n_seg = len(seg_tiles)
    per_seg = n_lhs + 1
    seg_refs = [refs[s * per_seg:(s + 1) * per_seg] for s in range(n_seg)]
    rest = refs[n_seg * per_seg:]
    last = part == n_parts - 1
    if last:
        gate_ref, g_ref, sh_ref, sc_ref, rw_ref = rest[:5]
        prev_refs = rest[5:5 + part]
        w_hbm, x1_ref, h2_ref, lg_ref, slab_ref, stage_ref, sem = rest[5 + part:]
    else:
        gate_ref, w_hbm, x1_ref, slab_ref, stage_ref, sem = rest
    i = pl.program_id(0)
    dn = slab_ref.shape[1]

    @pl.when(i == 0)
    def _():
        _load_weight_slab(w_hbm, slab_ref, stage_ref, sem, part * dn)

    def run(lhs_refs, x_ref):
        acc = None
        r0 = 0
        for lhs_ref in lhs_refs:
            kk = lhs_ref.shape[1]
            prod = _dot(lhs_ref[...], slab_ref[r0:r0 + kk, :])
            acc = prod if acc is None else acc + prod
            r0 += kk
        x1 = x_ref[...] + gate_ref[...] * acc
        x1_ref[...] = x1
        if last:
            cols = [p_ref[...] for p_ref in prev_refs] + [x1]
            d = dn * n_parts
            ms = sum(jnp.sum(v * v, axis=-1, keepdims=True) for v in cols) / d
            inv = lax.rsqrt(ms + EPS)
            lg = None
            for j, v in enumerate(cols):
                cs_ = slice(j * dn, (j + 1) * dn)
                h2 = ((v * inv) * g_ref[:, cs_]) * (1.0 + sc_ref[:, cs_]) + sh_ref[:, cs_]
                _store_token_major(h2_ref, h2, j * dn)
                pl_ = lax.dot_general(rw_ref[:, cs_], h2, (((1,), (1,)), ((), ())),
                                      precision=lax.Precision.HIGHEST, preferred_element_type=F32)
                lg = pl_ if lg is None else lg + pl_
            lg_ref[...] = lg

    if n_seg == 1:
        run(seg_refs[0][:n_lhs], seg_refs[0][n_lhs])
    else:
        lo = 0
        for s in range(n_seg):
            @pl.when((i >= lo) & (i < lo + seg_tiles[s]))
            def _(s=s):
                run(seg_refs[s][:n_lhs], seg_refs[s][n_lhs])

            lo += seg_tiles[s]


def _out_proj(segments, w, g, mod3, layer, row_fn, router_wt, n_parts=2, tm=512, stage_rows=256):
    n_lhs = len(segments[0][0])
    d = segments[0][1].shape[1]
    dn = d // n_parts
    kdim = w.shape[0]
    base = layer * 48
    ne = router_wt.shape[0]
    seg_tiles = tuple(x.shape[0] // tm for _, x in segments)
    t_total = sum(x.shape[0] for _, x in segments)
    rpt = d // LANES
    x1_parts = []
    for part in range(n_parts):
        last = part == n_parts - 1
        in_specs, args = [], []
        lo = 0
        for (lhs_list, x), nt in zip(segments, seg_tiles):
            def seg_row(i, lo=lo, nt=nt):
                return jnp.clip(i - lo, 0, nt - 1)

            for lhs in lhs_list:
                in_specs.append(pl.BlockSpec((tm, lhs.shape[1]), lambda i, f=seg_row: (f(i), 0)))
                args.append(lhs)
            in_specs.append(pl.BlockSpec((tm, dn), lambda i, f=seg_row: (f(i), part)))
            args.append(x)
            lo += nt
        in_specs.append(pl.BlockSpec((None, 1, dn), lambda i: (base + row_fn(i) * 6 + 2, 0, part)))
        args.append(mod3)
        out_specs = [pl.BlockSpec((tm, dn), lambda i: (i, 0))]
        out_shape = [jax.ShapeDtypeStruct((t_total, dn), F32)]
        if last:
            in_specs += [
                pl.BlockSpec((1, d), lambda i: (0, 0)),
                pl.BlockSpec((None, 1, d), lambda i: (base + row_fn(i) * 6 + 3, 0, 0)),
                pl.BlockSpec((None, 1, d), lambda i: (base + row_fn(i) * 6 + 4, 0, 0)),
                pl.BlockSpec((ne, d), lambda i: (0, 0)),
            ] + [pl.BlockSpec((tm, dn), lambda i: (i, 0))] * part
            args += [g.reshape(1, d), mod3, mod3, router_wt] + x1_parts
            out_specs += [
                pl.BlockSpec((tm * rpt, LANES), lambda i: (i, 0)),
                pl.BlockSpec((ne, tm), lambda i: (0, i)),
            ]
            out_shape += [
                jax.ShapeDtypeStruct((t_total * rpt, LANES), F32),
                jax.ShapeDtypeStruct((ne, t_total), F32),
            ]
        in_specs.append(pl.BlockSpec(memory_space=pl.ANY))
        args.append(w)
        outs = pl.pallas_call(
            functools.partial(_out_kernel, seg_tiles=seg_tiles, n_lhs=n_lhs, part=part, n_parts=n_parts),
            grid=(t_total // tm,),
            in_specs=in_specs,
            out_specs=out_specs,
            out_shape=out_shape,
            scratch_shapes=[
                pltpu.VMEM((kdim, dn), BF16),
                pltpu.VMEM((2, stage_rows, dn), F32),
                pltpu.SemaphoreType.DMA((2,)),
            ],
            compiler_params=_params("arbitrary"),
        )(*args)
        x1_parts.append(outs[0])
    return x1_parts, outs[1], outs[2]


def _first_index(vals, target):
    idx = jnp.full(target.shape, len(vals) - 1, jnp.int32)
    for k in range(len(vals) - 2, -1, -1):
        idx = jnp.where(vals[k] == target, k, idx)
    return idx


def _pick(vals, idx):
    out = vals[-1]
    for k in range(len(vals) - 2, -1, -1):
        out = jnp.where(idx == k, vals[k], out)
    return out


def _route_kernel(lg_ref, rb_ref, pos_ref, wt_ref, meta_ref, rank_ref, *, tile_rows, scan):
    logits = lg_ref[...]
    t = logits.shape[1]
    scores = jax.nn.sigmoid(logits)
    biased = scores + rb_ref[...]
    s_rows = [scores[e:e + 1, :] for e in range(N_EXPERTS)]
    b_rows = [biased[e:e + 1, :] for e in range(N_EXPERTS)]
    grp = []
    for g in range(N_GROUPS):
        r = b_rows[g * GROUP_SIZE:(g + 1) * GROUP_SIZE]
        best = None
        for a in range(GROUP_SIZE):
            for b in range(a + 1, GROUP_SIZE):
                pair = r[a] + r[b]
                best = pair if best is None else jnp.maximum(best, pair)
        grp.append(best)
    gmax = functools.reduce(jnp.maximum, grp)
    gsel = _first_index(grp, gmax)
    cand_b = [_pick([b_rows[g * GROUP_SIZE + k] for g in range(N_GROUPS)], gsel) for k in range(GROUP_SIZE)]
    cand_s = [_pick([s_rows[g * GROUP_SIZE + k] for g in range(N_GROUPS)], gsel) for k in range(GROUP_SIZE)]
    m1 = functools.reduce(jnp.maximum, cand_b)
    k1 = _first_index(cand_b, m1)
    rest = [jnp.where(k1 == k, -jnp.inf, cand_b[k]) for k in range(GROUP_SIZE)]
    m2 = functools.reduce(jnp.maximum, rest)
    k2 = _first_index(rest, m2)
    w1 = _pick(cand_s, k1)
    w2 = _pick(cand_s, k2)
    wsum = w1 + w2
    w1 = w1 / wsum
    w2 = w2 / wsum
    e1 = gsel * GROUP_SIZE + k1
    e2 = gsel * GROUP_SIZE + k2
    wt_ref[...] = jnp.zeros_like(wt_ref)
    wt_ref[0:1, :] = w1
    wt_ref[1:2, :] = w2

    eid = lax.broadcasted_iota(jnp.int32, logits.shape, 0)
    sel1 = eid == e1
    sel2 = eid == e2
    ind = jnp.where(sel1 | sel2, 1.0, 0.0)
    ua = lax.broadcasted_iota(jnp.int32, (scan, scan), 0)
    ub = lax.broadcasted_iota(jnp.int32, (scan, scan), 1)
    upper = jnp.where(ua <= ub, 1.0, 0.0).astype(BF16)
    carry = jnp.zeros((N_EXPERTS, 1), F32)
    for blk in range(t // scan):
        seg = ind[:, blk * scan:(blk + 1) * scan]
        incl = _dot(seg.astype(BF16), upper)
        rank_ref[:, blk * scan:(blk + 1) * scan] = incl - seg + carry
        carry = carry + incl[:, scan - 1:scan]
    n_tiles = jnp.floor((carry + (tile_rows - 1.0)) * (1.0 / tile_rows))
    la = lax.broadcasted_iota(jnp.int32, (N_EXPERTS, N_EXPERTS), 0)
    lb = lax.broadcasted_iota(jnp.int32, (N_EXPERTS, N_EXPERTS), 1)
    lower = jnp.where(lb < la, 1.0, 0.0).astype(BF16)
    start = _dot(lower, jnp.broadcast_to(n_tiles, (N_EXPERTS, LANES)).astype(BF16))
    end = start + n_tiles
    slot = start[:, 0:1] * tile_rows + rank_ref[...]
    pos_ref[...] = jnp.zeros_like(pos_ref)
    pos_ref[0:1, :] = jnp.sum(jnp.where(sel1, slot, 0.0), axis=0, keepdims=True).astype(jnp.int32)
    pos_ref[1:2, :] = jnp.sum(jnp.where(sel2, slot, 0.0), axis=0, keepdims=True).astype(jnp.int32)

    n_used = end[N_EXPERTS - 1:N_EXPERTS, :]
    tile = lax.broadcasted_iota(jnp.int32, (N_EXPERTS, LANES), 1).astype(F32)
    tile = jnp.minimum(tile, n_used - 1.0)
    tile_expert = jnp.sum(jnp.where(tile >= end, 1.0, 0.0), axis=0, keepdims=True)
    lane = lax.broadcasted_iota(jnp.int32, (1, LANES), 1)
    meta_ref[...] = jnp.zeros_like(meta_ref)
    meta_ref[0:1, :] = jnp.where(lane == LANES - 1, n_used, tile_expert).astype(jnp.int32)
    diag = lax.broadcasted_iota(jnp.int32, (N_EXPERTS, LANES), 0) == lax.broadcasted_iota(
        jnp.int32, (N_EXPERTS, LANES), 1)
    meta_ref[1:2, :] = jnp.sum(jnp.where(diag, carry, 0.0), axis=0, keepdims=True).astype(jnp.int32)
    meta_ref[2:3, :] = jnp.sum(jnp.where(diag, start * tile_rows, 0.0), axis=0, keepdims=True).astype(jnp.int32)


def _route(logits_t, router_b, tile_rows, scan=512):
    ne, t = logits_t.shape
    return pl.pallas_call(
        functools.partial(_route_kernel, tile_rows=tile_rows, scan=scan),
        out_shape=[
            jax.ShapeDtypeStruct((8, t), jnp.int32),
            jax.ShapeDtypeStruct((8, t), F32),
            jax.ShapeDtypeStruct((8, LANES), jnp.int32),
        ],
        scratch_shapes=[pltpu.VMEM((ne, t), F32)],
        compiler_params=pltpu.CompilerParams(vmem_limit_bytes=VMEM_LIMIT_BYTES),
    )(logits_t, router_b.reshape(ne, 1))


def _dispatch_kernel(pos_ref, meta_ref, h_ref, xs_ref, zero_ref, sem, *, t, rpt, chunk, tile_rows, n_tiles):
    c = pl.program_id(0)

    def slot_rows(slot):
        return xs_ref.at[pl.ds(pl.multiple_of(slot * rpt, rpt), rpt)]

    def body(r, carry):
        tok = c * chunk + r
        src = h_ref.at[pl.ds(pl.multiple_of(r * rpt, rpt), rpt)]
        for k in range(2):
            pltpu.make_async_copy(src, slot_rows(pos_ref[k * t + tok]), sem.at[0]).start(priority=k)
        return carry

    lax.fori_loop(0, chunk, body, 0, unroll=8)

    @pl.when(c == 0)
    def _():
        zero_ref[...] = jnp.zeros_like(zero_ref)
        zero_row = zero_ref.at[pl.ds(0, rpt)]
        for e in range(N_EXPERTS):
            count = meta_ref[LANES + e]
            first = meta_ref[2 * LANES + e] + count
            n_pad = (tile_rows - count % tile_rows) % tile_rows

            def pad_start(r, carry, first=first):
                pltpu.make_async_copy(zero_row, slot_rows(first + r), sem.at[1]).start()
                return carry

            def pad_wait(r, carry, first=first):
                pltpu.make_async_copy(zero_row, slot_rows(first + r), sem.at[1]).wait()
                return carry

            lax.fori_loop(0, n_pad, pad_start, 0)
            lax.fori_loop(0, n_pad, pad_wait, 0)

        def tile_copy(n):
            rows = tile_rows * rpt
            return pltpu.make_async_copy(zero_ref, xs_ref.at[pl.ds(pl.multiple_of(n * rows, rows), rows)],
                                         sem.at[1])

        def unused_start(n, carry):
            tile_copy(n).start()
            return carry

        def unused_wait(n, carry):
            tile_copy(n).wait()
            return carry

        lax.fori_loop(meta_ref[LANES - 1], n_tiles, unused_start, 0)
        lax.fori_loop(meta_ref[LANES - 1], n_tiles, unused_wait, 0)

    for _ in range(2):
        pltpu.make_async_copy(h_ref, xs_ref.at[pl.ds(0, chunk * rpt)], sem.at[0]).wait()


def _dispatch(pos_flat, meta_flat, h_lin, n_slots, rpt, tile_rows, chunk=512):
    t = pos_flat.shape[0] // 2
    return pl.pallas_call(
        functools.partial(_dispatch_kernel, t=t, rpt=rpt, chunk=chunk, tile_rows=tile_rows,
                          n_tiles=n_slots // tile_rows),
        grid_spec=pltpu.PrefetchScalarGridSpec(
            num_scalar_prefetch=2,
            grid=(t // chunk,),
            in_specs=[pl.BlockSpec((chunk * rpt, LANES), lambda c, p, m: (c, 0))],
            out_specs=pl.BlockSpec(memory_space=pl.ANY),
            scratch_shapes=[pltpu.VMEM((tile_rows * rpt, LANES), h_lin.dtype), pltpu.SemaphoreType.DMA((2,))],
        ),
        out_shape=jax.ShapeDtypeStruct((n_slots * rpt, LANES), h_lin.dtype),
        compiler_params=_params("arbitrary"),
    )(pos_flat, meta_flat, h_lin)


def _ffn_kernel(meta_ref, xs_ref, wg_ref, wu_ref, wd_ref, ys_ref, wgb_ref, wub_ref, wdb_ref, *, tm):
    n = pl.program_id(0)
    n_used = meta_ref[LANES - 1]

    @pl.when(n < n_used)
    def _():
        @pl.when((n == 0) | (meta_ref[n] != meta_ref[jnp.maximum(n - 1, 0)]))
        def _():
            wgb_ref[...] = wg_ref[...].astype(BF16)
            wub_ref[...] = wu_ref[...].astype(BF16)
            wdb_ref[...] = wd_ref[...].astype(BF16)

        d = wgb_ref.shape[0]
        a = b = None
        for k0 in range(0, d, MXU_DIM):
            x = jnp.concatenate(
                [_load_token_major(xs_ref, tm, k0 // LANES + j).astype(BF16) for j in range(MXU_DIM // LANES)],
                axis=1)
            pa = _dot(x, wgb_ref[k0:k0 + MXU_DIM, :])
            pb = _dot(x, wub_ref[k0:k0 + MXU_DIM, :])
            a = pa if a is None else a + pa
            b = pb if b is None else b + pb
        act = (_silu(a) * b).astype(BF16)
        for n0 in range(0, d, MXU_DIM):
            _store_token_major(ys_ref, _dot(act, wdb_ref[:, n0:n0 + MXU_DIM]), n0)

    @pl.when(n >= n_used)
    def _():
        ys_ref[...] = jnp.zeros_like(ys_ref)


def _ffn(meta, xs, w_gate, w_up, w_down, layer, tm, n_tiles):
    _, ne, d, f = w_gate.shape
    rpt = d // LANES

    def row_tile(n, m):
        return (jnp.minimum(n, m[LANES - 1] - 1), 0)

    return pl.pallas_call(
        functools.partial(_ffn_kernel, tm=tm),
        grid_spec=pltpu.PrefetchScalarGridSpec(
            num_scalar_prefetch=1,
            grid=(n_tiles,),
            in_specs=[
                pl.BlockSpec((tm * rpt, LANES), row_tile),
                pl.BlockSpec((None, None, d, f), lambda n, m: (layer, m[n], 0, 0)),
                pl.BlockSpec((None, None, d, f), lambda n, m: (layer, m[n], 0, 0)),
                pl.BlockSpec((None, None, f, d), lambda n, m: (layer, m[n], 0, 0)),
            ],
            out_specs=pl.BlockSpec((tm * rpt, LANES), lambda n, m: (n, 0)),
            scratch_shapes=[
                pltpu.VMEM((d, f), BF16),
                pltpu.VMEM((d, f), BF16),
                pltpu.VMEM((f, d), BF16),
            ],
        ),
        out_shape=jax.ShapeDtypeStruct((n_tiles * tm * rpt, LANES), F32),
        compiler_params=_params("arbitrary"),
    )(meta, xs, w_gate, w_up, w_down)


def _combine_kernel(pos_ref, ys_ref, *refs, t, rpt, pitch, tm, n_parts, final_tiles):
    x_refs = refs[:n_parts]
    if final_tiles is None:
        wt_ref, gate_ref, o_ref, a0, b0, a1, b1, sem = refs[n_parts:]
    else:
        wt_ref, gate_ref, fg_ref, op_ref, os_ref, a0, b0, a1, b1, sem = refs[n_parts:]
    i = pl.program_id(0)
    n = pl.num_programs(0)
    bufs = ((a0, b0), (a1, b1))

    def issue(tile, s):
        def body(r, carry):
            tok = tile * tm + r
            for k in range(2):
                slot_row = pos_ref[k * t + tok]
                pltpu.make_async_copy(
                    ys_ref.at[pl.ds(pl.multiple_of(slot_row * rpt, rpt), rpt)],
                    bufs[s][k].at[pl.ds(pl.multiple_of(r * pitch, 8), rpt)],
                    sem.at[s]).start(priority=k)
            return carry

        lax.fori_loop(0, tm, body, 0, unroll=8)

    def wait(s):
        for k in range(2):
            pltpu.make_async_copy(ys_ref.at[pl.ds(0, tm * rpt)], bufs[s][k].at[pl.ds(0, tm * rpt)],
                                  sem.at[s]).wait()

    def compute(s, out_ref):
        w = wt_ref[...]
        w0, w1 = w[:, 0:1], w[:, 1:2]
        dn = x_refs[0].shape[1]
        for c in range(rpt):
            cols = slice(c * LANES, (c + 1) * LANES)
            part, pc = divmod(c * LANES, dn)
            y = (w0 * bufs[s][0][pl.ds(c, tm, stride=pitch), :]
                 + w1 * bufs[s][1][pl.ds(c, tm, stride=pitch), :])
            out_ref[:, cols] = x_refs[part][:, pc:pc + LANES] + gate_ref[:, cols] * y
        if final_tiles is not None:
            out_ref[...] = _rms_f32(out_ref[...]) * fg_ref[...]

    @pl.when(i == 0)
    def _():
        issue(0, 0)

    for s in range(2):
        @pl.when(i % 2 == s)
        def _(s=s):
            @pl.when(i + 1 < n)
            def _():
                issue(i + 1, 1 - s)

            wait(s)
            if final_tiles is None:
                compute(s, o_ref)
            else:
                @pl.when(i < final_tiles)
                def _():
                    compute(s, op_ref)

                @pl.when(i >= final_tiles)
                def _():
                    compute(s, os_ref)


def _combine(pos_flat, ys, x1_parts, wt, mod3, layer, row_fn, final=None, tm=256):
    t, dn = x1_parts[0].shape
    n_parts = len(x1_parts)
    d = dn * n_parts
    rpt = d // LANES
    pitch = rpt + 8
    base = layer * 48
    in_specs = ([pl.BlockSpec(memory_space=pl.ANY)]
                + [pl.BlockSpec((tm, dn), lambda i, p: (i, 0))] * n_parts
                + [pl.BlockSpec((tm, 2), lambda i, p: (i, 0)),
                   pl.BlockSpec((None, 1, d), lambda i, p: (base + row_fn(i) * 6 + 5, 0, 0))])
    args = [pos_flat, ys, *x1_parts, wt, mod3]
    if final is None:
        final_tiles = None
        out_specs = pl.BlockSpec((tm, d), lambda i, p: (i, 0))
        out_shape = jax.ShapeDtypeStruct((t, d), F32)
    else:
        fg, t_ctx = final
        final_tiles = t_ctx // tm
        in_specs.append(pl.BlockSpec((1, d), lambda i, p: (0, 0)))
        args.append(fg.reshape(1, d))
        out_specs = [
            pl.BlockSpec((tm, d), lambda i, p: (jnp.minimum(i, final_tiles - 1), 0)),
            pl.BlockSpec((tm, d), lambda i, p: (jnp.maximum(i - final_tiles, 0), 0)),
        ]
        out_shape = [jax.ShapeDtypeStruct((t_ctx, d), F32), jax.ShapeDtypeStruct((t - t_ctx, d), F32)]
    return pl.pallas_call(
        functools.partial(_combine_kernel, t=t, rpt=rpt, pitch=pitch, tm=tm, n_parts=n_parts,
                          final_tiles=final_tiles),
        grid_spec=pltpu.PrefetchScalarGridSpec(
            num_scalar_prefetch=1,
            grid=(t // tm,),
            in_specs=in_specs,
            out_specs=out_specs,
            scratch_shapes=[pltpu.VMEM((tm * pitch, LANES), F32)] * 4 + [pltpu.SemaphoreType.DMA((2,))],
        ),
        out_shape=out_shape,
        compiler_params=_params("arbitrary"),
    )(*args)


def _moe(lg, router_b, h_lin, x1, w_gate, w_up, w_down, mod3, layer, row_fn, final=None, tile_rows=256):
    t = x1[0].shape[0]
    n_tiles = (2 * t) // tile_rows + N_EXPERTS
    pos, wts, meta = _route(lg, router_b, tile_rows)
    pos_flat = pos[:2].reshape(-1)
    xs = _dispatch(pos_flat, meta[:3].reshape(-1), h_lin, n_tiles * tile_rows, h_lin.shape[0] // t, tile_rows)
    ys = _ffn(meta[0], xs, w_gate, w_up, w_down, layer, tile_rows, n_tiles)
    return _combine(pos_flat, ys, x1, wts[:2].T, mod3, layer, row_fn, final)


def _sgu_kernel(u0_ref, u1_ref, v0_ref, v1_ref, ng_ref, ws_ref, bs_ref, o0_ref, o1_ref, *, n_chunks):
    half = v0_ref.shape[1]
    gw = 2 * half // H_C
    v0 = v0_ref[...].astype(F32)
    v1 = v1_ref[...].astype(F32)
    ms = (jnp.sum(v0 * v0, axis=-1, keepdims=True) + jnp.sum(v1 * v1, axis=-1, keepdims=True)) / (2 * half)
    inv = lax.rsqrt(ms + EPS)
    halves = ((u0_ref, (v0 * inv * ng_ref[:, :half]).astype(BF16), o0_ref),
              (u1_ref, (v1 * inv * ng_ref[:, half:]).astype(BF16), o1_ref))
    for n in range(n_chunks):
        r0 = n * CHUNK_C
        for h in range(H_C):
            u_ref, vn, o_ref = halves[h // (H_C // 2)]
            c0 = (h % (H_C // 2)) * gw
            mixed = _dot(ws_ref[h].astype(BF16), vn[r0:r0 + CHUNK_C, c0:c0 + gw]) + bs_ref[h]
            u = u_ref[r0:r0 + CHUNK_C, c0:c0 + gw].astype(F32)
            o_ref[r0:r0 + CHUNK_C, c0:c0 + gw] = (u * mixed).astype(BF16)


def _sgu(u0, u1, v0, v1, norm_g, w_s, b_s, tm=256):
    t, half = u0.shape
    blk = pl.BlockSpec((tm, half), lambda i: (i, 0))
    return pl.pallas_call(
        functools.partial(_sgu_kernel, n_chunks=tm // CHUNK_C),
        grid=(t // tm,),
        in_specs=[
            blk, blk, blk, blk,
            pl.BlockSpec((1, 2 * half), lambda i: (0, 0)),
            pl.BlockSpec((H_C, CHUNK_C, CHUNK_C), lambda i: (0, 0, 0)),
            pl.BlockSpec((H_C, CHUNK_C, 1), lambda i: (0, 0, 0)),
        ],
        out_specs=[blk, blk],
        out_shape=[jax.ShapeDtypeStruct((t, half), BF16)] * 2,
        compiler_params=_params("arbitrary"),
    )(u0, u1, v0, v1, norm_g.reshape(1, 2 * half), w_s, b_s.reshape(H_C, CHUNK_C, 1))


def _rope_tables(n_tok):
    rows = n_tok // GRID_W
    pos_r = jnp.repeat(jnp.arange(rows), GRID_W).astype(F32)
    pos_c = jnp.tile(jnp.arange(GRID_W), rows).astype(F32)
    n_freq = DK_A // 4
    inv = jnp.power(ROPE_BASE, -jnp.arange(n_freq, dtype=F32) / n_freq)
    ang_r = pos_r[:, None] * inv
    ang_c = pos_c[:, None] * inv
    cos_t = jnp.concatenate([jnp.cos(ang_r), jnp.cos(ang_r), jnp.cos(ang_c), jnp.cos(ang_c)], axis=1)
    sin_t = jnp.concatenate([-jnp.sin(ang_r), jnp.sin(ang_r), -jnp.sin(ang_c), jnp.sin(ang_c)], axis=1)
    return cos_t, sin_t


def kernel(x_prompt, x_sample, cache_k, cache_v, state_ret, c, c_ctx, norm_g, w_mod, b_mod, w_in_even, w_out_even, lam_q, lam_k, ret_decay_logit, w_in_odd, gmlp_norm_g, w_spatial, b_spatial, w_out_odd, router_w, router_b, w_gate, w_up, w_down, final_norm_g):
    batch, seq, d = x_prompt.shape
    dbatch, dseq, _ = x_sample.shape
    past = cache_k.shape[2]
    depth = norm_g.shape[0]
    t_ctx, t_lat = batch * seq, dbatch * dseq
    t_all = t_ctx + t_lat
    xc = x_prompt.reshape(t_ctx, d)
    xl = x_sample.reshape(t_lat, d)

    cond8 = jnp.concatenate([c_ctx[None, :], c, jnp.zeros((8 - 1 - dbatch, d), F32)], axis=0)
    mod3 = _modulation(cond8, w_mod, b_mod).reshape(depth * 8 * 6, 1, d)
    router_wt = router_w.T
    cos_t, sin_t = _rope_tables(dseq)

    tm_in = 512
    tm_out = 512
    tm_cmb = 256
    w_in0 = w_in_even.reshape(d, -1)
    w_out0 = w_out_even.reshape(-1, d)
    w_in1 = w_in_odd.reshape(d, -1)
    w_out1 = w_out_odd.reshape(-1, d)
    lq, lk, dlog = lam_q.reshape(2, DK_A), lam_k.reshape(2, DK_A), ret_decay_logit.reshape(2, H_B)

    def rows_ctx(i):
        return 0

    def rows_lat(tiles_per_batch):
        return lambda i: 1 + i // tiles_per_batch

    def rows_all(tm):
        n_ctx, per = t_ctx // tm, dseq // tm
        return lambda i: jnp.maximum(i - n_ctx, 0) // per + jnp.where(i >= n_ctx, 1, 0)

    lam_init = 0.8 - 0.6 * math.exp(-0.3 * 0)
    half0 = w_in0.shape[1] // 2
    g00 = norm_g[0, 0]
    rl = rows_lat(dseq // tm_in)
    pa_c, k_new, v_new = _in_proj(xc, g00, mod3, 0, rows_ctx, w_in0, 0, half0, "even_kv", tm_in)
    (pb_c,) = _in_proj(xc, g00, mod3, 0, rows_ctx, w_in0, half0, half0, "even", tm_in)
    (pa_l,) = _in_proj(xl, g00, mod3, 0, rl, w_in0, 0, half0, "even", tm_in)
    (pb_l,) = _in_proj(xl, g00, mod3, 0, rl, w_in0, half0, half0, "even", tm_in)
    oa_c, or_c, s_new = _mixer(pa_c, pb_c, lq, lk, dlog, lam_init, batch, seq, H_A)
    ctx = (cache_k.reshape(dbatch, past, H_A * 2 * DK_A), cache_v.reshape(dbatch, past, H_A * DV_A),
           state_ret.reshape(dbatch, 2, H_B, DK_B, DV_B), cos_t, sin_t)
    oa_l, or_l = _mixer(pa_l, pb_l, lq, lk, dlog, lam_init, dbatch, dseq, 1, ctx)
    x1, h2, lg = _out_proj([([oa_c, or_c], xc), ([oa_l, or_l], xl)], w_out0, norm_g[0, 1], mod3, 0,
                           rows_all(tm_out), router_wt, tm=tm_out)
    x2 = _moe(lg, router_b, h2, x1, w_gate, w_up, w_down, mod3, 0, rows_all(tm_cmb))

    q1 = w_in1.shape[1] // 4
    zs = [_in_proj(x2, norm_g[1, 0], mod3, 1, rows_all(tm_in), w_in1, j * q1, q1, "odd", tm_in)[0]
          for j in range(4)]
    g0, g1 = _sgu(zs[0], zs[1], zs[2], zs[3], gmlp_norm_g.reshape(-1),
                  w_spatial.reshape(H_C, CHUNK_C, CHUNK_C), b_spatial.reshape(H_C, CHUNK_C))
    x3, h4, lg2 = _out_proj([([g0, g1], x2)], w_out1, norm_g[1, 1], mod3, 1, rows_all(tm_out), router_wt,
                            tm=tm_out)
    y_p, y_s = _moe(lg2, router_b, h4, x3, w_gate, w_up, w_down, mod3, 1, rows_all(tm_cmb),
                    final=(final_norm_g, t_ctx))

    y_prompt = y_p.reshape(batch, seq, d)
    y_sample = y_s.reshape(dbatch, dseq, d)
    new_cache_k = k_new.reshape(batch, 1, seq, H_A, 2, DK_A)
    new_cache_v = v_new.reshape(batch, 1, seq, DV_A // LANES, H_A, LANES).swapaxes(3, 4).reshape(
        batch, 1, seq, H_A, DV_A)
    new_state_ret = s_new.reshape(batch, 1, 2, H_B, DK_B, DV_B)
    return (y_prompt, y_sample, new_cache_k, new_cache_v, new_state_ret)
```

```python
import functools
import math

import jax
import jax.numpy as jnp
from jax import lax
from jax.experimental import pallas as pl
from jax.experimental.pallas import tpu as pltpu

F32 = jnp.float32
BF16 = jnp.bfloat16

EPS = 1e-6
ROPE_BASE = 10000.0
GRID_W = 64
H_A, DK_A, DV_A = 4, 128, 256
H_B, DK_B, DV_B = 4, 128, 256
RET_CHUNK = 128
H_C, CHUNK_C = 8, 128
N_EXPERTS, N_GROUPS, GROUP_SIZE = 16, 4, 4

VMEM_LIMIT_BYTES = 56 * 1024 * 1024


def _params(*sem):
    return pltpu.CompilerParams(dimension_semantics=sem, vmem_limit_bytes=VMEM_LIMIT_BYTES)


def _rms_f32(x):
    return x * lax.rsqrt(jnp.mean(x * x, axis=-1, keepdims=True) + EPS)


def _silu(x):
    return x * jax.nn.sigmoid(x)


def _dot(a, b):
    return jnp.dot(a, b, preferred_element_type=F32)


def _dot_nt(a, b):
    return lax.dot_general(a, b, (((1,), (1,)), ((), ())), preferred_element_type=F32)


LANES = 128
MXU_DIM = 256


def _store_token_major(ref, val, col0=0):
    tm, width = val.shape
    rpt = ref.shape[0] // tm
    for c in range(width // LANES):
        ref[pl.ds(col0 // LANES + c, tm, stride=rpt), :] = val[:, c * LANES:(c + 1) * LANES]


def _load_token_major(ref, tm, c):
    rpt = ref.shape[0] // tm
    return ref[pl.ds(c, tm, stride=rpt), :]


def _mod_kernel(c_ref, w_ref, b_ref, o_ref):
    s = _silu(c_ref[...]).astype(BF16)
    o_ref[...] = _dot(s, w_ref[...].astype(BF16)) + b_ref[...]


def _modulation(cond8, w_mod, b_mod, tn=1024):
    depth, d, n = w_mod.shape
    return pl.pallas_call(
        _mod_kernel,
        grid=(depth, n // tn),
        in_specs=[
            pl.BlockSpec((8, d), lambda l, j: (0, 0)),
            pl.BlockSpec((None, d, tn), lambda l, j: (l, 0, j)),
            pl.BlockSpec((None, 1, tn), lambda l, j: (l, 0, j)),
        ],
        out_specs=pl.BlockSpec((None, 8, tn), lambda l, j: (l, 0, j)),
        out_shape=jax.ShapeDtypeStruct((depth, 8, n), F32),
        compiler_params=_params("arbitrary", "arbitrary"),
    )(cond8, w_mod, b_mod.reshape(depth, 1, n))


def _load_weight_slab(w_hbm, slab_ref, stage_ref, sem, col0):
    kdim, n = slab_ref.shape
    rows = stage_ref.shape[1]
    nb = kdim // rows

    def copy(j):
        return pltpu.make_async_copy(w_hbm.at[pl.ds(j * rows, rows), pl.ds(col0, n)],
                                     stage_ref.at[j % 2], sem.at[j % 2])

    copy(0).start()
    for j in range(nb):
        if j + 1 < nb:
            copy(j + 1).start()
        copy(j).wait()
        slab_ref[j * rows:(j + 1) * rows, :] = stage_ref[j % 2].astype(BF16)


def _gelu(x):
    return 0.5 * x * (1.0 + lax.erf(x * (2.0 ** -0.5)))


def _in_kernel(x_ref, g_ref, sh_ref, sc_ref, w_hbm, *rest, col0, mode, kv_cols, chunk, e_c):
    if mode == "even_kv":
        p_ref, k_ref, v_ref, slab_ref, stage_ref, h_ref, sem = rest
    elif mode == "gmlp_v":
        p_ref, ss_ref, slab_ref, stage_ref, h_ref, sem = rest
    elif mode == "gmlp_u":
        vb_ref, sa_ref, sb_ref, ng_ref, ws_ref, bs_ref, p_ref, slab_ref, stage_ref, h_ref, vn_ref, sem = rest
    else:
        p_ref, slab_ref, stage_ref, h_ref, sem = rest

    @pl.when(pl.program_id(0) == 0)
    def _():
        _load_weight_slab(w_hbm, slab_ref, stage_ref, sem, col0)

    h = (_rms_f32(x_ref[...]) * g_ref[...]) * (1.0 + sc_ref[...]) + sh_ref[...]
    h_ref[...] = h.astype(BF16)
    n = slab_ref.shape[1]
    tm = h_ref.shape[0]
    if mode == "gmlp_u":
        inv = lax.rsqrt((sa_ref[...] + sb_ref[...]) / e_c + EPS)
        vn_ref[...] = ((vb_ref[...].astype(F32) * inv) * ng_ref[...]).astype(BF16)
    ss = None
    for ci, c0 in enumerate(range(0, n, chunk)):
        acc = _dot(h_ref[...], slab_ref[:, c0:c0 + chunk])
        if mode == "gmlp_v":
            v = _gelu(acc)
            p_ref[:, c0:c0 + chunk] = v.astype(BF16)
            part = jnp.sum(v * v, axis=-1, keepdims=True)
            ss = part if ss is None else ss + part
        elif mode == "gmlp_u":
            u = _gelu(acc)
            wsb = ws_ref[ci].astype(BF16)
            for r0 in range(0, tm, CHUNK_C):
                mixed = _dot(wsb, vn_ref[r0:r0 + CHUNK_C, c0:c0 + chunk]) + bs_ref[ci]
                p_ref[r0:r0 + CHUNK_C, c0:c0 + chunk] = (u[r0:r0 + CHUNK_C] * mixed).astype(BF16)
        else:
            p_ref[:, c0:c0 + chunk] = acc.astype(BF16)
        if mode == "even_kv":
            (k0, k1), (v0, v1) = kv_cols
            a0 = col0 + c0
            if k0 <= a0 < k1:
                _store_token_major(k_ref, acc, a0 - k0)
            if v0 <= a0 < v1:
                tm = acc.shape[0]
                rpt = v_ref.shape[0] // tm
                for cc in range(chunk // LANES):
                    head, half = divmod((a0 - v0) // LANES + cc, DV_A // LANES)
                    v_ref[pl.ds(half * H_A + head, tm, stride=rpt), :] = acc[:, cc * LANES:(cc + 1) * LANES]
    if mode == "gmlp_v":
        ss_ref[...] = ss


def _in_proj(x, g, mod3, layer, row_fn, w, col0, n_cols, mode, gating=None, tm=512, chunk=512, stage_rows=256):
    t, d = x.shape
    base = layer * 48

    def mod_spec(kind):
        return pl.BlockSpec((None, 1, d), lambda i: (base + row_fn(i) * 6 + kind, 0, 0))

    in_specs = [
        pl.BlockSpec((tm, d), lambda i: (i, 0)),
        pl.BlockSpec((1, d), lambda i: (0, 0)),
        mod_spec(0),
        mod_spec(1),
        pl.BlockSpec(memory_space=pl.ANY),
    ]
    args = [x, g.reshape(1, d), mod3, mod3, w]
    out_specs = [pl.BlockSpec((tm, n_cols), lambda i: (i, 0))]
    out_shape = [jax.ShapeDtypeStruct((t, n_cols), BF16)]
    scratch = [
        pltpu.VMEM((d, n_cols), BF16),
        pltpu.VMEM((2, stage_rows, n_cols), F32),
        pltpu.VMEM((tm, d), BF16),
    ]
    kv_cols = None
    e_c = None
    if mode == "even_kv":
        a_qk = H_A * 2 * DK_A
        a_v = H_A * DV_A
        kv_cols = ((a_qk, 2 * a_qk), (2 * a_qk, 2 * a_qk + a_v))
        for width in (a_qk, a_v):
            out_specs.append(pl.BlockSpec((tm * width // LANES, LANES), lambda i: (i, 0)))
            out_shape.append(jax.ShapeDtypeStruct((t * width // LANES, LANES), F32))
    elif mode == "gmlp_v":
        out_specs.append(pl.BlockSpec((tm, 1), lambda i: (i, 0)))
        out_shape.append(jax.ShapeDtypeStruct((t, 1), F32))
    elif mode == "gmlp_u":
        v_blk, ss_a, ss_b, ng, w_s, b_s, blk = gating
        e_c = ng.shape[1]
        groups = w_s.shape[0] * n_cols // e_c
        chunk = n_cols // groups
        in_specs += [
            pl.BlockSpec((tm, n_cols), lambda i: (i, 0)),
            pl.BlockSpec((tm, 1), lambda i: (i, 0)),
            pl.BlockSpec((tm, 1), lambda i: (i, 0)),
            pl.BlockSpec((1, n_cols), lambda i: (0, blk)),
            pl.BlockSpec((groups, CHUNK_C, CHUNK_C), lambda i: (blk, 0, 0)),
            pl.BlockSpec((groups, CHUNK_C, 1), lambda i: (blk, 0, 0)),
        ]
        args += [v_blk, ss_a, ss_b, ng, w_s, b_s]
        scratch.append(pltpu.VMEM((tm, n_cols), BF16))
    return pl.pallas_call(
        functools.partial(_in_kernel, col0=col0, mode=mode, kv_cols=kv_cols, chunk=chunk, e_c=e_c),
        grid=(t // tm,),
        in_specs=in_specs,
        out_specs=out_specs,
        out_shape=out_shape,
        scratch_shapes=scratch + [pltpu.SemaphoreType.DMA((2,))],
        compiler_params=_params("arbitrary"),
    )(*args)


def _rope(x, c, s):
    lane = lax.broadcasted_iota(jnp.int32, x.shape, 1)
    partner = jnp.where((lane & 32) == 0, pltpu.roll(x, 96, 1), pltpu.roll(x, 32, 1))
    return x * c + partner * s


def _mixer_kernel(*refs, seq, has_ctx, lam_init, past, hps):
    if has_ctx:
        (aq_ref, ak_ref, av_ref, bq_ref, bk_ref, bv_ref, bg_ref, lq_ref, lk_ref, dl_ref,
         ck_ref, cv_ref, s0_ref, cos_ref, sin_ref,
         oa_ref, or_ref, kall_ref, vall_ref, of_ref, ob_ref, stf_ref, stb_ref) = refs
    else:
        (aq_ref, ak_ref, av_ref, bq_ref, bk_ref, bv_ref, bg_ref, lq_ref, lk_ref, dl_ref,
         oa_ref, or_ref, sout_ref, of_ref, ob_ref, stf_ref, stb_ref) = refs
    tq = 256
    rt = 256
    straight = seq <= tq
    cs = RET_CHUNK
    nc = seq // cs

    prod = lq_ref[...] * lk_ref[...]
    ee = jnp.exp(jnp.sum(prod, axis=1, keepdims=True))
    lam = ee[0:1, :] - ee[1:2, :] + lam_init
    scale = DK_A ** -0.5
    ii = lax.broadcasted_iota(jnp.int32, (cs, cs), 0)
    jj = lax.broadcasted_iota(jnp.int32, (cs, cs), 1)
    ic = lax.broadcasted_iota(jnp.int32, (cs, 1), 0).astype(F32)

    def loop(n, body):
        if straight:
            for it in range(n):
                body(it)
        else:
            lax.fori_loop(0, n, lambda it, c: (body(it), c)[1], 0)

    def rows(it, size):
        return pl.ds(it * size, size) if straight else pl.ds(pl.multiple_of(it * size, size), size)

    for hh in range(hps):
        qk_cols = slice(hh * 2 * DK_A, (hh + 1) * 2 * DK_A)
        va_cols = slice(hh * DV_A, (hh + 1) * DV_A)
        kb_cols = slice(hh * DK_B, (hh + 1) * DK_B)
        vb_cols = slice(hh * DV_B, (hh + 1) * DV_B)

        if has_ctx:
            for c in range(2):
                kall_ref[c, 0:past, :] = ck_ref[:, c * DK_A:(c + 1) * DK_A].astype(BF16)
            vall_ref[0:past, :] = cv_ref[...].astype(BF16)

            def copy_body(r):
                rr = rows(r, rt)
                cs_, sn_ = cos_ref[rr, :], sin_ref[rr, :]
                kk = ak_ref[rr, qk_cols].astype(F32)
                for c in range(2):
                    kall_ref[c, pl.ds(past + pl.multiple_of(r * rt, rt), rt), :] = _rope(
                        kk[:, c * DK_A:(c + 1) * DK_A], cs_, sn_).astype(BF16)
                vall_ref[pl.ds(past + pl.multiple_of(r * rt, rt), rt), :] = av_ref[rr, va_cols]

            loop(seq // rt, copy_body)

        def attn_body(qt, qk_cols=qk_cols, va_cols=va_cols):
            rr = rows(qt, tq)
            q = aq_ref[rr, qk_cols]
            if has_ctx:
                qf = q.astype(F32)
                cs_, sn_ = cos_ref[rr, :], sin_ref[rr, :]
            probs = []
            for c in range(2):
                if has_ctx:
                    qc = _rope(qf[:, c * DK_A:(c + 1) * DK_A], cs_, sn_).astype(BF16)
                    keys = kall_ref[c]
                else:
                    qc = q[:, c * DK_A:(c + 1) * DK_A]
                    keys = ak_ref[:, hh * 2 * DK_A + c * DK_A:hh * 2 * DK_A + (c + 1) * DK_A]
                s = _dot_nt(qc, keys) * scale
                e = jnp.exp(s - jnp.max(s, axis=-1, keepdims=True))
                probs.append(e * (1.0 / jnp.sum(e, axis=-1, keepdims=True)))
            a = (probs[0] - lam * probs[1]).astype(BF16)
            o = _dot(a, vall_ref[...] if has_ctx else av_ref[:, va_cols])
            oa_ref[rr, va_cols] = (_rms_f32(o) * (1.0 - lam_init)).astype(BF16)

        loop(seq // tq, attn_body)

        tables = []
        for d in range(2):
            lg = jax.nn.log_sigmoid(dl_ref[d, hh])
            diff = (ii - jj if d == 0 else jj - ii).astype(F32)
            mask = jnp.where(diff >= 0, jnp.exp(jnp.maximum(diff, 0.0) * lg), 0.0)
            if d == 0:
                qdec = jnp.exp((ic + 1.0) * lg)
                kdec = jnp.exp((cs - 1.0 - ic) * lg)
            else:
                qdec = jnp.exp((cs - ic) * lg)
                kdec = jnp.exp(ic * lg)
            tables.append((mask, qdec, kdec, jnp.exp(cs * lg)))
        st_refs = (stf_ref, stb_ref)
        o_refs = (of_ref, ob_ref)
        for d in range(2):
            if has_ctx:
                st_refs[d][hh] = s0_ref[d]
            else:
                st_refs[d][hh] = jnp.zeros((DK_B, DV_B), F32)

        def chunk_body(n, hh=hh, kb_cols=kb_cols, vb_cols=vb_cols, tables=tables):
            for d in range(2):
                mask, qdec, kdec, cdec = tables[d]
                rr = rows(n if d == 0 else nc - 1 - n, cs)
                qc = bq_ref[rr, kb_cols].astype(F32)
                kc = bk_ref[rr, kb_cols].astype(F32) * (DK_B ** -0.5)
                vc = bv_ref[rr, vb_cols]
                state = st_refs[d][hh]
                qk = _dot_nt(qc.astype(BF16), kc.astype(BF16)) * mask
                o_refs[d][rr, vb_cols] = (_dot(qk.astype(BF16), vc)
                                          + _dot((qc * qdec).astype(BF16), state.astype(BF16)))
                kd = (kc * kdec).T.astype(BF16)
                st_refs[d][hh] = cdec * state + _dot(kd, vc)

        loop(nc, chunk_body)

        def final_body(r, vb_cols=vb_cols):
            rr = rows(r, rt)
            tot = of_ref[rr, vb_cols] + ob_ref[rr, vb_cols]
            gate = _silu(bg_ref[rr, vb_cols].astype(F32))
            or_ref[rr, vb_cols] = (_rms_f32(tot) * gate).astype(BF16)

        loop(seq // rt, final_body)
        if not has_ctx:
            for d in range(2):
                sout_ref[d, hh] = st_refs[d][hh]


def _mixer(pa, pb, lam_q, lam_k, decay_logit, lam_init, batch, seq, hps, ctx=None):
    has_ctx = ctx is not None
    past = ctx[0].shape[1] if has_ctx else 0

    def col(head_width, off):
        width = hps * head_width
        return pl.BlockSpec((seq, width), lambda b, hg: (b, off // width + hg))

    a_qk = H_A * 2 * DK_A
    b_qk = H_B * DK_B
    b_v = H_B * DV_B
    in_specs = [
        col(2 * DK_A, 0), col(2 * DK_A, a_qk), col(DV_A, 2 * a_qk),
        col(DK_B, 0), col(DK_B, b_qk), col(DV_B, 2 * b_qk), col(DV_B, 2 * b_qk + b_v),
        pl.BlockSpec((2, DK_A), lambda b, hg: (0, 0)),
        pl.BlockSpec((2, DK_A), lambda b, hg: (0, 0)),
        pl.BlockSpec((2, hps, 1, 1), lambda b, hg: (0, hg, 0, 0)),
    ]
    args = [pa] * 3 + [pb] * 4 + [lam_q, lam_k, decay_logit.reshape(2, H_B, 1, 1)]
    out_specs = [
        pl.BlockSpec((seq, hps * DV_A), lambda b, hg: (b, hg)),
        pl.BlockSpec((seq, hps * DV_B), lambda b, hg: (b, hg)),
    ]
    out_shape = [
        jax.ShapeDtypeStruct((batch * seq, H_A * DV_A), BF16),
        jax.ShapeDtypeStruct((batch * seq, b_v), BF16),
    ]
    scratch = []
    if has_ctx:
        assert hps == 1
        ck, cv, s0, cos_t, sin_t = ctx
        in_specs += [
            pl.BlockSpec((None, past, 2 * DK_A), lambda b, hg: (b, 0, hg)),
            pl.BlockSpec((None, past, DV_A), lambda b, hg: (b, 0, hg)),
            pl.BlockSpec((None, 2, None, DK_B, DV_B), lambda b, hg: (b, 0, hg, 0, 0)),
            pl.BlockSpec((seq, DK_A), lambda b, hg: (0, 0)),
            pl.BlockSpec((seq, DK_A), lambda b, hg: (0, 0)),
        ]
        args += [ck, cv, s0, cos_t, sin_t]
        scratch += [pltpu.VMEM((2, seq + past, DK_A), BF16), pltpu.VMEM((seq + past, DV_A), BF16)]
    else:
        out_specs.append(pl.BlockSpec((None, 2, hps, DK_B, DV_B), lambda b, hg: (b, 0, hg, 0, 0)))
        out_shape.append(jax.ShapeDtypeStruct((batch, 2, H_B, DK_B, DV_B), F32))
    scratch += [
        pltpu.VMEM((seq, hps * DV_B), F32),
        pltpu.VMEM((seq, hps * DV_B), F32),
        pltpu.VMEM((hps, DK_B, DV_B), F32),
        pltpu.VMEM((hps, DK_B, DV_B), F32),
    ]
    return pl.pallas_call(
        functools.partial(_mixer_kernel, seq=seq, has_ctx=has_ctx, lam_init=lam_init, past=past, hps=hps),
        grid=(batch, H_A // hps),
        in_specs=in_specs,
        out_specs=out_specs,
        out_shape=out_shape,
        scratch_shapes=scratch,
        compiler_params=_params("arbitrary", "arbitrary"),
    )(*args)


def _out_kernel(*refs, seg_tiles, n_lhs, part, n_parts):
    n_seg = len(seg_tiles)
    per_seg = n_lhs + 1
    seg_refs = [refs[s * per_seg:(s + 1) * per_seg] for s in range(n_seg)]
    rest = refs[n_seg * per_seg:]
    last = part == n_parts - 1
    if last:
        gate_ref, g_ref, sh_ref, sc_ref, rw_ref = rest[:5]
        prev_refs = rest[5:5 + part]
        w_hbm, x1_ref, h2_ref, lg_ref, slab_ref, stage_ref, sem = rest[5 + part:]
    else:
        gate_ref, w_hbm, x1_ref, slab_ref, stage_ref, sem = rest
    i = pl.program_id(0)
    dn = slab_ref.shape[1]

    @pl.when(i == 0)
    def _():
        _load_weight_slab(w_hbm, slab_ref, stage_ref, sem, part * dn)

    def run(lhs_refs, x_ref):
        acc = None
        r0 = 0
        for lhs_ref in lhs_refs:
            kk = lhs_ref.shape[1]
            prod = _dot(lhs_ref[...], slab_ref[r0:r0 + kk, :])
            acc = prod if acc is None else acc + prod
            r0 += kk
        x1 = x_ref[...] + gate_ref[...] * acc
        x1_ref[...] = x1
        if last:
            cols = [p_ref[...] for p_ref in prev_refs] + [x1]
            d = dn * n_parts
            ms = sum(jnp.sum(v * v, axis=-1, keepdims=True) for v in cols) / d
            inv = lax.rsqrt(ms + EPS)
            lg = None
            for j, v in enumerate(cols):
                cs_ = slice(j * dn, (j + 1) * dn)
                h2 = ((v * inv) * g_ref[:, cs_]) * (1.0 + sc_ref[:, cs_]) + sh_ref[:, cs_]
                _store_token_major(h2_ref, h2, j * dn)
                pl_ = lax.dot_general(rw_ref[:, cs_], h2, (((1,), (1,)), ((), ())),
                                      precision=lax.Precision.HIGHEST, preferred_element_type=F32)
                lg = pl_ if lg is None else lg + pl_
            lg_ref[...] = lg

    if n_seg == 1:
        run(seg_refs[0][:n_lhs], seg_refs[0][n_lhs])
    else:
        lo = 0
        for s in range(n_seg):
            @pl.when((i >= lo) & (i < lo + seg_tiles[s]))
            def _(s=s):
                run(seg_refs[s][:n_lhs], seg_refs[s][n_lhs])

            lo += seg_tiles[s]


def _out_proj(segments, w, g, mod3, layer, row_fn, router_wt, n_parts=2, tm=512, stage_rows=256):
    n_lhs = len(segments[0][0])
    d = segments[0][1].shape[1]
    dn = d // n_parts
    kdim = w.shape[0]
    base = layer * 48
    ne = router_wt.shape[0]
    seg_tiles = tuple(x.shape[0] // tm for _, x in segments)
    t_total = sum(x.shape[0] for _, x in segments)
    rpt = d // LANES
    x1_parts = []
    for part in range(n_parts):
        last = part == n_parts - 1
        in_specs, args = [], []
        lo = 0
        for (lhs_list, x), nt in zip(segments, seg_tiles):
            def seg_row(i, lo=lo, nt=nt):
                return jnp.clip(i - lo, 0, nt - 1)

            for lhs in lhs_list:
                in_specs.append(pl.BlockSpec((tm, lhs.shape[1]), lambda i, f=seg_row: (f(i), 0)))
                args.append(lhs)
            in_specs.append(pl.BlockSpec((tm, dn), lambda i, f=seg_row: (f(i), part)))
            args.append(x)
            lo += nt
        in_specs.append(pl.BlockSpec((None, 1, dn), lambda i: (base + row_fn(i) * 6 + 2, 0, part)))
        args.append(mod3)
        out_specs = [pl.BlockSpec((tm, dn), lambda i: (i, 0))]
        out_shape = [jax.ShapeDtypeStruct((t_total, dn), F32)]
        if last:
            in_specs += [
                pl.BlockSpec((1, d), lambda i: (0, 0)),
                pl.BlockSpec((None, 1, d), lambda i: (base + row_fn(i) * 6 + 3, 0, 0)),
                pl.BlockSpec((None, 1, d), lambda i: (base + row_fn(i) * 6 + 4, 0, 0)),
                pl.BlockSpec((ne, d), lambda i: (0, 0)),
            ] + [pl.BlockSpec((tm, dn), lambda i: (i, 0))] * part
            args += [g.reshape(1, d), mod3, mod3, router_wt] + x1_parts
            out_specs += [
                pl.BlockSpec((tm * rpt, LANES), lambda i: (i, 0)),
                pl.BlockSpec((ne, tm), lambda i: (0, i)),
            ]
            out_shape += [
                jax.ShapeDtypeStruct((t_total * rpt, LANES), F32),
                jax.ShapeDtypeStruct((ne, t_total), F32),
            ]
        in_specs.append(pl.BlockSpec(memory_space=pl.ANY))
        args.append(w)
        outs = pl.pallas_call(
            functools.partial(_out_kernel, seg_tiles=seg_tiles, n_lhs=n_lhs, part=part, n_parts=n_parts),
            grid=(t_total // tm,),
            in_specs=in_specs,
            out_specs=out_specs,
            out_shape=out_shape,
            scratch_shapes=[
                pltpu.VMEM((kdim, dn), BF16),
                pltpu.VMEM((2, stage_rows, dn), F32),
                pltpu.SemaphoreType.DMA((2,)),
            ],
            compiler_params=_params("arbitrary"),
        )(*args)
        x1_parts.append(outs[0])
    return x1_parts, outs[1], outs[2]


def _first_index(vals, target):
    idx = jnp.full(target.shape, len(vals) - 1, jnp.int32)
    for k in range(len(vals) - 2, -1, -1):
        idx = jnp.where(vals[k] == target, k, idx)
    return idx


def _pick(vals, idx):
    out = vals[-1]
    for k in range(len(vals) - 2, -1, -1):
        out = jnp.where(idx == k, vals[k], out)
    return out


def _route_kernel(lg_ref, rb_ref, pos_ref, wt_ref, meta_ref, rank_ref, *, tile_rows, scan):
    logits = lg_ref[...]
    t = logits.shape[1]
    scores = jax.nn.sigmoid(logits)
    biased = scores + rb_ref[...]
    s_rows = [scores[e:e + 1, :] for e in range(N_EXPERTS)]
    b_rows = [biased[e:e + 1, :] for e in range(N_EXPERTS)]
    grp = []
    for g in range(N_GROUPS):
        r = b_rows[g * GROUP_SIZE:(g + 1) * GROUP_SIZE]
        best = None
        for a in range(GROUP_SIZE):
            for b in range(a + 1, GROUP_SIZE):
                pair = r[a] + r[b]
                best = pair if best is None else jnp.maximum(best, pair)
        grp.append(best)
    gmax = functools.reduce(jnp.maximum, grp)
    gsel = _first_index(grp, gmax)
    cand_b = [_pick([b_rows[g * GROUP_SIZE + k] for g in range(N_GROUPS)], gsel) for k in range(GROUP_SIZE)]
    cand_s = [_pick([s_rows[g * GROUP_SIZE + k] for g in range(N_GROUPS)], gsel) for k in range(GROUP_SIZE)]
    m1 = functools.reduce(jnp.maximum, cand_b)
    k1 = _first_index(cand_b, m1)
    rest = [jnp.where(k1 == k, -jnp.inf, cand_b[k]) for k in range(GROUP_SIZE)]
    m2 = functools.reduce(jnp.maximum, rest)
    k2 = _first_index(rest, m2)
    w1 = _pick(cand_s, k1)
    w2 = _pick(cand_s, k2)
    wsum = w1 + w2
    w1 = w1 / wsum
    w2 = w2 / wsum
    e1 = gsel * GROUP_SIZE + k1
    e2 = gsel * GROUP_SIZE + k2
    wt_ref[...] = jnp.zeros_like(wt_ref)
    wt_ref[0:1, :] = w1
    wt_ref[1:2, :] = w2

    eid = lax.broadcasted_iota(jnp.int32, logits.shape, 0)
    sel1 = eid == e1
    sel2 = eid == e2
    ind = jnp.where(sel1 | sel2, 1.0, 0.0)
    ua = lax.broadcasted_iota(jnp.int32, (scan, scan), 0)
    ub = lax.broadcasted_iota(jnp.int32, (scan, scan), 1)
    upper = jnp.where(ua <= ub, 1.0, 0.0).astype(BF16)
    carry = jnp.zeros((N_EXPERTS, 1), F32)
    for blk in range(t // scan):
        seg = ind[:, blk * scan:(blk + 1) * scan]
        incl = _dot(seg.astype(BF16), upper)
        rank_ref[:, blk * scan:(blk + 1) * scan] = incl - seg + carry
        carry = carry + incl[:, scan - 1:scan]
    n_tiles = jnp.floor((carry + (tile_rows - 1.0)) * (1.0 / tile_rows))
    la = lax.broadcasted_iota(jnp.int32, (N_EXPERTS, N_EXPERTS), 0)
    lb = lax.broadcasted_iota(jnp.int32, (N_EXPERTS, N_EXPERTS), 1)
    lower = jnp.where(lb < la, 1.0, 0.0).astype(BF16)
    start = _dot(lower, jnp.broadcast_to(n_tiles, (N_EXPERTS, LANES)).astype(BF16))
    end = start + n_tiles
    slot = start[:, 0:1] * tile_rows + rank_ref[...]
    pos_ref[...] = jnp.zeros_like(pos_ref)
    pos_ref[0:1, :] = jnp.sum(jnp.where(sel1, slot, 0.0), axis=0, keepdims=True).astype(jnp.int32)
    pos_ref[1:2, :] = jnp.sum(jnp.where(sel2, slot, 0.0), axis=0, keepdims=True).astype(jnp.int32)

    n_used = end[N_EXPERTS - 1:N_EXPERTS, :]
    tile = lax.broadcasted_iota(jnp.int32, (N_EXPERTS, LANES), 1).astype(F32)
    tile = jnp.minimum(tile, n_used - 1.0)
    tile_expert = jnp.sum(jnp.where(tile >= end, 1.0, 0.0), axis=0, keepdims=True)
    lane = lax.broadcasted_iota(jnp.int32, (1, LANES), 1)
    meta_ref[...] = jnp.zeros_like(meta_ref)
    meta_ref[0:1, :] = jnp.where(lane == LANES - 1, n_used, tile_expert).astype(jnp.int32)
    diag = lax.broadcasted_iota(jnp.int32, (N_EXPERTS, LANES), 0) == lax.broadcasted_iota(
        jnp.int32, (N_EXPERTS, LANES), 1)
    meta_ref[1:2, :] = jnp.sum(jnp.where(diag, carry, 0.0), axis=0, keepdims=True).astype(jnp.int32)
    meta_ref[2:3, :] = jnp.sum(jnp.where(diag, start * tile_rows, 0.0), axis=0, keepdims=True).astype(jnp.int32)


def _route(logits_t, router_b, tile_rows, scan=512):
    ne, t = logits_t.shape
    return pl.pallas_call(
        functools.partial(_route_kernel, tile_rows=tile_rows, scan=scan),
        out_shape=[
            jax.ShapeDtypeStruct((8, t), jnp.int32),
            jax.ShapeDtypeStruct((8, t), F32),
            jax.ShapeDtypeStruct((8, LANES), jnp.int32),
        ],
        scratch_shapes=[pltpu.VMEM((ne, t), F32)],
        compiler_params=pltpu.CompilerParams(vmem_limit_bytes=VMEM_LIMIT_BYTES),
    )(logits_t, router_b.reshape(ne, 1))


def _dispatch_kernel(pos_ref, meta_ref, h_ref, xs_ref, zero_ref, sem, *, t, rpt, chunk, tile_rows, n_tiles):
    c = pl.program_id(0)

    def slot_rows(slot):
        return xs_ref.at[pl.ds(pl.multiple_of(slot * rpt, rpt), rpt)]

    def body(r, carry):
        tok = c * chunk + r
        src = h_ref.at[pl.ds(pl.multiple_of(r * rpt, rpt), rpt)]
        for k in range(2):
            pltpu.make_async_copy(src, slot_rows(pos_ref[k * t + tok]), sem.at[0]).start(priority=k)
        return carry

    lax.fori_loop(0, chunk, body, 0, unroll=8)

    @pl.when(c == 0)
    def _():
        zero_ref[...] = jnp.zeros_like(zero_ref)
        zero_row = zero_ref.at[pl.ds(0, rpt)]
        for e in range(N_EXPERTS):
            count = meta_ref[LANES + e]
            first = meta_ref[2 * LANES + e] + count
            n_pad = (tile_rows - count % tile_rows) % tile_rows

            def pad_start(r, carry, first=first):
                pltpu.make_async_copy(zero_row, slot_rows(first + r), sem.at[1]).start()
                return carry

            def pad_wait(r, carry, first=first):
                pltpu.make_async_copy(zero_row, slot_rows(first + r), sem.at[1]).wait()
                return carry

            lax.fori_loop(0, n_pad, pad_start, 0)
            lax.fori_loop(0, n_pad, pad_wait, 0)

        def tile_copy(n):
            rows = tile_rows * rpt
            return pltpu.make_async_copy(zero_ref, xs_ref.at[pl.ds(pl.multiple_of(n * rows, rows), rows)],
                                         sem.at[1])

        def unused_start(n, carry):
            tile_copy(n).start()
            return carry

        def unused_wait(n, carry):
            tile_copy(n).wait()
            return carry

        lax.fori_loop(meta_ref[LANES - 1], n_tiles, unused_start, 0)
        lax.fori_loop(meta_ref[LANES - 1], n_tiles, unused_wait, 0)

    for _ in range(2):
        pltpu.make_async_copy(h_ref, xs_ref.at[pl.ds(0, chunk * rpt)], sem.at[0]).wait()


def _dispatch(pos_flat, meta_flat, h_lin, n_slots, rpt, tile_rows, chunk=512):
    t = pos_flat.shape[0] // 2
    return pl.pallas_call(
        functools.partial(_dispatch_kernel, t=t, rpt=rpt, chunk=chunk, tile_rows=tile_rows,
                          n_tiles=n_slots // tile_rows),
        grid_spec=pltpu.PrefetchScalarGridSpec(
            num_scalar_prefetch=2,
            grid=(t // chunk,),
            in_specs=[pl.BlockSpec((chunk * rpt, LANES), lambda c, p, m: (c, 0))],
            out_specs=pl.BlockSpec(memory_space=pl.ANY),
            scratch_shapes=[pltpu.VMEM((tile_rows * rpt, LANES), h_lin.dtype), pltpu.SemaphoreType.DMA((2,))],
        ),
        out_shape=jax.ShapeDtypeStruct((n_slots * rpt, LANES), h_lin.dtype),
        compiler_params=_params("arbitrary"),
    )(pos_flat, meta_flat, h_lin)


def _ffn_kernel(meta_ref, xs_ref, wg_ref, wu_ref, wd_ref, ys_ref, wgb_ref, wub_ref, wdb_ref, *, tm):
    n = pl.program_id(0)
    n_used = meta_ref[LANES - 1]

    @pl.when(n < n_used)
    def _():
        @pl.when((n == 0) | (meta_ref[n] != meta_ref[jnp.maximum(n - 1, 0)]))
        def _():
            wgb_ref[...] = wg_ref[...].astype(BF16)
            wub_ref[...] = wu_ref[...].astype(BF16)
            wdb_ref[...] = wd_ref[...].astype(BF16)

        d = wgb_ref.shape[0]
        a = b = None
        for k0 in range(0, d, MXU_DIM):
            x = jnp.concatenate(
                [_load_token_major(xs_ref, tm, k0 // LANES + j).astype(BF16) for j in range(MXU_DIM // LANES)],
                axis=1)
            pa = _dot(x, wgb_ref[k0:k0 + MXU_DIM, :])
            pb = _dot(x, wub_ref[k0:k0 + MXU_DIM, :])
            a = pa if a is None else a + pa
            b = pb if b is None else b + pb
        act = (_silu(a) * b).astype(BF16)
        for n0 in range(0, d, MXU_DIM):
            _store_token_major(ys_ref, _dot(act, wdb_ref[:, n0:n0 + MXU_DIM]), n0)

    @pl.when(n >= n_used)
    def _():
        ys_ref[...] = jnp.zeros_like(ys_ref)


def _ffn(meta, xs, w_gate, w_up, w_down, layer, tm, n_tiles):
    _, ne, d, f = w_gate.shape
    rpt = d // LANES

    def row_tile(n, m):
        return (jnp.minimum(n, m[LANES - 1] - 1), 0)

    return pl.pallas_call(
        functools.partial(_ffn_kernel, tm=tm),
        grid_spec=pltpu.PrefetchScalarGridSpec(
            num_scalar_prefetch=1,
            grid=(n_tiles,),
            in_specs=[
                pl.BlockSpec((tm * rpt, LANES), row_tile),
                pl.BlockSpec((None, None, d, f), lambda n, m: (layer, m[n], 0, 0)),
                pl.BlockSpec((None, None, d, f), lambda n, m: (layer, m[n], 0, 0)),
                pl.BlockSpec((None, None, f, d), lambda n, m: (layer, m[n], 0, 0)),
            ],
            out_specs=pl.BlockSpec((tm * rpt, LANES), lambda n, m: (n, 0)),
            scratch_shapes=[
                pltpu.VMEM((d, f), BF16),
                pltpu.VMEM((d, f), BF16),
                pltpu.VMEM((f, d), BF16),
            ],
        ),
        out_shape=jax.ShapeDtypeStruct((n_tiles * tm * rpt, LANES), F32),
        compiler_params=_params("arbitrary"),
    )(meta, xs, w_gate, w_up, w_down)


def _combine_kernel(pos_ref, ys_ref, *refs, t, rpt, pitch, tm, n_parts, final_tiles):
    x_refs = refs[:n_parts]
    if final_tiles is None:
        wt_ref, gate_ref, o_ref, a0, b0, a1, b1, sem = refs[n_parts:]
    else:
        wt_ref, gate_ref, fg_ref, op_ref, os_ref, a0, b0, a1, b1, sem = refs[n_parts:]
    i = pl.program_id(0)
    n = pl.num_programs(0)
    bufs = ((a0, b0), (a1, b1))

    def issue(tile, s):
        def body(r, carry):
            tok = tile * tm + r
            for k in range(2):
                slot_row = pos_ref[k * t + tok]
                pltpu.make_async_copy(
                    ys_ref.at[pl.ds(pl.multiple_of(slot_row * rpt, rpt), rpt)],
                    bufs[s][k].at[pl.ds(pl.multiple_of(r * pitch, 8), rpt)],
                    sem.at[s]).start(priority=k)
            return carry

        lax.fori_loop(0, tm, body, 0, unroll=8)

    def wait(s):
        for k in range(2):
            pltpu.make_async_copy(ys_ref.at[pl.ds(0, tm * rpt)], bufs[s][k].at[pl.ds(0, tm * rpt)],
                                  sem.at[s]).wait()

    def compute(s, out_ref):
        w = wt_ref[...]
        w0, w1 = w[:, 0:1], w[:, 1:2]
        dn = x_refs[0].shape[1]
        for c in range(rpt):
            cols = slice(c * LANES, (c + 1) * LANES)
            part, pc = divmod(c * LANES, dn)
            y = (w0 * bufs[s][0][pl.ds(c, tm, stride=pitch), :]
                 + w1 * bufs[s][1][pl.ds(c, tm, stride=pitch), :])
            out_ref[:, cols] = x_refs[part][:, pc:pc + LANES] + gate_ref[:, cols] * y
        if final_tiles is not None:
            out_ref[...] = _rms_f32(out_ref[...]) * fg_ref[...]

    @pl.when(i == 0)
    def _():
        issue(0, 0)

    for s in range(2):
        @pl.when(i % 2 == s)
        def _(s=s):
            @pl.when(i + 1 < n)
            def _():
                issue(i + 1, 1 - s)

            wait(s)
            if final_tiles is None:
                compute(s, o_ref)
            else:
                @pl.when(i < final_tiles)
                def _():
                    compute(s, op_ref)

                @pl.when(i >= final_tiles)
                def _():
                    compute(s, os_ref)


def _combine(pos_flat, ys, x1_parts, wt, mod3, layer, row_fn, final=None, tm=256):
    t, dn = x1_parts[0].shape
    n_parts = len(x1_parts)
    d = dn * n_parts
    rpt = d // LANES
    pitch = rpt + 8
    base = layer * 48
    in_specs = ([pl.BlockSpec(memory_space=pl.ANY)]
                + [pl.BlockSpec((tm, dn), lambda i, p: (i, 0))] * n_parts
                + [pl.BlockSpec((tm, 2), lambda i, p: (i, 0)),
                   pl.BlockSpec((None, 1, d), lambda i, p: (base + row_fn(i) * 6 + 5, 0, 0))])
    args = [pos_flat, ys, *x1_parts, wt, mod3]
    if final is None:
        final_tiles = None
        out_specs = pl.BlockSpec((tm, d), lambda i, p: (i, 0))
        out_shape = jax.ShapeDtypeStruct((t, d), F32)
    else:
        fg, t_ctx = final
        final_tiles = t_ctx // tm
        in_specs.append(pl.BlockSpec((1, d), lambda i, p: (0, 0)))
        args.append(fg.reshape(1, d))
        out_specs = [
            pl.BlockSpec((tm, d), lambda i, p: (jnp.minimum(i, final_tiles - 1), 0)),
            pl.BlockSpec((tm, d), lambda i, p: (jnp.maximum(i - final_tiles, 0), 0)),
        ]
        out_shape = [jax.ShapeDtypeStruct((t_ctx, d), F32), jax.ShapeDtypeStruct((t - t_ctx, d), F32)]
    return pl.pallas_call(
        functools.partial(_combine_kernel, t=t, rpt=rpt, pitch=pitch, tm=tm, n_parts=n_parts,
                          final_tiles=final_tiles),
        grid_spec=pltpu.PrefetchScalarGridSpec(
            num_scalar_prefetch=1,
            grid=(t // tm,),
            in_specs=in_specs,
            out_specs=out_specs,
            scratch_shapes=[pltpu.VMEM((tm * pitch, LANES), F32)] * 4 + [pltpu.SemaphoreType.DMA((2,))],
        ),
        out_shape=out_shape,
        compiler_params=_params("arbitrary"),
    )(*args)


def _moe(lg, router_b, h_lin, x1, w_gate, w_up, w_down, mod3, layer, row_fn, final=None, tile_rows=256):
    t = x1[0].shape[0]
    n_tiles = (2 * t) // tile_rows + N_EXPERTS
    pos, wts, meta = _route(lg, router_b, tile_rows)
    pos_flat = pos[:2].reshape(-1)
    xs = _dispatch(pos_flat, meta[:3].reshape(-1), h_lin, n_tiles * tile_rows, h_lin.shape[0] // t, tile_rows)
    ys = _ffn(meta[0], xs, w_gate, w_up, w_down, layer, tile_rows, n_tiles)
    return _combine(pos_flat, ys, x1, wts[:2].T, mod3, layer, row_fn, final)


def _rope_tables(n_tok):
    rows = n_tok // GRID_W
    pos_r = jnp.repeat(jnp.arange(rows), GRID_W).astype(F32)
    pos_c = jnp.tile(jnp.arange(GRID_W), rows).astype(F32)
    n_freq = DK_A // 4
    inv = jnp.power(ROPE_BASE, -jnp.arange(n_freq, dtype=F32) / n_freq)
    ang_r = pos_r[:, None] * inv
    ang_c = pos_c[:, None] * inv
    cos_t = jnp.concatenate([jnp.cos(ang_r), jnp.cos(ang_r), jnp.cos(ang_c), jnp.cos(ang_c)], axis=1)
    sin_t = jnp.concatenate([-jnp.sin(ang_r), jnp.sin(ang_r), -jnp.sin(ang_c), jnp.sin(ang_c)], axis=1)
    return cos_t, sin_t


def kernel(x_prompt, x_sample, cache_k, cache_v, state_ret, c, c_ctx, norm_g, w_mod, b_mod, w_in_even, w_out_even, lam_q, lam_k, ret_decay_logit, w_in_odd, gmlp_norm_g, w_spatial, b_spatial, w_out_odd, router_w, router_b, w_gate, w_up, w_down, final_norm_g):
    batch, seq, d = x_prompt.shape
    dbatch, dseq, _ = x_sample.shape
    past = cache_k.shape[2]
    depth = norm_g.shape[0]
    t_ctx, t_lat = batch * seq, dbatch * dseq
    t_all = t_ctx + t_lat
    xc = x_prompt.reshape(t_ctx, d)
    xl = x_sample.reshape(t_lat, d)

    cond8 = jnp.concatenate([c_ctx[None, :], c, jnp.zeros((8 - 1 - dbatch, d), F32)], axis=0)
    mod3 = _modulation(cond8, w_mod, b_mod).reshape(depth * 8 * 6, 1, d)
    router_wt = router_w.T
    cos_t, sin_t = _rope_tables(dseq)

    tm_in = 512
    tm_out = 512
    tm_cmb = 256
    w_in0 = w_in_even.reshape(d, -1)
    w_out0 = w_out_even.reshape(-1, d)
    w_in1 = w_in_odd.reshape(d, -1)
    w_out1 = w_out_odd.reshape(-1, d)
    lq, lk, dlog = lam_q.reshape(2, DK_A), lam_k.reshape(2, DK_A), ret_decay_logit.reshape(2, H_B)

    def rows_ctx(i):
        return 0

    def rows_lat(tiles_per_batch):
        return lambda i: 1 + i // tiles_per_batch

    def rows_all(tm):
        n_ctx, per = t_ctx // tm, dseq // tm
        return lambda i: jnp.maximum(i - n_ctx, 0) // per + jnp.where(i >= n_ctx, 1, 0)

    lam_init = 0.8 - 0.6 * math.exp(-0.3 * 0)
    half0 = w_in0.shape[1] // 2
    g00 = norm_g[0, 0]
    rl = rows_lat(dseq // tm_in)
    pa_c, k_new, v_new = _in_proj(xc, g00, mod3, 0, rows_ctx, w_in0, 0, half0, "even_kv", tm=tm_in)
    (pb_c,) = _in_proj(xc, g00, mod3, 0, rows_ctx, w_in0, half0, half0, "even", tm=tm_in)
    (pa_l,) = _in_proj(xl, g00, mod3, 0, rl, w_in0, 0, half0, "even", tm=tm_in)
    (pb_l,) = _in_proj(xl, g00, mod3, 0, rl, w_in0, half0, half0, "even", tm=tm_in)
    oa_c, or_c, s_new = _mixer(pa_c, pb_c, lq, lk, dlog, lam_init, batch, seq, H_A)
    ctx = (cache_k.reshape(dbatch, past, H_A * 2 * DK_A), cache_v.reshape(dbatch, past, H_A * DV_A),
           state_ret.reshape(dbatch, 2, H_B, DK_B, DV_B), cos_t, sin_t)
    oa_l, or_l = _mixer(pa_l, pb_l, lq, lk, dlog, lam_init, dbatch, dseq, 1, ctx)
    x1, h2, lg = _out_proj([([oa_c, or_c], xc), ([oa_l, or_l], xl)], w_out0, norm_g[0, 1], mod3, 0,
                           rows_all(tm_out), router_wt, tm=tm_out)
    x2 = _moe(lg, router_b, h2, x1, w_gate, w_up, w_down, mod3, 0, rows_all(tm_cmb))

    q1 = w_in1.shape[1] // 4
    g10 = norm_g[1, 0]
    ra = rows_all(tm_in)
    v_blk, v_ss = zip(*[_in_proj(x2, g10, mod3, 1, ra, w_in1, (2 + j) * q1, q1, "gmlp_v", tm=tm_in)
                        for j in range(2)])
    gmlp = (gmlp_norm_g.reshape(1, -1), w_spatial.reshape(H_C, CHUNK_C, CHUNK_C),
            b_spatial.reshape(H_C, CHUNK_C, 1))
    g0, g1 = [_in_proj(x2, g10, mod3, 1, ra, w_in1, j * q1, q1, "gmlp_u",
                       gating=(v_blk[j], v_ss[0], v_ss[1], *gmlp, j), tm=tm_in)[0] for j in range(2)]
    x3, h4, lg2 = _out_proj([([g0, g1], x2)], w_out1, norm_g[1, 1], mod3, 1, rows_all(tm_out), router_wt,
                            tm=tm_out)
    y_p, y_s = _moe(lg2, router_b, h4, x3, w_gate, w_up, w_down, mod3, 1, rows_all(tm_cmb),
                    final=(final_norm_g, t_ctx))

    y_prompt = y_p.reshape(batch, seq, d)
    y_sample = y_s.reshape(dbatch, dseq, d)
    new_cache_k = k_new.reshape(batch, 1, seq, H_A, 2, DK_A)
    new_cache_v = v_new.reshape(batch, 1, seq, DV_A // LANES, H_A, LANES).swapaxes(3, 4).reshape(
        batch, 1, seq, H_A, DV_A)
    new_state_ret = s_new.reshape(batch, 1, 2, H_B, DK_B, DV_B)
    return (y_prompt, y_sample, new_cache_k, new_cache_v, new_state_ret)
```

```python
import functools
import math

import jax
import jax.numpy as jnp
from jax import lax
from jax.experimental import pallas as pl
from jax.experimental.pallas import tpu as pltpu

F32 = jnp.float32
BF16 = jnp.bfloat16

EPS = 1e-6
ROPE_BASE = 10000.0
GRID_W = 64
H_A, DK_A, DV_A = 4, 128, 256
H_B, DK_B, DV_B = 4, 128, 256
RET_CHUNK = 128
H_C, CHUNK_C = 8, 128
N_EXPERTS, N_GROUPS, GROUP_SIZE = 16, 4, 4

VMEM_LIMIT_BYTES = 56 * 1024 * 1024


def _params(*sem):
    return pltpu.CompilerParams(dimension_semantics=sem, vmem_limit_bytes=VMEM_LIMIT_BYTES)


def _rms_f32(x):
    return x * lax.rsqrt(jnp.mean(x * x, axis=-1, keepdims=True) + EPS)


def _silu(x):
    return x * jax.nn.sigmoid(x)


def _dot(a, b):
    return jnp.dot(a, b, preferred_element_type=F32)


def _dot_nt(a, b):
    return lax.dot_general(a, b, (((1,), (1,)), ((), ())), preferred_element_type=F32)


def _dot_nt_x3(a, b):
    ah = a.astype(BF16)
    al = (a - ah.astype(F32)).astype(BF16)
    bh = b.astype(BF16)
    bl = (b - bh.astype(F32)).astype(BF16)
    return _dot_nt(ah, bh) + (_dot_nt(al, bh) + _dot_nt(ah, bl))


LANES = 128
MXU_DIM = 256


def _store_token_major(ref, val, col0=0):
    tm, width = val.shape
    rpt = ref.shape[0] // tm
    for c in range(width // LANES):
        ref[pl.ds(col0 // LANES + c, tm, stride=rpt), :] = val[:, c * LANES:(c + 1) * LANES]


def _load_token_major(ref, tm, c):
    rpt = ref.shape[0] // tm
    return ref[pl.ds(c, tm, stride=rpt), :]


def _mod_kernel(c_ref, w_ref, b_ref, o_ref):
    s = _silu(c_ref[...]).astype(BF16)
    o_ref[...] = _dot(s, w_ref[...].astype(BF16)) + b_ref[...]


def _modulation(cond8, w_mod, b_mod, tn=1024):
    depth, d, n = w_mod.shape
    return pl.pallas_call(
        _mod_kernel,
        grid=(depth, n // tn),
        in_specs=[
            pl.BlockSpec((8, d), lambda l, j: (0, 0)),
            pl.BlockSpec((None, d, tn), lambda l, j: (l, 0, j)),
            pl.BlockSpec((None, 1, tn), lambda l, j: (l, 0, j)),
        ],
        out_specs=pl.BlockSpec((None, 8, tn), lambda l, j: (l, 0, j)),
        out_shape=jax.ShapeDtypeStruct((depth, 8, n), F32),
        compiler_params=_params("arbitrary", "arbitrary"),
    )(cond8, w_mod, b_mod.reshape(depth, 1, n))


def _load_weight_slab(w_hbm, slab_ref, stage_ref, sem, col0):
    kdim, n = slab_ref.shape
    rows = stage_ref.shape[1]
    nb = kdim // rows

    def copy(j):
        return pltpu.make_async_copy(w_hbm.at[pl.ds(j * rows, rows), pl.ds(col0, n)],
                                     stage_ref.at[j % 2], sem.at[j % 2])

    copy(0).start()
    for j in range(nb):
        if j + 1 < nb:
            copy(j + 1).start()
        copy(j).wait()
        slab_ref[j * rows:(j + 1) * rows, :] = stage_ref[j % 2].astype(BF16)


def _gelu(x):
    return 0.5 * x * (1.0 + lax.erf(x * (2.0 ** -0.5)))


def _in_kernel(x_ref, g_ref, sh_ref, sc_ref, w_hbm, *rest, col0, mode, kv_cols, chunk, e_c):
    if mode == "even_kv":
        p_ref, k_ref, v_ref, slab_ref, stage_ref, h_ref, sem = rest
    elif mode == "gmlp_v":
        p_ref, ss_ref, slab_ref, stage_ref, h_ref, sem = rest
    elif mode == "gmlp_u":
        vb_ref, sa_ref, sb_ref, ng_ref, ws_ref, bs_ref, p_ref, slab_ref, stage_ref, h_ref, vn_ref, sem = rest
    else:
        p_ref, slab_ref, stage_ref, h_ref, sem = rest

    @pl.when(pl.program_id(0) == 0)
    def _():
        _load_weight_slab(w_hbm, slab_ref, stage_ref, sem, col0)

    h = (_rms_f32(x_ref[...]) * g_ref[...]) * (1.0 + sc_ref[...]) + sh_ref[...]
    h_ref[...] = h.astype(BF16)
    n = slab_ref.shape[1]
    tm = h_ref.shape[0]
    if mode == "gmlp_u":
        inv = lax.rsqrt((sa_ref[...] + sb_ref[...]) / e_c + EPS)
        vn_ref[...] = ((vb_ref[...].astype(F32) * inv) * ng_ref[...]).astype(BF16)
    ss = None
    for ci, c0 in enumerate(range(0, n, chunk)):
        acc = _dot(h_ref[...], slab_ref[:, c0:c0 + chunk])
        if mode == "gmlp_v":
            v = _gelu(acc)
            p_ref[:, c0:c0 + chunk] = v.astype(BF16)
            part = jnp.sum(v * v, axis=-1, keepdims=True)
            ss = part if ss is None else ss + part
        elif mode == "gmlp_u":
            u = _gelu(acc)
            wsb = ws_ref[ci].astype(BF16)
            for r0 in range(0, tm, CHUNK_C):
                mixed = _dot(wsb, vn_ref[r0:r0 + CHUNK_C, c0:c0 + chunk]) + bs_ref[ci]
                p_ref[r0:r0 + CHUNK_C, c0:c0 + chunk] = (u[r0:r0 + CHUNK_C] * mixed).astype(BF16)
        else:
            p_ref[:, c0:c0 + chunk] = acc.astype(BF16)
        if mode == "even_kv":
            (k0, k1), (v0, v1) = kv_cols
            a0 = col0 + c0
            if k0 <= a0 < k1:
                _store_token_major(k_ref, acc, a0 - k0)
            if v0 <= a0 < v1:
                tm = acc.shape[0]
                rpt = v_ref.shape[0] // tm
                for cc in range(chunk // LANES):
                    head, half = divmod((a0 - v0) // LANES + cc, DV_A // LANES)
                    v_ref[pl.ds(half * H_A + head, tm, stride=rpt), :] = acc[:, cc * LANES:(cc + 1) * LANES]
    if mode == "gmlp_v":
        ss_ref[...] = ss


def _in_proj(x, g, mod3, layer, row_fn, w, col0, n_cols, mode, gating=None, tm=512, chunk=512, stage_rows=256):
    t, d = x.shape
    base = layer * 48

    def mod_spec(kind):
        return pl.BlockSpec((None, 1, d), lambda i: (base + row_fn(i) * 6 + kind, 0, 0))

    in_specs = [
        pl.BlockSpec((tm, d), lambda i: (i, 0)),
        pl.BlockSpec((1, d), lambda i: (0, 0)),
        mod_spec(0),
        mod_spec(1),
        pl.BlockSpec(memory_space=pl.ANY),
    ]
    args = [x, g.reshape(1, d), mod3, mod3, w]
    out_specs = [pl.BlockSpec((tm, n_cols), lambda i: (i, 0))]
    out_shape = [jax.ShapeDtypeStruct((t, n_cols), BF16)]
    scratch = [
        pltpu.VMEM((d, n_cols), BF16),
        pltpu.VMEM((2, stage_rows, n_cols), F32),
        pltpu.VMEM((tm, d), BF16),
    ]
    kv_cols = None
    e_c = None
    if mode == "even_kv":
        a_qk = H_A * 2 * DK_A
        a_v = H_A * DV_A
        kv_cols = ((a_qk, 2 * a_qk), (2 * a_qk, 2 * a_qk + a_v))
        for width in (a_qk, a_v):
            out_specs.append(pl.BlockSpec((tm * width // LANES, LANES), lambda i: (i, 0)))
            out_shape.append(jax.ShapeDtypeStruct((t * width // LANES, LANES), F32))
    elif mode == "gmlp_v":
        out_specs.append(pl.BlockSpec((tm, 1), lambda i: (i, 0)))
        out_shape.append(jax.ShapeDtypeStruct((t, 1), F32))
    elif mode == "gmlp_u":
        v_blk, ss_a, ss_b, ng, w_s, b_s, blk = gating
        e_c = ng.shape[1]
        groups = w_s.shape[0] * n_cols // e_c
        chunk = n_cols // groups
        in_specs += [
            pl.BlockSpec((tm, n_cols), lambda i: (i, 0)),
            pl.BlockSpec((tm, 1), lambda i: (i, 0)),
            pl.BlockSpec((tm, 1), lambda i: (i, 0)),
            pl.BlockSpec((1, n_cols), lambda i: (0, blk)),
            pl.BlockSpec((groups, CHUNK_C, CHUNK_C), lambda i: (blk, 0, 0)),
            pl.BlockSpec((groups, CHUNK_C, 1), lambda i: (blk, 0, 0)),
        ]
        args += [v_blk, ss_a, ss_b, ng, w_s, b_s]
        scratch.append(pltpu.VMEM((tm, n_cols), BF16))
    return pl.pallas_call(
        functools.partial(_in_kernel, col0=col0, mode=mode, kv_cols=kv_cols, chunk=chunk, e_c=e_c),
        grid=(t // tm,),
        in_specs=in_specs,
        out_specs=out_specs,
        out_shape=out_shape,
        scratch_shapes=scratch + [pltpu.SemaphoreType.DMA((2,))],
        compiler_params=_params("arbitrary"),
    )(*args)


def _rope(x, c, s):
    lane = lax.broadcasted_iota(jnp.int32, x.shape, 1)
    partner = jnp.where((lane & 32) == 0, pltpu.roll(x, 96, 1), pltpu.roll(x, 32, 1))
    return x * c + partner * s


def _mixer_kernel(*refs, seq, has_ctx, lam_init, past, hps):
    if has_ctx:
        (aq_ref, ak_ref, av_ref, bq_ref, bk_ref, bv_ref, bg_ref, lq_ref, lk_ref, dl_ref,
         ck_ref, cv_ref, s0_ref, cos_ref, sin_ref,
         oa_ref, or_ref, kall_ref, vall_ref, of_ref, ob_ref, stf_ref, stb_ref) = refs
    else:
        (aq_ref, ak_ref, av_ref, bq_ref, bk_ref, bv_ref, bg_ref, lq_ref, lk_ref, dl_ref,
         oa_ref, or_ref, sout_ref, of_ref, ob_ref, stf_ref, stb_ref) = refs
    rt = 256
    straight = seq <= rt
    tq = 256 if straight else 512
    cs = RET_CHUNK
    nc = seq // cs

    prod = lq_ref[...] * lk_ref[...]
    ee = jnp.exp(jnp.sum(prod, axis=1, keepdims=True))
    lam = ee[0:1, :] - ee[1:2, :] + lam_init
    scale = DK_A ** -0.5
    ii = lax.broadcasted_iota(jnp.int32, (cs, cs), 0)
    jj = lax.broadcasted_iota(jnp.int32, (cs, cs), 1)
    ic = lax.broadcasted_iota(jnp.int32, (cs, 1), 0).astype(F32)

    def loop(n, body):
        if straight:
            for it in range(n):
                body(it)
        else:
            lax.fori_loop(0, n, lambda it, c: (body(it), c)[1], 0)

    def rows(it, size):
        return pl.ds(it * size, size) if straight else pl.ds(pl.multiple_of(it * size, size), size)

    for hh in range(hps):
        qk_cols = slice(hh * 2 * DK_A, (hh + 1) * 2 * DK_A)
        va_cols = slice(hh * DV_A, (hh + 1) * DV_A)
        kb_cols = slice(hh * DK_B, (hh + 1) * DK_B)
        vb_cols = slice(hh * DV_B, (hh + 1) * DV_B)

        if has_ctx:
            for c in range(2):
                kall_ref[c, 0:past, :] = ck_ref[:, c * DK_A:(c + 1) * DK_A].astype(BF16)
            vall_ref[0:past, :] = cv_ref[...].astype(BF16)

            def copy_body(r):
                rr = rows(r, rt)
                cs_, sn_ = cos_ref[rr, :], sin_ref[rr, :]
                kk = ak_ref[rr, qk_cols].astype(F32)
                for c in range(2):
                    kall_ref[c, pl.ds(past + pl.multiple_of(r * rt, rt), rt), :] = _rope(
                        kk[:, c * DK_A:(c + 1) * DK_A], cs_, sn_).astype(BF16)
                vall_ref[pl.ds(past + pl.multiple_of(r * rt, rt), rt), :] = av_ref[rr, va_cols]

            loop(seq // rt, copy_body)

        def attn_body(qt, qk_cols=qk_cols, va_cols=va_cols):
            rr = rows(qt, tq)
            q = aq_ref[rr, qk_cols]
            if has_ctx:
                qf = q.astype(F32)
                cs_, sn_ = cos_ref[rr, :], sin_ref[rr, :]
            probs = []
            for c in range(2):
                if has_ctx:
                    qc = _rope(qf[:, c * DK_A:(c + 1) * DK_A], cs_, sn_).astype(BF16)
                    keys = kall_ref[c]
                else:
                    qc = q[:, c * DK_A:(c + 1) * DK_A]
                    keys = ak_ref[:, hh * 2 * DK_A + c * DK_A:hh * 2 * DK_A + (c + 1) * DK_A]
                s = _dot_nt(qc, keys) * scale
                e = jnp.exp(s - jnp.max(s, axis=-1, keepdims=True))
                probs.append(e * (1.0 / jnp.sum(e, axis=-1, keepdims=True)))
            a = (probs[0] - lam * probs[1]).astype(BF16)
            o = _dot(a, vall_ref[...] if has_ctx else av_ref[:, va_cols])
            oa_ref[rr, va_cols] = (_rms_f32(o) * (1.0 - lam_init)).astype(BF16)

        loop(seq // tq, attn_body)

        tables = []
        for d in range(2):
            lg = jax.nn.log_sigmoid(dl_ref[d, hh])
            diff = (ii - jj if d == 0 else jj - ii).astype(F32)
            mask = jnp.where(diff >= 0, jnp.exp(jnp.maximum(diff, 0.0) * lg), 0.0)
            if d == 0:
                qdec = jnp.exp((ic + 1.0) * lg)
                kdec = jnp.exp((cs - 1.0 - ic) * lg)
            else:
                qdec = jnp.exp((cs - ic) * lg)
                kdec = jnp.exp(ic * lg)
            tables.append((mask, qdec, kdec, jnp.exp(cs * lg)))
        st_refs = (stf_ref, stb_ref)
        o_refs = (of_ref, ob_ref)
        for d in range(2):
            if has_ctx:
                st_refs[d][hh] = s0_ref[d]
            else:
                st_refs[d][hh] = jnp.zeros((DK_B, DV_B), F32)

        def chunk_body(n, hh=hh, kb_cols=kb_cols, vb_cols=vb_cols, tables=tables):
            for d in range(2):
                mask, qdec, kdec, cdec = tables[d]
                rr = rows(n if d == 0 else nc - 1 - n, cs)
                qc = bq_ref[rr, kb_cols].astype(F32)
                kc = bk_ref[rr, kb_cols].astype(F32) * (DK_B ** -0.5)
                vc = bv_ref[rr, vb_cols]
                state = st_refs[d][hh]
                qk = _dot_nt(qc.astype(BF16), kc.astype(BF16)) * mask
                o_refs[d][rr, vb_cols] = (_dot(qk.astype(BF16), vc)
                                          + _dot((qc * qdec).astype(BF16), state.astype(BF16)))
                kd = (kc * kdec).T.astype(BF16)
                st_refs[d][hh] = cdec * state + _dot(kd, vc)

        loop(nc, chunk_body)

        def final_body(r, vb_cols=vb_cols):
            rr = rows(r, rt)
            tot = of_ref[rr, vb_cols] + ob_ref[rr, vb_cols]
            gate = _silu(bg_ref[rr, vb_cols].astype(F32))
            or_ref[rr, vb_cols] = (_rms_f32(tot) * gate).astype(BF16)

        loop(seq // rt, final_body)
        if not has_ctx:
            for d in range(2):
                sout_ref[d, hh] = st_refs[d][hh]


def _mixer(pa, pb, lam_q, lam_k, decay_logit, lam_init, batch, seq, hps, ctx=None):
    has_ctx = ctx is not None
    past = ctx[0].shape[1] if has_ctx else 0

    def col(head_width, off):
        width = hps * head_width
        return pl.BlockSpec((seq, width), lambda b, hg: (b, off // width + hg))

    a_qk = H_A * 2 * DK_A
    b_qk = H_B * DK_B
    b_v = H_B * DV_B
    in_specs = [
        col(2 * DK_A, 0), col(2 * DK_A, a_qk), col(DV_A, 2 * a_qk),
        col(DK_B, 0), col(DK_B, b_qk), col(DV_B, 2 * b_qk), col(DV_B, 2 * b_qk + b_v),
        pl.BlockSpec((2, DK_A), lambda b, hg: (0, 0)),
        pl.BlockSpec((2, DK_A), lambda b, hg: (0, 0)),
        pl.BlockSpec((2, hps, 1, 1), lambda b, hg: (0, hg, 0, 0)),
    ]
    args = [pa] * 3 + [pb] * 4 + [lam_q, lam_k, decay_logit.reshape(2, H_B, 1, 1)]
    out_specs = [
        pl.BlockSpec((seq, hps * DV_A), lambda b, hg: (b, hg)),
        pl.BlockSpec((seq, hps * DV_B), lambda b, hg: (b, hg)),
    ]
    out_shape = [
        jax.ShapeDtypeStruct((batch * seq, H_A * DV_A), BF16),
        jax.ShapeDtypeStruct((batch * seq, b_v), BF16),
    ]
    scratch = []
    if has_ctx:
        assert hps == 1
        ck, cv, s0, cos_t, sin_t = ctx
        in_specs += [
            pl.BlockSpec((None, past, 2 * DK_A), lambda b, hg: (b, 0, hg)),
            pl.BlockSpec((None, past, DV_A), lambda b, hg: (b, 0, hg)),
            pl.BlockSpec((None, 2, None, DK_B, DV_B), lambda b, hg: (b, 0, hg, 0, 0)),
            pl.BlockSpec((seq, DK_A), lambda b, hg: (0, 0)),
            pl.BlockSpec((seq, DK_A), lambda b, hg: (0, 0)),
        ]
        args += [ck, cv, s0, cos_t, sin_t]
        scratch += [pltpu.VMEM((2, seq + past, DK_A), BF16), pltpu.VMEM((seq + past, DV_A), BF16)]
    else:
        out_specs.append(pl.BlockSpec((None, 2, hps, DK_B, DV_B), lambda b, hg: (b, 0, hg, 0, 0)))
        out_shape.append(jax.ShapeDtypeStruct((batch, 2, H_B, DK_B, DV_B), F32))
    scratch += [
        pltpu.VMEM((seq, hps * DV_B), F32),
        pltpu.VMEM((seq, hps * DV_B), F32),
        pltpu.VMEM((hps, DK_B, DV_B), F32),
        pltpu.VMEM((hps, DK_B, DV_B), F32),
    ]
    return pl.pallas_call(
        functools.partial(_mixer_kernel, seq=seq, has_ctx=has_ctx, lam_init=lam_init, past=past, hps=hps),
        grid=(batch, H_A // hps),
        in_specs=in_specs,
        out_specs=out_specs,
        out_shape=out_shape,
        scratch_shapes=scratch,
        compiler_params=_params("arbitrary", "arbitrary"),
    )(*args)


def _out_kernel(*refs, seg_tiles, n_lhs, part, n_parts):
    n_seg = len(seg_tiles)
    per_seg = n_lhs + 1
    seg_refs = [refs[s * per_seg:(s + 1) * per_seg] for s in range(n_seg)]
    rest = refs[n_seg * per_seg:]
    last = part == n_parts - 1
    if last:
        gate_ref, g_ref, sh_ref, sc_ref, rw_ref = rest[:5]
        prev_refs = rest[5:5 + part]
        w_hbm, x1_ref, h2_ref, lg_ref, slab_ref, stage_ref, sem = rest[5 + part:]
    else:
        gate_ref, w_hbm, x1_ref, slab_ref, stage_ref, sem = rest
    i = pl.program_id(0)
    dn = slab_ref.shape[1]

    @pl.when(i == 0)
    def _():
        _load_weight_slab(w_hbm, slab_ref, stage_ref, sem, part * dn)

    def run(lhs_refs, x_ref):
        acc = None
        r0 = 0
        for lhs_ref in lhs_refs:
            kk = lhs_ref.shape[1]
            prod = _dot(lhs_ref[...], slab_ref[r0:r0 + kk, :])
            acc = prod if acc is None else acc + prod
            r0 += kk
        x1 = x_ref[...] + gate_ref[...] * acc
        x1_ref[...] = x1
        if last:
            cols = [p_ref[...] for p_ref in prev_refs] + [x1]
            d = dn * n_parts
            ms = sum(jnp.sum(v * v, axis=-1, keepdims=True) for v in cols) / d
            inv = lax.rsqrt(ms + EPS)
            lg = None
            for j, v in enumerate(cols):
                cs_ = slice(j * dn, (j + 1) * dn)
                h2 = ((v * inv) * g_ref[:, cs_]) * (1.0 + sc_ref[:, cs_]) + sh_ref[:, cs_]
                _store_token_major(h2_ref, h2, j * dn)
                pl_ = _dot_nt_x3(rw_ref[:, cs_], h2)
                lg = pl_ if lg is None else lg + pl_
            lg_ref[...] = lg

    if n_seg == 1:
        run(seg_refs[0][:n_lhs], seg_refs[0][n_lhs])
    else:
        lo = 0
        for s in range(n_seg):
            @pl.when((i >= lo) & (i < lo + seg_tiles[s]))
            def _(s=s):
                run(seg_refs[s][:n_lhs], seg_refs[s][n_lhs])

            lo += seg_tiles[s]


def _out_proj(segments, w, g, mod3, layer, row_fn, router_wt, n_parts=2, tm=512, stage_rows=256):
    n_lhs = len(segments[0][0])
    d = segments[0][1].shape[1]
    dn = d // n_parts
    kdim = w.shape[0]
    base = layer * 48
    ne = router_wt.shape[0]
    seg_tiles = tuple(x.shape[0] // tm for _, x in segments)
    t_total = sum(x.shape[0] for _, x in segments)
    rpt = d // LANES
    x1_parts = []
    for part in range(n_parts):
        last = part == n_parts - 1
        in_specs, args = [], []
        lo = 0
        for (lhs_list, x), nt in zip(segments, seg_tiles):
            def seg_row(i, lo=lo, nt=nt):
                return jnp.clip(i - lo, 0, nt - 1)

            for lhs in lhs_list:
                in_specs.append(pl.BlockSpec((tm, lhs.shape[1]), lambda i, f=seg_row: (f(i), 0)))
                args.append(lhs)
            in_specs.append(pl.BlockSpec((tm, dn), lambda i, f=seg_row: (f(i), part)))
            args.append(x)
            lo += nt
        in_specs.append(pl.BlockSpec((None, 1, dn), lambda i: (base + row_fn(i) * 6 + 2, 0, part)))
        args.append(mod3)
        out_specs = [pl.BlockSpec((tm, dn), lambda i: (i, 0))]
        out_shape = [jax.ShapeDtypeStruct((t_total, dn), F32)]
        if last:
            in_specs += [
                pl.BlockSpec((1, d), lambda i: (0, 0)),
                pl.BlockSpec((None, 1, d), lambda i: (base + row_fn(i) * 6 + 3, 0, 0)),
                pl.BlockSpec((None, 1, d), lambda i: (base + row_fn(i) * 6 + 4, 0, 0)),
                pl.BlockSpec((ne, d), lambda i: (0, 0)),
            ] + [pl.BlockSpec((tm, dn), lambda i: (i, 0))] * part
            args += [g.reshape(1, d), mod3, mod3, router_wt] + x1_parts
            out_specs += [
                pl.BlockSpec((tm * rpt, LANES), lambda i: (i, 0)),
                pl.BlockSpec((ne, tm), lambda i: (0, i)),
            ]
            out_shape += [
                jax.ShapeDtypeStruct((t_total * rpt, LANES), F32),
                jax.ShapeDtypeStruct((ne, t_total), F32),
            ]
        in_specs.append(pl.BlockSpec(memory_space=pl.ANY))
        args.append(w)
        outs = pl.pallas_call(
            functools.partial(_out_kernel, seg_tiles=seg_tiles, n_lhs=n_lhs, part=part, n_parts=n_parts),
            grid=(t_total // tm,),
            in_specs=in_specs,
            out_specs=out_specs,
            out_shape=out_shape,
            scratch_shapes=[
                pltpu.VMEM((kdim, dn), BF16),
                pltpu.VMEM((2, stage_rows, dn), F32),
                pltpu.SemaphoreType.DMA((2,)),
            ],
            compiler_params=_params("arbitrary"),
        )(*args)
        x1_parts.append(outs[0])
    return x1_parts, outs[1], outs[2]


def _first_index(vals, target):
    idx = jnp.full(target.shape, len(vals) - 1, jnp.int32)
    for k in range(len(vals) - 2, -1, -1):
        idx = jnp.where(vals[k] == target, k, idx)
    return idx


def _pick(vals, idx):
    out = vals[-1]
    for k in range(len(vals) - 2, -1, -1):
        out = jnp.where(idx == k, vals[k], out)
    return out


def _route_kernel(lg_ref, rb_ref, pos_ref, wt_ref, meta_ref, rank_ref, *, tile_rows, scan):
    logits = lg_ref[...]
    t = logits.shape[1]
    scores = jax.nn.sigmoid(logits)
    biased = scores + rb_ref[...]
    s_rows = [scores[e:e + 1, :] for e in range(N_EXPERTS)]
    b_rows = [biased[e:e + 1, :] for e in range(N_EXPERTS)]
    grp = []
    for g in range(N_GROUPS):
        r = b_rows[g * GROUP_SIZE:(g + 1) * GROUP_SIZE]
        best = None
        for a in range(GROUP_SIZE):
            for b in range(a + 1, GROUP_SIZE):
                pair = r[a] + r[b]
                best = pair if best is None else jnp.maximum(best, pair)
        grp.append(best)
    gmax = functools.reduce(jnp.maximum, grp)
    gsel = _first_index(grp, gmax)
    cand_b = [_pick([b_rows[g * GROUP_SIZE + k] for g in range(N_GROUPS)], gsel) for k in range(GROUP_SIZE)]
    cand_s = [_pick([s_rows[g * GROUP_SIZE + k] for g in range(N_GROUPS)], gsel) for k in range(GROUP_SIZE)]
    m1 = functools.reduce(jnp.maximum, cand_b)
    k1 = _first_index(cand_b, m1)
    rest = [jnp.where(k1 == k, -jnp.inf, cand_b[k]) for k in range(GROUP_SIZE)]
    m2 = functools.reduce(jnp.maximum, rest)
    k2 = _first_index(rest, m2)
    w1 = _pick(cand_s, k1)
    w2 = _pick(cand_s, k2)
    wsum = w1 + w2
    w1 = w1 / wsum
    w2 = w2 / wsum
    e1 = gsel * GROUP_SIZE + k1
    e2 = gsel * GROUP_SIZE + k2
    wt_ref[...] = jnp.zeros_like(wt_ref)
    wt_ref[0:1, :] = w1
    wt_ref[1:2, :] = w2

    eid = lax.broadcasted_iota(jnp.int32, logits.shape, 0)
    sel1 = eid == e1
    sel2 = eid == e2
    ind = jnp.where(sel1 | sel2, 1.0, 0.0)
    ua = lax.broadcasted_iota(jnp.int32, (scan, scan), 0)
    ub = lax.broadcasted_iota(jnp.int32, (scan, scan), 1)
    upper = jnp.where(ua <= ub, 1.0, 0.0).astype(BF16)
    carry = jnp.zeros((N_EXPERTS, 1), F32)
    for blk in range(t // scan):
        seg = ind[:, blk * scan:(blk + 1) * scan]
        incl = _dot(seg.astype(BF16), upper)
        rank_ref[:, blk * scan:(blk + 1) * scan] = incl - seg + carry
        carry = carry + incl[:, scan - 1:scan]
    n_tiles = jnp.floor((carry + (tile_rows - 1.0)) * (1.0 / tile_rows))
    la = lax.broadcasted_iota(jnp.int32, (N_EXPERTS, N_EXPERTS), 0)
    lb = lax.broadcasted_iota(jnp.int32, (N_EXPERTS, N_EXPERTS), 1)
    lower = jnp.where(lb < la, 1.0, 0.0).astype(BF16)
    start = _dot(lower, jnp.broadcast_to(n_tiles, (N_EXPERTS, LANES)).astype(BF16))
    end = start + n_tiles
    slot = start[:, 0:1] * tile_rows + rank_ref[...]
    pos_ref[...] = jnp.zeros_like(pos_ref)
    pos_ref[0:1, :] = jnp.sum(jnp.where(sel1, slot, 0.0), axis=0, keepdims=True).astype(jnp.int32)
    pos_ref[1:2, :] = jnp.sum(jnp.where(sel2, slot, 0.0), axis=0, keepdims=True).astype(jnp.int32)

    n_used = end[N_EXPERTS - 1:N_EXPERTS, :]
    tile = lax.broadcasted_iota(jnp.int32, (N_EXPERTS, LANES), 1).astype(F32)
    tile = jnp.minimum(tile, n_used - 1.0)
    tile_expert = jnp.sum(jnp.where(tile >= end, 1.0, 0.0), axis=0, keepdims=True)
    lane = lax.broadcasted_iota(jnp.int32, (1, LANES), 1)
    meta_ref[...] = jnp.zeros_like(meta_ref)
    meta_ref[0:1, :] = jnp.where(lane == LANES - 1, n_used, tile_expert).astype(jnp.int32)
    diag = lax.broadcasted_iota(jnp.int32, (N_EXPERTS, LANES), 0) == lax.broadcasted_iota(
        jnp.int32, (N_EXPERTS, LANES), 1)
    meta_ref[1:2, :] = jnp.sum(jnp.where(diag, carry, 0.0), axis=0, keepdims=True).astype(jnp.int32)
    meta_ref[2:3, :] = jnp.sum(jnp.where(diag, start * tile_rows, 0.0), axis=0, keepdims=True).astype(jnp.int32)


def _route(logits_t, router_b, tile_rows, scan=512):
    ne, t = logits_t.shape
    return pl.pallas_call(
        functools.partial(_route_kernel, tile_rows=tile_rows, scan=scan),
        out_shape=[
            jax.ShapeDtypeStruct((8, t), jnp.int32),
            jax.ShapeDtypeStruct((8, t), F32),
            jax.ShapeDtypeStruct((8, LANES), jnp.int32),
        ],
        scratch_shapes=[pltpu.VMEM((ne, t), F32)],
        compiler_params=pltpu.CompilerParams(vmem_limit_bytes=VMEM_LIMIT_BYTES),
    )(logits_t, router_b.reshape(ne, 1))


def _dispatch_kernel(pos_ref, meta_ref, h_ref, xs_ref, zero_ref, sem, *, t, rpt, chunk, tile_rows, n_tiles):
    c = pl.program_id(0)

    def slot_rows(slot):
        return xs_ref.at[pl.ds(pl.multiple_of(slot * rpt, rpt), rpt)]

    def body(r, carry):
        tok = c * chunk + r
        src = h_ref.at[pl.ds(pl.multiple_of(r * rpt, rpt), rpt)]
        for k in range(2):
            pltpu.make_async_copy(src, slot_rows(pos_ref[k * t + tok]), sem.at[0]).start(priority=k)
        return carry

    lax.fori_loop(0, chunk, body, 0, unroll=8)

    @pl.when(c == 0)
    def _():
        zero_ref[...] = jnp.zeros_like(zero_ref)
        zero_row = zero_ref.at[pl.ds(0, rpt)]
        for e in range(N_EXPERTS):
            count = meta_ref[LANES + e]
            first = meta_ref[2 * LANES + e] + count
            n_pad = (tile_rows - count % tile_rows) % tile_rows

            def pad_start(r, carry, first=first):
                pltpu.make_async_copy(zero_row, slot_rows(first + r), sem.at[1]).start()
                return carry

            def pad_wait(r, carry, first=first):
                pltpu.make_async_copy(zero_row, slot_rows(first + r), sem.at[1]).wait()
                return carry

            lax.fori_loop(0, n_pad, pad_start, 0)
            lax.fori_loop(0, n_pad, pad_wait, 0)

        def tile_copy(n):
            rows = tile_rows * rpt
            return pltpu.make_async_copy(zero_ref, xs_ref.at[pl.ds(pl.multiple_of(n * rows, rows), rows)],
                                         sem.at[1])

        def unused_start(n, carry):
            tile_copy(n).start()
            return carry

        def unused_wait(n, carry):
            tile_copy(n).wait()
            return carry

        lax.fori_loop(meta_ref[LANES - 1], n_tiles, unused_start, 0)
        lax.fori_loop(meta_ref[LANES - 1], n_tiles, unused_wait, 0)

    for _ in range(2):
        pltpu.make_async_copy(h_ref, xs_ref.at[pl.ds(0, chunk * rpt)], sem.at[0]).wait()


def _dispatch(pos_flat, meta_flat, h_lin, n_slots, rpt, tile_rows, chunk=512):
    t = pos_flat.shape[0] // 2
    return pl.pallas_call(
        functools.partial(_dispatch_kernel, t=t, rpt=rpt, chunk=chunk, tile_rows=tile_rows,
                          n_tiles=n_slots // tile_rows),
        grid_spec=pltpu.PrefetchScalarGridSpec(
            num_scalar_prefetch=2,
            grid=(t // chunk,),
            in_specs=[pl.BlockSpec((chunk * rpt, LANES), lambda c, p, m: (c, 0))],
            out_specs=pl.BlockSpec(memory_space=pl.ANY),
            scratch_shapes=[pltpu.VMEM((tile_rows * rpt, LANES), h_lin.dtype), pltpu.SemaphoreType.DMA((2,))],
        ),
        out_shape=jax.ShapeDtypeStruct((n_slots * rpt, LANES), h_lin.dtype),
        compiler_params=_params("arbitrary"),
    )(pos_flat, meta_flat, h_lin)


def _ffn_kernel(meta_ref, xs_ref, wg_ref, wu_ref, wd_ref, ys_ref, wgb_ref, wub_ref, wdb_ref, *, tm):
    n = pl.program_id(0)
    n_used = meta_ref[LANES - 1]

    @pl.when(n < n_used)
    def _():
        @pl.when((n == 0) | (meta_ref[n] != meta_ref[jnp.maximum(n - 1, 0)]))
        def _():
            wgb_ref[...] = wg_ref[...].astype(BF16)
            wub_ref[...] = wu_ref[...].astype(BF16)
            wdb_ref[...] = wd_ref[...].astype(BF16)

        d = wgb_ref.shape[0]
        a = b = None
        for k0 in range(0, d, MXU_DIM):
            x = jnp.concatenate(
                [_load_token_major(xs_ref, tm, k0 // LANES + j).astype(BF16) for j in range(MXU_DIM // LANES)],
                axis=1)
            pa = _dot(x, wgb_ref[k0:k0 + MXU_DIM, :])
            pb = _dot(x, wub_ref[k0:k0 + MXU_DIM, :])
            a = pa if a is None else a + pa
            b = pb if b is None else b + pb
        act = (_silu(a) * b).astype(BF16)
        for n0 in range(0, d, MXU_DIM):
            _store_token_major(ys_ref, _dot(act, wdb_ref[:, n0:n0 + MXU_DIM]), n0)

    @pl.when(n >= n_used)
    def _():
        ys_ref[...] = jnp.zeros_like(ys_ref)


def _ffn(meta, xs, w_gate, w_up, w_down, layer, tm, n_tiles):
    _, ne, d, f = w_gate.shape
    rpt = d // LANES

    def row_tile(n, m):
        return (jnp.minimum(n, m[LANES - 1] - 1), 0)

    return pl.pallas_call(
        functools.partial(_ffn_kernel, tm=tm),
        grid_spec=pltpu.PrefetchScalarGridSpec(
            num_scalar_prefetch=1,
            grid=(n_tiles,),
            in_specs=[
                pl.BlockSpec((tm * rpt, LANES), row_tile),
                pl.BlockSpec((None, None, d, f), lambda n, m: (layer, m[n], 0, 0)),
                pl.BlockSpec((None, None, d, f), lambda n, m: (layer, m[n], 0, 0)),
                pl.BlockSpec((None, None, f, d), lambda n, m: (layer, m[n], 0, 0)),
            ],
            out_specs=pl.BlockSpec((tm * rpt, LANES), lambda n, m: (n, 0)),
            scratch_shapes=[
                pltpu.VMEM((d, f), BF16),
                pltpu.VMEM((d, f), BF16),
                pltpu.VMEM((f, d), BF16),
            ],
        ),
        out_shape=jax.ShapeDtypeStruct((n_tiles * tm * rpt, LANES), F32),
        compiler_params=_params("arbitrary"),
    )(meta, xs, w_gate, w_up, w_down)


def _combine_kernel(pos_ref, ys_ref, *refs, t, rpt, pitch, tm, n_parts, final_tiles):
    x_refs = refs[:n_parts]
    if final_tiles is None:
        wt_ref, gate_ref, o_ref, a0, b0, a1, b1, sem = refs[n_parts:]
    else:
        wt_ref, gate_ref, fg_ref, op_ref, os_ref, a0, b0, a1, b1, sem = refs[n_parts:]
    i = pl.program_id(0)
    n = pl.num_programs(0)
    bufs = ((a0, b0), (a1, b1))

    def issue(tile, s):
        def body(r, carry):
            tok = tile * tm + r
            for k in range(2):
                slot_row = pos_ref[k * t + tok]
                pltpu.make_async_copy(
                    ys_ref.at[pl.ds(pl.multiple_of(slot_row * rpt, rpt), rpt)],
                    bufs[s][k].at[pl.ds(pl.multiple_of(r * pitch, 8), rpt)],
                    sem.at[s]).start(priority=k)
            return carry

        lax.fori_loop(0, tm, body, 0, unroll=8)

    def wait(s):
        for k in range(2):
            pltpu.make_async_copy(ys_ref.at[pl.ds(0, tm * rpt)], bufs[s][k].at[pl.ds(0, tm * rpt)],
                                  sem.at[s]).wait()

    def compute(s, out_ref):
        w = wt_ref[...]
        w0, w1 = w[:, 0:1], w[:, 1:2]
        dn = x_refs[0].shape[1]
        for c in range(rpt):
            cols = slice(c * LANES, (c + 1) * LANES)
            part, pc = divmod(c * LANES, dn)
            y = (w0 * bufs[s][0][pl.ds(c, tm, stride=pitch), :]
                 + w1 * bufs[s][1][pl.ds(c, tm, stride=pitch), :])
            out_ref[:, cols] = x_refs[part][:, pc:pc + LANES] + gate_ref[:, cols] * y
        if final_tiles is not None:
            out_ref[...] = _rms_f32(out_ref[...]) * fg_ref[...]

    @pl.when(i == 0)
    def _():
        issue(0, 0)

    for s in range(2):
        @pl.when(i % 2 == s)
        def _(s=s):
            @pl.when(i + 1 < n)
            def _():
                issue(i + 1, 1 - s)

            wait(s)
            if final_tiles is None:
                compute(s, o_ref)
            else:
                @pl.when(i < final_tiles)
                def _():
                    compute(s, op_ref)

                @pl.when(i >= final_tiles)
                def _():
                    compute(s, os_ref)


def _combine(pos_flat, ys, x1_parts, wt, mod3, layer, row_fn, final=None, tm=256):
    t, dn = x1_parts[0].shape
    n_parts = len(x1_parts)
    d = dn * n_parts
    rpt = d // LANES
    pitch = rpt + 8
    base = layer * 48
    in_specs = ([pl.BlockSpec(memory_space=pl.ANY)]
                + [pl.BlockSpec((tm, dn), lambda i, p: (i, 0))] * n_parts
                + [pl.BlockSpec((tm, 2), lambda i, p: (i, 0)),
                   pl.BlockSpec((None, 1, d), lambda i, p: (base + row_fn(i) * 6 + 5, 0, 0))])
    args = [pos_flat, ys, *x1_parts, wt, mod3]
    if final is None:
        final_tiles = None
        out_specs = pl.BlockSpec((tm, d), lambda i, p: (i, 0))
        out_shape = jax.ShapeDtypeStruct((t, d), F32)
    else:
        fg, t_ctx = final
        final_tiles = t_ctx // tm
        in_specs.append(pl.BlockSpec((1, d), lambda i, p: (0, 0)))
        args.append(fg.reshape(1, d))
        out_specs = [
            pl.BlockSpec((tm, d), lambda i, p: (jnp.minimum(i, final_tiles - 1), 0)),
            pl.BlockSpec((tm, d), lambda i, p: (jnp.maximum(i - final_tiles, 0), 0)),
        ]
        out_shape = [jax.ShapeDtypeStruct((t_ctx, d), F32), jax.ShapeDtypeStruct((t - t_ctx, d), F32)]
    return pl.pallas_call(
        functools.partial(_combine_kernel, t=t, rpt=rpt, pitch=pitch, tm=tm, n_parts=n_parts,
                          final_tiles=final_tiles),
        grid_spec=pltpu.PrefetchScalarGridSpec(
            num_scalar_prefetch=1,
            grid=(t // tm,),
            in_specs=in_specs,
            out_specs=out_specs,
            scratch_shapes=[pltpu.VMEM((tm * pitch, LANES), F32)] * 4 + [pltpu.SemaphoreType.DMA((2,))],
        ),
        out_shape=out_shape,
        compiler_params=_params("arbitrary"),
    )(*args)


def _moe(lg, router_b, h_lin, x1, w_gate, w_up, w_down, mod3, layer, row_fn, final=None, tile_rows=256):
    t = x1[0].shape[0]
    n_tiles = (2 * t) // tile_rows + N_EXPERTS
    pos, wts, meta = _route(lg, router_b, tile_rows)
    pos_flat = pos[:2].reshape(-1)
    xs = _dispatch(pos_flat, meta[:3].reshape(-1), h_lin, n_tiles * tile_rows, h_lin.shape[0] // t, tile_rows)
    ys = _ffn(meta[0], xs, w_gate, w_up, w_down, layer, tile_rows, n_tiles)
    return _combine(pos_flat, ys, x1, wts[:2].T, mod3, layer, row_fn, final)


def _rope_tables(n_tok):
    rows = n_tok // GRID_W
    pos_r = jnp.repeat(jnp.arange(rows), GRID_W).astype(F32)
    pos_c = jnp.tile(jnp.arange(GRID_W), rows).astype(F32)
    n_freq = DK_A // 4
    inv = jnp.power(ROPE_BASE, -jnp.arange(n_freq, dtype=F32) / n_freq)
    ang_r = pos_r[:, None] * inv
    ang_c = pos_c[:, None] * inv
    cos_t = jnp.concatenate([jnp.cos(ang_r), jnp.cos(ang_r), jnp.cos(ang_c), jnp.cos(ang_c)], axis=1)
    sin_t = jnp.concatenate([-jnp.sin(ang_r), jnp.sin(ang_r), -jnp.sin(ang_c), jnp.sin(ang_c)], axis=1)
    return cos_t, sin_t


def kernel(x_prompt, x_sample, cache_k, cache_v, state_ret, c, c_ctx, norm_g, w_mod, b_mod, w_in_even, w_out_even, lam_q, lam_k, ret_decay_logit, w_in_odd, gmlp_norm_g, w_spatial, b_spatial, w_out_odd, router_w, router_b, w_gate, w_up, w_down, final_norm_g):
    batch, seq, d = x_prompt.shape
    dbatch, dseq, _ = x_sample.shape
    past = cache_k.shape[2]
    depth = norm_g.shape[0]
    t_ctx, t_lat = batch * seq, dbatch * dseq
    t_all = t_ctx + t_lat
    xc = x_prompt.reshape(t_ctx, d)
    xl = x_sample.reshape(t_lat, d)

    cond8 = jnp.concatenate([c_ctx[None, :], c, jnp.zeros((8 - 1 - dbatch, d), F32)], axis=0)
    mod3 = _modulation(cond8, w_mod, b_mod).reshape(depth * 8 * 6, 1, d)
    router_wt = router_w.T
    cos_t, sin_t = _rope_tables(dseq)

    tm_in = 512
    tm_out = 512
    tm_cmb = 256
    w_in0 = w_in_even.reshape(d, -1)
    w_out0 = w_out_even.reshape(-1, d)
    w_in1 = w_in_odd.reshape(d, -1)
    w_out1 = w_out_odd.reshape(-1, d)
    lq, lk, dlog = lam_q.reshape(2, DK_A), lam_k.reshape(2, DK_A), ret_decay_logit.reshape(2, H_B)

    def rows_ctx(i):
        return 0

    def rows_lat(tiles_per_batch):
        return lambda i: 1 + i // tiles_per_batch

    def rows_all(tm):
        n_ctx, per = t_ctx // tm, dseq // tm
        return lambda i: jnp.maximum(i - n_ctx, 0) // per + jnp.where(i >= n_ctx, 1, 0)

    lam_init = 0.8 - 0.6 * math.exp(-0.3 * 0)
    half0 = w_in0.shape[1] // 2
    g00 = norm_g[0, 0]
    rl = rows_lat(dseq // tm_in)
    pa_c, k_new, v_new = _in_proj(xc, g00, mod3, 0, rows_ctx, w_in0, 0, half0, "even_kv", tm=tm_in)
    (pb_c,) = _in_proj(xc, g00, mod3, 0, rows_ctx, w_in0, half0, half0, "even", tm=tm_in)
    (pa_l,) = _in_proj(xl, g00, mod3, 0, rl, w_in0, 0, half0, "even", tm=tm_in)
    (pb_l,) = _in_proj(xl, g00, mod3, 0, rl, w_in0, half0, half0, "even", tm=tm_in)
    oa_c, or_c, s_new = _mixer(pa_c, pb_c, lq, lk, dlog, lam_init, batch, seq, H_A)
    ctx = (cache_k.reshape(dbatch, past, H_A * 2 * DK_A), cache_v.reshape(dbatch, past, H_A * DV_A),
           state_ret.reshape(dbatch, 2, H_B, DK_B, DV_B), cos_t, sin_t)
    oa_l, or_l = _mixer(pa_l, pb_l, lq, lk, dlog, lam_init, dbatch, dseq, 1, ctx)
    x1, h2, lg = _out_proj([([oa_c, or_c], xc), ([oa_l, or_l], xl)], w_out0, norm_g[0, 1], mod3, 0,
                           rows_all(tm_out), router_wt, tm=tm_out)
    x2 = _moe(lg, router_b, h2, x1, w_gate, w_up, w_down, mod3, 0, rows_all(tm_cmb))

    q1 = w_in1.shape[1] // 4
    g10 = norm_g[1, 0]
    ra = rows_all(tm_in)
    v_blk, v_ss = zip(*[_in_proj(x2, g10, mod3, 1, ra, w_in1, (2 + j) * q1, q1, "gmlp_v", tm=tm_in)
                        for j in range(2)])
    gmlp = (gmlp_norm_g.reshape(1, -1), w_spatial.reshape(H_C, CHUNK_C, CHUNK_C),
            b_spatial.reshape(H_C, CHUNK_C, 1))
    g0, g1 = [_in_proj(x2, g10, mod3, 1, ra, w_in1, j * q1, q1, "gmlp_u",
                       gating=(v_blk[j], v_ss[0], v_ss[1], *gmlp, j), tm=tm_in)[0] for j in range(2)]
    x3, h4, lg2 = _out_proj([([g0, g1], x2)], w_out1, norm_g[1, 1], mod3, 1, rows_all(tm_out), router_wt,
                            tm=tm_out)
    y_p, y_s = _moe(lg2, router_b, h4, x3, w_gate, w_up, w_down, mod3, 1, rows_all(tm_cmb),
                    final=(final_norm_g, t_ctx))

    y_prompt = y_p.reshape(batch, seq, d)
    y_sample = y_s.reshape(dbatch, dseq, d)
    new_cache_k = k_new.reshape(batch, 1, seq, H_A, 2, DK_A)
    new_cache_v = v_new.reshape(batch, 1, seq, DV_A // LANES, H_A, LANES).swapaxes(3, 4).reshape(
        batch, 1, seq, H_A, DV_A)
    new_state_ret = s_new.reshape(batch, 1, 2, H_B, DK_B, DV_B)
    return (y_prompt, y_sample, new_cache_k, new_cache_v, new_state_ret)
```

```python
import functools
import math

import jax
import jax.numpy as jnp
from jax import lax
from jax.experimental import pallas as pl
from jax.experimental.pallas import tpu as pltpu

F32 = jnp.float32
BF16 = jnp.bfloat16

EPS = 1e-6
ROPE_BASE = 10000.0
GRID_W = 64
H_A, DK_A, DV_A = 4, 128, 256
H_B, DK_B, DV_B = 4, 128, 256
RET_CHUNK = 128
H_C, CHUNK_C = 8, 128
N_EXPERTS, N_GROUPS, GROUP_SIZE = 16, 4, 4

VMEM_LIMIT_BYTES = 56 * 1024 * 1024


def _params(*sem):
    return pltpu.CompilerParams(dimension_semantics=sem, vmem_limit_bytes=VMEM_LIMIT_BYTES)


def _rms_f32(x):
    return x * lax.rsqrt(jnp.mean(x * x, axis=-1, keepdims=True) + EPS)


def _silu(x):
    return x * jax.nn.sigmoid(x)


def _dot(a, b):
    return jnp.dot(a, b, preferred_element_type=F32)


def _dot_nt(a, b):
    return lax.dot_general(a, b, (((1,), (1,)), ((), ())), preferred_element_type=F32)


def _dot_nt_x3(a, b):
    ah = a.astype(BF16)
    al = (a - ah.astype(F32)).astype(BF16)
    bh = b.astype(BF16)
    bl = (b - bh.astype(F32)).astype(BF16)
    return _dot_nt(ah, bh) + (_dot_nt(al, bh) + _dot_nt(ah, bl))


LANES = 128
MXU_DIM = 256


def _store_token_major(ref, val, col0=0):
    tm, width = val.shape
    rpt = ref.shape[0] // tm
    for c in range(width // LANES):
        ref[pl.ds(col0 // LANES + c, tm, stride=rpt), :] = val[:, c * LANES:(c + 1) * LANES]


def _load_token_major(ref, tm, c):
    rpt = ref.shape[0] // tm
    return ref[pl.ds(c, tm, stride=rpt), :]


def _mod_kernel(c_ref, w_ref, b_ref, o_ref):
    s = _silu(c_ref[...]).astype(BF16)
    o_ref[...] = _dot(s, w_ref[...].astype(BF16)) + b_ref[...]


def _modulation(cond8, w_mod, b_mod, tn=1024):
    depth, d, n = w_mod.shape
    return pl.pallas_call(
        _mod_kernel,
        grid=(depth, n // tn),
        in_specs=[
            pl.BlockSpec((8, d), lambda l, j: (0, 0)),
            pl.BlockSpec((None, d, tn), lambda l, j: (l, 0, j)),
            pl.BlockSpec((None, 1, tn), lambda l, j: (l, 0, j)),
        ],
        out_specs=pl.BlockSpec((None, 8, tn), lambda l, j: (l, 0, j)),
        out_shape=jax.ShapeDtypeStruct((depth, 8, n), F32),
        compiler_params=_params("arbitrary", "arbitrary"),
    )(cond8, w_mod, b_mod.reshape(depth, 1, n))


def _load_weight_slab(w_hbm, slab_ref, stage_ref, sem, col0):
    kdim, n = slab_ref.shape
    rows = stage_ref.shape[1]
    nb = kdim // rows

    def copy(j):
        return pltpu.make_async_copy(w_hbm.at[pl.ds(j * rows, rows), pl.ds(col0, n)],
                                     stage_ref.at[j % 2], sem.at[j % 2])

    copy(0).start()
    for j in range(nb):
        if j + 1 < nb:
            copy(j + 1).start()
        copy(j).wait()
        slab_ref[j * rows:(j + 1) * rows, :] = stage_ref[j % 2].astype(BF16)


def _gelu(x):
    return 0.5 * x * (1.0 + lax.erf(x * (2.0 ** -0.5)))


def _in_kernel(x_ref, g_ref, sh_ref, sc_ref, w_hbm, *rest, col0, mode, kv_cols, chunk, e_c):
    if mode == "even_kv":
        p_ref, k_ref, v_ref, slab_ref, stage_ref, h_ref, sem = rest
    elif mode == "gmlp_v":
        p_ref, ss_ref, slab_ref, stage_ref, h_ref, sem = rest
    elif mode == "gmlp_u":
        vb_ref, sa_ref, sb_ref, ng_ref, ws_ref, bs_ref, p_ref, slab_ref, stage_ref, h_ref, vn_ref, sem = rest
    else:
        p_ref, slab_ref, stage_ref, h_ref, sem = rest

    @pl.when(pl.program_id(0) == 0)
    def _():
        _load_weight_slab(w_hbm, slab_ref, stage_ref, sem, col0)

    h = (_rms_f32(x_ref[...]) * g_ref[...]) * (1.0 + sc_ref[...]) + sh_ref[...]
    h_ref[...] = h.astype(BF16)
    n = slab_ref.shape[1]
    tm = h_ref.shape[0]
    if mode == "gmlp_u":
        inv = lax.rsqrt((sa_ref[...] + sb_ref[...]) / e_c + EPS)
        vn_ref[...] = ((vb_ref[...].astype(F32) * inv) * ng_ref[...]).astype(BF16)
    ss = None
    for ci, c0 in enumerate(range(0, n, chunk)):
        acc = _dot(h_ref[...], slab_ref[:, c0:c0 + chunk])
        if mode == "gmlp_v":
            v = _gelu(acc)
            p_ref[:, c0:c0 + chunk] = v.astype(BF16)
            part = jnp.sum(v * v, axis=-1, keepdims=True)
            ss = part if ss is None else ss + part
        elif mode == "gmlp_u":
            u = _gelu(acc)
            wsb = ws_ref[ci].astype(BF16)
            for r0 in range(0, tm, CHUNK_C):
                mixed = _dot(wsb, vn_ref[r0:r0 + CHUNK_C, c0:c0 + chunk]) + bs_ref[ci]
                p_ref[r0:r0 + CHUNK_C, c0:c0 + chunk] = (u[r0:r0 + CHUNK_C] * mixed).astype(BF16)
        else:
            p_ref[:, c0:c0 + chunk] = acc.astype(BF16)
        if mode == "even_kv":
            (k0, k1), (v0, v1) = kv_cols
            a0 = col0 + c0
            if k0 <= a0 < k1:
                _store_token_major(k_ref, acc, a0 - k0)
            if v0 <= a0 < v1:
                tm = acc.shape[0]
                rpt = v_ref.shape[0] // tm
                for cc in range(chunk // LANES):
                    head, half = divmod((a0 - v0) // LANES + cc, DV_A // LANES)
                    v_ref[pl.ds(half * H_A + head, tm, stride=rpt), :] = acc[:, cc * LANES:(cc + 1) * LANES]
    if mode == "gmlp_v":
        ss_ref[...] = ss


def _in_proj(x, g, mod3, layer, row_fn, w, col0, n_cols, mode, gating=None, tm=512, chunk=512, stage_rows=256):
    t, d = x.shape
    base = layer * 48

    def mod_spec(kind):
        return pl.BlockSpec((None, 1, d), lambda i: (base + row_fn(i) * 6 + kind, 0, 0))

    in_specs = [
        pl.BlockSpec((tm, d), lambda i: (i, 0)),
        pl.BlockSpec((1, d), lambda i: (0, 0)),
        mod_spec(0),
        mod_spec(1),
        pl.BlockSpec(memory_space=pl.ANY),
    ]
    args = [x, g.reshape(1, d), mod3, mod3, w]
    out_specs = [pl.BlockSpec((tm, n_cols), lambda i: (i, 0))]
    out_shape = [jax.ShapeDtypeStruct((t, n_cols), BF16)]
    scratch = [
        pltpu.VMEM((d, n_cols), BF16),
        pltpu.VMEM((2, stage_rows, n_cols), F32),
        pltpu.VMEM((tm, d), BF16),
    ]
    kv_cols = None
    e_c = None
    if mode == "even_kv":
        a_qk = H_A * 2 * DK_A
        a_v = H_A * DV_A
        kv_cols = ((a_qk, 2 * a_qk), (2 * a_qk, 2 * a_qk + a_v))
        for width in (a_qk, a_v):
            out_specs.append(pl.BlockSpec((tm * width // LANES, LANES), lambda i: (i, 0)))
            out_shape.append(jax.ShapeDtypeStruct((t * width // LANES, LANES), F32))
    elif mode == "gmlp_v":
        out_specs.append(pl.BlockSpec((tm, 1), lambda i: (i, 0)))
        out_shape.append(jax.ShapeDtypeStruct((t, 1), F32))
    elif mode == "gmlp_u":
        v_blk, ss_a, ss_b, ng, w_s, b_s, blk = gating
        e_c = ng.shape[1]
        groups = w_s.shape[0] * n_cols // e_c
        chunk = n_cols // groups
        in_specs += [
            pl.BlockSpec((tm, n_cols), lambda i: (i, 0)),
            pl.BlockSpec((tm, 1), lambda i: (i, 0)),
            pl.BlockSpec((tm, 1), lambda i: (i, 0)),
            pl.BlockSpec((1, n_cols), lambda i: (0, blk)),
            pl.BlockSpec((groups, CHUNK_C, CHUNK_C), lambda i: (blk, 0, 0)),
            pl.BlockSpec((groups, CHUNK_C, 1), lambda i: (blk, 0, 0)),
        ]
        args += [v_blk, ss_a, ss_b, ng, w_s, b_s]
        scratch.append(pltpu.VMEM((tm, n_cols), BF16))
    return pl.pallas_call(
        functools.partial(_in_kernel, col0=col0, mode=mode, kv_cols=kv_cols, chunk=chunk, e_c=e_c),
        grid=(t // tm,),
        in_specs=in_specs,
        out_specs=out_specs,
        out_shape=out_shape,
        scratch_shapes=scratch + [pltpu.SemaphoreType.DMA((2,))],
        compiler_params=_params("arbitrary"),
    )(*args)


def _rope(x, c, s):
    lane = lax.broadcasted_iota(jnp.int32, x.shape, 1)
    partner = jnp.where((lane & 32) == 0, pltpu.roll(x, 96, 1), pltpu.roll(x, 32, 1))
    return x * c + partner * s


def _mixer_kernel(*refs, seq, has_ctx, lam_init, past, hps):
    if has_ctx:
        (aq_ref, ak_ref, av_ref, bq_ref, bk_ref, bv_ref, bg_ref, lq_ref, lk_ref, dl_ref,
         ck_ref, cv_ref, s0_ref, cos_ref, sin_ref,
         oa_ref, or_ref, kall_ref, vall_ref, of_ref, ob_ref, stf_ref, stb_ref) = refs
    else:
        (aq_ref, ak_ref, av_ref, bq_ref, bk_ref, bv_ref, bg_ref, lq_ref, lk_ref, dl_ref,
         oa_ref, or_ref, sout_ref, of_ref, ob_ref, stf_ref, stb_ref) = refs
    rt = 256
    straight = seq <= rt
    tq = 256 if straight else 512
    cs = RET_CHUNK
    nc = seq // cs

    prod = lq_ref[...] * lk_ref[...]
    ee = jnp.exp(jnp.sum(prod, axis=1, keepdims=True))
    lam = ee[0:1, :] - ee[1:2, :] + lam_init
    scale = DK_A ** -0.5
    ii = lax.broadcasted_iota(jnp.int32, (cs, cs), 0)
    jj = lax.broadcasted_iota(jnp.int32, (cs, cs), 1)
    ic = lax.broadcasted_iota(jnp.int32, (cs, 1), 0).astype(F32)

    def loop(n, body):
        if straight:
            for it in range(n):
                body(it)
        else:
            lax.fori_loop(0, n, lambda it, c: (body(it), c)[1], 0)

    def rows(it, size):
        return pl.ds(it * size, size) if straight else pl.ds(pl.multiple_of(it * size, size), size)

    for hh in range(hps):
        qk_cols = slice(hh * 2 * DK_A, (hh + 1) * 2 * DK_A)
        va_cols = slice(hh * DV_A, (hh + 1) * DV_A)
        kb_cols = slice(hh * DK_B, (hh + 1) * DK_B)
        vb_cols = slice(hh * DV_B, (hh + 1) * DV_B)

        if has_ctx:
            for c in range(2):
                kall_ref[c, 0:past, :] = ck_ref[:, c * DK_A:(c + 1) * DK_A].astype(BF16)
            vall_ref[0:past, :] = cv_ref[...].astype(BF16)

            def copy_body(r):
                rr = rows(r, rt)
                cs_, sn_ = cos_ref[rr, :], sin_ref[rr, :]
                kk = ak_ref[rr, qk_cols].astype(F32)
                for c in range(2):
                    kall_ref[c, pl.ds(past + pl.multiple_of(r * rt, rt), rt), :] = _rope(
                        kk[:, c * DK_A:(c + 1) * DK_A], cs_, sn_).astype(BF16)
                vall_ref[pl.ds(past + pl.multiple_of(r * rt, rt), rt), :] = av_ref[rr, va_cols]

            loop(seq // rt, copy_body)

        def attn_body(qt, qk_cols=qk_cols, va_cols=va_cols):
            rr = rows(qt, tq)
            q = aq_ref[rr, qk_cols]
            if has_ctx:
                qf = q.astype(F32)
                cs_, sn_ = cos_ref[rr, :], sin_ref[rr, :]
            probs = []
            for c in range(2):
                if has_ctx:
                    qc = _rope(qf[:, c * DK_A:(c + 1) * DK_A], cs_, sn_).astype(BF16)
                    keys = kall_ref[c]
                else:
                    qc = q[:, c * DK_A:(c + 1) * DK_A]
                    keys = ak_ref[:, hh * 2 * DK_A + c * DK_A:hh * 2 * DK_A + (c + 1) * DK_A]
                s = _dot_nt(qc, keys) * scale
                e = jnp.exp(s - jnp.max(s, axis=-1, keepdims=True))
                probs.append(e * (1.0 / jnp.sum(e, axis=-1, keepdims=True)))
            a = (probs[0] - lam * probs[1]).astype(BF16)
            o = _dot(a, vall_ref[...] if has_ctx else av_ref[:, va_cols])
            oa_ref[rr, va_cols] = (_rms_f32(o) * (1.0 - lam_init)).astype(BF16)

        tables = []
        for d in range(2):
            lg = jax.nn.log_sigmoid(dl_ref[d, hh])
            diff = (ii - jj if d == 0 else jj - ii).astype(F32)
            mask = jnp.where(diff >= 0, jnp.exp(jnp.maximum(diff, 0.0) * lg), 0.0)
            if d == 0:
                qdec = jnp.exp((ic + 1.0) * lg)
                kdec = jnp.exp((cs - 1.0 - ic) * lg)
            else:
                qdec = jnp.exp((cs - ic) * lg)
                kdec = jnp.exp(ic * lg)
            tables.append((mask, qdec, kdec, jnp.exp(cs * lg)))
        st_refs = (stf_ref, stb_ref)
        o_refs = (of_ref, ob_ref)
        for d in range(2):
            if has_ctx:
                st_refs[d][hh] = s0_ref[d]
            else:
                st_refs[d][hh] = jnp.zeros((DK_B, DV_B), F32)

        def chunk_body(n, hh=hh, kb_cols=kb_cols, vb_cols=vb_cols, tables=tables):
            for d in range(2):
                mask, qdec, kdec, cdec = tables[d]
                rr = rows(n if d == 0 else nc - 1 - n, cs)
                qc = bq_ref[rr, kb_cols].astype(F32)
                kc = bk_ref[rr, kb_cols].astype(F32) * (DK_B ** -0.5)
                vc = bv_ref[rr, vb_cols]
                state = st_refs[d][hh]
                qk = _dot_nt(qc.astype(BF16), kc.astype(BF16)) * mask
                o_refs[d][rr, vb_cols] = (_dot(qk.astype(BF16), vc)
                                          + _dot((qc * qdec).astype(BF16), state.astype(BF16)))
                kd = (kc * kdec).T.astype(BF16)
                st_refs[d][hh] = cdec * state + _dot(kd, vc)

        n_q = seq // tq
        per_q = nc // n_q

        def fused_body(qt, attn_body=attn_body, chunk_body=chunk_body):
            attn_body(qt)
            for j in range(per_q):
                chunk_body(qt * per_q + j)

        loop(n_q, fused_body)

        def final_body(r, vb_cols=vb_cols):
            rr = rows(r, rt)
            tot = of_ref[rr, vb_cols] + ob_ref[rr, vb_cols]
            gate = _silu(bg_ref[rr, vb_cols].astype(F32))
            or_ref[rr, vb_cols] = (_rms_f32(tot) * gate).astype(BF16)

        loop(seq // rt, final_body)
        if not has_ctx:
            for d in range(2):
                sout_ref[d, hh] = st_refs[d][hh]


def _mixer(pa, pb, lam_q, lam_k, decay_logit, lam_init, batch, seq, hps, ctx=None):
    has_ctx = ctx is not None
    past = ctx[0].shape[1] if has_ctx else 0

    def col(head_width, off):
        width = hps * head_width
        return pl.BlockSpec((seq, width), lambda b, hg: (b, off // width + hg))

    a_qk = H_A * 2 * DK_A
    b_qk = H_B * DK_B
    b_v = H_B * DV_B
    in_specs = [
        col(2 * DK_A, 0), col(2 * DK_A, a_qk), col(DV_A, 2 * a_qk),
        col(DK_B, 0), col(DK_B, b_qk), col(DV_B, 2 * b_qk), col(DV_B, 2 * b_qk + b_v),
        pl.BlockSpec((2, DK_A), lambda b, hg: (0, 0)),
        pl.BlockSpec((2, DK_A), lambda b, hg: (0, 0)),
        pl.BlockSpec((2, hps, 1, 1), lambda b, hg: (0, hg, 0, 0)),
    ]
    args = [pa] * 3 + [pb] * 4 + [lam_q, lam_k, decay_logit.reshape(2, H_B, 1, 1)]
    out_specs = [
        pl.BlockSpec((seq, hps * DV_A), lambda b, hg: (b, hg)),
        pl.BlockSpec((seq, hps * DV_B), lambda b, hg: (b, hg)),
    ]
    out_shape = [
        jax.ShapeDtypeStruct((batch * seq, H_A * DV_A), BF16),
        jax.ShapeDtypeStruct((batch * seq, b_v), BF16),
    ]
    scratch = []
    if has_ctx:
        assert hps == 1
        ck, cv, s0, cos_t, sin_t = ctx
        in_specs += [
            pl.BlockSpec((None, past, 2 * DK_A), lambda b, hg: (b, 0, hg)),
            pl.BlockSpec((None, past, DV_A), lambda b, hg: (b, 0, hg)),
            pl.BlockSpec((None, 2, None, DK_B, DV_B), lambda b, hg: (b, 0, hg, 0, 0)),
            pl.BlockSpec((seq, DK_A), lambda b, hg: (0, 0)),
            pl.BlockSpec((seq, DK_A), lambda b, hg: (0, 0)),
        ]
        args += [ck, cv, s0, cos_t, sin_t]
        scratch += [pltpu.VMEM((2, seq + past, DK_A), BF16), pltpu.VMEM((seq + past, DV_A), BF16)]
    else:
        out_specs.append(pl.BlockSpec((None, 2, hps, DK_B, DV_B), lambda b, hg: (b, 0, hg, 0, 0)))
        out_shape.append(jax.ShapeDtypeStruct((batch, 2, H_B, DK_B, DV_B), F32))
    scratch += [
        pltpu.VMEM((seq, hps * DV_B), F32),
        pltpu.VMEM((seq, hps * DV_B), F32),
        pltpu.VMEM((hps, DK_B, DV_B), F32),
        pltpu.VMEM((hps, DK_B, DV_B), F32),
    ]
    return pl.pallas_call(
        functools.partial(_mixer_kernel, seq=seq, has_ctx=has_ctx, lam_init=lam_init, past=past, hps=hps),
        grid=(batch, H_A // hps),
        in_specs=in_specs,
        out_specs=out_specs,
        out_shape=out_shape,
        scratch_shapes=scratch,
        compiler_params=_params("arbitrary", "arbitrary"),
    )(*args)


def _out_kernel(*refs, seg_tiles, n_lhs, part, n_parts):
    n_seg = len(seg_tiles)
    per_seg = n_lhs + 1
    seg_refs = [refs[s * per_seg:(s + 1) * per_seg] for s in range(n_seg)]
    rest = refs[n_seg * per_seg:]
    last = part == n_parts - 1
    if last:
        gate_ref, g_ref, sh_ref, sc_ref, rw_ref = rest[:5]
        prev_refs = rest[5:5 + part]
        w_hbm, x1_ref, h2_ref, lg_ref, slab_ref, stage_ref, sem = rest[5 + part:]
    else:
        gate_ref, w_hbm, x1_ref, slab_ref, stage_ref, sem = rest
    i = pl.program_id(0)
    dn = slab_ref.shape[1]

    @pl.when(i == 0)
    def _():
        _load_weight_slab(w_hbm, slab_ref, stage_ref, sem, part * dn)

    def run(lhs_refs, x_ref):
        acc = None
        r0 = 0
        for lhs_ref in lhs_refs:
            kk = lhs_ref.shape[1]
            prod = _dot(lhs_ref[...], slab_ref[r0:r0 + kk, :])
            acc = prod if acc is None else acc + prod
            r0 += kk
        x1 = x_ref[...] + gate_ref[...] * acc
        x1_ref[...] = x1
        if last:
            cols = [p_ref[...] for p_ref in prev_refs] + [x1]
            d = dn * n_parts
            ms = sum(jnp.sum(v * v, axis=-1, keepdims=True) for v in cols) / d
            inv = lax.rsqrt(ms + EPS)
            lg = None
            for j, v in enumerate(cols):
                cs_ = slice(j * dn, (j + 1) * dn)
                h2 = ((v * inv) * g_ref[:, cs_]) * (1.0 + sc_ref[:, cs_]) + sh_ref[:, cs_]
                _store_token_major(h2_ref, h2, j * dn)
                pl_ = _dot_nt_x3(rw_ref[:, cs_], h2)
                lg = pl_ if lg is None else lg + pl_
            lg_ref[...] = lg

    if n_seg == 1:
        run(seg_refs[0][:n_lhs], seg_refs[0][n_lhs])
    else:
        lo = 0
        for s in range(n_seg):
            @pl.when((i >= lo) & (i < lo + seg_tiles[s]))
            def _(s=s):
                run(seg_refs[s][:n_lhs], seg_refs[s][n_lhs])

            lo += seg_tiles[s]


def _out_proj(segments, w, g, mod3, layer, row_fn, router_wt, n_parts=2, tm=512, stage_rows=256):
    n_lhs = len(segments[0][0])
    d = segments[0][1].shape[1]
    dn = d // n_parts
    kdim = w.shape[0]
    base = layer * 48
    ne = router_wt.shape[0]
    seg_tiles = tuple(x.shape[0] // tm for _, x in segments)
    t_total = sum(x.shape[0] for _, x in segments)
    rpt = d // LANES
    x1_parts = []
    for part in range(n_parts):
        last = part == n_parts - 1
        in_specs, args = [], []
        lo = 0
        for (lhs_list, x), nt in zip(segments, seg_tiles):
            def seg_row(i, lo=lo, nt=nt):
                return jnp.clip(i - lo, 0, nt - 1)

            for lhs in lhs_list:
                in_specs.append(pl.BlockSpec((tm, lhs.shape[1]), lambda i, f=seg_row: (f(i), 0)))
                args.append(lhs)
            in_specs.append(pl.BlockSpec((tm, dn), lambda i, f=seg_row: (f(i), part)))
            args.append(x)
            lo += nt
        in_specs.append(pl.BlockSpec((None, 1, dn), lambda i: (base + row_fn(i) * 6 + 2, 0, part)))
        args.append(mod3)
        out_specs = [pl.BlockSpec((tm, dn), lambda i: (i, 0))]
        out_shape = [jax.ShapeDtypeStruct((t_total, dn), F32)]
        if last:
            in_specs += [
                pl.BlockSpec((1, d), lambda i: (0, 0)),
                pl.BlockSpec((None, 1, d), lambda i: (base + row_fn(i) * 6 + 3, 0, 0)),
                pl.BlockSpec((None, 1, d), lambda i: (base + row_fn(i) * 6 + 4, 0, 0)),
                pl.BlockSpec((ne, d), lambda i: (0, 0)),
            ] + [pl.BlockSpec((tm, dn), lambda i: (i, 0))] * part
            args += [g.reshape(1, d), mod3, mod3, router_wt] + x1_parts
            out_specs += [
                pl.BlockSpec((tm * rpt, LANES), lambda i: (i, 0)),
                pl.BlockSpec((ne, tm), lambda i: (0, i)),
            ]
            out_shape += [
                jax.ShapeDtypeStruct((t_total * rpt, LANES), F32),
                jax.ShapeDtypeStruct((ne, t_total), F32),
            ]
        in_specs.append(pl.BlockSpec(memory_space=pl.ANY))
        args.append(w)
        outs = pl.pallas_call(
            functools.partial(_out_kernel, seg_tiles=seg_tiles, n_lhs=n_lhs, part=part, n_parts=n_parts),
            grid=(t_total // tm,),
            in_specs=in_specs,
            out_specs=out_specs,
            out_shape=out_shape,
            scratch_shapes=[
                pltpu.VMEM((kdim, dn), BF16),
                pltpu.VMEM((2, stage_rows, dn), F32),
                pltpu.SemaphoreType.DMA((2,)),
            ],
            compiler_params=_params("arbitrary"),
        )(*args)
        x1_parts.append(outs[0])
    return x1_parts, outs[1], outs[2]


def _first_index(vals, target):
    idx = jnp.full(target.shape, len(vals) - 1, jnp.int32)
    for k in range(len(vals) - 2, -1, -1):
        idx = jnp.where(vals[k] == target, k, idx)
    return idx


def _pick(vals, idx):
    out = vals[-1]
    for k in range(len(vals) - 2, -1, -1):
        out = jnp.where(idx == k, vals[k], out)
    return out


def _route_kernel(lg_ref, rb_ref, pos_ref, wt_ref, meta_ref, rank_ref, *, tile_rows, scan):
    logits = lg_ref[...]
    t = logits.shape[1]
    scores = jax.nn.sigmoid(logits)
    biased = scores + rb_ref[...]
    s_rows = [scores[e:e + 1, :] for e in range(N_EXPERTS)]
    b_rows = [biased[e:e + 1, :] for e in range(N_EXPERTS)]
    grp = []
    for g in range(N_GROUPS):
        r = b_rows[g * GROUP_SIZE:(g + 1) * GROUP_SIZE]
        best = None
        for a in range(GROUP_SIZE):
            for b in range(a + 1, GROUP_SIZE):
                pair = r[a] + r[b]
                best = pair if best is None else jnp.maximum(best, pair)
        grp.append(best)
    gmax = functools.reduce(jnp.maximum, grp)
    gsel = _first_index(grp, gmax)
    cand_b = [_pick([b_rows[g * GROUP_SIZE + k] for g in range(N_GROUPS)], gsel) for k in range(GROUP_SIZE)]
    cand_s = [_pick([s_rows[g * GROUP_SIZE + k] for g in range(N_GROUPS)], gsel) for k in range(GROUP_SIZE)]
    m1 = functools.reduce(jnp.maximum, cand_b)
    k1 = _first_index(cand_b, m1)
    rest = [jnp.where(k1 == k, -jnp.inf, cand_b[k]) for k in range(GROUP_SIZE)]
    m2 = functools.reduce(jnp.maximum, rest)
    k2 = _first_index(rest, m2)
    w1 = _pick(cand_s, k1)
    w2 = _pick(cand_s, k2)
    wsum = w1 + w2
    w1 = w1 / wsum
    w2 = w2 / wsum
    e1 = gsel * GROUP_SIZE + k1
    e2 = gsel * GROUP_SIZE + k2
    wt_ref[...] = jnp.zeros_like(wt_ref)
    wt_ref[0:1, :] = w1
    wt_ref[1:2, :] = w2

    eid = lax.broadcasted_iota(jnp.int32, logits.shape, 0)
    sel1 = eid == e1
    sel2 = eid == e2
    ind = jnp.where(sel1 | sel2, 1.0, 0.0)
    ua = lax.broadcasted_iota(jnp.int32, (scan, scan), 0)
    ub = lax.broadcasted_iota(jnp.int32, (scan, scan), 1)
    upper = jnp.where(ua <= ub, 1.0, 0.0).astype(BF16)
    carry = jnp.zeros((N_EXPERTS, 1), F32)
    for blk in range(t // scan):
        seg = ind[:, blk * scan:(blk + 1) * scan]
        incl = _dot(seg.astype(BF16), upper)
        rank_ref[:, blk * scan:(blk + 1) * scan] = incl - seg + carry
        carry = carry + incl[:, scan - 1:scan]
    n_tiles = jnp.floor((carry + (tile_rows - 1.0)) * (1.0 / tile_rows))
    la = lax.broadcasted_iota(jnp.int32, (N_EXPERTS, N_EXPERTS), 0)
    lb = lax.broadcasted_iota(jnp.int32, (N_EXPERTS, N_EXPERTS), 1)
    lower = jnp.where(lb < la, 1.0, 0.0).astype(BF16)
    start = _dot(lower, jnp.broadcast_to(n_tiles, (N_EXPERTS, LANES)).astype(BF16))
    end = start + n_tiles
    slot = start[:, 0:1] * tile_rows + rank_ref[...]
    pos_ref[...] = jnp.zeros_like(pos_ref)
    pos_ref[0:1, :] = jnp.sum(jnp.where(sel1, slot, 0.0), axis=0, keepdims=True).astype(jnp.int32)
    pos_ref[1:2, :] = jnp.sum(jnp.where(sel2, slot, 0.0), axis=0, keepdims=True).astype(jnp.int32)

    n_used = end[N_EXPERTS - 1:N_EXPERTS, :]
    tile = lax.broadcasted_iota(jnp.int32, (N_EXPERTS, LANES), 1).astype(F32)
    tile = jnp.minimum(tile, n_used - 1.0)
    tile_expert = jnp.sum(jnp.where(tile >= end, 1.0, 0.0), axis=0, keepdims=True)
    lane = lax.broadcasted_iota(jnp.int32, (1, LANES), 1)
    meta_ref[...] = jnp.zeros_like(meta_ref)
    meta_ref[0:1, :] = jnp.where(lane == LANES - 1, n_used, tile_expert).astype(jnp.int32)
    diag = lax.broadcasted_iota(jnp.int32, (N_EXPERTS, LANES), 0) == lax.broadcasted_iota(
        jnp.int32, (N_EXPERTS, LANES), 1)
    meta_ref[1:2, :] = jnp.sum(jnp.where(diag, carry, 0.0), axis=0, keepdims=True).astype(jnp.int32)
    meta_ref[2:3, :] = jnp.sum(jnp.where(diag, start * tile_rows, 0.0), axis=0, keepdims=True).astype(jnp.int32)


def _route(logits_t, router_b, tile_rows, scan=512):
    ne, t = logits_t.shape
    return pl.pallas_call(
        functools.partial(_route_kernel, tile_rows=tile_rows, scan=scan),
        out_shape=[
            jax.ShapeDtypeStruct((8, t), jnp.int32),
            jax.ShapeDtypeStruct((8, t), F32),
            jax.ShapeDtypeStruct((8, LANES), jnp.int32),
        ],
        scratch_shapes=[pltpu.VMEM((ne, t), F32)],
        compiler_params=pltpu.CompilerParams(vmem_limit_bytes=VMEM_LIMIT_BYTES),
    )(logits_t, router_b.reshape(ne, 1))


def _dispatch_kernel(pos_ref, meta_ref, h_ref, xs_ref, zero_ref, sem, *, t, rpt, chunk, tile_rows, n_tiles):
    c = pl.program_id(0)

    def slot_rows(slot):
        return xs_ref.at[pl.ds(pl.multiple_of(slot * rpt, rpt), rpt)]

    def body(r, carry):
        tok = c * chunk + r
        src = h_ref.at[pl.ds(pl.multiple_of(r * rpt, rpt), rpt)]
        for k in range(2):
            pltpu.make_async_copy(src, slot_rows(pos_ref[k * t + tok]), sem.at[0]).start(priority=k)
        return carry

    lax.fori_loop(0, chunk, body, 0, unroll=8)

    @pl.when(c == 0)
    def _():
        zero_ref[...] = jnp.zeros_like(zero_ref)
        zero_row = zero_ref.at[pl.ds(0, rpt)]
        for e in range(N_EXPERTS):
            count = meta_ref[LANES + e]
            first = meta_ref[2 * LANES + e] + count
            n_pad = (tile_rows - count % tile_rows) % tile_rows

            def pad_start(r, carry, first=first):
                pltpu.make_async_copy(zero_row, slot_rows(first + r), sem.at[1]).start()
                return carry

            def pad_wait(r, carry, first=first):
                pltpu.make_async_copy(zero_row, slot_rows(first + r), sem.at[1]).wait()
                return carry

            lax.fori_loop(0, n_pad, pad_start, 0)
            lax.fori_loop(0, n_pad, pad_wait, 0)

        def tile_copy(n):
            rows = tile_rows * rpt
            return pltpu.make_async_copy(zero_ref, xs_ref.at[pl.ds(pl.multiple_of(n * rows, rows), rows)],
                                         sem.at[1])

        def unused_start(n, carry):
            tile_copy(n).start()
            return carry

        def unused_wait(n, carry):
            tile_copy(n).wait()
            return carry

        lax.fori_loop(meta_ref[LANES - 1], n_tiles, unused_start, 0)
        lax.fori_loop(meta_ref[LANES - 1], n_tiles, unused_wait, 0)

    for _ in range(2):
        pltpu.make_async_copy(h_ref, xs_ref.at[pl.ds(0, chunk * rpt)], sem.at[0]).wait()


def _dispatch(pos_flat, meta_flat, h_lin, n_slots, rpt, tile_rows, chunk=512):
    t = pos_flat.shape[0] // 2
    return pl.pallas_call(
        functools.partial(_dispatch_kernel, t=t, rpt=rpt, chunk=chunk, tile_rows=tile_rows,
                          n_tiles=n_slots // tile_rows),
        grid_spec=pltpu.PrefetchScalarGridSpec(
            num_scalar_prefetch=2,
            grid=(t // chunk,),
            in_specs=[pl.BlockSpec((chunk * rpt, LANES), lambda c, p, m: (c, 0))],
            out_specs=pl.BlockSpec(memory_space=pl.ANY),
            scratch_shapes=[pltpu.VMEM((tile_rows * rpt, LANES), h_lin.dtype), pltpu.SemaphoreType.DMA((2,))],
        ),
        out_shape=jax.ShapeDtypeStruct((n_slots * rpt, LANES), h_lin.dtype),
        compiler_params=_params("arbitrary"),
    )(pos_flat, meta_flat, h_lin)


def _ffn_kernel(meta_ref, xs_ref, wg_ref, wu_ref, wd_ref, ys_ref, wgb_ref, wub_ref, wdb_ref, *, tm):
    n = pl.program_id(0)
    n_used = meta_ref[LANES - 1]

    @pl.when(n < n_used)
    def _():
        @pl.when((n == 0) | (meta_ref[n] != meta_ref[jnp.maximum(n - 1, 0)]))
        def _():
            wgb_ref[...] = wg_ref[...].astype(BF16)
            wub_ref[...] = wu_ref[...].astype(BF16)
            wdb_ref[...] = wd_ref[...].astype(BF16)

        d = wgb_ref.shape[0]
        a = b = None
        for k0 in range(0, d, MXU_DIM):
            x = jnp.concatenate(
                [_load_token_major(xs_ref, tm, k0 // LANES + j).astype(BF16) for j in range(MXU_DIM // LANES)],
                axis=1)
            pa = _dot(x, wgb_ref[k0:k0 + MXU_DIM, :])
            pb = _dot(x, wub_ref[k0:k0 + MXU_DIM, :])
            a = pa if a is None else a + pa
            b = pb if b is None else b + pb
        act = (_silu(a) * b).astype(BF16)
        for n0 in range(0, d, MXU_DIM):
            _store_token_major(ys_ref, _dot(act, wdb_ref[:, n0:n0 + MXU_DIM]), n0)

    @pl.when(n >= n_used)
    def _():
        ys_ref[...] = jnp.zeros_like(ys_ref)


def _ffn(meta, xs, w_gate, w_up, w_down, layer, tm, n_tiles):
    _, ne, d, f = w_gate.shape
    rpt = d // LANES

    def row_tile(n, m):
        return (jnp.minimum(n, m[LANES - 1] - 1), 0)

    return pl.pallas_call(
        functools.partial(_ffn_kernel, tm=tm),
        grid_spec=pltpu.PrefetchScalarGridSpec(
            num_scalar_prefetch=1,
            grid=(n_tiles,),
            in_specs=[
                pl.BlockSpec((tm * rpt, LANES), row_tile),
                pl.BlockSpec((None, None, d, f), lambda n, m: (layer, m[n], 0, 0)),
                pl.BlockSpec((None, None, d, f), lambda n, m: (layer, m[n], 0, 0)),
                pl.BlockSpec((None, None, f, d), lambda n, m: (layer, m[n], 0, 0)),
            ],
            out_specs=pl.BlockSpec((tm * rpt, LANES), lambda n, m: (n, 0)),
            scratch_shapes=[
                pltpu.VMEM((d, f), BF16),
                pltpu.VMEM((d, f), BF16),
                pltpu.VMEM((f, d), BF16),
            ],
        ),
        out_shape=jax.ShapeDtypeStruct((n_tiles * tm * rpt, LANES), F32),
        compiler_params=_params("arbitrary"),
    )(meta, xs, w_gate, w_up, w_down)


def _combine_kernel(pos_ref, ys_ref, *refs, t, rpt, pitch, tm, n_parts, final_tiles):
    x_refs = refs[:n_parts]
    if final_tiles is None:
        wt_ref, gate_ref, o_ref, a0, b0, a1, b1, sem = refs[n_parts:]
    else:
        wt_ref, gate_ref, fg_ref, op_ref, os_ref, a0, b0, a1, b1, sem = refs[n_parts:]
    i = pl.program_id(0)
    n = pl.num_programs(0)
    bufs = ((a0, b0), (a1, b1))

    def issue(tile, s):
        def body(r, carry):
            tok = tile * tm + r
            for k in range(2):
                slot_row = pos_ref[k * t + tok]
                pltpu.make_async_copy(
                    ys_ref.at[pl.ds(pl.multiple_of(slot_row * rpt, rpt), rpt)],
                    bufs[s][k].at[pl.ds(pl.multiple_of(r * pitch, 8), rpt)],
                    sem.at[s]).start(priority=k)
            return carry

        lax.fori_loop(0, tm, body, 0, unroll=8)

    def wait(s):
        for k in range(2):
            pltpu.make_async_copy(ys_ref.at[pl.ds(0, tm * rpt)], bufs[s][k].at[pl.ds(0, tm * rpt)],
                                  sem.at[s]).wait()

    def compute(s, out_ref):
        w = wt_ref[...]
        w0, w1 = w[:, 0:1], w[:, 1:2]
        dn = x_refs[0].shape[1]
        for c in range(rpt):
            cols = slice(c * LANES, (c + 1) * LANES)
            part, pc = divmod(c * LANES, dn)
            y = (w0 * bufs[s][0][pl.ds(c, tm, stride=pitch), :]
                 + w1 * bufs[s][1][pl.ds(c, tm, stride=pitch), :])
            out_ref[:, cols] = x_refs[part][:, pc:pc + LANES] + gate_ref[:, cols] * y
        if final_tiles is not None:
            out_ref[...] = _rms_f32(out_ref[...]) * fg_ref[...]

    @pl.when(i == 0)
    def _():
        issue(0, 0)

    for s in range(2):
        @pl.when(i % 2 == s)
        def _(s=s):
            @pl.when(i + 1 < n)
            def _():
                issue(i + 1, 1 - s)

            wait(s)
            if final_tiles is None:
                compute(s, o_ref)
            else:
                @pl.when(i < final_tiles)
                def _():
                    compute(s, op_ref)

                @pl.when(i >= final_tiles)
                def _():
                    compute(s, os_ref)


def _combine(pos_flat, ys, x1_parts, wt, mod3, layer, row_fn, final=None, tm=256):
    t, dn = x1_parts[0].shape
    n_parts = len(x1_parts)
    d = dn * n_parts
    rpt = d // LANES
    pitch = rpt + 8
    base = layer * 48
    in_specs = ([pl.BlockSpec(memory_space=pl.ANY)]
                + [pl.BlockSpec((tm, dn), lambda i, p: (i, 0))] * n_parts
                + [pl.BlockSpec((tm, 2), lambda i, p: (i, 0)),
                   pl.BlockSpec((None, 1, d), lambda i, p: (base + row_fn(i) * 6 + 5, 0, 0))])
    args = [pos_flat, ys, *x1_parts, wt, mod3]
    if final is None:
        final_tiles = None
        out_specs = pl.BlockSpec((tm, d), lambda i, p: (i, 0))
        out_shape = jax.ShapeDtypeStruct((t, d), F32)
    else:
        fg, t_ctx = final
        final_tiles = t_ctx // tm
        in_specs.append(pl.BlockSpec((1, d), lambda i, p: (0, 0)))
        args.append(fg.reshape(1, d))
        out_specs = [
            pl.BlockSpec((tm, d), lambda i, p: (jnp.minimum(i, final_tiles - 1), 0)),
            pl.BlockSpec((tm, d), lambda i, p: (jnp.maximum(i - final_tiles, 0), 0)),
        ]
        out_shape = [jax.ShapeDtypeStruct((t_ctx, d), F32), jax.ShapeDtypeStruct((t - t_ctx, d), F32)]
    return pl.pallas_call(
        functools.partial(_combine_kernel, t=t, rpt=rpt, pitch=pitch, tm=tm, n_parts=n_parts,
                          final_tiles=final_tiles),
        grid_spec=pltpu.PrefetchScalarGridSpec(
            num_scalar_prefetch=1,
            grid=(t // tm,),
            in_specs=in_specs,
            out_specs=out_specs,
            scratch_shapes=[pltpu.VMEM((tm * pitch, LANES), F32)] * 4 + [pltpu.SemaphoreType.DMA((2,))],
        ),
        out_shape=out_shape,
        compiler_params=_params("arbitrary"),
    )(*args)


def _moe(lg, router_b, h_lin, x1, w_gate, w_up, w_down, mod3, layer, row_fn, final=None, tile_rows=256):
    t = x1[0].shape[0]
    n_tiles = (2 * t) // tile_rows + N_EXPERTS
    pos, wts, meta = _route(lg, router_b, tile_rows)
    pos_flat = pos[:2].reshape(-1)
    xs = _dispatch(pos_flat, meta[:3].reshape(-1), h_lin, n_tiles * tile_rows, h_lin.shape[0] // t, tile_rows)
    ys = _ffn(meta[0], xs, w_gate, w_up, w_down, layer, tile_rows, n_tiles)
    return _combine(pos_flat, ys, x1, wts[:2].T, mod3, layer, row_fn, final)


def _rope_tables(n_tok):
    rows = n_tok // GRID_W
    pos_r = jnp.repeat(jnp.arange(rows), GRID_W).astype(F32)
    pos_c = jnp.tile(jnp.arange(GRID_W), rows).astype(F32)
    n_freq = DK_A // 4
    inv = jnp.power(ROPE_BASE, -jnp.arange(n_freq, dtype=F32) / n_freq)
    ang_r = pos_r[:, None] * inv
    ang_c = pos_c[:, None] * inv
    cos_t = jnp.concatenate([jnp.cos(ang_r), jnp.cos(ang_r), jnp.cos(ang_c), jnp.cos(ang_c)], axis=1)
    sin_t = jnp.concatenate([-jnp.sin(ang_r), jnp.sin(ang_r), -jnp.sin(ang_c), jnp.sin(ang_c)], axis=1)
    return cos_t, sin_t


def kernel(x_prompt, x_sample, cache_k, cache_v, state_ret, c, c_ctx, norm_g, w_mod, b_mod, w_in_even, w_out_even, lam_q, lam_k, ret_decay_logit, w_in_odd, gmlp_norm_g, w_spatial, b_spatial, w_out_odd, router_w, router_b, w_gate, w_up, w_down, final_norm_g):
    batch, seq, d = x_prompt.shape
    dbatch, dseq, _ = x_sample.shape
    past = cache_k.shape[2]
    depth = norm_g.shape[0]
    t_ctx, t_lat = batch * seq, dbatch * dseq
    t_all = t_ctx + t_lat
    xc = x_prompt.reshape(t_ctx, d)
    xl = x_sample.reshape(t_lat, d)

    cond8 = jnp.concatenate([c_ctx[None, :], c, jnp.zeros((8 - 1 - dbatch, d), F32)], axis=0)
    mod3 = _modulation(cond8, w_mod, b_mod).reshape(depth * 8 * 6, 1, d)
    router_wt = router_w.T
    cos_t, sin_t = _rope_tables(dseq)

    tm_in = 512
    tm_out = 512
    tm_cmb = 256
    w_in0 = w_in_even.reshape(d, -1)
    w_out0 = w_out_even.reshape(-1, d)
    w_in1 = w_in_odd.reshape(d, -1)
    w_out1 = w_out_odd.reshape(-1, d)
    lq, lk, dlog = lam_q.reshape(2, DK_A), lam_k.reshape(2, DK_A), ret_decay_logit.reshape(2, H_B)

    def rows_ctx(i):
        return 0

    def rows_lat(tiles_per_batch):
        return lambda i: 1 + i // tiles_per_batch

    def rows_all(tm):
        n_ctx, per = t_ctx // tm, dseq // tm
        return lambda i: jnp.maximum(i - n_ctx, 0) // per + jnp.where(i >= n_ctx, 1, 0)

    lam_init = 0.8 - 0.6 * math.exp(-0.3 * 0)
    half0 = w_in0.shape[1] // 2
    g00 = norm_g[0, 0]
    rl = rows_lat(dseq // tm_in)
    pa_c, k_new, v_new = _in_proj(xc, g00, mod3, 0, rows_ctx, w_in0, 0, half0, "even_kv", tm=tm_in)
    (pb_c,) = _in_proj(xc, g00, mod3, 0, rows_ctx, w_in0, half0, half0, "even", tm=tm_in)
    (pa_l,) = _in_proj(xl, g00, mod3, 0, rl, w_in0, 0, half0, "even", tm=tm_in)
    (pb_l,) = _in_proj(xl, g00, mod3, 0, rl, w_in0, half0, half0, "even", tm=tm_in)
    oa_c, or_c, s_new = _mixer(pa_c, pb_c, lq, lk, dlog, lam_init, batch, seq, H_A)
    ctx = (cache_k.reshape(dbatch, past, H_A * 2 * DK_A), cache_v.reshape(dbatch, past, H_A * DV_A),
           state_ret.reshape(dbatch, 2, H_B, DK_B, DV_B), cos_t, sin_t)
    oa_l, or_l = _mixer(pa_l, pb_l, lq, lk, dlog, lam_init, dbatch, dseq, 1, ctx)
    x1, h2, lg = _out_proj([([oa_c, or_c], xc), ([oa_l, or_l], xl)], w_out0, norm_g[0, 1], mod3, 0,
                           rows_all(tm_out), router_wt, tm=tm_out)
    x2 = _moe(lg, router_b, h2, x1, w_gate, w_up, w_down, mod3, 0, rows_all(tm_cmb))

    q1 = w_in1.shape[1] // 4
    g10 = norm_g[1, 0]
    ra = rows_all(tm_in)
    v_blk, v_ss = zip(*[_in_proj(x2, g10, mod3, 1, ra, w_in1, (2 + j) * q1, q1, "gmlp_v", tm=tm_in)
                        for j in range(2)])
    gmlp = (gmlp_norm_g.reshape(1, -1), w_spatial.reshape(H_C, CHUNK_C, CHUNK_C),
            b_spatial.reshape(H_C, CHUNK_C, 1))
    g0, g1 = [_in_proj(x2, g10, mod3, 1, ra, w_in1, j * q1, q1, "gmlp_u",
                       gating=(v_blk[j], v_ss[0], v_ss[1], *gmlp, j), tm=tm_in)[0] for j in range(2)]
    x3, h4, lg2 = _out_proj([([g0, g1], x2)], w_out1, norm_g[1, 1], mod3, 1, rows_all(tm_out), router_wt,
                            tm=tm_out)
    y_p, y_s = _moe(lg2, router_b, h4, x3, w_gate, w_up, w_down, mod3, 1, rows_all(tm_cmb),
                    final=(final_norm_g, t_ctx))

    y_prompt = y_p.reshape(batch, seq, d)
    y_sample = y_s.reshape(dbatch, dseq, d)
    new_cache_k = k_new.reshape(batch, 1, seq, H_A, 2, DK_A)
    new_cache_v = v_new.reshape(batch, 1, seq, DV_A // LANES, H_A, LANES).swapaxes(3, 4).reshape(
        batch, 1, seq, H_A, DV_A)
    new_state_ret = s_new.reshape(batch, 1, 2, H_B, DK_B, DV_B)
    return (y_prompt, y_sample, new_cache_k, new_cache_v, new_state_ret)
```

```python
import functools
import math

import jax
import jax.numpy as jnp
from jax import lax
from jax.experimental import pallas as pl
from jax.experimental.pallas import tpu as pltpu

F32 = jnp.float32
BF16 = jnp.bfloat16

EPS = 1e-6
ROPE_BASE = 10000.0
GRID_W = 64
H_A, DK_A, DV_A = 4, 128, 256
H_B, DK_B, DV_B = 4, 128, 256
RET_CHUNK = 128
H_C, CHUNK_C = 8, 128
N_EXPERTS, N_GROUPS, GROUP_SIZE = 16, 4, 4

VMEM_LIMIT_BYTES = 56 * 1024 * 1024


def _params(*sem):
    return pltpu.CompilerParams(dimension_semantics=sem, vmem_limit_bytes=VMEM_LIMIT_BYTES)


def _rms_f32(x):
    return x * lax.rsqrt(jnp.mean(x * x, axis=-1, keepdims=True) + EPS)


def _silu(x):
    return x * jax.nn.sigmoid(x)


def _dot(a, b):
    return jnp.dot(a, b, preferred_element_type=F32)


def _dot_nt(a, b):
    return lax.dot_general(a, b, (((1,), (1,)), ((), ())), preferred_element_type=F32)


def _dot_nt_x3(a, b):
    ah = a.astype(BF16)
    al = (a - ah.astype(F32)).astype(BF16)
    bh = b.astype(BF16)
    bl = (b - bh.astype(F32)).astype(BF16)
    return _dot_nt(ah, bh) + (_dot_nt(al, bh) + _dot_nt(ah, bl))


LANES = 128
MXU_DIM = 256


def _store_token_major(ref, val, col0=0):
    tm, width = val.shape
    rpt = ref.shape[0] // tm
    for c in range(width // LANES):
        ref[pl.ds(col0 // LANES + c, tm, stride=rpt), :] = val[:, c * LANES:(c + 1) * LANES]


def _load_token_major(ref, tm, c):
    rpt = ref.shape[0] // tm
    return ref[pl.ds(c, tm, stride=rpt), :]


def _mod_kernel(c_ref, w_ref, b_ref, o_ref):
    s = _silu(c_ref[...]).astype(BF16)
    o_ref[...] = _dot(s, w_ref[...].astype(BF16)) + b_ref[...]


def _modulation(cond8, w_mod, b_mod, tn=1024):
    depth, d, n = w_mod.shape
    return pl.pallas_call(
        _mod_kernel,
        grid=(depth, n // tn),
        in_specs=[
            pl.BlockSpec((8, d), lambda l, j: (0, 0)),
            pl.BlockSpec((None, d, tn), lambda l, j: (l, 0, j)),
            pl.BlockSpec((None, 1, tn), lambda l, j: (l, 0, j)),
        ],
        out_specs=pl.BlockSpec((None, 8, tn), lambda l, j: (l, 0, j)),
        out_shape=jax.ShapeDtypeStruct((depth, 8, n), F32),
        compiler_params=_params("arbitrary", "arbitrary"),
    )(cond8, w_mod, b_mod.reshape(depth, 1, n))


def _load_weight_slab(w_hbm, slab_ref, stage_ref, sem, col0):
    kdim, n = slab_ref.shape
    rows = stage_ref.shape[1]
    nb = kdim // rows

    def copy(j):
        return pltpu.make_async_copy(w_hbm.at[pl.ds(j * rows, rows), pl.ds(col0, n)],
                                     stage_ref.at[j % 2], sem.at[j % 2])

    copy(0).start()
    for j in range(nb):
        if j + 1 < nb:
            copy(j + 1).start()
        copy(j).wait()
        slab_ref[j * rows:(j + 1) * rows, :] = stage_ref[j % 2].astype(BF16)


def _gelu(x):
    return 0.5 * x * (1.0 + lax.erf(x * (2.0 ** -0.5)))


def _in_kernel(x_ref, g_ref, sh_ref, sc_ref, w_hbm, *rest, col0, mode, kv_cols, chunk, e_c):
    if mode == "even_kv":
        p_ref, k_ref, v_ref, slab_ref, stage_ref, h_ref, sem = rest
    elif mode == "gmlp_v":
        p_ref, ss_ref, slab_ref, stage_ref, h_ref, sem = rest
    elif mode == "gmlp_u":
        vb_ref, sa_ref, sb_ref, ng_ref, ws_ref, bs_ref, p_ref, slab_ref, stage_ref, h_ref, vn_ref, sem = rest
    else:
        p_ref, slab_ref, stage_ref, h_ref, sem = rest

    @pl.when(pl.program_id(0) == 0)
    def _():
        _load_weight_slab(w_hbm, slab_ref, stage_ref, sem, col0)

    h = (_rms_f32(x_ref[...]) * g_ref[...]) * (1.0 + sc_ref[...]) + sh_ref[...]
    h_ref[...] = h.astype(BF16)
    n = slab_ref.shape[1]
    tm = h_ref.shape[0]
    if mode == "gmlp_u":
        inv = lax.rsqrt((sa_ref[...] + sb_ref[...]) / e_c + EPS)
        vn_ref[...] = ((vb_ref[...].astype(F32) * inv) * ng_ref[...]).astype(BF16)
    ss = None
    for ci, c0 in enumerate(range(0, n, chunk)):
        acc = _dot(h_ref[...], slab_ref[:, c0:c0 + chunk])
        if mode == "gmlp_v":
            v = _gelu(acc)
            p_ref[:, c0:c0 + chunk] = v.astype(BF16)
            part = jnp.sum(v * v, axis=-1, keepdims=True)
            ss = part if ss is None else ss + part
        elif mode == "gmlp_u":
            u = _gelu(acc)
            wsb = ws_ref[ci].astype(BF16)
            for r0 in range(0, tm, CHUNK_C):
                mixed = _dot(wsb, vn_ref[r0:r0 + CHUNK_C, c0:c0 + chunk]) + bs_ref[ci]
                p_ref[r0:r0 + CHUNK_C, c0:c0 + chunk] = (u[r0:r0 + CHUNK_C] * mixed).astype(BF16)
        else:
            p_ref[:, c0:c0 + chunk] = acc.astype(BF16)
        if mode == "even_kv":
            (k0, k1), (v0, v1) = kv_cols
            a0 = col0 + c0
            if k0 <= a0 < k1:
                _store_token_major(k_ref, acc, a0 - k0)
            if v0 <= a0 < v1:
                tm = acc.shape[0]
                rpt = v_ref.shape[0] // tm
                for cc in range(chunk // LANES):
                    head, half = divmod((a0 - v0) // LANES + cc, DV_A // LANES)
                    v_ref[pl.ds(half * H_A + head, tm, stride=rpt), :] = acc[:, cc * LANES:(cc + 1) * LANES]
    if mode == "gmlp_v":
        ss_ref[...] = ss


def _in_proj(x, g, mod3, layer, row_fn, w, col0, n_cols, mode, gating=None, tm=512, chunk=512, stage_rows=256):
    t, d = x.shape
    base = layer * 48

    def mod_spec(kind):
        return pl.BlockSpec((None, 1, d), lambda i: (base + row_fn(i) * 6 + kind, 0, 0))

    in_specs = [
        pl.BlockSpec((tm, d), lambda i: (i, 0)),
        pl.BlockSpec((1, d), lambda i: (0, 0)),
        mod_spec(0),
        mod_spec(1),
        pl.BlockSpec(memory_space=pl.ANY),
    ]
    args = [x, g.reshape(1, d), mod3, mod3, w]
    out_specs = [pl.BlockSpec((tm, n_cols), lambda i: (i, 0))]
    out_shape = [jax.ShapeDtypeStruct((t, n_cols), BF16)]
    scratch = [
        pltpu.VMEM((d, n_cols), BF16),
        pltpu.VMEM((2, stage_rows, n_cols), F32),
        pltpu.VMEM((tm, d), BF16),
    ]
    kv_cols = None
    e_c = None
    if mode == "even_kv":
        a_qk = H_A * 2 * DK_A
        a_v = H_A * DV_A
        kv_cols = ((a_qk, 2 * a_qk), (2 * a_qk, 2 * a_qk + a_v))
        for width in (a_qk, a_v):
            out_specs.append(pl.BlockSpec((tm * width // LANES, LANES), lambda i: (i, 0)))
            out_shape.append(jax.ShapeDtypeStruct((t * width // LANES, LANES), F32))
    elif mode == "gmlp_v":
        out_specs.append(pl.BlockSpec((tm, 1), lambda i: (i, 0)))
        out_shape.append(jax.ShapeDtypeStruct((t, 1), F32))
    elif mode == "gmlp_u":
        v_blk, ss_a, ss_b, ng, w_s, b_s, blk = gating
        e_c = ng.shape[1]
        groups = w_s.shape[0] * n_cols // e_c
        chunk = n_cols // groups
        in_specs += [
            pl.BlockSpec((tm, n_cols), lambda i: (i, 0)),
            pl.BlockSpec((tm, 1), lambda i: (i, 0)),
            pl.BlockSpec((tm, 1), lambda i: (i, 0)),
            pl.BlockSpec((1, n_cols), lambda i: (0, blk)),
            pl.BlockSpec((groups, CHUNK_C, CHUNK_C), lambda i: (blk, 0, 0)),
            pl.BlockSpec((groups, CHUNK_C, 1), lambda i: (blk, 0, 0)),
        ]
        args += [v_blk, ss_a, ss_b, ng, w_s, b_s]
        scratch.append(pltpu.VMEM((tm, n_cols), BF16))
    return pl.pallas_call(
        functools.partial(_in_kernel, col0=col0, mode=mode, kv_cols=kv_cols, chunk=chunk, e_c=e_c),
        grid=(t // tm,),
        in_specs=in_specs,
        out_specs=out_specs,
        out_shape=out_shape,
        scratch_shapes=scratch + [pltpu.SemaphoreType.DMA((2,))],
        compiler_params=_params("arbitrary"),
    )(*args)


def _rope(x, c, s):
    lane = lax.broadcasted_iota(jnp.int32, x.shape, 1)
    partner = jnp.where((lane & 32) == 0, pltpu.roll(x, 96, 1), pltpu.roll(x, 32, 1))
    return x * c + partner * s


def _mixer_kernel(*refs, seq, has_ctx, lam_init, past, hps):
    if has_ctx:
        (aq_ref, ak_ref, av_ref, bq_ref, bk_ref, bv_ref, bg_ref, lq_ref, lk_ref, dl_ref,
         ck_ref, cv_ref, s0_ref, cos_ref, sin_ref,
         oa_ref, or_ref, kall_ref, vall_ref, of_ref, ob_ref, stf_ref, stb_ref) = refs
    else:
        (aq_ref, ak_ref, av_ref, bq_ref, bk_ref, bv_ref, bg_ref, lq_ref, lk_ref, dl_ref,
         oa_ref, or_ref, sout_ref, of_ref, ob_ref, stf_ref, stb_ref) = refs
    rt = 256
    straight = seq <= rt
    tq = 256 if straight else 512
    cs = RET_CHUNK
    nc = seq // cs

    prod = lq_ref[...] * lk_ref[...]
    ee = jnp.exp(jnp.sum(prod, axis=1, keepdims=True))
    lam = ee[0:1, :] - ee[1:2, :] + lam_init
    scale = DK_A ** -0.5
    ii = lax.broadcasted_iota(jnp.int32, (cs, cs), 0)
    jj = lax.broadcasted_iota(jnp.int32, (cs, cs), 1)
    ic = lax.broadcasted_iota(jnp.int32, (cs, 1), 0).astype(F32)

    def loop(n, body):
        if straight:
            for it in range(n):
                body(it)
        else:
            lax.fori_loop(0, n, lambda it, c: (body(it), c)[1], 0)

    def rows(it, size):
        return pl.ds(it * size, size) if straight else pl.ds(pl.multiple_of(it * size, size), size)

    for hh in range(hps):
        qk_cols = slice(hh * 2 * DK_A, (hh + 1) * 2 * DK_A)
        va_cols = slice(hh * DV_A, (hh + 1) * DV_A)
        kb_cols = slice(hh * DK_B, (hh + 1) * DK_B)
        vb_cols = slice(hh * DV_B, (hh + 1) * DV_B)

        if has_ctx:
            for c in range(2):
                kall_ref[c, 0:past, :] = ck_ref[:, c * DK_A:(c + 1) * DK_A].astype(BF16)
            vall_ref[0:past, :] = cv_ref[...].astype(BF16)

            def copy_body(r):
                rr = rows(r, rt)
                cs_, sn_ = cos_ref[rr, :], sin_ref[rr, :]
                kk = ak_ref[rr, qk_cols].astype(F32)
                for c in range(2):
                    kall_ref[c, pl.ds(past + pl.multiple_of(r * rt, rt), rt), :] = _rope(
                        kk[:, c * DK_A:(c + 1) * DK_A], cs_, sn_).astype(BF16)
                vall_ref[pl.ds(past + pl.multiple_of(r * rt, rt), rt), :] = av_ref[rr, va_cols]

            loop(seq // rt, copy_body)

        def attn_body(qt, qk_cols=qk_cols, va_cols=va_cols):
            rr = rows(qt, tq)
            q = aq_ref[rr, qk_cols]
            if has_ctx:
                qf = q.astype(F32)
                cs_, sn_ = cos_ref[rr, :], sin_ref[rr, :]
            probs = []
            for c in range(2):
                if has_ctx:
                    qc = _rope(qf[:, c * DK_A:(c + 1) * DK_A], cs_, sn_).astype(BF16)
                    keys = kall_ref[c]
                else:
                    qc = q[:, c * DK_A:(c + 1) * DK_A]
                    keys = ak_ref[:, hh * 2 * DK_A + c * DK_A:hh * 2 * DK_A + (c + 1) * DK_A]
                s = _dot_nt(qc, keys) * scale
                e = jnp.exp(s - jnp.max(s, axis=-1, keepdims=True))
                probs.append(e * (1.0 / jnp.sum(e, axis=-1, keepdims=True)))
            a = (probs[0] - lam * probs[1]).astype(BF16)
            o = _dot(a, vall_ref[...] if has_ctx else av_ref[:, va_cols])
            oa_ref[rr, va_cols] = (_rms_f32(o) * (1.0 - lam_init)).astype(BF16)

        tables = []
        for d in range(2):
            lg = jax.nn.log_sigmoid(dl_ref[d, hh])
            diff = (ii - jj if d == 0 else jj - ii).astype(F32)
            mask = jnp.where(diff >= 0, jnp.exp(jnp.maximum(diff, 0.0) * lg), 0.0)
            if d == 0:
                qdec = jnp.exp((ic + 1.0) * lg)
                kdec = jnp.exp((cs - 1.0 - ic) * lg)
            else:
                qdec = jnp.exp((cs - ic) * lg)
                kdec = jnp.exp(ic * lg)
            tables.append((mask, qdec, kdec, jnp.exp(cs * lg)))
        st_refs = (stf_ref, stb_ref)
        o_refs = (of_ref, ob_ref)
        for d in range(2):
            if has_ctx:
                st_refs[d][hh] = s0_ref[d]
            else:
                st_refs[d][hh] = jnp.zeros((DK_B, DV_B), F32)

        def chunk_body(n, hh=hh, kb_cols=kb_cols, vb_cols=vb_cols, tables=tables):
            for d in range(2):
                mask, qdec, kdec, cdec = tables[d]
                rr = rows(n if d == 0 else nc - 1 - n, cs)
                qc = bq_ref[rr, kb_cols].astype(F32)
                kc = bk_ref[rr, kb_cols].astype(F32) * (DK_B ** -0.5)
                vc = bv_ref[rr, vb_cols]
                state = st_refs[d][hh]
                qk = _dot_nt(qc.astype(BF16), kc.astype(BF16)) * mask
                o_refs[d][rr, vb_cols] = (_dot(qk.astype(BF16), vc)
                                          + _dot((qc * qdec).astype(BF16), state.astype(BF16)))
                kd = (kc * kdec).T.astype(BF16)
                st_refs[d][hh] = cdec * state + _dot(kd, vc)

        n_q = seq // tq
        per_q = nc // n_q

        def fused_body(qt, attn_body=attn_body, chunk_body=chunk_body):
            attn_body(qt)
            for j in range(per_q):
                chunk_body(qt * per_q + j)

        loop(n_q, fused_body)

        def final_body(r, vb_cols=vb_cols):
            rr = rows(r, rt)
            tot = of_ref[rr, vb_cols] + ob_ref[rr, vb_cols]
            gate = _silu(bg_ref[rr, vb_cols].astype(F32))
            or_ref[rr, vb_cols] = (_rms_f32(tot) * gate).astype(BF16)

        loop(seq // rt, final_body)
        if not has_ctx:
            for d in range(2):
                sout_ref[d, hh] = st_refs[d][hh]


def _mixer(pa, pb, lam_q, lam_k, decay_logit, lam_init, batch, seq, hps, ctx=None):
    has_ctx = ctx is not None
    past = ctx[0].shape[1] if has_ctx else 0

    def col(head_width, off):
        width = hps * head_width
        return pl.BlockSpec((seq, width), lambda b, hg: (b, off // width + hg))

    a_qk = H_A * 2 * DK_A
    b_qk = H_B * DK_B
    b_v = H_B * DV_B
    in_specs = [
        col(2 * DK_A, 0), col(2 * DK_A, a_qk), col(DV_A, 2 * a_qk),
        col(DK_B, 0), col(DK_B, b_qk), col(DV_B, 2 * b_qk), col(DV_B, 2 * b_qk + b_v),
        pl.BlockSpec((2, DK_A), lambda b, hg: (0, 0)),
        pl.BlockSpec((2, DK_A), lambda b, hg: (0, 0)),
        pl.BlockSpec((2, hps, 1, 1), lambda b, hg: (0, hg, 0, 0)),
    ]
    args = [pa] * 3 + [pb] * 4 + [lam_q, lam_k, decay_logit.reshape(2, H_B, 1, 1)]
    out_specs = [
        pl.BlockSpec((seq, hps * DV_A), lambda b, hg: (b, hg)),
        pl.BlockSpec((seq, hps * DV_B), lambda b, hg: (b, hg)),
    ]
    out_shape = [
        jax.ShapeDtypeStruct((batch * seq, H_A * DV_A), BF16),
        jax.ShapeDtypeStruct((batch * seq, b_v), BF16),
    ]
    scratch = []
    if has_ctx:
        assert hps == 1
        ck, cv, s0, cos_t, sin_t = ctx
        in_specs += [
            pl.BlockSpec((None, past, 2 * DK_A), lambda b, hg: (b, 0, hg)),
            pl.BlockSpec((None, past, DV_A), lambda b, hg: (b, 0, hg)),
            pl.BlockSpec((None, 2, None, DK_B, DV_B), lambda b, hg: (b, 0, hg, 0, 0)),
            pl.BlockSpec((seq, DK_A), lambda b, hg: (0, 0)),
            pl.BlockSpec((seq, DK_A), lambda b, hg: (0, 0)),
        ]
        args += [ck, cv, s0, cos_t, sin_t]
        scratch += [pltpu.VMEM((2, seq + past, DK_A), BF16), pltpu.VMEM((seq + past, DV_A), BF16)]
    else:
        out_specs.append(pl.BlockSpec((None, 2, hps, DK_B, DV_B), lambda b, hg: (b, 0, hg, 0, 0)))
        out_shape.append(jax.ShapeDtypeStruct((batch, 2, H_B, DK_B, DV_B), F32))
    scratch += [
        pltpu.VMEM((seq, hps * DV_B), F32),
        pltpu.VMEM((seq, hps * DV_B), F32),
        pltpu.VMEM((hps, DK_B, DV_B), F32),
        pltpu.VMEM((hps, DK_B, DV_B), F32),
    ]
    return pl.pallas_call(
        functools.partial(_mixer_kernel, seq=seq, has_ctx=has_ctx, lam_init=lam_init, past=past, hps=hps),
        grid=(batch, H_A // hps),
        in_specs=in_specs,
        out_specs=out_specs,
        out_shape=out_shape,
        scratch_shapes=scratch,
        compiler_params=_params("arbitrary", "arbitrary"),
    )(*args)


def _out_kernel(*refs, seg_tiles, n_lhs, part, n_parts):
    n_seg = len(seg_tiles)
    per_seg = n_lhs + 1
    seg_refs = [refs[s * per_seg:(s + 1) * per_seg] for s in range(n_seg)]
    rest = refs[n_seg * per_seg:]
    last = part == n_parts - 1
    if last:
        gate_ref, g_ref, sh_ref, sc_ref, rw_ref = rest[:5]
        prev_refs = rest[5:5 + part]
        w_hbm, x1_ref, h2_ref, lg_ref, slab_ref, stage_ref, sem = rest[5 + part:]
    else:
        gate_ref, w_hbm, x1_ref, slab_ref, stage_ref, sem = rest
    i = pl.program_id(0)
    dn = slab_ref.shape[1]

    @pl.when(i == 0)
    def _():
        _load_weight_slab(w_hbm, slab_ref, stage_ref, sem, part * dn)

    def run(lhs_refs, x_ref):
        acc = None
        r0 = 0
        for lhs_ref in lhs_refs:
            kk = lhs_ref.shape[1]
            prod = _dot(lhs_ref[...], slab_ref[r0:r0 + kk, :])
            acc = prod if acc is None else acc + prod
            r0 += kk
        x1 = x_ref[...] + gate_ref[...] * acc
        x1_ref[...] = x1
        if last:
            cols = [p_ref[...] for p_ref in prev_refs] + [x1]
            d = dn * n_parts
            ms = sum(jnp.sum(v * v, axis=-1, keepdims=True) for v in cols) / d
            inv = lax.rsqrt(ms + EPS)
            lg = None
            for j, v in enumerate(cols):
                cs_ = slice(j * dn, (j + 1) * dn)
                h2 = ((v * inv) * g_ref[:, cs_]) * (1.0 + sc_ref[:, cs_]) + sh_ref[:, cs_]
                _store_token_major(h2_ref, h2, j * dn)
                pl_ = _dot_nt_x3(rw_ref[:, cs_], h2)
                lg = pl_ if lg is None else lg + pl_
            lg_ref[...] = lg

    if n_seg == 1:
        run(seg_refs[0][:n_lhs], seg_refs[0][n_lhs])
    else:
        lo = 0
        for s in range(n_seg):
            @pl.when((i >= lo) & (i < lo + seg_tiles[s]))
            def _(s=s):
                run(seg_refs[s][:n_lhs], seg_refs[s][n_lhs])

            lo += seg_tiles[s]


def _out_proj(segments, w, g, mod3, layer, row_fn, router_wt, n_parts=2, tm=512, stage_rows=256):
    n_lhs = len(segments[0][0])
    d = segments[0][1].shape[1]
    dn = d // n_parts
    kdim = w.shape[0]
    base = layer * 48
    ne = router_wt.shape[0]
    seg_tiles = tuple(x.shape[0] // tm for _, x in segments)
    t_total = sum(x.shape[0] for _, x in segments)
    rpt = d // LANES
    x1_parts = []
    for part in range(n_parts):
        last = part == n_parts - 1
        in_specs, args = [], []
        lo = 0
        for (lhs_list, x), nt in zip(segments, seg_tiles):
            def seg_row(i, lo=lo, nt=nt):
                return jnp.clip(i - lo, 0, nt - 1)

            for lhs in lhs_list:
                in_specs.append(pl.BlockSpec((tm, lhs.shape[1]), lambda i, f=seg_row: (f(i), 0)))
                args.append(lhs)
            in_specs.append(pl.BlockSpec((tm, dn), lambda i, f=seg_row: (f(i), part)))
            args.append(x)
            lo += nt
        in_specs.append(pl.BlockSpec((None, 1, dn), lambda i: (base + row_fn(i) * 6 + 2, 0, part)))
        args.append(mod3)
        out_specs = [pl.BlockSpec((tm, dn), lambda i: (i, 0))]
        out_shape = [jax.ShapeDtypeStruct((t_total, dn), F32)]
        if last:
            in_specs += [
                pl.BlockSpec((1, d), lambda i: (0, 0)),
                pl.BlockSpec((None, 1, d), lambda i: (base + row_fn(i) * 6 + 3, 0, 0)),
                pl.BlockSpec((None, 1, d), lambda i: (base + row_fn(i) * 6 + 4, 0, 0)),
                pl.BlockSpec((ne, d), lambda i: (0, 0)),
            ] + [pl.BlockSpec((tm, dn), lambda i: (i, 0))] * part
            args += [g.reshape(1, d), mod3, mod3, router_wt] + x1_parts
            out_specs += [
                pl.BlockSpec((tm * rpt, LANES), lambda i: (i, 0)),
                pl.BlockSpec((ne, tm), lambda i: (0, i)),
            ]
            out_shape += [
                jax.ShapeDtypeStruct((t_total * rpt, LANES), F32),
                jax.ShapeDtypeStruct((ne, t_total), F32),
            ]
        in_specs.append(pl.BlockSpec(memory_space=pl.ANY))
        args.append(w)
        outs = pl.pallas_call(
            functools.partial(_out_kernel, seg_tiles=seg_tiles, n_lhs=n_lhs, part=part, n_parts=n_parts),
            grid=(t_total // tm,),
            in_specs=in_specs,
            out_specs=out_specs,
            out_shape=out_shape,
            scratch_shapes=[
                pltpu.VMEM((kdim, dn), BF16),
                pltpu.VMEM((2, stage_rows, dn), F32),
                pltpu.SemaphoreType.DMA((2,)),
            ],
            compiler_params=_params("arbitrary"),
        )(*args)
        x1_parts.append(outs[0])
    return x1_parts, outs[1], outs[2]


def _first_index(vals, target):
    idx = jnp.full(target.shape, len(vals) - 1, jnp.int32)
    for k in range(len(vals) - 2, -1, -1):
        idx = jnp.where(vals[k] == target, k, idx)
    return idx


def _pick(vals, idx):
    out = vals[-1]
    for k in range(len(vals) - 2, -1, -1):
        out = jnp.where(idx == k, vals[k], out)
    return out


def _route_kernel(lg_ref, rb_ref, pos_ref, wt_ref, meta_ref, rank_ref, *, tile_rows, scan):
    logits = lg_ref[...]
    t = logits.shape[1]
    scores = jax.nn.sigmoid(logits)
    biased = scores + rb_ref[...]
    s_rows = [scores[e:e + 1, :] for e in range(N_EXPERTS)]
    b_rows = [biased[e:e + 1, :] for e in range(N_EXPERTS)]
    grp = []
    for g in range(N_GROUPS):
        r = b_rows[g * GROUP_SIZE:(g + 1) * GROUP_SIZE]
        best = None
        for a in range(GROUP_SIZE):
            for b in range(a + 1, GROUP_SIZE):
                pair = r[a] + r[b]
                best = pair if best is None else jnp.maximum(best, pair)
        grp.append(best)
    gmax = functools.reduce(jnp.maximum, grp)
    gsel = _first_index(grp, gmax)
    cand_b = [_pick([b_rows[g * GROUP_SIZE + k] for g in range(N_GROUPS)], gsel) for k in range(GROUP_SIZE)]
    cand_s = [_pick([s_rows[g * GROUP_SIZE + k] for g in range(N_GROUPS)], gsel) for k in range(GROUP_SIZE)]
    m1 = functools.reduce(jnp.maximum, cand_b)
    k1 = _first_index(cand_b, m1)
    rest = [jnp.where(k1 == k, -jnp.inf, cand_b[k]) for k in range(GROUP_SIZE)]
    m2 = functools.reduce(jnp.maximum, rest)
    k2 = _first_index(rest, m2)
    w1 = _pick(cand_s, k1)
    w2 = _pick(cand_s, k2)
    wsum = w1 + w2
    w1 = w1 / wsum
    w2 = w2 / wsum
    e1 = gsel * GROUP_SIZE + k1
    e2 = gsel * GROUP_SIZE + k2
    wt_ref[...] = jnp.zeros_like(wt_ref)
    wt_ref[0:1, :] = w1
    wt_ref[1:2, :] = w2

    eid = lax.broadcasted_iota(jnp.int32, logits.shape, 0)
    sel1 = eid == e1
    sel2 = eid == e2
    ind = jnp.where(sel1 | sel2, 1.0, 0.0)
    ua = lax.broadcasted_iota(jnp.int32, (scan, scan), 0)
    ub = lax.broadcasted_iota(jnp.int32, (scan, scan), 1)
    upper = jnp.where(ua <= ub, 1.0, 0.0).astype(BF16)
    carry = jnp.zeros((N_EXPERTS, 1), F32)
    for blk in range(t // scan):
        seg = ind[:, blk * scan:(blk + 1) * scan]
        incl = _dot(seg.astype(BF16), upper)
        rank_ref[:, blk * scan:(blk + 1) * scan] = incl - seg + carry
        carry = carry + incl[:, scan - 1:scan]
    n_tiles = jnp.floor((carry + (tile_rows - 1.0)) * (1.0 / tile_rows))
    la = lax.broadcasted_iota(jnp.int32, (N_EXPERTS, N_EXPERTS), 0)
    lb = lax.broadcasted_iota(jnp.int32, (N_EXPERTS, N_EXPERTS), 1)
    lower = jnp.where(lb < la, 1.0, 0.0).astype(BF16)
    start = _dot(lower, jnp.broadcast_to(n_tiles, (N_EXPERTS, LANES)).astype(BF16))
    end = start + n_tiles
    slot = start[:, 0:1] * tile_rows + rank_ref[...]
    pos_ref[...] = jnp.zeros_like(pos_ref)
    pos_ref[0:1, :] = jnp.sum(jnp.where(sel1, slot, 0.0), axis=0, keepdims=True).astype(jnp.int32)
    pos_ref[1:2, :] = jnp.sum(jnp.where(sel2, slot, 0.0), axis=0, keepdims=True).astype(jnp.int32)

    n_used = end[N_EXPERTS - 1:N_EXPERTS, :]
    tile = lax.broadcasted_iota(jnp.int32, (N_EXPERTS, LANES), 1).astype(F32)
    tile = jnp.minimum(tile, n_used - 1.0)
    tile_expert = jnp.sum(jnp.where(tile >= end, 1.0, 0.0), axis=0, keepdims=True)
    lane = lax.broadcasted_iota(jnp.int32, (1, LANES), 1)
    meta_ref[...] = jnp.zeros_like(meta_ref)
    meta_ref[0:1, :] = jnp.where(lane == LANES - 1, n_used, tile_expert).astype(jnp.int32)
    diag = lax.broadcasted_iota(jnp.int32, (N_EXPERTS, LANES), 0) == lax.broadcasted_iota(
        jnp.int32, (N_EXPERTS, LANES), 1)
    meta_ref[1:2, :] = jnp.sum(jnp.where(diag, carry, 0.0), axis=0, keepdims=True).astype(jnp.int32)
    meta_ref[2:3, :] = jnp.sum(jnp.where(diag, start * tile_rows, 0.0), axis=0, keepdims=True).astype(jnp.int32)


def _route(logits_t, router_b, tile_rows, scan=512):
    ne, t = logits_t.shape
    return pl.pallas_call(
        functools.partial(_route_kernel, tile_rows=tile_rows, scan=scan),
        out_shape=[
            jax.ShapeDtypeStruct((8, t), jnp.int32),
            jax.ShapeDtypeStruct((8, t), F32),
            jax.ShapeDtypeStruct((8, LANES), jnp.int32),
        ],
        scratch_shapes=[pltpu.VMEM((ne, t), F32)],
        compiler_params=pltpu.CompilerParams(vmem_limit_bytes=VMEM_LIMIT_BYTES),
    )(logits_t, router_b.reshape(ne, 1))


def _dispatch_kernel(pos_ref, meta_ref, h_ref, xs_ref, zero_ref, sem, *, t, rpt, chunk, tile_rows, n_tiles):
    c = pl.program_id(0)

    def slot_rows(slot):
        return xs_ref.at[pl.ds(pl.multiple_of(slot * rpt, rpt), rpt)]

    def body(r, carry):
        tok = c * chunk + r
        src = h_ref.at[pl.ds(pl.multiple_of(r * rpt, rpt), rpt)]
        for k in range(2):
            pltpu.make_async_copy(src, slot_rows(pos_ref[k * t + tok]), sem.at[0]).start(priority=k)
        return carry

    lax.fori_loop(0, chunk, body, 0, unroll=8)

    @pl.when(c == 0)
    def _():
        zero_ref[...] = jnp.zeros_like(zero_ref)
        zero_row = zero_ref.at[pl.ds(0, rpt)]
        for e in range(N_EXPERTS):
            count = meta_ref[LANES + e]
            first = meta_ref[2 * LANES + e] + count
            n_pad = (tile_rows - count % tile_rows) % tile_rows

            def pad_start(r, carry, first=first):
                pltpu.make_async_copy(zero_row, slot_rows(first + r), sem.at[1]).start()
                return carry

            def pad_wait(r, carry, first=first):
                pltpu.make_async_copy(zero_row, slot_rows(first + r), sem.at[1]).wait()
                return carry

            lax.fori_loop(0, n_pad, pad_start, 0)
            lax.fori_loop(0, n_pad, pad_wait, 0)

        def tile_copy(n):
            rows = tile_rows * rpt
            return pltpu.make_async_copy(zero_ref, xs_ref.at[pl.ds(pl.multiple_of(n * rows, rows), rows)],
                                         sem.at[1])

        def unused_start(n, carry):
            tile_copy(n).start()
            return carry

        def unused_wait(n, carry):
            tile_copy(n).wait()
            return carry

        lax.fori_loop(meta_ref[LANES - 1], n_tiles, unused_start, 0)
        lax.fori_loop(meta_ref[LANES - 1], n_tiles, unused_wait, 0)

    for _ in range(2):
        pltpu.make_async_copy(h_ref, xs_ref.at[pl.ds(0, chunk * rpt)], sem.at[0]).wait()


def _dispatch(pos_flat, meta_flat, h_lin, n_slots, rpt, tile_rows, chunk=1024):
    t = pos_flat.shape[0] // 2
    return pl.pallas_call(
        functools.partial(_dispatch_kernel, t=t, rpt=rpt, chunk=chunk, tile_rows=tile_rows,
                          n_tiles=n_slots // tile_rows),
        grid_spec=pltpu.PrefetchScalarGridSpec(
            num_scalar_prefetch=2,
            grid=(t // chunk,),
            in_specs=[pl.BlockSpec((chunk * rpt, LANES), lambda c, p, m: (c, 0))],
            out_specs=pl.BlockSpec(memory_space=pl.ANY),
            scratch_shapes=[pltpu.VMEM((tile_rows * rpt, LANES), h_lin.dtype), pltpu.SemaphoreType.DMA((2,))],
        ),
        out_shape=jax.ShapeDtypeStruct((n_slots * rpt, LANES), h_lin.dtype),
        compiler_params=_params("arbitrary"),
    )(pos_flat, meta_flat, h_lin)


def _ffn_kernel(meta_ref, xs_ref, wg_ref, wu_ref, wd_ref, ys_ref, wgb_ref, wub_ref, wdb_ref, *, tm):
    n = pl.program_id(0)
    n_used = meta_ref[LANES - 1]

    @pl.when(n < n_used)
    def _():
        @pl.when((n == 0) | (meta_ref[n] != meta_ref[jnp.maximum(n - 1, 0)]))
        def _():
            wgb_ref[...] = wg_ref[...].astype(BF16)
            wub_ref[...] = wu_ref[...].astype(BF16)
            wdb_ref[...] = wd_ref[...].astype(BF16)

        d = wgb_ref.shape[0]
        a = b = None
        for k0 in range(0, d, MXU_DIM):
            x = jnp.concatenate(
                [_load_token_major(xs_ref, tm, k0 // LANES + j).astype(BF16) for j in range(MXU_DIM // LANES)],
                axis=1)
            pa = _dot(x, wgb_ref[k0:k0 + MXU_DIM, :])
            pb = _dot(x, wub_ref[k0:k0 + MXU_DIM, :])
            a = pa if a is None else a + pa
            b = pb if b is None else b + pb
        act = (_silu(a) * b).astype(BF16)
        for n0 in range(0, d, MXU_DIM):
            _store_token_major(ys_ref, _dot(act, wdb_ref[:, n0:n0 + MXU_DIM]), n0)

    @pl.when(n >= n_used)
    def _():
        ys_ref[...] = jnp.zeros_like(ys_ref)


def _ffn(meta, xs, w_gate, w_up, w_down, layer, tm, n_tiles):
    _, ne, d, f = w_gate.shape
    rpt = d // LANES

    def row_tile(n, m):
        return (jnp.minimum(n, m[LANES - 1] - 1), 0)

    return pl.pallas_call(
        functools.partial(_ffn_kernel, tm=tm),
        grid_spec=pltpu.PrefetchScalarGridSpec(
            num_scalar_prefetch=1,
            grid=(n_tiles,),
            in_specs=[
                pl.BlockSpec((tm * rpt, LANES), row_tile),
                pl.BlockSpec((None, None, d, f), lambda n, m: (layer, m[n], 0, 0)),
                pl.BlockSpec((None, None, d, f), lambda n, m: (layer, m[n], 0, 0)),
                pl.BlockSpec((None, None, f, d), lambda n, m: (layer, m[n], 0, 0)),
            ],
            out_specs=pl.BlockSpec((tm * rpt, LANES), lambda n, m: (n, 0)),
            scratch_shapes=[
                pltpu.VMEM((d, f), BF16),
                pltpu.VMEM((d, f), BF16),
                pltpu.VMEM((f, d), BF16),
            ],
        ),
        out_shape=jax.ShapeDtypeStruct((n_tiles * tm * rpt, LANES), F32),
        compiler_params=_params("arbitrary"),
    )(meta, xs, w_gate, w_up, w_down)


def _combine_kernel(pos_ref, ys_ref, *refs, t, rpt, pitch, tm, n_parts, final_tiles):
    x_refs = refs[:n_parts]
    if final_tiles is None:
        wt_ref, gate_ref, o_ref, a0, b0, a1, b1, sem = refs[n_parts:]
    else:
        wt_ref, gate_ref, fg_ref, op_ref, os_ref, a0, b0, a1, b1, sem = refs[n_parts:]
    i = pl.program_id(0)
    n = pl.num_programs(0)
    bufs = ((a0, b0), (a1, b1))

    def issue(tile, s):
        def body(r, carry):
            tok = tile * tm + r
            for k in range(2):
                slot_row = pos_ref[k * t + tok]
                pltpu.make_async_copy(
                    ys_ref.at[pl.ds(pl.multiple_of(slot_row * rpt, rpt), rpt)],
                    bufs[s][k].at[pl.ds(pl.multiple_of(r * pitch, 8), rpt)],
                    sem.at[s]).start(priority=k)
            return carry

        lax.fori_loop(0, tm, body, 0, unroll=8)

    def wait(s):
        for k in range(2):
            pltpu.make_async_copy(ys_ref.at[pl.ds(0, tm * rpt)], bufs[s][k].at[pl.ds(0, tm * rpt)],
                                  sem.at[s]).wait()

    def compute(s, out_ref):
        w = wt_ref[...]
        w0, w1 = w[:, 0:1], w[:, 1:2]
        dn = x_refs[0].shape[1]
        for c in range(rpt):
            cols = slice(c * LANES, (c + 1) * LANES)
            part, pc = divmod(c * LANES, dn)
            y = (w0 * bufs[s][0][pl.ds(c, tm, stride=pitch), :]
                 + w1 * bufs[s][1][pl.ds(c, tm, stride=pitch), :])
            out_ref[:, cols] = x_refs[part][:, pc:pc + LANES] + gate_ref[:, cols] * y
        if final_tiles is not None:
            out_ref[...] = _rms_f32(out_ref[...]) * fg_ref[...]

    @pl.when(i == 0)
    def _():
        issue(0, 0)

    for s in range(2):
        @pl.when(i % 2 == s)
        def _(s=s):
            @pl.when(i + 1 < n)
            def _():
                issue(i + 1, 1 - s)

            wait(s)
            if final_tiles is None:
                compute(s, o_ref)
            else:
                @pl.when(i < final_tiles)
                def _():
                    compute(s, op_ref)

                @pl.when(i >= final_tiles)
                def _():
                    compute(s, os_ref)


def _combine(pos_flat, ys, x1_parts, wt, mod3, layer, row_fn, final=None, tm=512):
    t, dn = x1_parts[0].shape
    n_parts = len(x1_parts)
    d = dn * n_parts
    rpt = d // LANES
    pitch = rpt + 8
    base = layer * 48
    in_specs = ([pl.BlockSpec(memory_space=pl.ANY)]
                + [pl.BlockSpec((tm, dn), lambda i, p: (i, 0))] * n_parts
                + [pl.BlockSpec((tm, 2), lambda i, p: (i, 0)),
                   pl.BlockSpec((None, 1, d), lambda i, p: (base + row_fn(i) * 6 + 5, 0, 0))])
    args = [pos_flat, ys, *x1_parts, wt, mod3]
    if final is None:
        final_tiles = None
        out_specs = pl.BlockSpec((tm, d), lambda i, p: (i, 0))
        out_shape = jax.ShapeDtypeStruct((t, d), F32)
    else:
        fg, t_ctx = final
        final_tiles = t_ctx // tm
        in_specs.append(pl.BlockSpec((1, d), lambda i, p: (0, 0)))
        args.append(fg.reshape(1, d))
        out_specs = [
            pl.BlockSpec((tm, d), lambda i, p: (jnp.minimum(i, final_tiles - 1), 0)),
            pl.BlockSpec((tm, d), lambda i, p: (jnp.maximum(i - final_tiles, 0), 0)),
        ]
        out_shape = [jax.ShapeDtypeStruct((t_ctx, d), F32), jax.ShapeDtypeStruct((t - t_ctx, d), F32)]
    return pl.pallas_call(
        functools.partial(_combine_kernel, t=t, rpt=rpt, pitch=pitch, tm=tm, n_parts=n_parts,
                          final_tiles=final_tiles),
        grid_spec=pltpu.PrefetchScalarGridSpec(
            num_scalar_prefetch=1,
            grid=(t // tm,),
            in_specs=in_specs,
            out_specs=out_specs,
            scratch_shapes=[pltpu.VMEM((tm * pitch, LANES), F32)] * 4 + [pltpu.SemaphoreType.DMA((2,))],
        ),
        out_shape=out_shape,
        compiler_params=_params("arbitrary"),
    )(*args)


def _moe(lg, router_b, h_lin, x1, w_gate, w_up, w_down, mod3, layer, row_fn, final=None, tile_rows=256):
    t = x1[0].shape[0]
    n_tiles = (2 * t) // tile_rows + N_EXPERTS
    pos, wts, meta = _route(lg, router_b, tile_rows)
    pos_flat = pos[:2].reshape(-1)
    xs = _dispatch(pos_flat, meta[:3].reshape(-1), h_lin, n_tiles * tile_rows, h_lin.shape[0] // t, tile_rows)
    ys = _ffn(meta[0], xs, w_gate, w_up, w_down, layer, tile_rows, n_tiles)
    return _combine(pos_flat, ys, x1, wts[:2].T, mod3, layer, row_fn, final)


def _rope_tables(n_tok):
    rows = n_tok // GRID_W
    pos_r = jnp.repeat(jnp.arange(rows), GRID_W).astype(F32)
    pos_c = jnp.tile(jnp.arange(GRID_W), rows).astype(F32)
    n_freq = DK_A // 4
    inv = jnp.power(ROPE_BASE, -jnp.arange(n_freq, dtype=F32) / n_freq)
    ang_r = pos_r[:, None] * inv
    ang_c = pos_c[:, None] * inv
    cos_t = jnp.concatenate([jnp.cos(ang_r), jnp.cos(ang_r), jnp.cos(ang_c), jnp.cos(ang_c)], axis=1)
    sin_t = jnp.concatenate([-jnp.sin(ang_r), jnp.sin(ang_r), -jnp.sin(ang_c), jnp.sin(ang_c)], axis=1)
    return cos_t, sin_t


def kernel(x_prompt, x_sample, cache_k, cache_v, state_ret, c, c_ctx, norm_g, w_mod, b_mod, w_in_even, w_out_even, lam_q, lam_k, ret_decay_logit, w_in_odd, gmlp_norm_g, w_spatial, b_spatial, w_out_odd, router_w, router_b, w_gate, w_up, w_down, final_norm_g):
    batch, seq, d = x_prompt.shape
    dbatch, dseq, _ = x_sample.shape
    past = cache_k.shape[2]
    depth = norm_g.shape[0]
    t_ctx, t_lat = batch * seq, dbatch * dseq
    t_all = t_ctx + t_lat
    xc = x_prompt.reshape(t_ctx, d)
    xl = x_sample.reshape(t_lat, d)

    cond8 = jnp.concatenate([c_ctx[None, :], c, jnp.zeros((8 - 1 - dbatch, d), F32)], axis=0)
    mod3 = _modulation(cond8, w_mod, b_mod).reshape(depth * 8 * 6, 1, d)
    router_wt = router_w.T
    cos_t, sin_t = _rope_tables(dseq)

    tm_in = 512
    tm_out = 512
    tm_cmb = 512
    w_in0 = w_in_even.reshape(d, -1)
    w_out0 = w_out_even.reshape(-1, d)
    w_in1 = w_in_odd.reshape(d, -1)
    w_out1 = w_out_odd.reshape(-1, d)
    lq, lk, dlog = lam_q.reshape(2, DK_A), lam_k.reshape(2, DK_A), ret_decay_logit.reshape(2, H_B)

    def rows_ctx(i):
        return 0

    def rows_lat(tiles_per_batch):
        return lambda i: 1 + i // tiles_per_batch

    def rows_all(tm):
        n_ctx, per = t_ctx // tm, dseq // tm
        return lambda i: jnp.maximum(i - n_ctx, 0) // per + jnp.where(i >= n_ctx, 1, 0)

    lam_init = 0.8 - 0.6 * math.exp(-0.3 * 0)
    half0 = w_in0.shape[1] // 2
    g00 = norm_g[0, 0]
    rl = rows_lat(dseq // tm_in)
    pa_c, k_new, v_new = _in_proj(xc, g00, mod3, 0, rows_ctx, w_in0, 0, half0, "even_kv", tm=tm_in)
    (pb_c,) = _in_proj(xc, g00, mod3, 0, rows_ctx, w_in0, half0, half0, "even", tm=tm_in)
    (pa_l,) = _in_proj(xl, g00, mod3, 0, rl, w_in0, 0, half0, "even", tm=tm_in)
    (pb_l,) = _in_proj(xl, g00, mod3, 0, rl, w_in0, half0, half0, "even", tm=tm_in)
    oa_c, or_c, s_new = _mixer(pa_c, pb_c, lq, lk, dlog, lam_init, batch, seq, H_A)
    ctx = (cache_k.reshape(dbatch, past, H_A * 2 * DK_A), cache_v.reshape(dbatch, past, H_A * DV_A),
           state_ret.reshape(dbatch, 2, H_B, DK_B, DV_B), cos_t, sin_t)
    oa_l, or_l = _mixer(pa_l, pb_l, lq, lk, dlog, lam_init, dbatch, dseq, 1, ctx)
    x1, h2, lg = _out_proj([([oa_c, or_c], xc), ([oa_l, or_l], xl)], w_out0, norm_g[0, 1], mod3, 0,
                           rows_all(tm_out), router_wt, tm=tm_out)
    x2 = _moe(lg, router_b, h2, x1, w_gate, w_up, w_down, mod3, 0, rows_all(tm_cmb))

    q1 = w_in1.shape[1] // 4
    g10 = norm_g[1, 0]
    ra = rows_all(tm_in)
    v_blk, v_ss = zip(*[_in_proj(x2, g10, mod3, 1, ra, w_in1, (2 + j) * q1, q1, "gmlp_v", tm=tm_in)
                        for j in range(2)])
    gmlp = (gmlp_norm_g.reshape(1, -1), w_spatial.reshape(H_C, CHUNK_C, CHUNK_C),
            b_spatial.reshape(H_C, CHUNK_C, 1))
    g0, g1 = [_in_proj(x2, g10, mod3, 1, ra, w_in1, j * q1, q1, "gmlp_u",
                       gating=(v_blk[j], v_ss[0], v_ss[1], *gmlp, j), tm=tm_in)[0] for j in range(2)]
    x3, h4, lg2 = _out_proj([([g0, g1], x2)], w_out1, norm_g[1, 1], mod3, 1, rows_all(tm_out), router_wt,
                            tm=tm_out)
    y_p, y_s = _moe(lg2, router_b, h4, x3, w_gate, w_up, w_down, mod3, 1, rows_all(tm_cmb),
                    final=(final_norm_g, t_ctx))

    y_prompt = y_p.reshape(batch, seq, d)
    y_sample = y_s.reshape(dbatch, dseq, d)
    new_cache_k = k_new.reshape(batch, 1, seq, H_A, 2, DK_A)
    new_cache_v = v_new.reshape(batch, 1, seq, DV_A // LANES, H_A, LANES).swapaxes(3, 4).reshape(
        batch, 1, seq, H_A, DV_A)
    new_state_ret = s_new.reshape(batch, 1, 2, H_B, DK_B, DV_B)
    return (y_prompt, y_sample, new_cache_k, new_cache_v, new_state_ret)
```

```python
import functools
import math

import jax
import jax.numpy as jnp
from jax import lax
from jax.experimental import pallas as pl
from jax.experimental.pallas import tpu as pltpu

F32 = jnp.float32
BF16 = jnp.bfloat16

EPS = 1e-6
ROPE_BASE = 10000.0
GRID_W = 64
H_A, DK_A, DV_A = 4, 128, 256
H_B, DK_B, DV_B = 4, 128, 256
RET_CHUNK = 128
H_C, CHUNK_C = 8, 128
N_EXPERTS, N_GROUPS, GROUP_SIZE = 16, 4, 4

VMEM_LIMIT_BYTES = 56 * 1024 * 1024


def _params(*sem):
    return pltpu.CompilerParams(dimension_semantics=sem, vmem_limit_bytes=VMEM_LIMIT_BYTES)


def _rms_f32(x):
    return x * lax.rsqrt(jnp.mean(x * x, axis=-1, keepdims=True) + EPS)


def _silu(x):
    return x * jax.nn.sigmoid(x)


def _dot(a, b):
    return jnp.dot(a, b, preferred_element_type=F32)


def _dot_nt(a, b):
    return lax.dot_general(a, b, (((1,), (1,)), ((), ())), preferred_element_type=F32)


def _dot_nt_x3(a, b):
    ah = a.astype(BF16)
    al = (a - ah.astype(F32)).astype(BF16)
    bh = b.astype(BF16)
    bl = (b - bh.astype(F32)).astype(BF16)
    return _dot_nt(ah, bh) + (_dot_nt(al, bh) + _dot_nt(ah, bl))


LANES = 128
MXU_DIM = 256


def _store_token_major(ref, val, col0=0):
    tm, width = val.shape
    rpt = ref.shape[0] // tm
    for c in range(width // LANES):
        ref[pl.ds(col0 // LANES + c, tm, stride=rpt), :] = val[:, c * LANES:(c + 1) * LANES]


def _load_token_major(ref, tm, c):
    rpt = ref.shape[0] // tm
    return ref[pl.ds(c, tm, stride=rpt), :]


def _mod_kernel(c_ref, w_ref, b_ref, o_ref):
    s = _silu(c_ref[...]).astype(BF16)
    o_ref[...] = _dot(s, w_ref[...].astype(BF16)) + b_ref[...]


def _modulation(cond8, w_mod, b_mod, tn=1024):
    depth, d, n = w_mod.shape
    return pl.pallas_call(
        _mod_kernel,
        grid=(depth, n // tn),
        in_specs=[
            pl.BlockSpec((8, d), lambda l, j: (0, 0)),
            pl.BlockSpec((None, d, tn), lambda l, j: (l, 0, j)),
            pl.BlockSpec((None, 1, tn), lambda l, j: (l, 0, j)),
        ],
        out_specs=pl.BlockSpec((None, 8, tn), lambda l, j: (l, 0, j)),
        out_shape=jax.ShapeDtypeStruct((depth, 8, n), F32),
        compiler_params=_params("arbitrary", "arbitrary"),
    )(cond8, w_mod, b_mod.reshape(depth, 1, n))


def _load_weight_slab(w_hbm, slab_ref, stage_ref, sem, col0):
    kdim, n = slab_ref.shape
    rows = stage_ref.shape[1]
    nb = kdim // rows

    def copy(j):
        return pltpu.make_async_copy(w_hbm.at[pl.ds(j * rows, rows), pl.ds(col0, n)],
                                     stage_ref.at[j % 2], sem.at[j % 2])

    copy(0).start()
    for j in range(nb):
        if j + 1 < nb:
            copy(j + 1).start()
        copy(j).wait()
        slab_ref[j * rows:(j + 1) * rows, :] = stage_ref[j % 2].astype(BF16)


def _gelu(x):
    return 0.5 * x * (1.0 + lax.erf(x * (2.0 ** -0.5)))


def _in_kernel(x_ref, g_ref, sh_ref, sc_ref, w_hbm, *rest, col0, mode, kv_cols, chunk, e_c):
    if mode == "even_kv":
        p_ref, k_ref, v_ref, slab_ref, stage_ref, h_ref, sem = rest
    elif mode == "gmlp_v":
        p_ref, ss_ref, slab_ref, stage_ref, h_ref, sem = rest
    elif mode == "gmlp_u":
        vb_ref, sa_ref, sb_ref, ng_ref, ws_ref, bs_ref, p_ref, slab_ref, stage_ref, h_ref, vn_ref, sem = rest
    else:
        p_ref, slab_ref, stage_ref, h_ref, sem = rest

    @pl.when(pl.program_id(0) == 0)
    def _():
        _load_weight_slab(w_hbm, slab_ref, stage_ref, sem, col0)

    h = (_rms_f32(x_ref[...]) * g_ref[...]) * (1.0 + sc_ref[...]) + sh_ref[...]
    h_ref[...] = h.astype(BF16)
    n = slab_ref.shape[1]
    tm = h_ref.shape[0]
    if mode == "gmlp_u":
        inv = lax.rsqrt((sa_ref[...] + sb_ref[...]) / e_c + EPS)
        vn_ref[...] = ((vb_ref[...].astype(F32) * inv) * ng_ref[...]).astype(BF16)
    ss = None
    for ci, c0 in enumerate(range(0, n, chunk)):
        acc = _dot(h_ref[...], slab_ref[:, c0:c0 + chunk])
        if mode == "gmlp_v":
            v = _gelu(acc)
            p_ref[:, c0:c0 + chunk] = v.astype(BF16)
            part = jnp.sum(v * v, axis=-1, keepdims=True)
            ss = part if ss is None else ss + part
        elif mode == "gmlp_u":
            u = _gelu(acc)
            wsb = ws_ref[ci].astype(BF16)
            for r0 in range(0, tm, CHUNK_C):
                mixed = _dot(wsb, vn_ref[r0:r0 + CHUNK_C, c0:c0 + chunk]) + bs_ref[ci]
                p_ref[r0:r0 + CHUNK_C, c0:c0 + chunk] = (u[r0:r0 + CHUNK_C] * mixed).astype(BF16)
        else:
            p_ref[:, c0:c0 + chunk] = acc.astype(BF16)
        if mode == "even_kv":
            (k0, k1), (v0, v1) = kv_cols
            a0 = col0 + c0
            if k0 <= a0 < k1:
                _store_token_major(k_ref, acc, a0 - k0)
            if v0 <= a0 < v1:
                tm = acc.shape[0]
                rpt = v_ref.shape[0] // tm
                for cc in range(chunk // LANES):
                    head, half = divmod((a0 - v0) // LANES + cc, DV_A // LANES)
                    v_ref[pl.ds(half * H_A + head, tm, stride=rpt), :] = acc[:, cc * LANES:(cc + 1) * LANES]
    if mode == "gmlp_v":
        ss_ref[...] = ss


def _in_proj(x, g, mod3, layer, row_fn, w, col0, n_cols, mode, gating=None, tm=512, chunk=512, stage_rows=256):
    t, d = x.shape
    base = layer * 48

    def mod_spec(kind):
        return pl.BlockSpec((None, 1, d), lambda i: (base + row_fn(i) * 6 + kind, 0, 0))

    in_specs = [
        pl.BlockSpec((tm, d), lambda i: (i, 0)),
        pl.BlockSpec((1, d), lambda i: (0, 0)),
        mod_spec(0),
        mod_spec(1),
        pl.BlockSpec(memory_space=pl.ANY),
    ]
    args = [x, g.reshape(1, d), mod3, mod3, w]
    out_specs = [pl.BlockSpec((tm, n_cols), lambda i: (i, 0))]
    out_shape = [jax.ShapeDtypeStruct((t, n_cols), BF16)]
    scratch = [
        pltpu.VMEM((d, n_cols), BF16),
        pltpu.VMEM((2, stage_rows, n_cols), F32),
        pltpu.VMEM((tm, d), BF16),
    ]
    kv_cols = None
    e_c = None
    if mode == "even_kv":
        a_qk = H_A * 2 * DK_A
        a_v = H_A * DV_A
        kv_cols = ((a_qk, 2 * a_qk), (2 * a_qk, 2 * a_qk + a_v))
        for width in (a_qk, a_v):
            out_specs.append(pl.BlockSpec((tm * width // LANES, LANES), lambda i: (i, 0)))
            out_shape.append(jax.ShapeDtypeStruct((t * width // LANES, LANES), F32))
    elif mode == "gmlp_v":
        out_specs.append(pl.BlockSpec((tm, 1), lambda i: (i, 0)))
        out_shape.append(jax.ShapeDtypeStruct((t, 1), F32))
    elif mode == "gmlp_u":
        v_blk, ss_a, ss_b, ng, w_s, b_s, blk = gating
        e_c = ng.shape[1]
        groups = w_s.shape[0] * n_cols // e_c
        chunk = n_cols // groups
        in_specs += [
            pl.BlockSpec((tm, n_cols), lambda i: (i, 0)),
            pl.BlockSpec((tm, 1), lambda i: (i, 0)),
            pl.BlockSpec((tm, 1), lambda i: (i, 0)),
            pl.BlockSpec((1, n_cols), lambda i: (0, blk)),
            pl.BlockSpec((groups, CHUNK_C, CHUNK_C), lambda i: (blk, 0, 0)),
            pl.BlockSpec((groups, CHUNK_C, 1), lambda i: (blk, 0, 0)),
        ]
        args += [v_blk, ss_a, ss_b, ng, w_s, b_s]
        scratch.append(pltpu.VMEM((tm, n_cols), BF16))
    return pl.pallas_call(
        functools.partial(_in_kernel, col0=col0, mode=mode, kv_cols=kv_cols, chunk=chunk, e_c=e_c),
        grid=(t // tm,),
        in_specs=in_specs,
        out_specs=out_specs,
        out_shape=out_shape,
        scratch_shapes=scratch + [pltpu.SemaphoreType.DMA((2,))],
        compiler_params=_params("arbitrary"),
    )(*args)


def _rope(x, c, s):
    lane = lax.broadcasted_iota(jnp.int32, x.shape, 1)
    partner = jnp.where((lane & 32) == 0, pltpu.roll(x, 96, 1), pltpu.roll(x, 32, 1))
    return x * c + partner * s


def _mixer_kernel(*refs, seq, has_ctx, lam_init, past, hps):
    if has_ctx:
        (aq_ref, ak_ref, av_ref, bq_ref, bk_ref, bv_ref, bg_ref, lq_ref, lk_ref, dl_ref,
         ck_ref, cv_ref, s0_ref, cos_ref, sin_ref,
         oa_ref, or_ref, kall_ref, vall_ref, of_ref, ob_ref, stf_ref, stb_ref) = refs
    else:
        (aq_ref, ak_ref, av_ref, bq_ref, bk_ref, bv_ref, bg_ref, lq_ref, lk_ref, dl_ref,
         oa_ref, or_ref, sout_ref, of_ref, ob_ref, stf_ref, stb_ref) = refs
    rt = 256
    straight = seq <= rt
    tq = 256 if straight else 512
    cs = RET_CHUNK
    nc = seq // cs

    prod = lq_ref[...] * lk_ref[...]
    ee = jnp.exp(jnp.sum(prod, axis=1, keepdims=True))
    lam = ee[0:1, :] - ee[1:2, :] + lam_init
    scale = DK_A ** -0.5
    ii = lax.broadcasted_iota(jnp.int32, (cs, cs), 0)
    jj = lax.broadcasted_iota(jnp.int32, (cs, cs), 1)
    ic = lax.broadcasted_iota(jnp.int32, (cs, 1), 0).astype(F32)

    def loop(n, body):
        if straight:
            for it in range(n):
                body(it)
        else:
            lax.fori_loop(0, n, lambda it, c: (body(it), c)[1], 0)

    def rows(it, size):
        return pl.ds(it * size, size) if straight else pl.ds(pl.multiple_of(it * size, size), size)

    for hh in range(hps):
        qk_cols = slice(hh * 2 * DK_A, (hh + 1) * 2 * DK_A)
        va_cols = slice(hh * DV_A, (hh + 1) * DV_A)
        kb_cols = slice(hh * DK_B, (hh + 1) * DK_B)
        vb_cols = slice(hh * DV_B, (hh + 1) * DV_B)

        if has_ctx:
            for c in range(2):
                kall_ref[c, 0:past, :] = ck_ref[:, c * DK_A:(c + 1) * DK_A].astype(BF16)
            vall_ref[0:past, :] = cv_ref[...].astype(BF16)

            def copy_body(r):
                rr = rows(r, rt)
                cs_, sn_ = cos_ref[rr, :], sin_ref[rr, :]
                kk = ak_ref[rr, qk_cols].astype(F32)
                for c in range(2):
                    kall_ref[c, pl.ds(past + pl.multiple_of(r * rt, rt), rt), :] = _rope(
                        kk[:, c * DK_A:(c + 1) * DK_A], cs_, sn_).astype(BF16)
                vall_ref[pl.ds(past + pl.multiple_of(r * rt, rt), rt), :] = av_ref[rr, va_cols]

            loop(seq // rt, copy_body)

        def attn_body(qt, qk_cols=qk_cols, va_cols=va_cols):
            rr = rows(qt, tq)
            q = aq_ref[rr, qk_cols]
            if has_ctx:
                qf = q.astype(F32)
                cs_, sn_ = cos_ref[rr, :], sin_ref[rr, :]
            probs = []
            for c in range(2):
                if has_ctx:
                    qc = _rope(qf[:, c * DK_A:(c + 1) * DK_A], cs_, sn_).astype(BF16)
                    keys = kall_ref[c]
                else:
                    qc = q[:, c * DK_A:(c + 1) * DK_A]
                    keys = ak_ref[:, hh * 2 * DK_A + c * DK_A:hh * 2 * DK_A + (c + 1) * DK_A]
                s = _dot_nt(qc, keys) * scale
                e = jnp.exp(s - jnp.max(s, axis=-1, keepdims=True))
                probs.append(e * (1.0 / jnp.sum(e, axis=-1, keepdims=True)))
            a = (probs[0] - lam * probs[1]).astype(BF16)
            o = _dot(a, vall_ref[...] if has_ctx else av_ref[:, va_cols])
            oa_ref[rr, va_cols] = (_rms_f32(o) * (1.0 - lam_init)).astype(BF16)

        tables = []
        for d in range(2):
            lg = jax.nn.log_sigmoid(dl_ref[d, hh])
            diff = (ii - jj if d == 0 else jj - ii).astype(F32)
            mask = jnp.where(diff >= 0, jnp.exp(jnp.maximum(diff, 0.0) * lg), 0.0)
            if d == 0:
                qdec = jnp.exp((ic + 1.0) * lg)
                kdec = jnp.exp((cs - 1.0 - ic) * lg)
            else:
                qdec = jnp.exp((cs - ic) * lg)
                kdec = jnp.exp(ic * lg)
            tables.append((mask, qdec, kdec, jnp.exp(cs * lg)))
        st_refs = (stf_ref, stb_ref)
        o_refs = (of_ref, ob_ref)
        for d in range(2):
            if has_ctx:
                st_refs[d][hh] = s0_ref[d]
            else:
                st_refs[d][hh] = jnp.zeros((DK_B, DV_B), F32)

        def chunk_body(n, hh=hh, kb_cols=kb_cols, vb_cols=vb_cols, tables=tables):
            for d in range(2):
                mask, qdec, kdec, cdec = tables[d]
                rr = rows(n if d == 0 else nc - 1 - n, cs)
                qc = bq_ref[rr, kb_cols].astype(F32)
                kc = bk_ref[rr, kb_cols].astype(F32) * (DK_B ** -0.5)
                vc = bv_ref[rr, vb_cols]
                state = st_refs[d][hh]
                qk = _dot_nt(qc.astype(BF16), kc.astype(BF16)) * mask
                o_refs[d][rr, vb_cols] = (_dot(qk.astype(BF16), vc)
                                          + _dot((qc * qdec).astype(BF16), state.astype(BF16)))
                kd = (kc * kdec).T.astype(BF16)
                st_refs[d][hh] = cdec * state + _dot(kd, vc)

        n_q = seq // tq
        per_q = nc // n_q

        def fused_body(qt, attn_body=attn_body, chunk_body=chunk_body):
            attn_body(qt)
            for j in range(per_q):
                chunk_body(qt * per_q + j)

        loop(n_q, fused_body)

        def final_body(r, vb_cols=vb_cols):
            rr = rows(r, rt)
            tot = of_ref[rr, vb_cols] + ob_ref[rr, vb_cols]
            gate = _silu(bg_ref[rr, vb_cols].astype(F32))
            or_ref[rr, vb_cols] = (_rms_f32(tot) * gate).astype(BF16)

        loop(seq // rt, final_body)
        if not has_ctx:
            for d in range(2):
                sout_ref[d, hh] = st_refs[d][hh]


def _mixer(pa, pb, lam_q, lam_k, decay_logit, lam_init, batch, seq, hps, ctx=None):
    has_ctx = ctx is not None
    past = ctx[0].shape[1] if has_ctx else 0

    def col(head_width, off):
        width = hps * head_width
        return pl.BlockSpec((seq, width), lambda b, hg: (b, off // width + hg))

    a_qk = H_A * 2 * DK_A
    b_qk = H_B * DK_B
    b_v = H_B * DV_B
    in_specs = [
        col(2 * DK_A, 0), col(2 * DK_A, a_qk), col(DV_A, 2 * a_qk),
        col(DK_B, 0), col(DK_B, b_qk), col(DV_B, 2 * b_qk), col(DV_B, 2 * b_qk + b_v),
        pl.BlockSpec((2, DK_A), lambda b, hg: (0, 0)),
        pl.BlockSpec((2, DK_A), lambda b, hg: (0, 0)),
        pl.BlockSpec((2, hps, 1, 1), lambda b, hg: (0, hg, 0, 0)),
    ]
    args = [pa] * 3 + [pb] * 4 + [lam_q, lam_k, decay_logit.reshape(2, H_B, 1, 1)]
    out_specs = [
        pl.BlockSpec((seq, hps * DV_A), lambda b, hg: (b, hg)),
        pl.BlockSpec((seq, hps * DV_B), lambda b, hg: (b, hg)),
    ]
    out_shape = [
        jax.ShapeDtypeStruct((batch * seq, H_A * DV_A), BF16),
        jax.ShapeDtypeStruct((batch * seq, b_v), BF16),
    ]
    scratch = []
    if has_ctx:
        assert hps == 1
        ck, cv, s0, cos_t, sin_t = ctx
        in_specs += [
            pl.BlockSpec((None, past, 2 * DK_A), lambda b, hg: (b, 0, hg)),
            pl.BlockSpec((None, past, DV_A), lambda b, hg: (b, 0, hg)),
            pl.BlockSpec((None, 2, None, DK_B, DV_B), lambda b, hg: (b, 0, hg, 0, 0)),
            pl.BlockSpec((seq, DK_A), lambda b, hg: (0, 0)),
            pl.BlockSpec((seq, DK_A), lambda b, hg: (0, 0)),
        ]
        args += [ck, cv, s0, cos_t, sin_t]
        scratch += [pltpu.VMEM((2, seq + past, DK_A), BF16), pltpu.VMEM((seq + past, DV_A), BF16)]
    else:
        out_specs.append(pl.BlockSpec((None, 2, hps, DK_B, DV_B), lambda b, hg: (b, 0, hg, 0, 0)))
        out_shape.append(jax.ShapeDtypeStruct((batch, 2, H_B, DK_B, DV_B), F32))
    scratch += [
        pltpu.VMEM((seq, hps * DV_B), F32),
        pltpu.VMEM((seq, hps * DV_B), F32),
        pltpu.VMEM((hps, DK_B, DV_B), F32),
        pltpu.VMEM((hps, DK_B, DV_B), F32),
    ]
    return pl.pallas_call(
        functools.partial(_mixer_kernel, seq=seq, has_ctx=has_ctx, lam_init=lam_init, past=past, hps=hps),
        grid=(batch, H_A // hps),
        in_specs=in_specs,
        out_specs=out_specs,
        out_shape=out_shape,
        scratch_shapes=scratch,
        compiler_params=_params("arbitrary", "arbitrary"),
    )(*args)


def _out_kernel(*refs, seg_tiles, n_lhs, part, n_parts):
    n_seg = len(seg_tiles)
    per_seg = n_lhs + 1
    seg_refs = [refs[s * per_seg:(s + 1) * per_seg] for s in range(n_seg)]
    rest = refs[n_seg * per_seg:]
    last = part == n_parts - 1
    if last:
        gate_ref, g_ref, sh_ref, sc_ref, rw_ref = rest[:5]
        prev_refs = rest[5:5 + part]
        w_hbm, x1_ref, h2_ref, lg_ref, slab_ref, stage_ref, sem = rest[5 + part:]
    else:
        gate_ref, w_hbm, x1_ref, slab_ref, stage_ref, sem = rest
    i = pl.program_id(0)
    dn = slab_ref.shape[1]

    @pl.when(i == 0)
    def _():
        _load_weight_slab(w_hbm, slab_ref, stage_ref, sem, part * dn)

    def run(lhs_refs, x_ref):
        acc = None
        r0 = 0
        for lhs_ref in lhs_refs:
            kk = lhs_ref.shape[1]
            prod = _dot(lhs_ref[...], slab_ref[r0:r0 + kk, :])
            acc = prod if acc is None else acc + prod
            r0 += kk
        x1 = x_ref[...] + gate_ref[...] * acc
        x1_ref[...] = x1
        if last:
            cols = [p_ref[...] for p_ref in prev_refs] + [x1]
            d = dn * n_parts
            ms = sum(jnp.sum(v * v, axis=-1, keepdims=True) for v in cols) / d
            inv = lax.rsqrt(ms + EPS)
            lg = None
            for j, v in enumerate(cols):
                cs_ = slice(j * dn, (j + 1) * dn)
                h2 = ((v * inv) * g_ref[:, cs_]) * (1.0 + sc_ref[:, cs_]) + sh_ref[:, cs_]
                _store_token_major(h2_ref, h2, j * dn)
                pl_ = _dot_nt_x3(rw_ref[:, cs_], h2)
                lg = pl_ if lg is None else lg + pl_
            lg_ref[...] = lg

    if n_seg == 1:
        run(seg_refs[0][:n_lhs], seg_refs[0][n_lhs])
    else:
        lo = 0
        for s in range(n_seg):
            @pl.when((i >= lo) & (i < lo + seg_tiles[s]))
            def _(s=s):
                run(seg_refs[s][:n_lhs], seg_refs[s][n_lhs])

            lo += seg_tiles[s]


def _out_proj(segments, w, g, mod3, layer, row_fn, router_wt, n_parts=2, tm=512, stage_rows=256):
    n_lhs = len(segments[0][0])
    d = segments[0][1].shape[1]
    dn = d // n_parts
    kdim = w.shape[0]
    base = layer * 48
    ne = router_wt.shape[0]
    seg_tiles = tuple(x.shape[0] // tm for _, x in segments)
    t_total = sum(x.shape[0] for _, x in segments)
    rpt = d // LANES
    x1_parts = []
    for part in range(n_parts):
        last = part == n_parts - 1
        in_specs, args = [], []
        lo = 0
        for (lhs_list, x), nt in zip(segments, seg_tiles):
            def seg_row(i, lo=lo, nt=nt):
                return jnp.clip(i - lo, 0, nt - 1)

            for lhs in lhs_list:
                in_specs.append(pl.BlockSpec((tm, lhs.shape[1]), lambda i, f=seg_row: (f(i), 0)))
                args.append(lhs)
            in_specs.append(pl.BlockSpec((tm, dn), lambda i, f=seg_row: (f(i), part)))
            args.append(x)
            lo += nt
        in_specs.append(pl.BlockSpec((None, 1, dn), lambda i: (base + row_fn(i) * 6 + 2, 0, part)))
        args.append(mod3)
        out_specs = [pl.BlockSpec((tm, dn), lambda i: (i, 0))]
        out_shape = [jax.ShapeDtypeStruct((t_total, dn), F32)]
        if last:
            in_specs += [
                pl.BlockSpec((1, d), lambda i: (0, 0)),
                pl.BlockSpec((None, 1, d), lambda i: (base + row_fn(i) * 6 + 3, 0, 0)),
                pl.BlockSpec((None, 1, d), lambda i: (base + row_fn(i) * 6 + 4, 0, 0)),
                pl.BlockSpec((ne, d), lambda i: (0, 0)),
            ] + [pl.BlockSpec((tm, dn), lambda i: (i, 0))] * part
            args += [g.reshape(1, d), mod3, mod3, router_wt] + x1_parts
            out_specs += [
                pl.BlockSpec((tm * rpt, LANES), lambda i: (i, 0)),
                pl.BlockSpec((ne, tm), lambda i: (0, i)),
            ]
            out_shape += [
                jax.ShapeDtypeStruct((t_total * rpt, LANES), F32),
                jax.ShapeDtypeStruct((ne, t_total), F32),
            ]
        in_specs.append(pl.BlockSpec(memory_space=pl.ANY))
        args.append(w)
        outs = pl.pallas_call(
            functools.partial(_out_kernel, seg_tiles=seg_tiles, n_lhs=n_lhs, part=part, n_parts=n_parts),
            grid=(t_total // tm,),
            in_specs=in_specs,
            out_specs=out_specs,
            out_shape=out_shape,
            scratch_shapes=[
                pltpu.VMEM((kdim, dn), BF16),
                pltpu.VMEM((2, stage_rows, dn), F32),
                pltpu.SemaphoreType.DMA((2,)),
            ],
            compiler_params=_params("arbitrary"),
        )(*args)
        x1_parts.append(outs[0])
    return x1_parts, outs[1], outs[2]


def _first_index(vals, target):
    idx = jnp.full(target.shape, len(vals) - 1, jnp.int32)
    for k in range(len(vals) - 2, -1, -1):
        idx = jnp.where(vals[k] == target, k, idx)
    return idx


def _pick(vals, idx):
    out = vals[-1]
    for k in range(len(vals) - 2, -1, -1):
        out = jnp.where(idx == k, vals[k], out)
    return out


def _route_kernel(lg_ref, rb_ref, pos_ref, wt_ref, meta_ref, rank_ref, *, tile_rows, scan):
    logits = lg_ref[...]
    t = logits.shape[1]
    scores = jax.nn.sigmoid(logits)
    biased = scores + rb_ref[...]
    s_rows = [scores[e:e + 1, :] for e in range(N_EXPERTS)]
    b_rows = [biased[e:e + 1, :] for e in range(N_EXPERTS)]
    grp = []
    for g in range(N_GROUPS):
        r = b_rows[g * GROUP_SIZE:(g + 1) * GROUP_SIZE]
        best = None
        for a in range(GROUP_SIZE):
            for b in range(a + 1, GROUP_SIZE):
                pair = r[a] + r[b]
                best = pair if best is None else jnp.maximum(best, pair)
        grp.append(best)
    gmax = functools.reduce(jnp.maximum, grp)
    gsel = _first_index(grp, gmax)
    cand_b = [_pick([b_rows[g * GROUP_SIZE + k] for g in range(N_GROUPS)], gsel) for k in range(GROUP_SIZE)]
    cand_s = [_pick([s_rows[g * GROUP_SIZE + k] for g in range(N_GROUPS)], gsel) for k in range(GROUP_SIZE)]
    m1 = functools.reduce(jnp.maximum, cand_b)
    k1 = _first_index(cand_b, m1)
    rest = [jnp.where(k1 == k, -jnp.inf, cand_b[k]) for k in range(GROUP_SIZE)]
    m2 = functools.reduce(jnp.maximum, rest)
    k2 = _first_index(rest, m2)
    w1 = _pick(cand_s, k1)
    w2 = _pick(cand_s, k2)
    wsum = w1 + w2
    w1 = w1 / wsum
    w2 = w2 / wsum
    e1 = gsel * GROUP_SIZE + k1
    e2 = gsel * GROUP_SIZE + k2
    wt_ref[...] = jnp.zeros_like(wt_ref)
    wt_ref[0:1, :] = w1
    wt_ref[1:2, :] = w2

    eid = lax.broadcasted_iota(jnp.int32, logits.shape, 0)
    sel1 = eid == e1
    sel2 = eid == e2
    ind = jnp.where(sel1 | sel2, 1.0, 0.0)
    ua = lax.broadcasted_iota(jnp.int32, (scan, scan), 0)
    ub = lax.broadcasted_iota(jnp.int32, (scan, scan), 1)
    upper = jnp.where(ua <= ub, 1.0, 0.0).astype(BF16)
    carry = jnp.zeros((N_EXPERTS, 1), F32)
    for blk in range(t // scan):
        seg = ind[:, blk * scan:(blk + 1) * scan]
        incl = _dot(seg.astype(BF16), upper)
        rank_ref[:, blk * scan:(blk + 1) * scan] = incl - seg + carry
        carry = carry + incl[:, scan - 1:scan]
    n_tiles = jnp.floor((carry + (tile_rows - 1.0)) * (1.0 / tile_rows))
    la = lax.broadcasted_iota(jnp.int32, (N_EXPERTS, N_EXPERTS), 0)
    lb = lax.broadcasted_iota(jnp.int32, (N_EXPERTS, N_EXPERTS), 1)
    lower = jnp.where(lb < la, 1.0, 0.0).astype(BF16)
    start = _dot(lower, jnp.broadcast_to(n_tiles, (N_EXPERTS, LANES)).astype(BF16))
    end = start + n_tiles
    slot = start[:, 0:1] * tile_rows + rank_ref[...]
    pos_ref[...] = jnp.zeros_like(pos_ref)
    pos_ref[0:1, :] = jnp.sum(jnp.where(sel1, slot, 0.0), axis=0, keepdims=True).astype(jnp.int32)
    pos_ref[1:2, :] = jnp.sum(jnp.where(sel2, slot, 0.0), axis=0, keepdims=True).astype(jnp.int32)

    n_used = end[N_EXPERTS - 1:N_EXPERTS, :]
    tile = lax.broadcasted_iota(jnp.int32, (N_EXPERTS, LANES), 1).astype(F32)
    tile = jnp.minimum(tile, n_used - 1.0)
    tile_expert = jnp.sum(jnp.where(tile >= end, 1.0, 0.0), axis=0, keepdims=True)
    lane = lax.broadcasted_iota(jnp.int32, (1, LANES), 1)
    meta_ref[...] = jnp.zeros_like(meta_ref)
    meta_ref[0:1, :] = jnp.where(lane == LANES - 1, n_used, tile_expert).astype(jnp.int32)
    diag = lax.broadcasted_iota(jnp.int32, (N_EXPERTS, LANES), 0) == lax.broadcasted_iota(
        jnp.int32, (N_EXPERTS, LANES), 1)
    meta_ref[1:2, :] = jnp.sum(jnp.where(diag, carry, 0.0), axis=0, keepdims=True).astype(jnp.int32)
    meta_ref[2:3, :] = jnp.sum(jnp.where(diag, start * tile_rows, 0.0), axis=0, keepdims=True).astype(jnp.int32)


def _route(logits_t, router_b, tile_rows, scan=512):
    ne, t = logits_t.shape
    return pl.pallas_call(
        functools.partial(_route_kernel, tile_rows=tile_rows, scan=scan),
        out_shape=[
            jax.ShapeDtypeStruct((8, t), jnp.int32),
            jax.ShapeDtypeStruct((8, t), F32),
            jax.ShapeDtypeStruct((8, LANES), jnp.int32),
        ],
        scratch_shapes=[pltpu.VMEM((ne, t), F32)],
        compiler_params=pltpu.CompilerParams(vmem_limit_bytes=VMEM_LIMIT_BYTES),
    )(logits_t, router_b.reshape(ne, 1))


def _dispatch_kernel(pos_ref, meta_ref, h_ref, xs_ref, zero_ref, sem, *, t, rpt, chunk, tile_rows, n_tiles):
    c = pl.program_id(0)

    def slot_rows(slot):
        return xs_ref.at[pl.ds(pl.multiple_of(slot * rpt, rpt), rpt)]

    def body(r, carry):
        tok = c * chunk + r
        src = h_ref.at[pl.ds(pl.multiple_of(r * rpt, rpt), rpt)]
        for k in range(2):
            pltpu.make_async_copy(src, slot_rows(pos_ref[k * t + tok]), sem.at[0]).start(priority=k)
        return carry

    lax.fori_loop(0, chunk, body, 0, unroll=8)

    @pl.when(c == 0)
    def _():
        zero_ref[...] = jnp.zeros_like(zero_ref)
        zero_row = zero_ref.at[pl.ds(0, rpt)]
        for e in range(N_EXPERTS):
            count = meta_ref[LANES + e]
            first = meta_ref[2 * LANES + e] + count
            n_pad = (tile_rows - count % tile_rows) % tile_rows

            def pad_start(r, carry, first=first):
                pltpu.make_async_copy(zero_row, slot_rows(first + r), sem.at[1]).start()
                return carry

            def pad_wait(r, carry, first=first):
                pltpu.make_async_copy(zero_row, slot_rows(first + r), sem.at[1]).wait()
                return carry

            lax.fori_loop(0, n_pad, pad_start, 0)
            lax.fori_loop(0, n_pad, pad_wait, 0)

        def tile_copy(n):
            rows = tile_rows * rpt
            return pltpu.make_async_copy(zero_ref, xs_ref.at[pl.ds(pl.multiple_of(n * rows, rows), rows)],
                                         sem.at[1])

        def unused_start(n, carry):
            tile_copy(n).start()
            return carry

        def unused_wait(n, carry):
            tile_copy(n).wait()
            return carry

        lax.fori_loop(meta_ref[LANES - 1], n_tiles, unused_start, 0)
        lax.fori_loop(meta_ref[LANES - 1], n_tiles, unused_wait, 0)

    for _ in range(2):
        pltpu.make_async_copy(h_ref, xs_ref.at[pl.ds(0, chunk * rpt)], sem.at[0]).wait()


def _dispatch(pos_flat, meta_flat, h_lin, n_slots, rpt, tile_rows, chunk=1024):
    t = pos_flat.shape[0] // 2
    return pl.pallas_call(
        functools.partial(_dispatch_kernel, t=t, rpt=rpt, chunk=chunk, tile_rows=tile_rows,
                          n_tiles=n_slots // tile_rows),
        grid_spec=pltpu.PrefetchScalarGridSpec(
            num_scalar_prefetch=2,
            grid=(t // chunk,),
            in_specs=[pl.BlockSpec((chunk * rpt, LANES), lambda c, p, m: (c, 0))],
            out_specs=pl.BlockSpec(memory_space=pl.ANY),
            scratch_shapes=[pltpu.VMEM((tile_rows * rpt, LANES), h_lin.dtype), pltpu.SemaphoreType.DMA((2,))],
        ),
        out_shape=jax.ShapeDtypeStruct((n_slots * rpt, LANES), h_lin.dtype),
        compiler_params=_params("arbitrary"),
    )(pos_flat, meta_flat, h_lin)


def _ffn_kernel(meta_ref, xs_ref, wg_ref, wu_ref, wd_ref, ys_ref, wgb_ref, wub_ref, wdb_ref, *, tm):
    n = pl.program_id(0)
    n_used = meta_ref[LANES - 1]

    @pl.when(n < n_used)
    def _():
        @pl.when((n == 0) | (meta_ref[n] != meta_ref[jnp.maximum(n - 1, 0)]))
        def _():
            wgb_ref[...] = wg_ref[...].astype(BF16)
            wub_ref[...] = wu_ref[...].astype(BF16)
            wdb_ref[...] = wd_ref[...].astype(BF16)

        d = wgb_ref.shape[0]
        a = b = None
        for k0 in range(0, d, MXU_DIM):
            x = jnp.concatenate(
                [_load_token_major(xs_ref, tm, k0 // LANES + j).astype(BF16) for j in range(MXU_DIM // LANES)],
                axis=1)
            pa = _dot(x, wgb_ref[k0:k0 + MXU_DIM, :])
            pb = _dot(x, wub_ref[k0:k0 + MXU_DIM, :])
            a = pa if a is None else a + pa
            b = pb if b is None else b + pb
        act = (_silu(a) * b).astype(BF16)
        for n0 in range(0, d, MXU_DIM):
            _store_token_major(ys_ref, _dot(act, wdb_ref[:, n0:n0 + MXU_DIM]), n0)

    @pl.when(n >= n_used)
    def _():
        ys_ref[...] = jnp.zeros_like(ys_ref)


def _ffn(meta, xs, w_gate, w_up, w_down, layer, tm, n_tiles):
    _, ne, d, f = w_gate.shape
    rpt = d // LANES

    def row_tile(n, m):
        return (jnp.minimum(n, m[LANES - 1] - 1), 0)

    return pl.pallas_call(
        functools.partial(_ffn_kernel, tm=tm),
        grid_spec=pltpu.PrefetchScalarGridSpec(
            num_scalar_prefetch=1,
            grid=(n_tiles,),
            in_specs=[
                pl.BlockSpec((tm * rpt, LANES), row_tile),
                pl.BlockSpec((None, None, d, f), lambda n, m: (layer, m[n], 0, 0)),
                pl.BlockSpec((None, None, d, f), lambda n, m: (layer, m[n], 0, 0)),
                pl.BlockSpec((None, None, f, d), lambda n, m: (layer, m[n], 0, 0)),
            ],
            out_specs=pl.BlockSpec((tm * rpt, LANES), lambda n, m: (n, 0)),
            scratch_shapes=[
                pltpu.VMEM((d, f), BF16),
                pltpu.VMEM((d, f), BF16),
                pltpu.VMEM((f, d), BF16),
            ],
        ),
        out_shape=jax.ShapeDtypeStruct((n_tiles * tm * rpt, LANES), F32),
        compiler_params=_params("arbitrary"),
    )(meta, xs, w_gate, w_up, w_down)


def _combine_kernel(pos_ref, ys_ref, *refs, t, rpt, pitch, tm, n_parts, final_tiles):
    x_refs = refs[:n_parts]
    if final_tiles is None:
        wt_ref, gate_ref, o_ref, a0, b0, a1, b1, sem = refs[n_parts:]
    else:
        wt_ref, gate_ref, fg_ref, op_ref, os_ref, a0, b0, a1, b1, sem = refs[n_parts:]
    i = pl.program_id(0)
    n = pl.num_programs(0)
    bufs = ((a0, b0), (a1, b1))

    def issue(tile, s):
        def body(r, carry):
            tok = tile * tm + r
            for k in range(2):
                slot_row = pos_ref[k * t + tok]
                pltpu.make_async_copy(
                    ys_ref.at[pl.ds(pl.multiple_of(slot_row * rpt, rpt), rpt)],
                    bufs[s][k].at[pl.ds(pl.multiple_of(r * pitch, 8), rpt)],
                    sem.at[s]).start(priority=k)
            return carry

        lax.fori_loop(0, tm, body, 0, unroll=8)

    def wait(s):
        for k in range(2):
            pltpu.make_async_copy(ys_ref.at[pl.ds(0, tm * rpt)], bufs[s][k].at[pl.ds(0, tm * rpt)],
                                  sem.at[s]).wait()

    def compute(s, out_ref):
        w = wt_ref[...]
        w0, w1 = w[:, 0:1], w[:, 1:2]
        dn = x_refs[0].shape[1]
        for c in range(rpt):
            cols = slice(c * LANES, (c + 1) * LANES)
            part, pc = divmod(c * LANES, dn)
            y = (w0 * bufs[s][0][pl.ds(c, tm, stride=pitch), :]
                 + w1 * bufs[s][1][pl.ds(c, tm, stride=pitch), :])
            out_ref[:, cols] = x_refs[part][:, pc:pc + LANES] + gate_ref[:, cols] * y
        if final_tiles is not None:
            out_ref[...] = _rms_f32(out_ref[...]) * fg_ref[...]

    @pl.when(i == 0)
    def _():
        issue(0, 0)

    for s in range(2):
        @pl.when(i % 2 == s)
        def _(s=s):
            @pl.when(i + 1 < n)
            def _():
                issue(i + 1, 1 - s)

            wait(s)
            if final_tiles is None:
                compute(s, o_ref)
            else:
                @pl.when(i < final_tiles)
                def _():
                    compute(s, op_ref)

                @pl.when(i >= final_tiles)
                def _():
                    compute(s, os_ref)


def _combine(pos_flat, ys, x1_parts, wt, mod3, layer, row_fn, final=None, tm=256):
    t, dn = x1_parts[0].shape
    n_parts = len(x1_parts)
    d = dn * n_parts
    rpt = d // LANES
    pitch = rpt + 8
    base = layer * 48
    in_specs = ([pl.BlockSpec(memory_space=pl.ANY)]
                + [pl.BlockSpec((tm, dn), lambda i, p: (i, 0))] * n_parts
                + [pl.BlockSpec((tm, 2), lambda i, p: (i, 0)),
                   pl.BlockSpec((None, 1, d), lambda i, p: (base + row_fn(i) * 6 + 5, 0, 0))])
    args = [pos_flat, ys, *x1_parts, wt, mod3]
    if final is None:
        final_tiles = None
        out_specs = pl.BlockSpec((tm, d), lambda i, p: (i, 0))
        out_shape = jax.ShapeDtypeStruct((t, d), F32)
    else:
        fg, t_ctx = final
        final_tiles = t_ctx // tm
        in_specs.append(pl.BlockSpec((1, d), lambda i, p: (0, 0)))
        args.append(fg.reshape(1, d))
        out_specs = [
            pl.BlockSpec((tm, d), lambda i, p: (jnp.minimum(i, final_tiles - 1), 0)),
            pl.BlockSpec((tm, d), lambda i, p: (jnp.maximum(i - final_tiles, 0), 0)),
        ]
        out_shape = [jax.ShapeDtypeStruct((t_ctx, d), F32), jax.ShapeDtypeStruct((t - t_ctx, d), F32)]
    return pl.pallas_call(
        functools.partial(_combine_kernel, t=t, rpt=rpt, pitch=pitch, tm=tm, n_parts=n_parts,
                          final_tiles=final_tiles),
        grid_spec=pltpu.PrefetchScalarGridSpec(
            num_scalar_prefetch=1,
            grid=(t // tm,),
            in_specs=in_specs,
            out_specs=out_specs,
            scratch_shapes=[pltpu.VMEM((tm * pitch, LANES), F32)] * 4 + [pltpu.SemaphoreType.DMA((2,))],
        ),
        out_shape=out_shape,
        compiler_params=_params("arbitrary"),
    )(*args)


def _moe(lg, router_b, h_lin, x1, w_gate, w_up, w_down, mod3, layer, row_fn, final=None, tile_rows=256):
    t = x1[0].shape[0]
    n_tiles = (2 * t) // tile_rows + N_EXPERTS
    pos, wts, meta = _route(lg, router_b, tile_rows)
    pos_flat = pos[:2].reshape(-1)
    xs = _dispatch(pos_flat, meta[:3].reshape(-1), h_lin, n_tiles * tile_rows, h_lin.shape[0] // t, tile_rows)
    ys = _ffn(meta[0], xs, w_gate, w_up, w_down, layer, tile_rows, n_tiles)
    return _combine(pos_flat, ys, x1, wts[:2].T, mod3, layer, row_fn, final)


def _rope_tables(n_tok):
    rows = n_tok // GRID_W
    pos_r = jnp.repeat(jnp.arange(rows), GRID_W).astype(F32)
    pos_c = jnp.tile(jnp.arange(GRID_W), rows).astype(F32)
    n_freq = DK_A // 4
    inv = jnp.power(ROPE_BASE, -jnp.arange(n_freq, dtype=F32) / n_freq)
    ang_r = pos_r[:, None] * inv
    ang_c = pos_c[:, None] * inv
    cos_t = jnp.concatenate([jnp.cos(ang_r), jnp.cos(ang_r), jnp.cos(ang_c), jnp.cos(ang_c)], axis=1)
    sin_t = jnp.concatenate([-jnp.sin(ang_r), jnp.sin(ang_r), -jnp.sin(ang_c), jnp.sin(ang_c)], axis=1)
    return cos_t, sin_t


def kernel(x_prompt, x_sample, cache_k, cache_v, state_ret, c, c_ctx, norm_g, w_mod, b_mod, w_in_even, w_out_even, lam_q, lam_k, ret_decay_logit, w_in_odd, gmlp_norm_g, w_spatial, b_spatial, w_out_odd, router_w, router_b, w_gate, w_up, w_down, final_norm_g):
    batch, seq, d = x_prompt.shape
    dbatch, dseq, _ = x_sample.shape
    past = cache_k.shape[2]
    depth = norm_g.shape[0]
    t_ctx, t_lat = batch * seq, dbatch * dseq
    t_all = t_ctx + t_lat
    xc = x_prompt.reshape(t_ctx, d)
    xl = x_sample.reshape(t_lat, d)

    cond8 = jnp.concatenate([c_ctx[None, :], c, jnp.zeros((8 - 1 - dbatch, d), F32)], axis=0)
    mod3 = _modulation(cond8, w_mod, b_mod).reshape(depth * 8 * 6, 1, d)
    router_wt = router_w.T
    cos_t, sin_t = _rope_tables(dseq)

    tm_in = 512
    tm_out = 512
    tm_cmb = 256
    w_in0 = w_in_even.reshape(d, -1)
    w_out0 = w_out_even.reshape(-1, d)
    w_in1 = w_in_odd.reshape(d, -1)
    w_out1 = w_out_odd.reshape(-1, d)
    lq, lk, dlog = lam_q.reshape(2, DK_A), lam_k.reshape(2, DK_A), ret_decay_logit.reshape(2, H_B)

    def rows_ctx(i):
        return 0

    def rows_lat(tiles_per_batch):
        return lambda i: 1 + i // tiles_per_batch

    def rows_all(tm):
        n_ctx, per = t_ctx // tm, dseq // tm
        return lambda i: jnp.maximum(i - n_ctx, 0) // per + jnp.where(i >= n_ctx, 1, 0)

    lam_init = 0.8 - 0.6 * math.exp(-0.3 * 0)
    half0 = w_in0.shape[1] // 2
    g00 = norm_g[0, 0]
    rl = rows_lat(dseq // tm_in)
    pa_c, k_new, v_new = _in_proj(xc, g00, mod3, 0, rows_ctx, w_in0, 0, half0, "even_kv", tm=tm_in)
    (pb_c,) = _in_proj(xc, g00, mod3, 0, rows_ctx, w_in0, half0, half0, "even", tm=tm_in)
    (pa_l,) = _in_proj(xl, g00, mod3, 0, rl, w_in0, 0, half0, "even", tm=tm_in)
    (pb_l,) = _in_proj(xl, g00, mod3, 0, rl, w_in0, half0, half0, "even", tm=tm_in)
    oa_c, or_c, s_new = _mixer(pa_c, pb_c, lq, lk, dlog, lam_init, batch, seq, H_A)
    ctx = (cache_k.reshape(dbatch, past, H_A * 2 * DK_A), cache_v.reshape(dbatch, past, H_A * DV_A),
           state_ret.reshape(dbatch, 2, H_B, DK_B, DV_B), cos_t, sin_t)
    oa_l, or_l = _mixer(pa_l, pb_l, lq, lk, dlog, lam_init, dbatch, dseq, 1, ctx)
    x1, h2, lg = _out_proj([([oa_c, or_c], xc), ([oa_l, or_l], xl)], w_out0, norm_g[0, 1], mod3, 0,
                           rows_all(tm_out), router_wt, tm=tm_out)
    x2 = _moe(lg, router_b, h2, x1, w_gate, w_up, w_down, mod3, 0, rows_all(tm_cmb))

    q1 = w_in1.shape[1] // 4
    g10 = norm_g[1, 0]
    ra = rows_all(tm_in)
    v_blk, v_ss = zip(*[_in_proj(x2, g10, mod3, 1, ra, w_in1, (2 + j) * q1, q1, "gmlp_v", tm=tm_in)
                        for j in range(2)])
    gmlp = (gmlp_norm_g.reshape(1, -1), w_spatial.reshape(H_C, CHUNK_C, CHUNK_C),
            b_spatial.reshape(H_C, CHUNK_C, 1))
    g0, g1 = [_in_proj(x2, g10, mod3, 1, ra, w_in1, j * q1, q1, "gmlp_u",
                       gating=(v_blk[j], v_ss[0], v_ss[1], *gmlp, j), tm=tm_in)[0] for j in range(2)]
    x3, h4, lg2 = _out_proj([([g0, g1], x2)], w_out1, norm_g[1, 1], mod3, 1, rows_all(tm_out), router_wt,
                            tm=tm_out)
    y_p, y_s = _moe(lg2, router_b, h4, x3, w_gate, w_up, w_down, mod3, 1, rows_all(tm_cmb),
                    final=(final_norm_g, t_ctx))

    y_prompt = y_p.reshape(batch, seq, d)
    y_sample = y_s.reshape(dbatch, dseq, d)
    new_cache_k = k_new.reshape(batch, 1, seq, H_A, 2, DK_A)
    new_cache_v = v_new.reshape(batch, 1, seq, DV_A // LANES, H_A, LANES).swapaxes(3, 4).reshape(
        batch, 1, seq, H_A, DV_A)
    new_state_ret = s_new.reshape(batch, 1, 2, H_B, DK_B, DV_B)
    return (y_prompt, y_sample, new_cache_k, new_cache_v, new_state_ret)
```

```python
import functools
import math

import jax
import jax.numpy as jnp
from jax import lax
from jax.experimental import pallas as pl
from jax.experimental.pallas import tpu as pltpu

F32 = jnp.float32
BF16 = jnp.bfloat16

EPS = 1e-6
ROPE_BASE = 10000.0
GRID_W = 64
H_A, DK_A, DV_A = 4, 128, 256
H_B, DK_B, DV_B = 4, 128, 256
RET_CHUNK = 128
H_C, CHUNK_C = 8, 128
N_EXPERTS, N_GROUPS, GROUP_SIZE = 16, 4, 4

VMEM_LIMIT_BYTES = 56 * 1024 * 1024


def _params(*sem):
    return pltpu.CompilerParams(dimension_semantics=sem, vmem_limit_bytes=VMEM_LIMIT_BYTES)


def _rms_f32(x):
    return x * lax.rsqrt(jnp.mean(x * x, axis=-1, keepdims=True) + EPS)


def _silu(x):
    return x * jax.nn.sigmoid(x)


def _dot(a, b):
    return jnp.dot(a, b, preferred_element_type=F32)


def _dot_nt(a, b):
    return lax.dot_general(a, b, (((1,), (1,)), ((), ())), preferred_element_type=F32)


def _dot_nt_x3(a, b):
    ah = a.astype(BF16)
    al = (a - ah.astype(F32)).astype(BF16)
    bh = b.astype(BF16)
    bl = (b - bh.astype(F32)).astype(BF16)
    return _dot_nt(ah, bh) + (_dot_nt(al, bh) + _dot_nt(ah, bl))


LANES = 128
MXU_DIM = 256


def _store_token_major(ref, val, col0=0):
    tm, width = val.shape
    rpt = ref.shape[0] // tm
    for c in range(width // LANES):
        ref[pl.ds(col0 // LANES + c, tm, stride=rpt), :] = val[:, c * LANES:(c + 1) * LANES]


def _load_token_major(ref, tm, c):
    rpt = ref.shape[0] // tm
    return ref[pl.ds(c, tm, stride=rpt), :]


def _mod_kernel(c_ref, w_ref, b_ref, o_ref):
    s = _silu(c_ref[...]).astype(BF16)
    o_ref[...] = _dot(s, w_ref[...].astype(BF16)) + b_ref[...]


def _modulation(cond8, w_mod, b_mod, tn=1024):
    depth, d, n = w_mod.shape
    return pl.pallas_call(
        _mod_kernel,
        grid=(depth, n // tn),
        in_specs=[
            pl.BlockSpec((8, d), lambda l, j: (0, 0)),
            pl.BlockSpec((None, d, tn), lambda l, j: (l, 0, j)),
            pl.BlockSpec((None, 1, tn), lambda l, j: (l, 0, j)),
        ],
        out_specs=pl.BlockSpec((None, 8, tn), lambda l, j: (l, 0, j)),
        out_shape=jax.ShapeDtypeStruct((depth, 8, n), F32),
        compiler_params=_params("arbitrary", "arbitrary"),
    )(cond8, w_mod, b_mod.reshape(depth, 1, n))


def _load_weight_slab(w_hbm, slab_ref, stage_ref, sem, col0):
    kdim, n = slab_ref.shape
    rows = stage_ref.shape[1]
    nb = kdim // rows

    def copy(j):
        return pltpu.make_async_copy(w_hbm.at[pl.ds(j * rows, rows), pl.ds(col0, n)],
                                     stage_ref.at[j % 2], sem.at[j % 2])

    copy(0).start()
    for j in range(nb):
        if j + 1 < nb:
            copy(j + 1).start()
        copy(j).wait()
        slab_ref[j * rows:(j + 1) * rows, :] = stage_ref[j % 2].astype(BF16)


def _gelu(x):
    return 0.5 * x * (1.0 + lax.erf(x * (2.0 ** -0.5)))


def _in_kernel(x_ref, g_ref, sh_ref, sc_ref, w_hbm, *rest, col0, mode, kv_cols, chunk, e_c):
    if mode == "even_kv":
        p_ref, k_ref, v_ref, slab_ref, stage_ref, h_ref, sem = rest
    elif mode == "gmlp_v":
        p_ref, ss_ref, slab_ref, stage_ref, h_ref, sem = rest
    elif mode == "gmlp_u":
        vb_ref, sa_ref, sb_ref, ng_ref, ws_ref, bs_ref, p_ref, slab_ref, stage_ref, h_ref, vn_ref, sem = rest
    else:
        p_ref, slab_ref, stage_ref, h_ref, sem = rest

    @pl.when(pl.program_id(0) == 0)
    def _():
        _load_weight_slab(w_hbm, slab_ref, stage_ref, sem, col0)

    h = (_rms_f32(x_ref[...]) * g_ref[...]) * (1.0 + sc_ref[...]) + sh_ref[...]
    h_ref[...] = h.astype(BF16)
    n = slab_ref.shape[1]
    tm = h_ref.shape[0]
    if mode == "gmlp_u":
        inv = lax.rsqrt((sa_ref[...] + sb_ref[...]) / e_c + EPS)
        vn_ref[...] = ((vb_ref[...].astype(F32) * inv) * ng_ref[...]).astype(BF16)
    ss = None
    for ci, c0 in enumerate(range(0, n, chunk)):
        acc = _dot(h_ref[...], slab_ref[:, c0:c0 + chunk])
        if mode == "gmlp_v":
            v = _gelu(acc)
            p_ref[:, c0:c0 + chunk] = v.astype(BF16)
            part = jnp.sum(v * v, axis=-1, keepdims=True)
            ss = part if ss is None else ss + part
        elif mode == "gmlp_u":
            u = _gelu(acc)
            wsb = ws_ref[ci].astype(BF16)
            for r0 in range(0, tm, CHUNK_C):
                mixed = _dot(wsb, vn_ref[r0:r0 + CHUNK_C, c0:c0 + chunk]) + bs_ref[ci]
                p_ref[r0:r0 + CHUNK_C, c0:c0 + chunk] = (u[r0:r0 + CHUNK_C] * mixed).astype(BF16)
        else:
            p_ref[:, c0:c0 + chunk] = acc.astype(BF16)
        if mode == "even_kv":
            (k0, k1), (v0, v1) = kv_cols
            a0 = col0 + c0
            if k0 <= a0 < k1:
                _store_token_major(k_ref, acc, a0 - k0)
            if v0 <= a0 < v1:
                tm = acc.shape[0]
                rpt = v_ref.shape[0] // tm
                for cc in range(chunk // LANES):
                    head, half = divmod((a0 - v0) // LANES + cc, DV_A // LANES)
                    v_ref[pl.ds(half * H_A + head, tm, stride=rpt), :] = acc[:, cc * LANES:(cc + 1) * LANES]
    if mode == "gmlp_v":
        ss_ref[...] = ss


def _in_proj(x, g, mod3, layer, row_fn, w, col0, n_cols, mode, gating=None, tm=512, chunk=512, stage_rows=256):
    t, d = x.shape
    base = layer * 48

    def mod_spec(kind):
        return pl.BlockSpec((None, 1, d), lambda i: (base + row_fn(i) * 6 + kind, 0, 0))

    in_specs = [
        pl.BlockSpec((tm, d), lambda i: (i, 0)),
        pl.BlockSpec((1, d), lambda i: (0, 0)),
        mod_spec(0),
        mod_spec(1),
        pl.BlockSpec(memory_space=pl.ANY),
    ]
    args = [x, g.reshape(1, d), mod3, mod3, w]
    out_specs = [pl.BlockSpec((tm, n_cols), lambda i: (i, 0))]
    out_shape = [jax.ShapeDtypeStruct((t, n_cols), BF16)]
    scratch = [
        pltpu.VMEM((d, n_cols), BF16),
        pltpu.VMEM((2, stage_rows, n_cols), F32),
        pltpu.VMEM((tm, d), BF16),
    ]
    kv_cols = None
    e_c = None
    if mode == "even_kv":
        a_qk = H_A * 2 * DK_A
        a_v = H_A * DV_A
        kv_cols = ((a_qk, 2 * a_qk), (2 * a_qk, 2 * a_qk + a_v))
        for width in (a_qk, a_v):
            out_specs.append(pl.BlockSpec((tm * width // LANES, LANES), lambda i: (i, 0)))
            out_shape.append(jax.ShapeDtypeStruct((t * width // LANES, LANES), F32))
    elif mode == "gmlp_v":
        out_specs.append(pl.BlockSpec((tm, 1), lambda i: (i, 0)))
        out_shape.append(jax.ShapeDtypeStruct((t, 1), F32))
    elif mode == "gmlp_u":
        v_blk, ss_a, ss_b, ng, w_s, b_s, blk = gating
        e_c = ng.shape[1]
        groups = w_s.shape[0] * n_cols // e_c
        chunk = n_cols // groups
        in_specs += [
            pl.BlockSpec((tm, n_cols), lambda i: (i, 0)),
            pl.BlockSpec((tm, 1), lambda i: (i, 0)),
            pl.BlockSpec((tm, 1), lambda i: (i, 0)),
            pl.BlockSpec((1, n_cols), lambda i: (0, blk)),
            pl.BlockSpec((groups, CHUNK_C, CHUNK_C), lambda i: (blk, 0, 0)),
            pl.BlockSpec((groups, CHUNK_C, 1), lambda i: (blk, 0, 0)),
        ]
        args += [v_blk, ss_a, ss_b, ng, w_s, b_s]
        scratch.append(pltpu.VMEM((tm, n_cols), BF16))
    return pl.pallas_call(
        functools.partial(_in_kernel, col0=col0, mode=mode, kv_cols=kv_cols, chunk=chunk, e_c=e_c),
        grid=(t // tm,),
        in_specs=in_specs,
        out_specs=out_specs,
        out_shape=out_shape,
        scratch_shapes=scratch + [pltpu.SemaphoreType.DMA((2,))],
        compiler_params=_params("arbitrary"),
    )(*args)


def _rope(x, c, s):
    lane = lax.broadcasted_iota(jnp.int32, x.shape, 1)
    partner = jnp.where((lane & 32) == 0, pltpu.roll(x, 96, 1), pltpu.roll(x, 32, 1))
    return x * c + partner * s


def _mixer_kernel(*refs, seq, has_ctx, lam_init, past, hps):
    if has_ctx:
        (aq_ref, ak_ref, av_ref, bq_ref, bk_ref, bv_ref, bg_ref, lq_ref, lk_ref, dl_ref,
         ck_ref, cv_ref, s0_ref, cos_ref, sin_ref,
         oa_ref, or_ref, kall_ref, vall_ref, of_ref, ob_ref, stf_ref, stb_ref) = refs
    else:
        (aq_ref, ak_ref, av_ref, bq_ref, bk_ref, bv_ref, bg_ref, lq_ref, lk_ref, dl_ref,
         oa_ref, or_ref, sout_ref, of_ref, ob_ref, stf_ref, stb_ref) = refs
    rt = 256
    straight = seq <= rt
    tq = 256 if straight else 512
    cs = RET_CHUNK
    nc = seq // cs

    prod = lq_ref[...] * lk_ref[...]
    ee = jnp.exp(jnp.sum(prod, axis=1, keepdims=True))
    lam = ee[0:1, :] - ee[1:2, :] + lam_init
    scale = DK_A ** -0.5
    ii = lax.broadcasted_iota(jnp.int32, (cs, cs), 0)
    jj = lax.broadcasted_iota(jnp.int32, (cs, cs), 1)
    ic = lax.broadcasted_iota(jnp.int32, (cs, 1), 0).astype(F32)

    def loop(n, body):
        if straight:
            for it in range(n):
                body(it)
        else:
            lax.fori_loop(0, n, lambda it, c: (body(it), c)[1], 0)

    def rows(it, size):
        return pl.ds(it * size, size) if straight else pl.ds(pl.multiple_of(it * size, size), size)

    for hh in range(hps):
        qk_cols = slice(hh * 2 * DK_A, (hh + 1) * 2 * DK_A)
        va_cols = slice(hh * DV_A, (hh + 1) * DV_A)
        kb_cols = slice(hh * DK_B, (hh + 1) * DK_B)
        vb_cols = slice(hh * DV_B, (hh + 1) * DV_B)

        if has_ctx:
            for c in range(2):
                kall_ref[c, 0:past, :] = ck_ref[:, c * DK_A:(c + 1) * DK_A].astype(BF16)
            vall_ref[0:past, :] = cv_ref[...].astype(BF16)

            def copy_body(r):
                rr = rows(r, rt)
                cs_, sn_ = cos_ref[rr, :], sin_ref[rr, :]
                kk = ak_ref[rr, qk_cols].astype(F32)
                for c in range(2):
                    kall_ref[c, pl.ds(past + pl.multiple_of(r * rt, rt), rt), :] = _rope(
                        kk[:, c * DK_A:(c + 1) * DK_A], cs_, sn_).astype(BF16)
                vall_ref[pl.ds(past + pl.multiple_of(r * rt, rt), rt), :] = av_ref[rr, va_cols]

            loop(seq // rt, copy_body)

        def attn_body(qt, qk_cols=qk_cols, va_cols=va_cols):
            rr = rows(qt, tq)
            q = aq_ref[rr, qk_cols]
            if has_ctx:
                qf = q.astype(F32)
                cs_, sn_ = cos_ref[rr, :], sin_ref[rr, :]
            probs = []
            for c in range(2):
                if has_ctx:
                    qc = _rope(qf[:, c * DK_A:(c + 1) * DK_A], cs_, sn_).astype(BF16)
                    keys = kall_ref[c]
                else:
                    qc = q[:, c * DK_A:(c + 1) * DK_A]
                    keys = ak_ref[:, hh * 2 * DK_A + c * DK_A:hh * 2 * DK_A + (c + 1) * DK_A]
                s = _dot_nt(qc, keys) * scale
                e = jnp.exp(s - jnp.max(s, axis=-1, keepdims=True))
                probs.append(e * (1.0 / jnp.sum(e, axis=-1, keepdims=True)))
            a = (probs[0] - lam * probs[1]).astype(BF16)
            o = _dot(a, vall_ref[...] if has_ctx else av_ref[:, va_cols])
            oa_ref[rr, va_cols] = (_rms_f32(o) * (1.0 - lam_init)).astype(BF16)

        tables = []
        for d in range(2):
            lg = jax.nn.log_sigmoid(dl_ref[d, hh])
            diff = (ii - jj if d == 0 else jj - ii).astype(F32)
            mask = jnp.where(diff >= 0, jnp.exp(jnp.maximum(diff, 0.0) * lg), 0.0)
            if d == 0:
                qdec = jnp.exp((ic + 1.0) * lg)
                kdec = jnp.exp((cs - 1.0 - ic) * lg)
            else:
                qdec = jnp.exp((cs - ic) * lg)
                kdec = jnp.exp(ic * lg)
            tables.append((mask, qdec, kdec, jnp.exp(cs * lg)))
        st_refs = (stf_ref, stb_ref)
        o_refs = (of_ref, ob_ref)
        for d in range(2):
            if has_ctx:
                st_refs[d][hh] = s0_ref[d]
            else:
                st_refs[d][hh] = jnp.zeros((DK_B, DV_B), F32)

        def chunk_body(n, hh=hh, kb_cols=kb_cols, vb_cols=vb_cols, tables=tables):
            for d in range(2):
                mask, qdec, kdec, cdec = tables[d]
                rr = rows(n if d == 0 else nc - 1 - n, cs)
                qc = bq_ref[rr, kb_cols].astype(F32)
                kc = bk_ref[rr, kb_cols].astype(F32) * (DK_B ** -0.5)
                vc = bv_ref[rr, vb_cols]
                state = st_refs[d][hh]
                qk = _dot_nt(qc.astype(BF16), kc.astype(BF16)) * mask
                o_refs[d][rr, vb_cols] = (_dot(qk.astype(BF16), vc)
                                          + _dot((qc * qdec).astype(BF16), state.astype(BF16)))
                kd = (kc * kdec).T.astype(BF16)
                st_refs[d][hh] = cdec * state + _dot(kd, vc)

        n_q = seq // tq
        per_q = nc // n_q

        def fused_body(qt, attn_body=attn_body, chunk_body=chunk_body):
            attn_body(qt)
            for j in range(per_q):
                chunk_body(qt * per_q + j)

        loop(n_q, fused_body)

        def final_body(r, vb_cols=vb_cols):
            rr = rows(r, rt)
            tot = of_ref[rr, vb_cols] + ob_ref[rr, vb_cols]
            gate = _silu(bg_ref[rr, vb_cols].astype(F32))
            or_ref[rr, vb_cols] = (_rms_f32(tot) * gate).astype(BF16)

        loop(seq // rt, final_body)
        if not has_ctx:
            for d in range(2):
                sout_ref[d, hh] = st_refs[d][hh]


def _mixer(pa, pb, lam_q, lam_k, decay_logit, lam_init, batch, seq, hps, ctx=None):
    has_ctx = ctx is not None
    past = ctx[0].shape[1] if has_ctx else 0

    def col(head_width, off):
        width = hps * head_width
        return pl.BlockSpec((seq, width), lambda b, hg: (b, off // width + hg))

    a_qk = H_A * 2 * DK_A
    b_qk = H_B * DK_B
    b_v = H_B * DV_B
    in_specs = [
        col(2 * DK_A, 0), col(2 * DK_A, a_qk), col(DV_A, 2 * a_qk),
        col(DK_B, 0), col(DK_B, b_qk), col(DV_B, 2 * b_qk), col(DV_B, 2 * b_qk + b_v),
        pl.BlockSpec((2, DK_A), lambda b, hg: (0, 0)),
        pl.BlockSpec((2, DK_A), lambda b, hg: (0, 0)),
        pl.BlockSpec((2, hps, 1, 1), lambda b, hg: (0, hg, 0, 0)),
    ]
    args = [pa] * 3 + [pb] * 4 + [lam_q, lam_k, decay_logit.reshape(2, H_B, 1, 1)]
    out_specs = [
        pl.BlockSpec((seq, hps * DV_A), lambda b, hg: (b, hg)),
        pl.BlockSpec((seq, hps * DV_B), lambda b, hg: (b, hg)),
    ]
    out_shape = [
        jax.ShapeDtypeStruct((batch * seq, H_A * DV_A), BF16),
        jax.ShapeDtypeStruct((batch * seq, b_v), BF16),
    ]
    scratch = []
    if has_ctx:
        assert hps == 1
        ck, cv, s0, cos_t, sin_t = ctx
        in_specs += [
            pl.BlockSpec((None, past, 2 * DK_A), lambda b, hg: (b, 0, hg)),
            pl.BlockSpec((None, past, DV_A), lambda b, hg: (b, 0, hg)),
            pl.BlockSpec((None, 2, None, DK_B, DV_B), lambda b, hg: (b, 0, hg, 0, 0)),
            pl.BlockSpec((seq, DK_A), lambda b, hg: (0, 0)),
            pl.BlockSpec((seq, DK_A), lambda b, hg: (0, 0)),
        ]
        args += [ck, cv, s0, cos_t, sin_t]
        scratch += [pltpu.VMEM((2, seq + past, DK_A), BF16), pltpu.VMEM((seq + past, DV_A), BF16)]
    else:
        out_specs.append(pl.BlockSpec((None, 2, hps, DK_B, DV_B), lambda b, hg: (b, 0, hg, 0, 0)))
        out_shape.append(jax.ShapeDtypeStruct((batch, 2, H_B, DK_B, DV_B), F32))
    scratch += [
        pltpu.VMEM((seq, hps * DV_B), F32),
        pltpu.VMEM((seq, hps * DV_B), F32),
        pltpu.VMEM((hps, DK_B, DV_B), F32),
        pltpu.VMEM((hps, DK_B, DV_B), F32),
    ]
    return pl.pallas_call(
        functools.partial(_mixer_kernel, seq=seq, has_ctx=has_ctx, lam_init=lam_init, past=past, hps=hps),
        grid=(batch, H_A // hps),
        in_specs=in_specs,
        out_specs=out_specs,
        out_shape=out_shape,
        scratch_shapes=scratch,
        compiler_params=_params("arbitrary", "arbitrary"),
    )(*args)


def _out_kernel(*refs, seg_tiles, n_lhs, part, n_parts):
    n_seg = len(seg_tiles)
    per_seg = n_lhs + 1
    seg_refs = [refs[s * per_seg:(s + 1) * per_seg] for s in range(n_seg)]
    rest = refs[n_seg * per_seg:]
    last = part == n_parts - 1
    if last:
        gate_ref, g_ref, sh_ref, sc_ref, rw_ref = rest[:5]
        prev_refs = rest[5:5 + part]
        w_hbm, x1_ref, h2_ref, lg_ref, slab_ref, stage_ref, sem = rest[5 + part:]
    else:
        gate_ref, w_hbm, x1_ref, slab_ref, stage_ref, sem = rest
    i = pl.program_id(0)
    dn = slab_ref.shape[1]

    @pl.when(i == 0)
    def _():
        _load_weight_slab(w_hbm, slab_ref, stage_ref, sem, part * dn)

    def run(lhs_refs, x_ref):
        acc = None
        r0 = 0
        for lhs_ref in lhs_refs:
            kk = lhs_ref.shape[1]
            prod = _dot(lhs_ref[...], slab_ref[r0:r0 + kk, :])
            acc = prod if acc is None else acc + prod
            r0 += kk
        x1 = x_ref[...] + gate_ref[...] * acc
        x1_ref[...] = x1
        if last:
            cols = [p_ref[...] for p_ref in prev_refs] + [x1]
            d = dn * n_parts
            ms = sum(jnp.sum(v * v, axis=-1, keepdims=True) for v in cols) / d
            inv = lax.rsqrt(ms + EPS)
            lg = None
            for j, v in enumerate(cols):
                cs_ = slice(j * dn, (j + 1) * dn)
                h2 = ((v * inv) * g_ref[:, cs_]) * (1.0 + sc_ref[:, cs_]) + sh_ref[:, cs_]
                _store_token_major(h2_ref, h2, j * dn)
                pl_ = _dot_nt_x3(rw_ref[:, cs_], h2)
                lg = pl_ if lg is None else lg + pl_
            lg_ref[...] = lg

    if n_seg == 1:
        run(seg_refs[0][:n_lhs], seg_refs[0][n_lhs])
    else:
        lo = 0
        for s in range(n_seg):
            @pl.when((i >= lo) & (i < lo + seg_tiles[s]))
            def _(s=s):
                run(seg_refs[s][:n_lhs], seg_refs[s][n_lhs])

            lo += seg_tiles[s]


def _out_proj(segments, w, g, mod3, layer, row_fn, router_wt, n_parts=2, tm=512, stage_rows=256):
    n_lhs = len(segments[0][0])
    d = segments[0][1].shape[1]
    dn = d // n_parts
    kdim = w.shape[0]
    base = layer * 48
    ne = router_wt.shape[0]
    seg_tiles = tuple(x.shape[0] // tm for _, x in segments)
    t_total = sum(x.shape[0] for _, x in segments)
    rpt = d // LANES
    x1_parts = []
    for part in range(n_parts):
        last = part == n_parts - 1
        in_specs, args = [], []
        lo = 0
        for (lhs_list, x), nt in zip(segments, seg_tiles):
            def seg_row(i, lo=lo, nt=nt):
                return jnp.clip(i - lo, 0, nt - 1)

            for lhs in lhs_list:
                in_specs.append(pl.BlockSpec((tm, lhs.shape[1]), lambda i, f=seg_row: (f(i), 0)))
                args.append(lhs)
            in_specs.append(pl.BlockSpec((tm, dn), lambda i, f=seg_row: (f(i), part)))
            args.append(x)
            lo += nt
        in_specs.append(pl.BlockSpec((None, 1, dn), lambda i: (base + row_fn(i) * 6 + 2, 0, part)))
        args.append(mod3)
        out_specs = [pl.BlockSpec((tm, dn), lambda i: (i, 0))]
        out_shape = [jax.ShapeDtypeStruct((t_total, dn), F32)]
        if last:
            in_specs += [
                pl.BlockSpec((1, d), lambda i: (0, 0)),
                pl.BlockSpec((None, 1, d), lambda i: (base + row_fn(i) * 6 + 3, 0, 0)),
                pl.BlockSpec((None, 1, d), lambda i: (base + row_fn(i) * 6 + 4, 0, 0)),
                pl.BlockSpec((ne, d), lambda i: (0, 0)),
            ] + [pl.BlockSpec((tm, dn), lambda i: (i, 0))] * part
            args += [g.reshape(1, d), mod3, mod3, router_wt] + x1_parts
            out_specs += [
                pl.BlockSpec((tm * rpt, LANES), lambda i: (i, 0)),
                pl.BlockSpec((ne, tm), lambda i: (0, i)),
            ]
            out_shape += [
                jax.ShapeDtypeStruct((t_total * rpt, LANES), F32),
                jax.ShapeDtypeStruct((ne, t_total), F32),
            ]
        in_specs.append(pl.BlockSpec(memory_space=pl.ANY))
        args.append(w)
        outs = pl.pallas_call(
            functools.partial(_out_kernel, seg_tiles=seg_tiles, n_lhs=n_lhs, part=part, n_parts=n_parts),
            grid=(t_total // tm,),
            in_specs=in_specs,
            out_specs=out_specs,
            out_shape=out_shape,
            scratch_shapes=[
                pltpu.VMEM((kdim, dn), BF16),
                pltpu.VMEM((2, stage_rows, dn), F32),
                pltpu.SemaphoreType.DMA((2,)),
            ],
            compiler_params=_params("arbitrary"),
        )(*args)
        x1_parts.append(outs[0])
    return x1_parts, outs[1], outs[2]


def _first_index(vals, target):
    idx = jnp.full(target.shape, len(vals) - 1, jnp.int32)
    for k in range(len(vals) - 2, -1, -1):
        idx = jnp.where(vals[k] == target, k, idx)
    return idx


def _pick(vals, idx):
    out = vals[-1]
    for k in range(len(vals) - 2, -1, -1):
        out = jnp.where(idx == k, vals[k], out)
    return out


def _route_kernel(lg_ref, rb_ref, pos_ref, wt_ref, meta_ref, rank_ref, *, tile_rows, scan):
    logits = lg_ref[...]
    t = logits.shape[1]
    scores = jax.nn.sigmoid(logits)
    biased = scores + rb_ref[...]
    s_rows = [scores[e:e + 1, :] for e in range(N_EXPERTS)]
    b_rows = [biased[e:e + 1, :] for e in range(N_EXPERTS)]
    grp = []
    for g in range(N_GROUPS):
        r = b_rows[g * GROUP_SIZE:(g + 1) * GROUP_SIZE]
        best = None
        for a in range(GROUP_SIZE):
            for b in range(a + 1, GROUP_SIZE):
                pair = r[a] + r[b]
                best = pair if best is None else jnp.maximum(best, pair)
        grp.append(best)
    gmax = functools.reduce(jnp.maximum, grp)
    gsel = _first_index(grp, gmax)
    cand_b = [_pick([b_rows[g * GROUP_SIZE + k] for g in range(N_GROUPS)], gsel) for k in range(GROUP_SIZE)]
    cand_s = [_pick([s_rows[g * GROUP_SIZE + k] for g in range(N_GROUPS)], gsel) for k in range(GROUP_SIZE)]
    m1 = functools.reduce(jnp.maximum, cand_b)
    k1 = _first_index(cand_b, m1)
    rest = [jnp.where(k1 == k, -jnp.inf, cand_b[k]) for k in range(GROUP_SIZE)]
    m2 = functools.reduce(jnp.maximum, rest)
    k2 = _first_index(rest, m2)
    w1 = _pick(cand_s, k1)
    w2 = _pick(cand_s, k2)
    wsum = w1 + w2
    w1 = w1 / wsum
    w2 = w2 / wsum
    e1 = gsel * GROUP_SIZE + k1
    e2 = gsel * GROUP_SIZE + k2
    wt_ref[...] = jnp.zeros_like(wt_ref)
    wt_ref[0:1, :] = w1
    wt_ref[1:2, :] = w2

    eid = lax.broadcasted_iota(jnp.int32, logits.shape, 0)
    sel1 = eid == e1
    sel2 = eid == e2
    ind = jnp.where(sel1 | sel2, 1.0, 0.0)
    ua = lax.broadcasted_iota(jnp.int32, (scan, scan), 0)
    ub = lax.broadcasted_iota(jnp.int32, (scan, scan), 1)
    upper = jnp.where(ua <= ub, 1.0, 0.0).astype(BF16)
    carry = jnp.zeros((N_EXPERTS, 1), F32)
    for blk in range(t // scan):
        seg = ind[:, blk * scan:(blk + 1) * scan]
        incl = _dot(seg.astype(BF16), upper)
        rank_ref[:, blk * scan:(blk + 1) * scan] = incl - seg + carry
        carry = carry + incl[:, scan - 1:scan]
    n_tiles = jnp.floor((carry + (tile_rows - 1.0)) * (1.0 / tile_rows))
    la = lax.broadcasted_iota(jnp.int32, (N_EXPERTS, N_EXPERTS), 0)
    lb = lax.broadcasted_iota(jnp.int32, (N_EXPERTS, N_EXPERTS), 1)
    lower = jnp.where(lb < la, 1.0, 0.0).astype(BF16)
    start = _dot(lower, jnp.broadcast_to(n_tiles, (N_EXPERTS, LANES)).astype(BF16))
    end = start + n_tiles
    slot = start[:, 0:1] * tile_rows + rank_ref[...]
    pos_ref[...] = jnp.zeros_like(pos_ref)
    pos_ref[0:1, :] = jnp.sum(jnp.where(sel1, slot, 0.0), axis=0, keepdims=True).astype(jnp.int32)
    pos_ref[1:2, :] = jnp.sum(jnp.where(sel2, slot, 0.0), axis=0, keepdims=True).astype(jnp.int32)

    n_used = end[N_EXPERTS - 1:N_EXPERTS, :]
    tile = lax.broadcasted_iota(jnp.int32, (N_EXPERTS, LANES), 1).astype(F32)
    tile = jnp.minimum(tile, n_used - 1.0)
    tile_expert = jnp.sum(jnp.where(tile >= end, 1.0, 0.0), axis=0, keepdims=True)
    lane = lax.broadcasted_iota(jnp.int32, (1, LANES), 1)
    meta_ref[...] = jnp.zeros_like(meta_ref)
    meta_ref[0:1, :] = jnp.where(lane == LANES - 1, n_used, tile_expert).astype(jnp.int32)
    diag = lax.broadcasted_iota(jnp.int32, (N_EXPERTS, LANES), 0) == lax.broadcasted_iota(
        jnp.int32, (N_EXPERTS, LANES), 1)
    meta_ref[1:2, :] = jnp.sum(jnp.where(diag, carry, 0.0), axis=0, keepdims=True).astype(jnp.int32)
    meta_ref[2:3, :] = jnp.sum(jnp.where(diag, start * tile_rows, 0.0), axis=0, keepdims=True).astype(jnp.int32)


def _route(logits_t, router_b, tile_rows, scan=512):
    ne, t = logits_t.shape
    return pl.pallas_call(
        functools.partial(_route_kernel, tile_rows=tile_rows, scan=scan),
        out_shape=[
            jax.ShapeDtypeStruct((8, t), jnp.int32),
            jax.ShapeDtypeStruct((8, t), F32),
            jax.ShapeDtypeStruct((8, LANES), jnp.int32),
        ],
        scratch_shapes=[pltpu.VMEM((ne, t), F32)],
        compiler_params=pltpu.CompilerParams(vmem_limit_bytes=VMEM_LIMIT_BYTES),
    )(logits_t, router_b.reshape(ne, 1))


def _dispatch_kernel(pos_ref, meta_ref, h_ref, xs_ref, zero_ref, sem, *, t, rpt, chunk, tile_rows, n_tiles):
    c = pl.program_id(0)

    def slot_rows(slot):
        return xs_ref.at[pl.ds(pl.multiple_of(slot * rpt, rpt), rpt)]

    def body(r, carry):
        tok = c * chunk + r
        src = h_ref.at[pl.ds(pl.multiple_of(r * rpt, rpt), rpt)]
        for k in range(2):
            pltpu.make_async_copy(src, slot_rows(pos_ref[k * t + tok]), sem.at[0]).start(priority=k)
        return carry

    lax.fori_loop(0, chunk, body, 0, unroll=8)

    @pl.when(c == 0)
    def _():
        zero_ref[...] = jnp.zeros_like(zero_ref)
        zero_row = zero_ref.at[pl.ds(0, rpt)]
        for e in range(N_EXPERTS):
            count = meta_ref[LANES + e]
            first = meta_ref[2 * LANES + e] + count
            n_pad = (tile_rows - count % tile_rows) % tile_rows

            def pad_start(r, carry, first=first):
                pltpu.make_async_copy(zero_row, slot_rows(first + r), sem.at[1]).start()
                return carry

            def pad_wait(r, carry, first=first):
                pltpu.make_async_copy(zero_row, slot_rows(first + r), sem.at[1]).wait()
                return carry

            lax.fori_loop(0, n_pad, pad_start, 0)
            lax.fori_loop(0, n_pad, pad_wait, 0)

        def tile_copy(n):
            rows = tile_rows * rpt
            return pltpu.make_async_copy(zero_ref, xs_ref.at[pl.ds(pl.multiple_of(n * rows, rows), rows)],
                                         sem.at[1])

        def unused_start(n, carry):
            tile_copy(n).start()
            return carry

        def unused_wait(n, carry):
            tile_copy(n).wait()
            return carry

        lax.fori_loop(meta_ref[LANES - 1], n_tiles, unused_start, 0)
        lax.fori_loop(meta_ref[LANES - 1], n_tiles, unused_wait, 0)

    for _ in range(2):
        pltpu.make_async_copy(h_ref, xs_ref.at[pl.ds(0, chunk * rpt)], sem.at[0]).wait()


def _dispatch(pos_flat, meta_flat, h_lin, n_slots, rpt, tile_rows, chunk=1024):
    t = pos_flat.shape[0] // 2
    return pl.pallas_call(
        functools.partial(_dispatch_kernel, t=t, rpt=rpt, chunk=chunk, tile_rows=tile_rows,
                          n_tiles=n_slots // tile_rows),
        grid_spec=pltpu.PrefetchScalarGridSpec(
            num_scalar_prefetch=2,
            grid=(t // chunk,),
            in_specs=[pl.BlockSpec((chunk * rpt, LANES), lambda c, p, m: (c, 0))],
            out_specs=pl.BlockSpec(memory_space=pl.ANY),
            scratch_shapes=[pltpu.VMEM((tile_rows * rpt, LANES), h_lin.dtype), pltpu.SemaphoreType.DMA((2,))],
        ),
        out_shape=jax.ShapeDtypeStruct((n_slots * rpt, LANES), h_lin.dtype),
        compiler_params=_params("arbitrary"),
    )(pos_flat, meta_flat, h_lin)


def _ffn_kernel(meta_ref, xs_ref, wg_hbm, wu_hbm, wd_hbm, ys_ref, wgs_ref, wus_ref, wds_ref,
                wgb_ref, wub_ref, wdb_ref, sem, *, tm, layer):
    n = pl.program_id(0)
    n_used = meta_ref[LANES - 1]

    def fetch(e):
        return (pltpu.make_async_copy(wg_hbm.at[layer, e], wgs_ref, sem.at[0]),
                pltpu.make_async_copy(wu_hbm.at[layer, e], wus_ref, sem.at[1]),
                pltpu.make_async_copy(wd_hbm.at[layer, e], wds_ref, sem.at[2]))

    @pl.when(n < n_used)
    def _():
        @pl.when((n == 0) | (meta_ref[n] != meta_ref[jnp.maximum(n - 1, 0)]))
        def _():
            e = meta_ref[n]

            @pl.when(n == 0)
            def _():
                for cp in fetch(e):
                    cp.start()

            for cp in fetch(e):
                cp.wait()
            wgb_ref[...] = wgs_ref[...].astype(BF16)
            wub_ref[...] = wus_ref[...].astype(BF16)
            wdb_ref[...] = wds_ref[...].astype(BF16)
            nxt = lax.while_loop(lambda m: (m < n_used) & (meta_ref[m] == e), lambda m: m + 1, n + 1)

            @pl.when(nxt < n_used)
            def _():
                for cp in fetch(meta_ref[nxt]):
                    cp.start()

        d = wgb_ref.shape[0]
        a = b = None
        for k0 in range(0, d, MXU_DIM):
            x = jnp.concatenate(
                [_load_token_major(xs_ref, tm, k0 // LANES + j).astype(BF16) for j in range(MXU_DIM // LANES)],
                axis=1)
            pa = _dot(x, wgb_ref[k0:k0 + MXU_DIM, :])
            pb = _dot(x, wub_ref[k0:k0 + MXU_DIM, :])
            a = pa if a is None else a + pa
            b = pb if b is None else b + pb
        act = (_silu(a) * b).astype(BF16)
        for n0 in range(0, d, MXU_DIM):
            _store_token_major(ys_ref, _dot(act, wdb_ref[:, n0:n0 + MXU_DIM]), n0)

    @pl.when(n >= n_used)
    def _():
        ys_ref[...] = jnp.zeros_like(ys_ref)


def _ffn(meta, xs, w_gate, w_up, w_down, layer, tm, n_tiles):
    _, ne, d, f = w_gate.shape
    rpt = d // LANES

    def row_tile(n, m):
        return (jnp.minimum(n, m[LANES - 1] - 1), 0)

    return pl.pallas_call(
        functools.partial(_ffn_kernel, tm=tm, layer=layer),
        grid_spec=pltpu.PrefetchScalarGridSpec(
            num_scalar_prefetch=1,
            grid=(n_tiles,),
            in_specs=[
                pl.BlockSpec((tm * rpt, LANES), row_tile),
                pl.BlockSpec(memory_space=pl.ANY),
                pl.BlockSpec(memory_space=pl.ANY),
                pl.BlockSpec(memory_space=pl.ANY),
            ],
            out_specs=pl.BlockSpec((tm * rpt, LANES), lambda n, m: (n, 0)),
            scratch_shapes=[
                pltpu.VMEM((d, f), F32),
                pltpu.VMEM((d, f), F32),
                pltpu.VMEM((f, d), F32),
                pltpu.VMEM((d, f), BF16),
                pltpu.VMEM((d, f), BF16),
                pltpu.VMEM((f, d), BF16),
                pltpu.SemaphoreType.DMA((3,)),
            ],
        ),
        out_shape=jax.ShapeDtypeStruct((n_tiles * tm * rpt, LANES), F32),
        compiler_params=_params("arbitrary"),
    )(meta, xs, w_gate, w_up, w_down)


def _combine_kernel(pos_ref, ys_ref, *refs, t, rpt, pitch, tm, n_parts, final_tiles):
    x_refs = refs[:n_parts]
    if final_tiles is None:
        wt_ref, gate_ref, o_ref, a0, b0, a1, b1, sem = refs[n_parts:]
    else:
        wt_ref, gate_ref, fg_ref, op_ref, os_ref, a0, b0, a1, b1, sem = refs[n_parts:]
    i = pl.program_id(0)
    n = pl.num_programs(0)
    bufs = ((a0, b0), (a1, b1))

    def issue(tile, s):
        def body(r, carry):
            tok = tile * tm + r
            for k in range(2):
                slot_row = pos_ref[k * t + tok]
                pltpu.make_async_copy(
                    ys_ref.at[pl.ds(pl.multiple_of(slot_row * rpt, rpt), rpt)],
                    bufs[s][k].at[pl.ds(pl.multiple_of(r * pitch, 8), rpt)],
                    sem.at[s]).start(priority=k)
            return carry

        lax.fori_loop(0, tm, body, 0, unroll=8)

    def wait(s):
        for k in range(2):
            pltpu.make_async_copy(ys_ref.at[pl.ds(0, tm * rpt)], bufs[s][k].at[pl.ds(0, tm * rpt)],
                                  sem.at[s]).wait()

    def compute(s, out_ref):
        w = wt_ref[...]
        w0, w1 = w[:, 0:1], w[:, 1:2]
        dn = x_refs[0].shape[1]
        for c in range(rpt):
            cols = slice(c * LANES, (c + 1) * LANES)
            part, pc = divmod(c * LANES, dn)
            y = (w0 * bufs[s][0][pl.ds(c, tm, stride=pitch), :]
                 + w1 * bufs[s][1][pl.ds(c, tm, stride=pitch), :])
            out_ref[:, cols] = x_refs[part][:, pc:pc + LANES] + gate_ref[:, cols] * y
        if final_tiles is not None:
            out_ref[...] = _rms_f32(out_ref[...]) * fg_ref[...]

    @pl.when(i == 0)
    def _():
        issue(0, 0)

    for s in range(2):
        @pl.when(i % 2 == s)
        def _(s=s):
            @pl.when(i + 1 < n)
            def _():
                issue(i + 1, 1 - s)

            wait(s)
            if final_tiles is None:
                compute(s, o_ref)
            else:
                @pl.when(i < final_tiles)
                def _():
                    compute(s, op_ref)

                @pl.when(i >= final_tiles)
                def _():
                    compute(s, os_ref)


def _combine(pos_flat, ys, x1_parts, wt, mod3, layer, row_fn, final=None, tm=256):
    t, dn = x1_parts[0].shape
    n_parts = len(x1_parts)
    d = dn * n_parts
    rpt = d // LANES
    pitch = rpt + 8
    base = layer * 48
    in_specs = ([pl.BlockSpec(memory_space=pl.ANY)]
                + [pl.BlockSpec((tm, dn), lambda i, p: (i, 0))] * n_parts
                + [pl.BlockSpec((tm, 2), lambda i, p: (i, 0)),
                   pl.BlockSpec((None, 1, d), lambda i, p: (base + row_fn(i) * 6 + 5, 0, 0))])
    args = [pos_flat, ys, *x1_parts, wt, mod3]
    if final is None:
        final_tiles = None
        out_specs = pl.BlockSpec((tm, d), lambda i, p: (i, 0))
        out_shape = jax.ShapeDtypeStruct((t, d), F32)
    else:
        fg, t_ctx = final
        final_tiles = t_ctx // tm
        in_specs.append(pl.BlockSpec((1, d), lambda i, p: (0, 0)))
        args.append(fg.reshape(1, d))
        out_specs = [
            pl.BlockSpec((tm, d), lambda i, p: (jnp.minimum(i, final_tiles - 1), 0)),
            pl.BlockSpec((tm, d), lambda i, p: (jnp.maximum(i - final_tiles, 0), 0)),
        ]
        out_shape = [jax.ShapeDtypeStruct((t_ctx, d), F32), jax.ShapeDtypeStruct((t - t_ctx, d), F32)]
    return pl.pallas_call(
        functools.partial(_combine_kernel, t=t, rpt=rpt, pitch=pitch, tm=tm, n_parts=n_parts,
                          final_tiles=final_tiles),
        grid_spec=pltpu.PrefetchScalarGridSpec(
            num_scalar_prefetch=1,
            grid=(t // tm,),
            in_specs=in_specs,
            out_specs=out_specs,
            scratch_shapes=[pltpu.VMEM((tm * pitch, LANES), F32)] * 4 + [pltpu.SemaphoreType.DMA((2,))],
        ),
        out_shape=out_shape,
        compiler_params=_params("arbitrary"),
    )(*args)


def _moe(lg, router_b, h_lin, x1, w_gate, w_up, w_down, mod3, layer, row_fn, final=None, tile_rows=256):
    t = x1[0].shape[0]
    n_tiles = (2 * t) // tile_rows + N_EXPERTS
    pos, wts, meta = _route(lg, router_b, tile_rows)
    pos_flat = pos[:2].reshape(-1)
    xs = _dispatch(pos_flat, meta[:3].reshape(-1), h_lin, n_tiles * tile_rows, h_lin.shape[0] // t, tile_rows)
    ys = _ffn(meta[0], xs, w_gate, w_up, w_down, layer, tile_rows, n_tiles)
    return _combine(pos_flat, ys, x1, wts[:2].T, mod3, layer, row_fn, final)


def _rope_tables(n_tok):
    rows = n_tok // GRID_W
    pos_r = jnp.repeat(jnp.arange(rows), GRID_W).astype(F32)
    pos_c = jnp.tile(jnp.arange(GRID_W), rows).astype(F32)
    n_freq = DK_A // 4
    inv = jnp.power(ROPE_BASE, -jnp.arange(n_freq, dtype=F32) / n_freq)
    ang_r = pos_r[:, None] * inv
    ang_c = pos_c[:, None] * inv
    cos_t = jnp.concatenate([jnp.cos(ang_r), jnp.cos(ang_r), jnp.cos(ang_c), jnp.cos(ang_c)], axis=1)
    sin_t = jnp.concatenate([-jnp.sin(ang_r), jnp.sin(ang_r), -jnp.sin(ang_c), jnp.sin(ang_c)], axis=1)
    return cos_t, sin_t


def kernel(x_prompt, x_sample, cache_k, cache_v, state_ret, c, c_ctx, norm_g, w_mod, b_mod, w_in_even, w_out_even, lam_q, lam_k, ret_decay_logit, w_in_odd, gmlp_norm_g, w_spatial, b_spatial, w_out_odd, router_w, router_b, w_gate, w_up, w_down, final_norm_g):
    batch, seq, d = x_prompt.shape
    dbatch, dseq, _ = x_sample.shape
    past = cache_k.shape[2]
    depth = norm_g.shape[0]
    t_ctx, t_lat = batch * seq, dbatch * dseq
    t_all = t_ctx + t_lat
    xc = x_prompt.reshape(t_ctx, d)
    xl = x_sample.reshape(t_lat, d)

    cond8 = jnp.concatenate([c_ctx[None, :], c, jnp.zeros((8 - 1 - dbatch, d), F32)], axis=0)
    mod3 = _modulation(cond8, w_mod, b_mod).reshape(depth * 8 * 6, 1, d)
    router_wt = router_w.T
    cos_t, sin_t = _rope_tables(dseq)

    tm_in = 512
    tm_out = 512
    tm_cmb = 256
    w_in0 = w_in_even.reshape(d, -1)
    w_out0 = w_out_even.reshape(-1, d)
    w_in1 = w_in_odd.reshape(d, -1)
    w_out1 = w_out_odd.reshape(-1, d)
    lq, lk, dlog = lam_q.reshape(2, DK_A), lam_k.reshape(2, DK_A), ret_decay_logit.reshape(2, H_B)

    def rows_ctx(i):
        return 0

    def rows_lat(tiles_per_batch):
        return lambda i: 1 + i // tiles_per_batch

    def rows_all(tm):
        n_ctx, per = t_ctx // tm, dseq // tm
        return lambda i: jnp.maximum(i - n_ctx, 0) // per + jnp.where(i >= n_ctx, 1, 0)

    lam_init = 0.8 - 0.6 * math.exp(-0.3 * 0)
    half0 = w_in0.shape[1] // 2
    g00 = norm_g[0, 0]
    rl = rows_lat(dseq // tm_in)
    pa_c, k_new, v_new = _in_proj(xc, g00, mod3, 0, rows_ctx, w_in0, 0, half0, "even_kv", tm=tm_in)
    (pb_c,) = _in_proj(xc, g00, mod3, 0, rows_ctx, w_in0, half0, half0, "even", tm=tm_in)
    (pa_l,) = _in_proj(xl, g00, mod3, 0, rl, w_in0, 0, half0, "even", tm=tm_in)
    (pb_l,) = _in_proj(xl, g00, mod3, 0, rl, w_in0, half0, half0, "even", tm=tm_in)
    oa_c, or_c, s_new = _mixer(pa_c, pb_c, lq, lk, dlog, lam_init, batch, seq, H_A)
    ctx = (cache_k.reshape(dbatch, past, H_A * 2 * DK_A), cache_v.reshape(dbatch, past, H_A * DV_A),
           state_ret.reshape(dbatch, 2, H_B, DK_B, DV_B), cos_t, sin_t)
    oa_l, or_l = _mixer(pa_l, pb_l, lq, lk, dlog, lam_init, dbatch, dseq, 1, ctx)
    x1, h2, lg = _out_proj([([oa_c, or_c], xc), ([oa_l, or_l], xl)], w_out0, norm_g[0, 1], mod3, 0,
                           rows_all(tm_out), router_wt, tm=tm_out)
    x2 = _moe(lg, router_b, h2, x1, w_gate, w_up, w_down, mod3, 0, rows_all(tm_cmb))

    q1 = w_in1.shape[1] // 4
    g10 = norm_g[1, 0]
    ra = rows_all(tm_in)
    v_blk, v_ss = zip(*[_in_proj(x2, g10, mod3, 1, ra, w_in1, (2 + j) * q1, q1, "gmlp_v", tm=tm_in)
                        for j in range(2)])
    gmlp = (gmlp_norm_g.reshape(1, -1), w_spatial.reshape(H_C, CHUNK_C, CHUNK_C),
            b_spatial.reshape(H_C, CHUNK_C, 1))
    g0, g1 = [_in_proj(x2, g10, mod3, 1, ra, w_in1, j * q1, q1, "gmlp_u",
                       gating=(v_blk[j], v_ss[0], v_ss[1], *gmlp, j), tm=tm_in)[0] for j in range(2)]
    x3, h4, lg2 = _out_proj([([g0, g1], x2)], w_out1, norm_g[1, 1], mod3, 1, rows_all(tm_out), router_wt,
                            tm=tm_out)
    y_p, y_s = _moe(lg2, router_b, h4, x3, w_gate, w_up, w_down, mod3, 1, rows_all(tm_cmb),
                    final=(final_norm_g, t_ctx))

    y_prompt = y_p.reshape(batch, seq, d)
    y_sample = y_s.reshape(dbatch, dseq, d)
    new_cache_k = k_new.reshape(batch, 1, seq, H_A, 2, DK_A)
    new_cache_v = v_new.reshape(batch, 1, seq, DV_A // LANES, H_A, LANES).swapaxes(3, 4).reshape(
        batch, 1, seq, H_A, DV_A)
    new_state_ret = s_new.reshape(batch, 1, 2, H_B, DK_B, DV_B)
    return (y_prompt, y_sample, new_cache_k, new_cache_v, new_state_ret)
```
